```python
import jax, jax.numpy as jnp
from jax import lax
import numpy as np

D_MODEL = 1024
BATCH = 2
SEQ = 16384
DEPTH = 2

HEAD_DIM = 64
ROT_DIM = HEAD_DIM // 4
ROPE_THETA = 500000.0
NSA_HEADS = 8
NSA_KV_HEADS = 2
NSA_GROUP = NSA_HEADS // NSA_KV_HEADS
CMP_LEN = 32
CMP_STRIDE = 16
CMP_HID = 2 * HEAD_DIM
SLC_LEN = 64
SLC_TOPN = 16
WIN = 512
MOBA_HEADS = 8
MOBA_BLOCK = 256
MOBA_TOPK = 3
Q_CHUNK = 64
PEER_HEADS = 8
PEER_NKEYS = 128
PEER_EXPERTS = PEER_NKEYS * PEER_NKEYS
PEER_QDIM = 256
PEER_TOPK = 16
PEER_CHUNK = 128
RMS_EPS = 1e-6
NEG = -1e30
SEL_FORCE = 1e4

NSA_Q = NSA_HEADS * HEAD_DIM
NSA_KV = NSA_KV_HEADS * HEAD_DIM
MOBA_W = MOBA_HEADS * HEAD_DIM
IN_COLS = NSA_Q + 6 * NSA_KV + 3 * NSA_HEADS + 3 * MOBA_W + 2 * D_MODEL

kernel_name = "hybrid_nsa_moba_peer_adaln"


def rms_norm(x, g):
    xf = x.astype(jnp.float32)
    y = xf * lax.rsqrt(jnp.mean(xf * xf, axis=-1, keepdims=True) + RMS_EPS)
    return (y * g.astype(jnp.float32)).astype(x.dtype)


def partial_rope(x, pos):
    half = ROT_DIM // 2
    inv = ROPE_THETA ** (-jnp.arange(half, dtype=jnp.float32) / half)
    ang = pos.astype(jnp.float32)[:, None] * inv[None, :]
    cos = jnp.cos(ang)[None, :, None, :].astype(x.dtype)
    sin = jnp.sin(ang)[None, :, None, :].astype(x.dtype)
    x1, x2, rest = x[..., :half], x[..., half:ROT_DIM], x[..., ROT_DIM:]
    return jnp.concatenate([x1 * cos - x2 * sin, x2 * cos + x1 * sin, rest], axis=-1)


def masked_softmax(s, mask):
    s = jnp.where(mask, s.astype(jnp.float32), NEG)
    m = jnp.max(s, axis=-1, keepdims=True)
    p = jnp.where(mask, jnp.exp(s - m), 0.0)
    return p / jnp.maximum(jnp.sum(p, axis=-1, keepdims=True), 1e-30)


def split_cols(x, sizes):
    out, o = [], 0
    for n in sizes:
        out.append(x[..., o:o + n])
        o += n
    return out


def nsa_compress(k, pe, w1, w2):
    b, s = k.shape[:2]
    nc = (s - CMP_LEN) // CMP_STRIDE + 1
    idx = jnp.arange(nc)[:, None] * CMP_STRIDE + jnp.arange(CMP_LEN)[None, :]
    blk = k[:, idx] + pe[None, None, :, None, :]
    blk = jnp.transpose(blk, (0, 1, 3, 2, 4)).reshape(b, nc, NSA_KV_HEADS, CMP_LEN * HEAD_DIM)
    return jax.nn.gelu(blk @ w1) @ w2


def nsa_attention(q_rot, q_raw, kc, vc, ks, vs, kw, vw, gates):
    b, s = q_rot.shape[:2]
    nc = kc.shape[1]
    ns = s // SLC_LEN
    n_sel = min(SLC_TOPN, ns)
    scale = HEAD_DIM ** -0.5
    c_start = jnp.arange(nc) * CMP_STRIDE
    cmp_end = c_start + CMP_LEN - 1
    s_start = jnp.arange(ns) * SLC_LEN
    overlap = jnp.maximum(
        jnp.minimum(c_start[:, None] + CMP_LEN, s_start[None, :] + SLC_LEN)
        - jnp.maximum(c_start[:, None], s_start[None, :]), 0).astype(jnp.float32) / CMP_LEN
    ks_blk = ks.reshape(b, ns, SLC_LEN, NSA_KV_HEADS, HEAD_DIM).transpose(0, 3, 1, 2, 4)
    vs_blk = vs.reshape(b, ns, SLC_LEN, NSA_KV_HEADS, HEAD_DIM).transpose(0, 3, 1, 2, 4)
    kw_pad = jnp.pad(kw, ((0, 0), (WIN, 0), (0, 0), (0, 0)))
    vw_pad = jnp.pad(vw, ((0, 0), (WIN, 0), (0, 0), (0, 0)))
    bi = jnp.arange(b)[:, None, None, None]
    hi = jnp.arange(NSA_KV_HEADS)[None, :, None, None]
    blk_id = jnp.arange(ns)

    def chunk(i):
        s0 = i * Q_CHUNK
        t = s0 + jnp.arange(Q_CHUNK)
        qr = lax.dynamic_slice_in_dim(q_rot, s0, Q_CHUNK, 1).reshape(b, Q_CHUNK, NSA_KV_HEADS, NSA_GROUP, HEAD_DIM)
        qu = lax.dynamic_slice_in_dim(q_raw, s0, Q_CHUNK, 1).reshape(b, Q_CHUNK, NSA_KV_HEADS, NSA_GROUP, HEAD_DIM)
        g = lax.dynamic_slice_in_dim(gates, s0, Q_CHUNK, 1)
        sc = jnp.einsum('bqkgd,bckd->bkgqc', qu, kc).astype(jnp.float32) * scale
        p_c = masked_softmax(sc, cmp_end[None, :] <= t[:, None])
        o_c = jnp.einsum('bkgqc,bckd->bqkgd', p_c.astype(vc.dtype), vc)
        imp = jnp.einsum('bkgqc,cn->bkqn', p_c, overlap)
        cur = t // SLC_LEN
        valid = blk_id[None, :] <= cur[:, None]
        forced = (blk_id[None, :] == 0) | (blk_id[None, :] == cur[:, None]) | (blk_id[None, :] == cur[:, None] - 1)
        score = jnp.where(valid, jnp.where(forced, SEL_FORCE, imp), NEG)
        vals, idx = lax.top_k(score, n_sel)
        ok = vals > NEG * 0.5
        kg = ks_blk[bi, hi, idx]
        vg = vs_blk[bi, hi, idx]
        kpos = idx[..., None] * SLC_LEN + jnp.arange(SLC_LEN)
        smask = (ok[..., None] & (kpos <= t[None, None, :, None, None])).reshape(b, NSA_KV_HEADS, 1, Q_CHUNK, n_sel * SLC_LEN)
        ss = jnp.einsum('bqkgd,bkqnld->bkgqnl', qr, kg).reshape(b, NSA_KV_HEADS, NSA_GROUP, Q_CHUNK, n_sel * SLC_LEN)
        p_s = masked_softmax(ss.astype(jnp.float32) * scale, smask)
        p_s = p_s.reshape(b, NSA_KV_HEADS, NSA_GROUP, Q_CHUNK, n_sel, SLC_LEN).astype(vg.dtype)
        o_s = jnp.einsum('bkgqnl,bkqnld->bqkgd', p_s, vg)
        kwc = lax.dynamic_slice_in_dim(kw_pad, s0, WIN + Q_CHUNK, 1)
        vwc = lax.dynamic_slice_in_dim(vw_pad, s0, WIN + Q_CHUNK, 1)
        wpos = s0 - WIN + jnp.arange(WIN + Q_CHUNK)
        wmask = (wpos[None, :] <= t[:, None]) & (wpos[None, :] > t[:, None] - WIN) & (wpos[None, :] >= 0)
        sw = jnp.einsum('bqkgd,bjkd->bkgqj', qr, kwc).astype(jnp.float32) * scale
        p_w = masked_softmax(sw, wmask)
        o_w = jnp.einsum('bkgqj,bjkd->bqkgd', p_w.astype(vwc.dtype), vwc)
        shp = (b, Q_CHUNK, NSA_HEADS, HEAD_DIM)
        return (g[..., 0:1] * o_c.reshape(shp) + g[..., 1:2] * o_s.reshape(shp) + g[..., 2:3] * o_w.reshape(shp))

    out = lax.map(chunk, jnp.arange(s // Q_CHUNK))
    return jnp.transpose(out, (1, 0, 2, 3, 4)).reshape(b, s, NSA_Q)


def moba_attention(q, k, v):
    b, s = q.shape[:2]
    nb = -(-s // MOBA_BLOCK)
    pad = nb * MOBA_BLOCK - s
    n_top = min(MOBA_TOPK, nb)
    scale = HEAD_DIM ** -0.5
    k_pad = jnp.pad(k, ((0, 0), (0, pad), (0, 0), (0, 0)))
    v_pad = jnp.pad(v, ((0, 0), (0, pad), (0, 0), (0, 0)))
    k_blk = k_pad.reshape(b, nb, MOBA_BLOCK, MOBA_HEADS, HEAD_DIM).transpose(0, 3, 1, 2, 4)
    v_blk = v_pad.reshape(b, nb, MOBA_BLOCK, MOBA_HEADS, HEAD_DIM).transpose(0, 3, 1, 2, 4)
    k_mean = jnp.mean(k_blk.astype(jnp.float32), axis=3)
    bi = jnp.arange(b)[:, None, None, None]
    hi = jnp.arange(MOBA_HEADS)[None, :, None, None]

    def chunk(i):
        s0 = i * Q_CHUNK
        t = s0 + jnp.arange(Q_CHUNK)
        cur = s0 // MOBA_BLOCK
        qc = lax.dynamic_slice_in_dim(q, s0, Q_CHUNK, 1)
        gs = jnp.einsum('bqhd,bhnd->bhqn', qc.astype(jnp.float32), k_mean)
        gs = jnp.where(jnp.arange(nb) < cur, gs, NEG)
        vals, idx = lax.top_k(gs, n_top)
        ok = vals > NEG * 0.5
        kg = k_blk[bi, hi, idx]
        vg = v_blk[bi, hi, idx]
        s_sel = jnp.einsum('bqhd,bhqnld->bhqnl', qc, kg).reshape(b, MOBA_HEADS, Q_CHUNK, n_top * MOBA_BLOCK)
        m_sel = jnp.broadcast_to(ok[..., None], (b, MOBA_HEADS, Q_CHUNK, n_top, MOBA_BLOCK)).reshape(b, MOBA_HEADS, Q_CHUNK, n_top * MOBA_BLOCK)
        k_own = lax.dynamic_slice_in_dim(k_pad, cur * MOBA_BLOCK, MOBA_BLOCK, 1)
        v_own = lax.dynamic_slice_in_dim(v_pad, cur * MOBA_BLOCK, MOBA_BLOCK, 1)
        opos = cur * MOBA_BLOCK + jnp.arange(MOBA_BLOCK)
        s_own = jnp.einsum('bqhd,blhd->bhql', qc, k_own)
        m_own = jnp.broadcast_to(opos[None, :] <= t[:, None], (b, MOBA_HEADS, Q_CHUNK, MOBA_BLOCK))
        scores = jnp.concatenate([s_sel, s_own], axis=-1).astype(jnp.float32) * scale
        p = masked_softmax(scores, jnp.concatenate([m_sel, m_own], axis=-1)).astype(v.dtype)
        p_sel = p[..., :n_top * MOBA_BLOCK].reshape(b, MOBA_HEADS, Q_CHUNK, n_top, MOBA_BLOCK)
        p_own = p[..., n_top * MOBA_BLOCK:]
        return jnp.einsum('bhqnl,bhqnld->bqhd', p_sel, vg) + jnp.einsum('bhql,blhd->bqhd', p_own, v_own)

    out = lax.map(chunk, jnp.arange(s // Q_CHUNK))
    return jnp.transpose(out, (1, 0, 2, 3, 4)).reshape(b, s, MOBA_W)


def token_mixer(h, w_in, cmp_pe, cmp_w1, cmp_w2, w_up_nsa, w_up_moba, w_out):
    b, s, _ = h.shape
    pos = jnp.arange(s, dtype=jnp.int32)
    proj = h @ w_in
    sizes = (NSA_Q, NSA_KV, NSA_KV, NSA_KV, NSA_KV, NSA_KV, NSA_KV, 3 * NSA_HEADS,
             MOBA_W, MOBA_W, MOBA_W, D_MODEL, D_MODEL)
    nq, kc, vc, ks, vs, kw, vw, ng, mq, mk, mv, gn, gm = split_cols(proj, sizes)
    nq = nq.reshape(b, s, NSA_HEADS, HEAD_DIM)
    kvh = lambda z: z.reshape(b, s, NSA_KV_HEADS, HEAD_DIM)
    mh = lambda z: z.reshape(b, s, MOBA_HEADS, HEAD_DIM)
    kc = nsa_compress(kvh(kc), cmp_pe[0], cmp_w1[0], cmp_w2[0])
    vc = nsa_compress(kvh(vc), cmp_pe[1], cmp_w1[1], cmp_w2[1])
    nsa_gates = jax.nn.sigmoid(ng).reshape(b, s, NSA_HEADS, 3)
    o_nsa = nsa_attention(partial_rope(nq, pos), nq, kc, vc, partial_rope(kvh(ks), pos), kvh(vs),
                          partial_rope(kvh(kw), pos), kvh(vw), nsa_gates)
    o_moba = moba_attention(partial_rope(mh(mq), pos), partial_rope(mh(mk), pos), mh(mv))
    y = jax.nn.sigmoid(gn) * (o_nsa @ w_up_nsa) + jax.nn.sigmoid(gm) * (o_moba @ w_up_moba)
    return y @ w_out


def peer_ffn(h, wq, k1, k2, u, v):
    b, s, d = h.shape
    half = PEER_QDIM // 2
    tokens = h.reshape(-1, PEER_CHUNK, d)

    def chunk(ht):
        q = (ht @ wq).reshape(PEER_CHUNK, PEER_HEADS, 2, half)
        s1 = jnp.einsum('thd,nd->thn', q[:, :, 0], k1).astype(jnp.float32)
        s2 = jnp.einsum('thd,nd->thn', q[:, :, 1], k2).astype(jnp.float32)
        v1, i1 = lax.top_k(s1, PEER_TOPK)
        v2, i2 = lax.top_k(s2, PEER_TOPK)
        cand = (v1[..., :, None] + v2[..., None, :]).reshape(PEER_CHUNK, PEER_HEADS, PEER_TOPK * PEER_TOPK)
        vals, ci = lax.top_k(cand, PEER_TOPK)
        e = (jnp.take_along_axis(i1, ci // PEER_TOPK, axis=-1) * PEER_NKEYS
             + jnp.take_along_axis(i2, ci % PEER_TOPK, axis=-1))
        gate = jax.nn.softmax(vals, axis=-1).astype(ht.dtype)
        act = jax.nn.gelu(jnp.einsum('td,thkd->thk', ht, u[e]))
        return jnp.einsum('thk,thkd->td', gate * act, v[e])

    return lax.map(chunk, tokens).reshape(b, s, d)


def setup_inputs(seed: int = 0) -> dict:
    key = jax.random.key(seed)
    ks = jax.random.split(key, 20)
    nrm = lambda k, shape, sc: jax.random.normal(k, shape, jnp.float32) * sc
    D = D_MODEL
    return {
        "x": nrm(ks[0], (BATCH, SEQ, D), 1.0),
        "c": nrm(ks[1], (BATCH, D), 1.0),
        "w_ada": nrm(ks[2], (DEPTH, D, 6 * D), 0.5 * D ** -0.5),
        "b_ada": nrm(ks[3], (DEPTH, 6 * D), 0.02),
        "g_attn": 1.0 + nrm(ks[4], (DEPTH, D), 0.02),
        "g_ffn": 1.0 + nrm(ks[5], (DEPTH, D), 0.02),
        "w_in": nrm(ks[6], (DEPTH, D, IN_COLS), D ** -0.5),
        "cmp_pe": nrm(ks[7], (DEPTH, 2, CMP_LEN, HEAD_DIM), 0.1),
        "cmp_w1": nrm(ks[8], (DEPTH, 2, CMP_LEN * HEAD_DIM, CMP_HID), (CMP_LEN * HEAD_DIM) ** -0.5),
        "cmp_w2": nrm(ks[9], (DEPTH, 2, CMP_HID, HEAD_DIM), CMP_HID ** -0.5),
        "w_up_nsa": nrm(ks[10], (DEPTH, NSA_Q, D), NSA_Q ** -0.5),
        "w_up_moba": nrm(ks[11], (DEPTH, MOBA_W, D), MOBA_W ** -0.5),
        "w_out": nrm(ks[12], (DEPTH, D, D), D ** -0.5),
        "peer_wq": nrm(ks[13], (DEPTH, D, PEER_HEADS * PEER_QDIM), D ** -0.5),
        "peer_k1": nrm(ks[14], (DEPTH, PEER_NKEYS, PEER_QDIM // 2), (PEER_QDIM // 2) ** -0.5),
        "peer_k2": nrm(ks[15], (DEPTH, PEER_NKEYS, PEER_QDIM // 2), (PEER_QDIM // 2) ** -0.5),
        "peer_u": nrm(ks[16], (DEPTH, PEER_EXPERTS, D), D ** -0.5),
        "peer_v": nrm(ks[17], (DEPTH, PEER_EXPERTS, D), PEER_HEADS ** -0.5),
        "g_final": 1.0 + nrm(ks[18], (D,), 0.02),
    }


def reference(x, c, w_ada, b_ada, g_attn, g_ffn, w_in, cmp_pe, cmp_w1, cmp_w2, w_up_nsa, w_up_moba,
              w_out, peer_wq, peer_k1, peer_k2, peer_u, peer_v, g_final):
    sc_c = jax.nn.silu(c)
    for l in range(DEPTH):
        mod = sc_c @ w_ada[l] + b_ada[l]
        sh1, sc1, ga1, sh2, sc2, ga2 = jnp.split(mod, 6, axis=-1)
        h = rms_norm(x, g_attn[l]) * (1.0 + sc1[:, None, :]) + sh1[:, None, :]
        x = x + ga1[:, None, :] * token_mixer(h, w_in[l], cmp_pe[l], cmp_w1[l], cmp_w2[l],
                                              w_up_nsa[l], w_up_moba[l], w_out[l])
        h = rms_norm(x, g_ffn[l]) * (1.0 + sc2[:, None, :]) + sh2[:, None, :]
        x = x + ga2[:, None, :] * peer_ffn(h, peer_wq[l], peer_k1[l], peer_k2[l], peer_u[l], peer_v[l])
    return rms_norm(x, g_final)
```

```python
import functools

import jax
import jax.numpy as jnp
import numpy as np
from jax import lax
from jax.experimental import pallas as pl
from jax.experimental.pallas import tpu as pltpu

D_MODEL = 1024
HEAD_DIM = 64
ROT_DIM = HEAD_DIM // 4
ROPE_THETA = 500000.0
NSA_HEADS = 8
NSA_KV_HEADS = 2
NSA_GROUP = NSA_HEADS // NSA_KV_HEADS
CMP_LEN = 32
CMP_STRIDE = 16
CMP_HID = 2 * HEAD_DIM
SLC_LEN = 64
SLC_TOPN = 16
WIN = 512
MOBA_HEADS = 8
MOBA_BLOCK = 256
MOBA_TOPK = 3
PEER_HEADS = 8
PEER_NKEYS = 128
PEER_QDIM = 256
PEER_TOPK = 16
RMS_EPS = 1e-6
NEG = -1e30
SEL_FORCE = 1e4
LOWEST = -3.0e38

NSA_Q = NSA_HEADS * HEAD_DIM
NSA_KV = NSA_KV_HEADS * HEAD_DIM
MOBA_W = MOBA_HEADS * HEAD_DIM

CDT = jnp.bfloat16
V7X_VMEM_LIMIT = 56 * 1024 * 1024

COL_QROT = 0
COL_KS = 512
COL_KW = 640
COL_MQ = 768
COL_MK = 1280
ROPE_COLS = 1792
COL_NG = 1792
COL_GN = 2048
COL_GM = 3072
COL_QRAW = 4096
COL_KC = 4608
COL_VC = 4736
COL_VS = 4864
COL_VW = 4992
COL_MV = 5120
PROJ_COLS = 5632
PROJ_TN = 256


def _params(sem):
    return pltpu.CompilerParams(dimension_semantics=sem, vmem_limit_bytes=V7X_VMEM_LIMIT)


def _dot(a, b):
    return jnp.dot(a, b, preferred_element_type=jnp.float32)


def _dot_nt(a, b):
    return lax.dot_general(a, b, (((1,), (1,)), ((), ())), preferred_element_type=jnp.float32)


def _split_hi_lo(x):
    hi = x.astype(CDT)
    lo = (x - hi.astype(jnp.float32)).astype(CDT)
    return hi, lo


def _gelu(x):
    return 0.5 * x * (1.0 + jnp.tanh(0.7978845608028654 * (x + 0.044715 * (x * x * x))))


def _sigmoid(x):
    return 1.0 / (1.0 + jnp.exp(-x))


def _mod_kernel(c_ref, w_ref, b_ref, o_ref):
    c = c_ref[...]
    sc = c * _sigmoid(c)
    o_ref[0] = jnp.dot(sc, w_ref[0], preferred_element_type=jnp.float32,
                       precision=lax.Precision.HIGHEST) + b_ref[0]


def _adaln_mod(c, w_ada, b_ada):
    depth, d, six_d = w_ada.shape
    b = c.shape[0]
    rows = 8
    c_pad = jnp.zeros((rows, d), jnp.float32).at[:b].set(c)
    tn = 1024
    out = pl.pallas_call(
        _mod_kernel,
        out_shape=jax.ShapeDtypeStruct((depth, rows, six_d), jnp.float32),
        grid=(depth, six_d // tn),
        in_specs=[pl.BlockSpec((rows, d), lambda l, j: (0, 0)),
                  pl.BlockSpec((1, d, tn), lambda l, j: (l, 0, j)),
                  pl.BlockSpec((1, 1, tn), lambda l, j: (l, 0, j))],
        out_specs=pl.BlockSpec((1, rows, tn), lambda l, j: (l, 0, j)),
        compiler_params=_params(("arbitrary", "arbitrary")),
        name="adaln_mod",
    )(c_pad, w_ada, b_ada.reshape(depth, 1, six_d))
    return out[:, :b]


def _nmm_kernel(*refs, n_rope, emit_h):
    if n_rope:
        x_ref, g_ref, sc_ref, sh_ref, w_ref, cos_ref, sa_ref, sb_ref = refs[:8]
        rest = refs[8:]
    else:
        x_ref, g_ref, sc_ref, sh_ref, w_ref = refs[:5]
        rest = refs[5:]
    if emit_h:
        o_ref, ho_ref, h_ref = rest
    else:
        o_ref, h_ref = rest
    j = pl.program_id(1)

    @pl.when(j == 0)
    def _():
        x = x_ref[...]
        ms = jnp.mean(x * x, axis=-1, keepdims=True)
        y = x * lax.rsqrt(ms + RMS_EPS) * g_ref[...]
        h = (y * (1.0 + sc_ref[0]) + sh_ref[0]).astype(h_ref.dtype)
        h_ref[...] = h
        if emit_h:
            ho_ref[...] = h

    acc = _dot(h_ref[...], w_ref[...])
    if n_rope:
        @pl.when(j < n_rope)
        def _():
            cos, sa, sb = cos_ref[...], sa_ref[...], sb_ref[...]
            parts = []
            for c0 in range(0, acc.shape[1], 128):
                a = acc[:, c0:c0 + 128]
                parts.append(a * cos + pltpu.roll(a, 128 - ROT_DIM // 2, axis=1) * sa
                             + pltpu.roll(a, ROT_DIM // 2, axis=1) * sb)
            o_ref[...] = jnp.concatenate(parts, axis=1).astype(o_ref.dtype)

        @pl.when(j >= n_rope)
        def _():
            o_ref[...] = acc.astype(o_ref.dtype)
    else:
        o_ref[...] = acc.astype(o_ref.dtype)


def _norm_mod_matmul(x2d, g, sc, sh, w, seq, rope=None, n_rope=0, emit_h=False, tn=PROJ_TN):
    n, d = x2d.shape
    cols = w.shape[1]
    tm = min(1024, seq)
    per_seq = seq // tm
    in_specs = [pl.BlockSpec((tm, d), lambda i, j: (i, 0)),
                pl.BlockSpec((1, d), lambda i, j: (0, 0)),
                pl.BlockSpec((1, 1, d), lambda i, j: (i // per_seq, 0, 0)),
                pl.BlockSpec((1, 1, d), lambda i, j: (i // per_seq, 0, 0)),
                pl.BlockSpec((d, tn), lambda i, j: (0, j))]
    args = [x2d, g.reshape(1, d), sc, sh, w]
    if n_rope:
        in_specs += [pl.BlockSpec((tm, 128), lambda i, j: (i % per_seq, 0))] * 3
        args += list(rope)
    out_shape = [jax.ShapeDtypeStruct((n, cols), CDT)]
    out_specs = [pl.BlockSpec((tm, tn), lambda i, j: (i, j))]
    if emit_h:
        out_shape.append(jax.ShapeDtypeStruct((n, d), CDT))
        out_specs.append(pl.BlockSpec((tm, d), lambda i, j: (i, 0)))
    res = pl.pallas_call(
        functools.partial(_nmm_kernel, n_rope=n_rope, emit_h=emit_h),
        out_shape=out_shape,
        grid=(n // tm, cols // tn),
        in_specs=in_specs,
        out_specs=out_specs,
        scratch_shapes=[pltpu.VMEM((tm, d), CDT)],
        compiler_params=_params(("arbitrary", "arbitrary")),
        name="norm_mod_matmul",
    )(*args)
    return res if emit_h else res[0]


def _rope_tables(seq):
    half = ROT_DIM // 2
    inv = ROPE_THETA ** (-jnp.arange(half, dtype=jnp.float32) / half)
    ang = jnp.arange(seq, dtype=jnp.float32)[:, None] * inv[None, :]
    cos, sin = jnp.cos(ang), jnp.sin(ang)
    ones = jnp.ones((seq, HEAD_DIM - ROT_DIM), jnp.float32)
    zeros = jnp.zeros((seq, HEAD_DIM - ROT_DIM), jnp.float32)
    zh = jnp.zeros((seq, half), jnp.float32)
    c64 = jnp.concatenate([cos, cos, ones], axis=1)
    sa64 = jnp.concatenate([-sin, zh, zeros], axis=1)
    sb64 = jnp.concatenate([zh, sin, zeros], axis=1)
    return tuple(jnp.concatenate([t, t], axis=1) for t in (c64, sa64, sb64))


def _reorder_w_in(w_in):
    o = 0
    pieces = {}
    for name, size in (("nq", NSA_Q), ("kc", NSA_KV), ("vc", NSA_KV), ("ks", NSA_KV), ("vs", NSA_KV),
                       ("kw", NSA_KV), ("vw", NSA_KV), ("ng", 3 * NSA_HEADS), ("mq", MOBA_W),
                       ("mk", MOBA_W), ("mv", MOBA_W), ("gn", D_MODEL), ("gm", D_MODEL)):
        pieces[name] = w_in[:, o:o + size]
        o += size
    scale = HEAD_DIM ** -0.5
    pad = jnp.zeros((w_in.shape[0], COL_GN - COL_NG - 3 * NSA_HEADS), w_in.dtype)
    w = jnp.concatenate([pieces["nq"] * scale, pieces["ks"], pieces["kw"], pieces["mq"] * scale,
                         pieces["mk"], pieces["ng"], pad, pieces["gn"], pieces["gm"],
                         pieces["nq"] * scale, pieces["kc"], pieces["vc"], pieces["vs"],
                         pieces["vw"], pieces["mv"]], axis=1)
    assert w.shape[1] == PROJ_COLS
    return w.astype(CDT)


def _compress_kernel(x_ref, w1c_ref, pe_ref, w1_ref, w2_ref, o_ref):
    nb = x_ref.shape[3]
    ab = _dot(x_ref[0, 0, 0], w1c_ref[0])
    a = ab[:, :CMP_HID]
    b_next = pltpu.roll(ab[:, CMP_HID:], nb - 1, axis=0)
    bias = _dot(pe_ref[0], w1_ref[0])[0:1]
    hid = _gelu(a + b_next + bias)
    o_ref[0, 0, 0] = _dot(hid.astype(CDT), w2_ref[0]).astype(o_ref.dtype)


def _nsa_compress(proj, cmp_pe, cmp_w1, cmp_w2, batch, seq):
    nb = seq // CMP_STRIDE
    half = CMP_STRIDE * HEAD_DIM
    kcvc = proj[:, COL_KC:COL_KC + 2 * NSA_KV].reshape(batch, nb, CMP_STRIDE, 2, NSA_KV_HEADS, HEAD_DIM)
    x = jnp.transpose(kcvc, (3, 0, 4, 1, 2, 5)).reshape(2, batch, NSA_KV_HEADS, nb, half)
    w1c = jnp.concatenate([cmp_w1[:, :half], cmp_w1[:, half:]], axis=2).astype(CDT)
    pe = jnp.broadcast_to(cmp_pe.reshape(2, 1, CMP_LEN * HEAD_DIM), (2, 8, CMP_LEN * HEAD_DIM)).astype(CDT)
    return pl.pallas_call(
        _compress_kernel,
        out_shape=jax.ShapeDtypeStruct((2, batch, NSA_KV_HEADS, nb, HEAD_DIM), CDT),
        grid=(2, batch, NSA_KV_HEADS),
        in_specs=[pl.BlockSpec((1, 1, 1, nb, half), lambda w, b, k: (w, b, k, 0, 0)),
                  pl.BlockSpec((1, half, 2 * CMP_HID), lambda w, b, k: (w, 0, 0)),
                  pl.BlockSpec((1, 8, 2 * half), lambda w, b, k: (w, 0, 0)),
                  pl.BlockSpec((1, 2 * half, CMP_HID), lambda w, b, k: (w, 0, 0)),
                  pl.BlockSpec((1, CMP_HID, HEAD_DIM), lambda w, b, k: (w, 0, 0))],
        out_specs=pl.BlockSpec((1, 1, 1, nb, HEAD_DIM), lambda w, b, k: (w, b, k, 0, 0)),
        compiler_params=_params(("arbitrary", "arbitrary", "arbitrary")),
        name="nsa_compress",
    )(x, w1c, pe, cmp_w1.astype(CDT), cmp_w2.astype(CDT))


def _stack_heads(qblk):
    return jnp.concatenate([qblk[:, g * HEAD_DIM:(g + 1) * HEAD_DIM] for g in range(NSA_GROUP)], axis=0)


def _unstack_heads(o, tq):
    return jnp.concatenate([o[g * tq:(g + 1) * tq] for g in range(NSA_GROUP)], axis=1)


def _topk_mask(score, n_sel):
    shape = score.shape
    lane = lax.broadcasted_iota(jnp.int32, shape, 1).astype(jnp.float32)
    width = float(shape[1])

    def body(_, carry):
        sc, sel = carry
        m = jnp.max(sc, axis=-1, keepdims=True)
        idx = jnp.min(jnp.where(sc == m, lane, width), axis=-1, keepdims=True)
        pick = lane == idx
        return jnp.where(pick, LOWEST, sc), jnp.where(pick, 1.0, sel)

    _, sel = lax.fori_loop(0, n_sel, body, (score, jnp.zeros(shape, jnp.float32)))
    return sel


def _nsa_cmp_kernel(q_ref, kc_ref, vc_ref, ov_ref, oc_ref, sb_ref, *, tq, n_sel):
    i = pl.program_id(2)
    s0 = i * tq
    nb = kc_ref.shape[3]
    ns = ov_ref.shape[1]
    q4 = _stack_heads(q_ref[...])
    s = _dot_nt(q4, kc_ref[0, 0, 0])
    tq_col = s0 + lax.broadcasted_iota(jnp.int32, (tq, 1), 0)
    t4 = jnp.concatenate([tq_col] * NSA_GROUP, axis=0)
    cend = lax.broadcasted_iota(jnp.int32, (1, nb), 1) * CMP_STRIDE + (CMP_LEN - 1)
    mask = cend <= t4
    s = jnp.where(mask, s, NEG)
    m = jnp.max(s, axis=-1, keepdims=True)
    p = jnp.where(mask, jnp.exp(s - m), 0.0)
    p = p / jnp.maximum(jnp.sum(p, axis=-1, keepdims=True), 1e-30)
    o = _dot(p.astype(CDT), vc_ref[0, 0, 0])
    oc_ref[...] = _unstack_heads(o, tq).astype(oc_ref.dtype)

    psum = p[0:tq]
    for g in range(1, NSA_GROUP):
        psum = psum + p[g * tq:(g + 1) * tq]
    hi, lo = _split_hi_lo(psum)
    imp = _dot(hi, ov_ref[...]) + _dot(lo, ov_ref[...])

    blk = lax.broadcasted_iota(jnp.int32, (1, ns), 1)
    cur = tq_col // SLC_LEN
    valid = blk <= cur
    forced = (blk == 0) | (blk == cur) | (blk == cur - 1)
    score = jnp.where(valid, jnp.where(forced, SEL_FORCE, imp), NEG)
    sel = _topk_mask(score, n_sel)
    sb_ref[0, 0] = jnp.where((sel > 0.5) & valid, 0.0, NEG).astype(sb_ref.dtype)


def _nsa_compressed(proj, cmp_kv, batch, seq):
    tq = 128
    nq = seq // tq
    nb = seq // CMP_STRIDE
    ns = seq // SLC_LEN
    n_sel = min(SLC_TOPN, ns)
    c_start = np.arange(nb) * CMP_STRIDE
    s_start = np.arange(ns) * SLC_LEN
    ov = np.maximum(np.minimum(c_start[:, None] + CMP_LEN, s_start[None, :] + SLC_LEN)
                    - np.maximum(c_start[:, None], s_start[None, :]), 0).astype(np.float32) / CMP_LEN
    ov[nb - 1] = 0.0
    qb = COL_QRAW // (NSA_GROUP * HEAD_DIM)
    return pl.pallas_call(
        functools.partial(_nsa_cmp_kernel, tq=tq, n_sel=n_sel),
        out_shape=[jax.ShapeDtypeStruct((batch * seq, NSA_Q), CDT),
                   jax.ShapeDtypeStruct((batch, NSA_KV_HEADS, seq, ns), CDT)],
        grid=(batch, NSA_KV_HEADS, nq),
        in_specs=[pl.BlockSpec((tq, NSA_GROUP * HEAD_DIM), lambda b, k, i: (b * nq + i, qb + k)),
                  pl.BlockSpec((1, 1, 1, nb, HEAD_DIM), lambda b, k, i: (0, b, k, 0, 0)),
                  pl.BlockSpec((1, 1, 1, nb, HEAD_DIM), lambda b, k, i: (1, b, k, 0, 0)),
                  pl.BlockSpec((nb, ns), lambda b, k, i: (0, 0))],
        out_specs=[pl.BlockSpec((tq, NSA_GROUP * HEAD_DIM), lambda b, k, i: (b * nq + i, k)),
                   pl.BlockSpec((1, 1, tq, ns), lambda b, k, i: (b, k, i, 0))],
        compiler_params=_params(("arbitrary", "arbitrary", "arbitrary")),
        name="nsa_compressed",
    )(proj, cmp_kv, cmp_kv, jnp.asarray(ov, CDT))


def _flash_update(s, v_tile, m_ref, l_ref, acc_ref):
    m_prev = m_ref[...]
    m_new = jnp.maximum(m_prev, jnp.max(s, axis=-1, keepdims=True))
    alpha = jnp.exp(m_prev - m_new)
    p = jnp.exp(s - m_new)
    l_ref[...] = alpha * l_ref[...] + jnp.sum(p, axis=-1, keepdims=True)
    acc_ref[...] = alpha * acc_ref[...] + _dot(p.astype(CDT), v_tile)
    m_ref[...] = m_new


def _nsa_sel_kernel(q_ref, kt_ref, v_ref, sb_ref, ex_ref, o_ref, m_ref, l_ref, acc_ref, *, tq, tk):
    k = pl.program_id(1)
    i = pl.program_id(2)
    s0 = i * tq
    ns = sb_ref.shape[3]
    per_tile = tk // SLC_LEN
    q4 = _stack_heads(q_ref[...])
    sb = sb_ref[0, 0]
    tile_of_blk = lax.broadcasted_iota(jnp.int32, (1, ns), 1) // per_tile
    m_ref[...] = jnp.full(m_ref.shape, NEG, jnp.float32)
    l_ref[...] = jnp.zeros(l_ref.shape, jnp.float32)
    acc_ref[...] = jnp.zeros(acc_ref.shape, jnp.float32)

    def scores(kt):
        start = pl.multiple_of(kt * tk, tk)
        s = _dot(q4, kt_ref[0, 0, :, pl.ds(start, tk)])
        bias = _dot(jnp.where(tile_of_blk == kt, sb, jnp.zeros_like(sb)), ex_ref[...])
        s = (s.reshape(NSA_GROUP, tq, tk) + bias[None]).reshape(NSA_GROUP * tq, tk)
        return s, start

    def body(kt, carry):
        s, start = scores(kt)
        _flash_update(s, v_ref[pl.ds(start, tk), :], m_ref, l_ref, acc_ref)
        return carry

    n_full = s0 // tk
    lax.fori_loop(0, n_full, body, 0)
    s, start = scores(n_full)
    tq_col = s0 + lax.broadcasted_iota(jnp.int32, (tq, 1), 0)
    t4 = jnp.concatenate([tq_col] * NSA_GROUP, axis=0)
    kpos = start + lax.broadcasted_iota(jnp.int32, (1, tk), 1)
    s = jnp.where(kpos <= t4, s, NEG)
    _flash_update(s, v_ref[pl.ds(start, tk), :], m_ref, l_ref, acc_ref)
    o = acc_ref[...] / l_ref[...]
    o = jnp.where(k == 0, o[:, :HEAD_DIM], o[:, HEAD_DIM:])
    o_ref[...] = _unstack_heads(o, tq).astype(o_ref.dtype)


def _nsa_selected(proj, ks_t, selb, batch, seq):
    tq = 256
    tk = 512
    nq = seq // tq
    ns = seq // SLC_LEN
    per_tile = tk // SLC_LEN
    ex = (np.arange(ns)[:, None] % per_tile == np.arange(tk)[None, :] // SLC_LEN).astype(np.float32)
    qb = COL_QROT // (NSA_GROUP * HEAD_DIM)
    vb = COL_VS // 128
    rows = NSA_GROUP * tq
    return pl.pallas_call(
        functools.partial(_nsa_sel_kernel, tq=tq, tk=tk),
        out_shape=jax.ShapeDtypeStruct((batch * seq, NSA_Q), CDT),
        grid=(batch, NSA_KV_HEADS, nq),
        in_specs=[pl.BlockSpec((tq, NSA_GROUP * HEAD_DIM), lambda b, k, i: (b * nq + i, qb + k)),
                  pl.BlockSpec((1, 1, HEAD_DIM, seq), lambda b, k, i: (b, k, 0, 0)),
                  pl.BlockSpec((seq, 128), lambda b, k, i: (b, vb)),
                  pl.BlockSpec((1, 1, tq, ns), lambda b, k, i: (b, k, i, 0)),
                  pl.BlockSpec((ns, tk), lambda b, k, i: (0, 0))],
        out_specs=pl.BlockSpec((tq, NSA_GROUP * HEAD_DIM), lambda b, k, i: (b * nq + i, k)),
        scratch_shapes=[pltpu.VMEM((rows, 1), jnp.float32), pltpu.VMEM((rows, 1), jnp.float32),
                        pltpu.VMEM((rows, 128), jnp.float32)],
        compiler_params=_params(("arbitrary", "arbitrary", "arbitrary")),
        name="nsa_selected",
    )(proj, ks_t, proj, selb, jnp.asarray(ex, CDT))


def _nsa_win_kernel(q_ref, kt_ref, v_ref, o_ref, *, tq):
    k = pl.program_id(1)
    i = pl.program_id(2)
    s0 = i * tq
    span = WIN + tq
    start = pl.multiple_of(jnp.maximum(s0 - WIN, 0), tq)
    q4 = _stack_heads(q_ref[...])
    s = _dot(q4, kt_ref[0, 0, :, pl.ds(start, span)])
    tq_col = s0 + lax.broadcasted_iota(jnp.int32, (tq, 1), 0)
    t4 = jnp.concatenate([tq_col] * NSA_GROUP, axis=0)
    kpos = start + lax.broadcasted_iota(jnp.int32, (1, span), 1)
    mask = (kpos <= t4) & (kpos > t4 - WIN)
    s = jnp.where(mask, s, NEG)
    m = jnp.max(s, axis=-1, keepdims=True)
    p = jnp.where(mask, jnp.exp(s - m), 0.0)
    l = jnp.maximum(jnp.sum(p, axis=-1, keepdims=True), 1e-30)
    o = _dot(p.astype(CDT), v_ref[pl.ds(start, span), :]) / l
    o = jnp.where(k == 0, o[:, :HEAD_DIM], o[:, HEAD_DIM:])
    o_ref[...] = _unstack_heads(o, tq).astype(o_ref.dtype)


def _nsa_window(proj, kw_t, batch, seq):
    tq = 256
    nq = seq // tq
    qb = COL_QROT // (NSA_GROUP * HEAD_DIM)
    vb = COL_VW // 128
    return pl.pallas_call(
        functools.partial(_nsa_win_kernel, tq=tq),
        out_shape=jax.ShapeDtypeStruct((batch * seq, NSA_Q), CDT),
        grid=(batch, NSA_KV_HEADS, nq),
        in_specs=[pl.BlockSpec((tq, NSA_GROUP * HEAD_DIM), lambda b, k, i: (b * nq + i, qb + k)),
                  pl.BlockSpec((1, 1, HEAD_DIM, seq), lambda b, k, i: (b, k, 0, 0)),
                  pl.BlockSpec((seq, 128), lambda b, k, i: (b, vb))],
        out_specs=pl.BlockSpec((tq, NSA_GROUP * HEAD_DIM), lambda b, k, i: (b * nq + i, k)),
        compiler_params=_params(("arbitrary", "arbitrary", "arbitrary")),
        name="nsa_window",
    )(proj, kw_t, proj)


def _moba_mean_kernel(k_ref, o_ref):
    seq, w = k_ref.shape
    nbm = seq // MOBA_BLOCK
    k = k_ref[...].astype(jnp.float32).reshape(nbm, MOBA_BLOCK, w)
    o_ref[0] = jnp.sum(k, axis=1) * (1.0 / MOBA_BLOCK)


def _moba_kmean(proj, batch, seq):
    nbm = seq // MOBA_BLOCK
    kb = COL_MK // 128
    return pl.pallas_call(
        _moba_mean_kernel,
        out_shape=jax.ShapeDtypeStruct((batch, nbm, MOBA_W), jnp.float32),
        grid=(batch, MOBA_W // 128),
        in_specs=[pl.BlockSpec((seq, 128), lambda b, j: (b, kb + j))],
        out_specs=pl.BlockSpec((1, nbm, 128), lambda b, j: (b, 0, j)),
        compiler_params=_params(("arbitrary", "arbitrary")),
        name="moba_kmean",
    )(proj)


def _moba_kernel(q_ref, ka_ref, v_ref, km_ref, o_ref, m_ref, l_ref, acc_ref, *, tq, n_top):
    i = pl.program_id(2)
    s0 = i * tq
    cur = s0 // MOBA_BLOCK
    nbm = km_ref.shape[1]
    blk = lax.broadcasted_iota(jnp.int32, (1, nbm), 1)
    tq_col = s0 + lax.broadcasted_iota(jnp.int32, (tq, 1), 0)
    outs = []
    for hh in range(2):
        q = q_ref[:, hh * HEAD_DIM:(hh + 1) * HEAD_DIM]
        km_hi, km_lo = _split_hi_lo(km_ref[0][:, hh * HEAD_DIM:(hh + 1) * HEAD_DIM])
        gs = _dot_nt(q, km_hi) + _dot_nt(q, km_lo)
        gs = jnp.where(blk < cur, gs, NEG)
        sel = _topk_mask(gs, n_top)
        open_blk = ((sel > 0.5) & (gs > NEG * 0.5)) | (blk == cur)
        bias = jnp.where(open_blk, 0.0, NEG).astype(CDT)
        q_aug = jnp.concatenate([q, bias], axis=1)
        m_ref[...] = jnp.full(m_ref.shape, NEG, jnp.float32)
        l_ref[...] = jnp.zeros(l_ref.shape, jnp.float32)
        acc_ref[...] = jnp.zeros(acc_ref.shape, jnp.float32)

        def body(n, carry, q_aug=q_aug, hh=hh):
            start = pl.multiple_of(n * MOBA_BLOCK, MOBA_BLOCK)
            s = _dot(q_aug, ka_ref[0, hh, :, pl.ds(start, MOBA_BLOCK)])
            _flash_update(s, v_ref[pl.ds(start, MOBA_BLOCK), :], m_ref, l_ref, acc_ref)
            return carry

        lax.fori_loop(0, cur, body, 0)
        start = pl.multiple_of(cur * MOBA_BLOCK, MOBA_BLOCK)
        s = _dot(q_aug, ka_ref[0, hh, :, pl.ds(start, MOBA_BLOCK)])
        kpos = start + lax.broadcasted_iota(jnp.int32, (1, MOBA_BLOCK), 1)
        s = jnp.where(kpos <= tq_col, s, NEG)
        _flash_update(s, v_ref[pl.ds(start, MOBA_BLOCK), :], m_ref, l_ref, acc_ref)
        o = acc_ref[...] / l_ref[...]
        outs.append(o[:, hh * HEAD_DIM:(hh + 1) * HEAD_DIM])
    o_ref[...] = jnp.concatenate(outs, axis=1).astype(o_ref.dtype)


def _moba(proj, mk_aug_t, kmean, batch, seq):
    tq = MOBA_BLOCK
    nq = seq // tq
    nbm = seq // MOBA_BLOCK
    n_top = min(MOBA_TOPK, nbm)
    qb = COL_MQ // 128
    vb = COL_MV // 128
    aug = HEAD_DIM + nbm
    return pl.pallas_call(
        functools.partial(_moba_kernel, tq=tq, n_top=n_top),
        out_shape=jax.ShapeDtypeStruct((batch * seq, MOBA_W), CDT),
        grid=(batch, MOBA_HEADS // 2, nq),
        in_specs=[pl.BlockSpec((tq, 128), lambda b, p, i: (b * nq + i, qb + p)),
                  pl.BlockSpec((1, 2, aug, seq), lambda b, p, i: (b, p, 0, 0)),
                  pl.BlockSpec((seq, 128), lambda b, p, i: (b, vb + p)),
                  pl.BlockSpec((1, nbm, 128), lambda b, p, i: (b, 0, p))],
        out_specs=pl.BlockSpec((tq, 128), lambda b, p, i: (b * nq + i, p)),
        scratch_shapes=[pltpu.VMEM((tq, 1), jnp.float32), pltpu.VMEM((tq, 1), jnp.float32),
                        pltpu.VMEM((tq, 128), jnp.float32)],
        compiler_params=_params(("arbitrary", "arbitrary", "arbitrary")),
        name="moba",
    )(proj, mk_aug_t, proj, kmean)


def _merge_kernel(oc_ref, os_ref, ow_ref, om_ref, ng_ref, gn_ref, gm_ref, x_ref, ga_ref,
                  ex_ref, wun_ref, wum_ref, wo_ref, o_ref):
    gates = _sigmoid(ng_ref[...].astype(jnp.float32))
    hi, lo = _split_hi_lo(gates)
    e = _dot(hi, ex_ref[...]) + _dot(lo, ex_ref[...])
    o_nsa = (e[:, :NSA_Q] * oc_ref[...].astype(jnp.float32)
             + e[:, NSA_Q:2 * NSA_Q] * os_ref[...].astype(jnp.float32)
             + e[:, 2 * NSA_Q:] * ow_ref[...].astype(jnp.float32))
    y = (_sigmoid(gn_ref[...].astype(jnp.float32)) * _dot(o_nsa.astype(CDT), wun_ref[...])
         + _sigmoid(gm_ref[...].astype(jnp.float32)) * _dot(om_ref[...], wum_ref[...]))
    o_ref[...] = x_ref[...] + ga_ref[0] * _dot(y.astype(CDT), wo_ref[...])


def _merge(o_c, o_s, o_w, o_m, proj, x2d, ga, w_up_nsa, w_up_moba, w_out, seq):
    n, d = x2d.shape
    tm = min(512, seq)
    per_seq = seq // tm
    ng_w = COL_GN - COL_NG
    ex = np.zeros((ng_w, 3 * NSA_Q), np.float32)
    for h in range(NSA_HEADS):
        for j in range(3):
            ex[h * 3 + j, j * NSA_Q + h * HEAD_DIM: j * NSA_Q + (h + 1) * HEAD_DIM] = 1.0
    row = lambda i: (i, 0)
    const = lambda i: (0, 0)
    return pl.pallas_call(
        _merge_kernel,
        out_shape=jax.ShapeDtypeStruct((n, d), jnp.float32),
        grid=(n // tm,),
        in_specs=[pl.BlockSpec((tm, NSA_Q), row), pl.BlockSpec((tm, NSA_Q), row),
                  pl.BlockSpec((tm, NSA_Q), row), pl.BlockSpec((tm, MOBA_W), row),
                  pl.BlockSpec((tm, ng_w), lambda i: (i, COL_NG // ng_w)),
                  pl.BlockSpec((tm, d), lambda i: (i, COL_GN // d)),
                  pl.BlockSpec((tm, d), lambda i: (i, COL_GM // d)),
                  pl.BlockSpec((tm, d), row),
                  pl.BlockSpec((1, 1, d), lambda i: (i // per_seq, 0, 0)),
                  pl.BlockSpec((ng_w, 3 * NSA_Q), const),
                  pl.BlockSpec((NSA_Q, d), const), pl.BlockSpec((MOBA_W, d), const),
                  pl.BlockSpec((d, d), const)],
        out_specs=pl.BlockSpec((tm, d), row),
        compiler_params=_params(("arbitrary",)),
        name="mixer_merge",
    )(o_c, o_s, o_w, o_m, proj, proj, proj, x2d, ga, jnp.asarray(ex, CDT),
      w_up_nsa.astype(CDT), w_up_moba.astype(CDT), w_out.astype(CDT))


def _sorted_topk_rows(x, k):
    rows = x.shape[0]
    ridx = lax.broadcasted_iota(jnp.int32, x.shape, 0).astype(jnp.float32)
    vals = []
    for _ in range(k):
        m = jnp.max(x, axis=0, keepdims=True)
        idx = jnp.min(jnp.where(x == m, ridx, float(rows)), axis=0, keepdims=True)
        x = jnp.where(ridx == idx, LOWEST, x)
        vals.append(m)
    return jnp.concatenate(vals, axis=0)


def _peer_score_kernel(q_ref, k1_ref, k2_ref, s1_ref, s2_ref, e1_ref, e2_ref, tau_ref):
    half = PEER_QDIM // 2
    q = q_ref[...]
    s1 = _dot_nt(k1_ref[...], q[:, :half])
    s2 = _dot_nt(k2_ref[...], q[:, half:])
    s1 = s1 - jnp.max(s1, axis=0, keepdims=True)
    s2 = s2 - jnp.max(s2, axis=0, keepdims=True)
    v1 = _sorted_topk_rows(s1, PEER_TOPK)
    v2 = _sorted_topk_rows(s2, PEER_TOPK)
    cand = jnp.concatenate([v1[r:r + 1] + v2 for r in range(PEER_TOPK)], axis=0)
    vals = _sorted_topk_rows(cand, PEER_TOPK)
    z = jnp.sum(jnp.exp(vals - vals[0:1]), axis=0, keepdims=True)
    s1_ref[0] = s1
    s2_ref[0] = s2
    e1_ref[0] = jnp.exp(s1 - vals[0:1]) / z
    e2_ref[0] = jnp.exp(s2)
    tau_ref[0] = vals[PEER_TOPK - 1:PEER_TOPK]


def _peer_scores(qp, k1, k2):
    n = qp.shape[0]
    tt = 256
    shp = jax.ShapeDtypeStruct((PEER_HEADS, PEER_NKEYS, n), jnp.float32)
    big = pl.BlockSpec((1, PEER_NKEYS, tt), lambda i, h: (h, 0, i))
    return pl.pallas_call(
        _peer_score_kernel,
        out_shape=[shp, shp, shp, shp, jax.ShapeDtypeStruct((PEER_HEADS, 1, n), jnp.float32)],
        grid=(n // tt, PEER_HEADS),
        in_specs=[pl.BlockSpec((tt, PEER_QDIM), lambda i, h: (i, h)),
                  pl.BlockSpec((PEER_NKEYS, PEER_QDIM // 2), lambda i, h: (0, 0)),
                  pl.BlockSpec((PEER_NKEYS, PEER_QDIM // 2), lambda i, h: (0, 0))],
        out_specs=[big, big, big, big, pl.BlockSpec((1, 1, tt), lambda i, h: (h, 0, i))],
        compiler_params=_params(("arbitrary", "arbitrary")),
        name="peer_scores",
    )(qp, k1.astype(CDT), k2.astype(CDT))


def _peer_expert_kernel(h_ref, u_ref, vt_ref, s1_ref, s2_ref, e1_ref, e2_ref, tau_ref, x_ref, ga_ref,
                        o_ref, acc_ref, *, eb):
    j = pl.program_id(1)

    @pl.when(j == 0)
    def _():
        acc_ref[...] = jnp.zeros(acc_ref.shape, jnp.float32)

    act = _gelu(_dot_nt(u_ref[...], h_ref[...]))
    parts = []
    for al in range(eb // PEER_NKEYS):
        a = j * (eb // PEER_NKEYS) + al
        w = None
        for hd in range(PEER_HEADS):
            s1row = s1_ref[hd, pl.ds(a, 1), :]
            e1row = e1_ref[hd, pl.ds(a, 1), :]
            ssum = s1row + s2_ref[hd]
            contrib = jnp.where(ssum >= tau_ref[hd], e1row * e2_ref[hd], 0.0)
            w = contrib if w is None else w + contrib
        parts.append(w)
    pw = (jnp.concatenate(parts, axis=0) * act).astype(CDT)
    acc_ref[...] += _dot(vt_ref[...], pw)

    @pl.when(j == pl.num_programs(1) - 1)
    def _():
        o_ref[...] = x_ref[...] + ga_ref[0] * acc_ref[...].T


def _peer_experts(h2, u, v_t, s1, s2, e1, e2, tau, x2d, ga, seq):
    n, d = x2d.shape
    tt = min(512, seq)
    per_seq = seq // tt
    eb = 512
    n_exp = u.shape[0]
    big = pl.BlockSpec((PEER_HEADS, PEER_NKEYS, tt), lambda i, j: (0, 0, i))
    return pl.pallas_call(
        functools.partial(_peer_expert_kernel, eb=eb),
        out_shape=jax.ShapeDtypeStruct((n, d), jnp.float32),
        grid=(n // tt, n_exp // eb),
        in_specs=[pl.BlockSpec((tt, d), lambda i, j: (i, 0)),
                  pl.BlockSpec((eb, d), lambda i, j: (j, 0)),
                  pl.BlockSpec((d, eb), lambda i, j: (0, j)),
                  big, big, big, big,
                  pl.BlockSpec((PEER_HEADS, 1, tt), lambda i, j: (0, 0, i)),
                  pl.BlockSpec((tt, d), lambda i, j: (i, 0)),
                  pl.BlockSpec((1, 1, d), lambda i, j: (i // per_seq, 0, 0))],
        out_specs=pl.BlockSpec((tt, d), lambda i, j: (i, 0)),
        scratch_shapes=[pltpu.VMEM((d, tt), jnp.float32)],
        compiler_params=_params(("arbitrary", "arbitrary")),
        name="peer_experts",
    )(h2, u, v_t, s1, s2, e1, e2, tau, x2d, ga)


def _rms_kernel(x_ref, g_ref, o_ref):
    x = x_ref[...]
    ms = jnp.mean(x * x, axis=-1, keepdims=True)
    o_ref[...] = x * lax.rsqrt(ms + RMS_EPS) * g_ref[...]


def _final_norm(x2d, g):
    n, d = x2d.shape
    tm = 512
    return pl.pallas_call(
        _rms_kernel,
        out_shape=jax.ShapeDtypeStruct((n, d), jnp.float32),
        grid=(n // tm,),
        in_specs=[pl.BlockSpec((tm, d), lambda i: (i, 0)), pl.BlockSpec((1, d), lambda i: (0, 0))],
        out_specs=pl.BlockSpec((tm, d), lambda i: (i, 0)),
        compiler_params=_params(("arbitrary",)),
        name="final_rmsnorm",
    )(x2d, g.reshape(1, d))


def _key_major(proj, col, heads, batch, seq):
    k = proj[:, col:col + heads * HEAD_DIM].reshape(batch, seq, heads, HEAD_DIM)
    return jnp.transpose(k, (0, 2, 3, 1))


def kernel(x, c, w_ada, b_ada, g_attn, g_ffn, w_in, cmp_pe, cmp_w1, cmp_w2, w_up_nsa, w_up_moba, w_out,
           peer_wq, peer_k1, peer_k2, peer_u, peer_v, g_final):
    batch, seq, d = x.shape
    depth = w_ada.shape[0]
    n = batch * seq
    nbm = seq // MOBA_BLOCK
    x2d = x.reshape(n, d)
    mod = _adaln_mod(c, w_ada, b_ada)
    rope = _rope_tables(seq)
    blk_onehot = (jnp.arange(nbm)[:, None] == jnp.arange(seq)[None, :] // MOBA_BLOCK).astype(CDT)
    for l in range(depth):
        sh1, sc1, ga1, sh2, sc2, ga2 = [m.reshape(batch, 1, d) for m in jnp.split(mod[l], 6, axis=-1)]
        proj = _norm_mod_matmul(x2d, g_attn[l], sc1, sh1, _reorder_w_in(w_in[l]), seq,
                                rope=rope, n_rope=ROPE_COLS // PROJ_TN)
        cmp_kv = _nsa_compress(proj, cmp_pe[l], cmp_w1[l], cmp_w2[l], batch, seq)
        o_c, selb = _nsa_compressed(proj, cmp_kv, batch, seq)
        o_s = _nsa_selected(proj, _key_major(proj, COL_KS, NSA_KV_HEADS, batch, seq), selb, batch, seq)
        o_w = _nsa_window(proj, _key_major(proj, COL_KW, NSA_KV_HEADS, batch, seq), batch, seq)
        mk_t = _key_major(proj, COL_MK, MOBA_HEADS, batch, seq)
        mk_aug = jnp.concatenate(
            [mk_t, jnp.broadcast_to(blk_onehot, (batch, MOBA_HEADS, nbm, seq))], axis=2)
        o_m = _moba(proj, mk_aug, _moba_kmean(proj, batch, seq), batch, seq)
        x2d = _merge(o_c, o_s, o_w, o_m, proj, x2d, ga1, w_up_nsa[l], w_up_moba[l], w_out[l], seq)
        qp, h2 = _norm_mod_matmul(x2d, g_ffn[l], sc2, sh2, peer_wq[l].astype(CDT), seq, emit_h=True)
        s1, s2, e1, e2, tau = _peer_scores(qp, peer_k1[l], peer_k2[l])
        x2d = _peer_experts(h2, peer_u[l].astype(CDT), peer_v[l].T.astype(CDT), s1, s2, e1, e2, tau,
                            x2d, ga2, seq)
    return _final_norm(x2d, g_final).reshape(batch, seq, d)
```

```python
import functools

import jax
import jax.numpy as jnp
import numpy as np
from jax import lax
from jax.experimental import pallas as pl
from jax.experimental.pallas import tpu as pltpu

D_MODEL = 1024
HEAD_DIM = 64
ROT_DIM = HEAD_DIM // 4
ROPE_THETA = 500000.0
NSA_HEADS = 8
NSA_KV_HEADS = 2
NSA_GROUP = NSA_HEADS // NSA_KV_HEADS
CMP_LEN = 32
CMP_STRIDE = 16
CMP_HID = 2 * HEAD_DIM
SLC_LEN = 64
SLC_TOPN = 16
WIN = 512
MOBA_HEADS = 8
MOBA_BLOCK = 256
MOBA_TOPK = 3
PEER_HEADS = 8
PEER_NKEYS = 128
PEER_QDIM = 256
PEER_TOPK = 16
RMS_EPS = 1e-6
NEG = -1e30
SEL_FORCE = 1e4
LOWEST = -3.0e38
LOG2E = 1.4426950408889634

NSA_Q = NSA_HEADS * HEAD_DIM
NSA_KV = NSA_KV_HEADS * HEAD_DIM
MOBA_W = MOBA_HEADS * HEAD_DIM

CDT = jnp.bfloat16
V7X_VMEM_LIMIT = 56 * 1024 * 1024

COL_QROT = 0
COL_KS = 512
COL_KW = 640
COL_MQ = 768
COL_MK = 1280
ROPE_COLS = 1792
COL_NG = 1792
COL_GN = 2048
COL_GM = 3072
COL_QRAW = 4096
COL_KC = 4608
COL_VC = 4736
COL_VS = 4864
COL_VW = 4992
COL_MV = 5120
PROJ_COLS = 5632
PROJ_TN = 256


def _params(sem):
    return pltpu.CompilerParams(dimension_semantics=sem, vmem_limit_bytes=V7X_VMEM_LIMIT)


def _dot(a, b):
    return jnp.dot(a, b, preferred_element_type=jnp.float32)


def _dot_nt(a, b):
    return lax.dot_general(a, b, (((1,), (1,)), ((), ())), preferred_element_type=jnp.float32)


def _split_hi_lo(x):
    hi = x.astype(CDT)
    lo = (x - hi.astype(jnp.float32)).astype(CDT)
    return hi, lo


def _gelu(x):
    return 0.5 * x * (1.0 + jnp.tanh(0.7978845608028654 * (x + 0.044715 * (x * x * x))))


def _sigmoid(x):
    return 1.0 / (1.0 + jnp.exp(-x))


def _mod_kernel(c_ref, w_ref, b_ref, o_ref):
    c = c_ref[...]
    sc = c * _sigmoid(c)
    o_ref[0] = jnp.dot(sc, w_ref[0], preferred_element_type=jnp.float32,
                       precision=lax.Precision.HIGHEST) + b_ref[0]


def _adaln_mod(c, w_ada, b_ada):
    depth, d, six_d = w_ada.shape
    b = c.shape[0]
    rows = 8
    c_pad = jnp.zeros((rows, d), jnp.float32).at[:b].set(c)
    tn = 1024
    out = pl.pallas_call(
        _mod_kernel,
        out_shape=jax.ShapeDtypeStruct((depth, rows, six_d), jnp.float32),
        grid=(depth, six_d // tn),
        in_specs=[pl.BlockSpec((rows, d), lambda l, j: (0, 0)),
                  pl.BlockSpec((1, d, tn), lambda l, j: (l, 0, j)),
                  pl.BlockSpec((1, 1, tn), lambda l, j: (l, 0, j))],
        out_specs=pl.BlockSpec((1, rows, tn), lambda l, j: (l, 0, j)),
        compiler_params=_params(("arbitrary", "arbitrary")),
        name="adaln_mod",
    )(c_pad, w_ada, b_ada.reshape(depth, 1, six_d))
    return out[:, :b]


def _nmm_kernel(*refs, n_rope, emit_h):
    if n_rope:
        x_ref, g_ref, sc_ref, sh_ref, w_ref, cos_ref, sa_ref, sb_ref = refs[:8]
        rest = refs[8:]
    else:
        x_ref, g_ref, sc_ref, sh_ref, w_ref = refs[:5]
        rest = refs[5:]
    if emit_h:
        o_ref, ho_ref, h_ref = rest
    else:
        o_ref, h_ref = rest
    j = pl.program_id(1)

    @pl.when(j == 0)
    def _():
        x = x_ref[...]
        ms = jnp.mean(x * x, axis=-1, keepdims=True)
        y = x * lax.rsqrt(ms + RMS_EPS) * g_ref[...]
        h = (y * (1.0 + sc_ref[0]) + sh_ref[0]).astype(h_ref.dtype)
        h_ref[...] = h
        if emit_h:
            ho_ref[...] = h

    acc = _dot(h_ref[...], w_ref[...])
    if n_rope:
        @pl.when(j < n_rope)
        def _():
            cos, sa, sb = cos_ref[...], sa_ref[...], sb_ref[...]
            parts = []
            for c0 in range(0, acc.shape[1], 128):
                a = acc[:, c0:c0 + 128]
                parts.append(a * cos + pltpu.roll(a, 128 - ROT_DIM // 2, axis=1) * sa
                             + pltpu.roll(a, ROT_DIM // 2, axis=1) * sb)
            o_ref[...] = jnp.concatenate(parts, axis=1).astype(o_ref.dtype)

        @pl.when(j >= n_rope)
        def _():
            o_ref[...] = acc.astype(o_ref.dtype)
    else:
        o_ref[...] = acc.astype(o_ref.dtype)


def _norm_mod_matmul(x2d, g, sc, sh, w, seq, rope=None, n_rope=0, emit_h=False, tn=PROJ_TN):
    n, d = x2d.shape
    cols = w.shape[1]
    tm = min(1024, seq)
    per_seq = seq // tm
    in_specs = [pl.BlockSpec((tm, d), lambda i, j: (i, 0)),
                pl.BlockSpec((1, d), lambda i, j: (0, 0)),
                pl.BlockSpec((1, 1, d), lambda i, j: (i // per_seq, 0, 0)),
                pl.BlockSpec((1, 1, d), lambda i, j: (i // per_seq, 0, 0)),
                pl.BlockSpec((d, tn), lambda i, j: (0, j))]
    args = [x2d, g.reshape(1, d), sc, sh, w]
    if n_rope:
        in_specs += [pl.BlockSpec((tm, 128), lambda i, j: (i % per_seq, 0))] * 3
        args += list(rope)
    out_shape = [jax.ShapeDtypeStruct((n, cols), CDT)]
    out_specs = [pl.BlockSpec((tm, tn), lambda i, j: (i, j))]
    if emit_h:
        out_shape.append(jax.ShapeDtypeStruct((n, d), CDT))
        out_specs.append(pl.BlockSpec((tm, d), lambda i, j: (i, 0)))
    res = pl.pallas_call(
        functools.partial(_nmm_kernel, n_rope=n_rope, emit_h=emit_h),
        out_shape=out_shape,
        grid=(n // tm, cols // tn),
        in_specs=in_specs,
        out_specs=out_specs,
        scratch_shapes=[pltpu.VMEM((tm, d), CDT)],
        compiler_params=_params(("arbitrary", "arbitrary")),
        name="norm_mod_matmul",
    )(*args)
    return res if emit_h else res[0]


def _rope_tables(seq):
    half = ROT_DIM // 2
    inv = ROPE_THETA ** (-jnp.arange(half, dtype=jnp.float32) / half)
    ang = jnp.arange(seq, dtype=jnp.float32)[:, None] * inv[None, :]
    cos, sin = jnp.cos(ang), jnp.sin(ang)
    ones = jnp.ones((seq, HEAD_DIM - ROT_DIM), jnp.float32)
    zeros = jnp.zeros((seq, HEAD_DIM - ROT_DIM), jnp.float32)
    zh = jnp.zeros((seq, half), jnp.float32)
    c64 = jnp.concatenate([cos, cos, ones], axis=1)
    sa64 = jnp.concatenate([-sin, zh, zeros], axis=1)
    sb64 = jnp.concatenate([zh, sin, zeros], axis=1)
    return tuple(jnp.concatenate([t, t], axis=1) for t in (c64, sa64, sb64))


def _reorder_w_in(w_in):
    o = 0
    pieces = {}
    for name, size in (("nq", NSA_Q), ("kc", NSA_KV), ("vc", NSA_KV), ("ks", NSA_KV), ("vs", NSA_KV),
                       ("kw", NSA_KV), ("vw", NSA_KV), ("ng", 3 * NSA_HEADS), ("mq", MOBA_W),
                       ("mk", MOBA_W), ("mv", MOBA_W), ("gn", D_MODEL), ("gm", D_MODEL)):
        pieces[name] = w_in[:, o:o + size]
        o += size
    scale = HEAD_DIM ** -0.5 * LOG2E
    pad = jnp.zeros((w_in.shape[0], COL_GN - COL_NG - 3 * NSA_HEADS), w_in.dtype)
    w = jnp.concatenate([pieces["nq"] * scale, pieces["ks"], pieces["kw"], pieces["mq"] * scale,
                         pieces["mk"], pieces["ng"], pad, pieces["gn"], pieces["gm"],
                         pieces["nq"] * scale, pieces["kc"], pieces["vc"], pieces["vs"],
                         pieces["vw"], pieces["mv"]], axis=1)
    assert w.shape[1] == PROJ_COLS
    return w.astype(CDT)


def _compress_kernel(x_ref, w1c_ref, pe_ref, w1_ref, w2_ref, o_ref):
    nb = x_ref.shape[3]
    ab = _dot(x_ref[0, 0, 0], w1c_ref[0])
    a = ab[:, :CMP_HID]
    b_next = pltpu.roll(ab[:, CMP_HID:], nb - 1, axis=0)
    bias = _dot(pe_ref[0], w1_ref[0])[0:1]
    hid = _gelu(a + b_next + bias)
    o_ref[0, 0, 0] = _dot(hid.astype(CDT), w2_ref[0]).astype(o_ref.dtype)


def _nsa_compress(proj, cmp_pe, cmp_w1, cmp_w2, batch, seq):
    nb = seq // CMP_STRIDE
    half = CMP_STRIDE * HEAD_DIM
    kcvc = proj[:, COL_KC:COL_KC + 2 * NSA_KV].reshape(batch, nb, CMP_STRIDE, 2, NSA_KV_HEADS, HEAD_DIM)
    x = jnp.transpose(kcvc, (3, 0, 4, 1, 2, 5)).reshape(2, batch, NSA_KV_HEADS, nb, half)
    w1c = jnp.concatenate([cmp_w1[:, :half], cmp_w1[:, half:]], axis=2).astype(CDT)
    pe = jnp.broadcast_to(cmp_pe.reshape(2, 1, CMP_LEN * HEAD_DIM), (2, 8, CMP_LEN * HEAD_DIM)).astype(CDT)
    return pl.pallas_call(
        _compress_kernel,
        out_shape=jax.ShapeDtypeStruct((2, batch, NSA_KV_HEADS, nb, HEAD_DIM), CDT),
        grid=(2, batch, NSA_KV_HEADS),
        in_specs=[pl.BlockSpec((1, 1, 1, nb, half), lambda w, b, k: (w, b, k, 0, 0)),
                  pl.BlockSpec((1, half, 2 * CMP_HID), lambda w, b, k: (w, 0, 0)),
                  pl.BlockSpec((1, 8, 2 * half), lambda w, b, k: (w, 0, 0)),
                  pl.BlockSpec((1, 2 * half, CMP_HID), lambda w, b, k: (w, 0, 0)),
                  pl.BlockSpec((1, CMP_HID, HEAD_DIM), lambda w, b, k: (w, 0, 0))],
        out_specs=pl.BlockSpec((1, 1, 1, nb, HEAD_DIM), lambda w, b, k: (w, b, k, 0, 0)),
        compiler_params=_params(("arbitrary", "arbitrary", "arbitrary")),
        name="nsa_compress",
    )(x, w1c, pe, cmp_w1.astype(CDT), cmp_w2.astype(CDT))


def _stack_heads(qblk):
    return jnp.concatenate([qblk[:, g * HEAD_DIM:(g + 1) * HEAD_DIM] for g in range(NSA_GROUP)], axis=0)


def _unstack_heads(o, tq):
    return jnp.concatenate([o[g * tq:(g + 1) * tq] for g in range(NSA_GROUP)], axis=1)


def _topk_mask(score, n_sel):
    shape = score.shape
    lane = lax.broadcasted_iota(jnp.int32, shape, 1).astype(jnp.float32)
    width = float(shape[1])

    def body(_, carry):
        sc, sel = carry
        m = jnp.max(sc, axis=-1, keepdims=True)
        idx = jnp.min(jnp.where(sc == m, lane, width), axis=-1, keepdims=True)
        pick = lane == idx
        return jnp.where(pick, LOWEST, sc), jnp.where(pick, 1.0, sel)

    _, sel = lax.fori_loop(0, n_sel, body, (score, jnp.zeros(shape, jnp.float32)))
    return sel


def _nsa_cmp_kernel(q_ref, kc_ref, vc_ref, ov_ref, oc_ref, sb_ref, *, tq, n_sel):
    i = pl.program_id(2)
    s0 = i * tq
    nb = kc_ref.shape[3]
    ns = ov_ref.shape[1]
    q4 = _stack_heads(q_ref[...])
    s = _dot_nt(q4, kc_ref[0, 0, 0])
    tq_col = s0 + lax.broadcasted_iota(jnp.int32, (tq, 1), 0)
    t4 = jnp.concatenate([tq_col] * NSA_GROUP, axis=0)
    cend = lax.broadcasted_iota(jnp.int32, (1, nb), 1) * CMP_STRIDE + (CMP_LEN - 1)
    mask = cend <= t4
    s = jnp.where(mask, s, NEG)
    m = jnp.max(s, axis=-1, keepdims=True)
    p = jnp.where(mask, jnp.exp2(s - m), 0.0)
    p = p / jnp.maximum(jnp.sum(p, axis=-1, keepdims=True), 1e-30)
    o = _dot(p.astype(CDT), vc_ref[0, 0, 0])
    oc_ref[...] = _unstack_heads(o, tq).astype(oc_ref.dtype)

    psum = p[0:tq]
    for g in range(1, NSA_GROUP):
        psum = psum + p[g * tq:(g + 1) * tq]
    hi, lo = _split_hi_lo(psum)
    imp = _dot(hi, ov_ref[...]) + _dot(lo, ov_ref[...])

    blk = lax.broadcasted_iota(jnp.int32, (1, ns), 1)
    cur = tq_col // SLC_LEN
    valid = blk <= cur
    forced = (blk == 0) | (blk == cur) | (blk == cur - 1)
    score = jnp.where(valid, jnp.where(forced, SEL_FORCE, imp), NEG)
    sel = _topk_mask(score, n_sel)
    sb_ref[0, 0] = jnp.where((sel > 0.5) & valid, 0.0, NEG).astype(sb_ref.dtype)


def _nsa_compressed(proj, cmp_kv, batch, seq):
    tq = 128
    nq = seq // tq
    nb = seq // CMP_STRIDE
    ns = seq // SLC_LEN
    n_sel = min(SLC_TOPN, ns)
    c_start = np.arange(nb) * CMP_STRIDE
    s_start = np.arange(ns) * SLC_LEN
    ov = np.maximum(np.minimum(c_start[:, None] + CMP_LEN, s_start[None, :] + SLC_LEN)
                    - np.maximum(c_start[:, None], s_start[None, :]), 0).astype(np.float32) / CMP_LEN
    ov[nb - 1] = 0.0
    qb = COL_QRAW // (NSA_GROUP * HEAD_DIM)
    return pl.pallas_call(
        functools.partial(_nsa_cmp_kernel, tq=tq, n_sel=n_sel),
        out_shape=[jax.ShapeDtypeStruct((batch * seq, NSA_Q), CDT),
                   jax.ShapeDtypeStruct((batch, NSA_KV_HEADS, seq, ns), CDT)],
        grid=(batch, NSA_KV_HEADS, nq),
        in_specs=[pl.BlockSpec((tq, NSA_GROUP * HEAD_DIM), lambda b, k, i: (b * nq + i, qb + k)),
                  pl.BlockSpec((1, 1, 1, nb, HEAD_DIM), lambda b, k, i: (0, b, k, 0, 0)),
                  pl.BlockSpec((1, 1, 1, nb, HEAD_DIM), lambda b, k, i: (1, b, k, 0, 0)),
                  pl.BlockSpec((nb, ns), lambda b, k, i: (0, 0))],
        out_specs=[pl.BlockSpec((tq, NSA_GROUP * HEAD_DIM), lambda b, k, i: (b * nq + i, k)),
                   pl.BlockSpec((1, 1, tq, ns), lambda b, k, i: (b, k, i, 0))],
        compiler_params=_params(("arbitrary", "arbitrary", "arbitrary")),
        name="nsa_compressed",
    )(proj, cmp_kv, cmp_kv, jnp.asarray(ov, CDT))


def _flash_tile(s_ref, p_ref, m_ref, acc_ref, rc, tq, bias_ref=None, causal=None):
    rows, tk = s_ref.shape

    def tile(r0, c0):
        s = s_ref[r0:r0 + rc, c0:c0 + 128]
        if bias_ref is not None:
            s = s + bias_ref[r0 % tq:r0 % tq + rc, c0:c0 + 128]
        if causal is not None:
            kpos, t0 = causal
            t = t0 + (r0 % tq) + lax.broadcasted_iota(jnp.int32, (rc, 1), 0)
            s = jnp.where(kpos[:, c0:c0 + 128] <= t, s, NEG)
        return s

    for r0 in range(0, rows, rc):
        mx = tile(r0, 0)
        for c0 in range(128, tk, 128):
            mx = jnp.maximum(mx, tile(r0, c0))
        m_prev = m_ref[r0:r0 + rc, :]
        m_new = jnp.maximum(m_prev, jnp.max(mx, axis=-1, keepdims=True))
        acc_ref[r0:r0 + rc, :] = acc_ref[r0:r0 + rc, :] * jnp.exp2(m_prev - m_new)
        m_ref[r0:r0 + rc, :] = m_new
    for r0 in range(0, rows, rc):
        m_new = m_ref[r0:r0 + rc, :]
        for c0 in range(0, tk, 128):
            p_ref[r0:r0 + rc, c0:c0 + 128] = jnp.exp2(tile(r0, c0) - m_new).astype(p_ref.dtype)


def _nsa_sel_kernel(q_ref, kt_ref, va_ref, sb_ref, ex_ref, o_ref,
                    q4_ref, s_ref, p_ref, b_ref, m_ref, acc_ref, *, tq, tk, rc):
    i = pl.program_id(2)
    s0 = i * tq
    ns = sb_ref.shape[3]
    per_tile = tk // SLC_LEN
    q4_ref[...] = _stack_heads(q_ref[...])
    tile_of_blk = lax.broadcasted_iota(jnp.int32, (1, ns), 1) // per_tile
    m_ref[...] = jnp.full(m_ref.shape, NEG, jnp.float32)
    acc_ref[...] = jnp.zeros(acc_ref.shape, jnp.float32)

    def step(kt, masked):
        start = pl.multiple_of(kt * tk, tk)
        s_ref[...] = _dot(q4_ref[...], kt_ref[0, 0, :, pl.ds(start, tk)])
        sb = sb_ref[0, 0]
        b_ref[...] = _dot(jnp.where(tile_of_blk == kt, sb, jnp.zeros_like(sb)), ex_ref[...])
        causal = (start + lax.broadcasted_iota(jnp.int32, (1, tk), 1), s0) if masked else None
        _flash_tile(s_ref, p_ref, m_ref, acc_ref, rc, tq, bias_ref=b_ref, causal=causal)
        acc_ref[...] += _dot(p_ref[...], va_ref[0, 0, pl.ds(start, tk), :])

    def body(kt, carry):
        step(kt, False)
        return carry

    n_full = s0 // tk
    lax.fori_loop(0, n_full, body, 0)
    step(n_full, True)
    acc = acc_ref[...]
    o = acc[:, :HEAD_DIM] / acc[:, HEAD_DIM:]
    o_ref[...] = _unstack_heads(o, tq).astype(o_ref.dtype)


def _nsa_selected(proj, ks_t, vs_aug, selb, batch, seq):
    tq = 256
    tk = 512
    nq = seq // tq
    ns = seq // SLC_LEN
    per_tile = tk // SLC_LEN
    ex = (np.arange(ns)[:, None] % per_tile == np.arange(tk)[None, :] // SLC_LEN).astype(np.float32)
    qb = COL_QROT // (NSA_GROUP * HEAD_DIM)
    rows = NSA_GROUP * tq
    return pl.pallas_call(
        functools.partial(_nsa_sel_kernel, tq=tq, tk=tk, rc=64),
        out_shape=jax.ShapeDtypeStruct((batch * seq, NSA_Q), CDT),
        grid=(batch, NSA_KV_HEADS, nq),
        in_specs=[pl.BlockSpec((tq, NSA_GROUP * HEAD_DIM), lambda b, k, i: (b * nq + i, qb + k)),
                  pl.BlockSpec((1, 1, HEAD_DIM, seq), lambda b, k, i: (b, k, 0, 0)),
                  pl.BlockSpec((1, 1, seq, 128), lambda b, k, i: (b, k, 0, 0)),
                  pl.BlockSpec((1, 1, tq, ns), lambda b, k, i: (b, k, i, 0)),
                  pl.BlockSpec((ns, tk), lambda b, k, i: (0, 0))],
        out_specs=pl.BlockSpec((tq, NSA_GROUP * HEAD_DIM), lambda b, k, i: (b * nq + i, k)),
        scratch_shapes=[pltpu.VMEM((rows, HEAD_DIM), CDT), pltpu.VMEM((rows, tk), jnp.float32),
                        pltpu.VMEM((rows, tk), CDT), pltpu.VMEM((tq, tk), jnp.float32),
                        pltpu.VMEM((rows, 128), jnp.float32), pltpu.VMEM((rows, 128), jnp.float32)],
        compiler_params=_params(("arbitrary", "arbitrary", "arbitrary")),
        name="nsa_selected",
    )(proj, ks_t, vs_aug, selb, jnp.asarray(ex, CDT))


def _nsa_win_kernel(q_ref, kt_ref, v_ref, o_ref, *, tq):
    k = pl.program_id(1)
    i = pl.program_id(2)
    s0 = i * tq
    span = WIN + tq
    start = pl.multiple_of(jnp.maximum(s0 - WIN, 0), tq)
    q4 = _stack_heads(q_ref[...])
    s = _dot(q4, kt_ref[0, 0, :, pl.ds(start, span)])
    tq_col = s0 + lax.broadcasted_iota(jnp.int32, (tq, 1), 0)
    t4 = jnp.concatenate([tq_col] * NSA_GROUP, axis=0)
    kpos = start + lax.broadcasted_iota(jnp.int32, (1, span), 1)
    mask = (kpos <= t4) & (kpos > t4 - WIN)
    s = jnp.where(mask, s, NEG)
    m = jnp.max(s, axis=-1, keepdims=True)
    p = jnp.where(mask, jnp.exp2(s - m), 0.0)
    l = jnp.maximum(jnp.sum(p, axis=-1, keepdims=True), 1e-30)
    o = _dot(p.astype(CDT), v_ref[pl.ds(start, span), :]) / l
    o = jnp.where(k == 0, o[:, :HEAD_DIM], o[:, HEAD_DIM:])
    o_ref[...] = _unstack_heads(o, tq).astype(o_ref.dtype)


def _nsa_window(proj, kw_t, batch, seq):
    tq = 256
    nq = seq // tq
    qb = COL_QROT // (NSA_GROUP * HEAD_DIM)
    vb = COL_VW // 128
    return pl.pallas_call(
        functools.partial(_nsa_win_kernel, tq=tq),
        out_shape=jax.ShapeDtypeStruct((batch * seq, NSA_Q), CDT),
        grid=(batch, NSA_KV_HEADS, nq),
        in_specs=[pl.BlockSpec((tq, NSA_GROUP * HEAD_DIM), lambda b, k, i: (b * nq + i, qb + k)),
                  pl.BlockSpec((1, 1, HEAD_DIM, seq), lambda b, k, i: (b, k, 0, 0)),
                  pl.BlockSpec((seq, 128), lambda b, k, i: (b, vb))],
        out_specs=pl.BlockSpec((tq, NSA_GROUP * HEAD_DIM), lambda b, k, i: (b * nq + i, k)),
        compiler_params=_params(("arbitrary", "arbitrary", "arbitrary")),
        name="nsa_window",
    )(proj, kw_t, proj)


def _moba_mean_kernel(k_ref, o_ref):
    seq, w = k_ref.shape
    nbm = seq // MOBA_BLOCK
    k = k_ref[...].astype(jnp.float32).reshape(nbm, MOBA_BLOCK, w)
    o_ref[0] = jnp.sum(k, axis=1) * (1.0 / MOBA_BLOCK)


def _moba_kmean(proj, batch, seq):
    nbm = seq // MOBA_BLOCK
    kb = COL_MK // 128
    return pl.pallas_call(
        _moba_mean_kernel,
        out_shape=jax.ShapeDtypeStruct((batch, nbm, MOBA_W), jnp.float32),
        grid=(batch, MOBA_W // 128),
        in_specs=[pl.BlockSpec((seq, 128), lambda b, j: (b, kb + j))],
        out_specs=pl.BlockSpec((1, nbm, 128), lambda b, j: (b, 0, j)),
        compiler_params=_params(("arbitrary", "arbitrary")),
        name="moba_kmean",
    )(proj)


def _moba_kernel(q_ref, ka_ref, va_ref, km_ref, o_ref, qa_ref, s_ref, p_ref, m_ref, acc_ref,
                 *, tq, tk, n_top, rc):
    i = pl.program_id(2)
    s0 = i * tq
    nbm = km_ref.shape[1]
    blk = lax.broadcasted_iota(jnp.int32, (1, nbm), 1)
    cur = (s0 + lax.broadcasted_iota(jnp.int32, (tq, 1), 0)) // MOBA_BLOCK
    for hh in range(2):
        q = q_ref[:, hh * HEAD_DIM:(hh + 1) * HEAD_DIM]
        km_hi, km_lo = _split_hi_lo(km_ref[0][:, hh * HEAD_DIM:(hh + 1) * HEAD_DIM])
        gs = _dot_nt(q, km_hi) + _dot_nt(q, km_lo)
        gs = jnp.where(blk < cur, gs, NEG)
        sel = _topk_mask(gs, n_top)
        open_blk = ((sel > 0.5) & (gs > NEG * 0.5)) | (blk == cur)
        bias = jnp.where(open_blk, 0.0, NEG).astype(CDT)
        qa_ref[hh] = jnp.concatenate([q, bias], axis=1)
    m_ref[...] = jnp.full(m_ref.shape, NEG, jnp.float32)
    acc_ref[...] = jnp.zeros(acc_ref.shape, jnp.float32)

    def step(kt, masked):
        start = pl.multiple_of(kt * tk, tk)
        for hh in range(2):
            s_ref[hh] = _dot(qa_ref[hh], ka_ref[0, hh, :, pl.ds(start, tk)])
        causal = (start + lax.broadcasted_iota(jnp.int32, (1, tk), 1), s0) if masked else None
        for hh in range(2):
            _flash_tile(s_ref.at[hh], p_ref.at[hh], m_ref.at[hh], acc_ref.at[hh], rc, tq, causal=causal)
            acc_ref[hh] += _dot(p_ref[hh], va_ref[0, hh, pl.ds(start, tk), :])

    def body(kt, carry):
        step(kt, False)
        return carry

    n_full = s0 // tk
    lax.fori_loop(0, n_full, body, 0)
    for d in range(tq // tk):
        step(n_full + d, True)
    outs = []
    for hh in range(2):
        acc = acc_ref[hh]
        outs.append(acc[:, :HEAD_DIM] / acc[:, HEAD_DIM:])
    o_ref[...] = jnp.concatenate(outs, axis=1).astype(o_ref.dtype)


def _moba(proj, mk_aug_t, mv_aug, kmean, batch, seq):
    tq = min(1024, seq)
    tk = 512
    nq = seq // tq
    nbm = seq // MOBA_BLOCK
    n_top = min(MOBA_TOPK, nbm)
    qb = COL_MQ // 128
    aug = HEAD_DIM + nbm
    return pl.pallas_call(
        functools.partial(_moba_kernel, tq=tq, tk=tk, n_top=n_top, rc=64),
        out_shape=jax.ShapeDtypeStruct((batch * seq, MOBA_W), CDT),
        grid=(batch, MOBA_HEADS // 2, nq),
        in_specs=[pl.BlockSpec((tq, 128), lambda b, p, i: (b * nq + i, qb + p)),
                  pl.BlockSpec((1, 2, aug, seq), lambda b, p, i: (b, p, 0, 0)),
                  pl.BlockSpec((1, 2, seq, 128), lambda b, p, i: (b, p, 0, 0)),
                  pl.BlockSpec((1, nbm, 128), lambda b, p, i: (b, 0, p))],
        out_specs=pl.BlockSpec((tq, 128), lambda b, p, i: (b * nq + i, p)),
        scratch_shapes=[pltpu.VMEM((2, tq, aug), CDT), pltpu.VMEM((2, tq, tk), jnp.float32),
                        pltpu.VMEM((2, tq, tk), CDT), pltpu.VMEM((2, tq, 128), jnp.float32),
                        pltpu.VMEM((2, tq, 128), jnp.float32)],
        compiler_params=_params(("arbitrary", "arbitrary", "arbitrary")),
        name="moba",
    )(proj, mk_aug_t, mv_aug, kmean)


def _merge_kernel(oc_ref, os_ref, ow_ref, om_ref, ng_ref, gn_ref, gm_ref, x_ref, ga_ref,
                  ex_ref, wun_ref, wum_ref, wo_ref, o_ref):
    gates = _sigmoid(ng_ref[...].astype(jnp.float32))
    hi, lo = _split_hi_lo(gates)
    e = _dot(hi, ex_ref[...]) + _dot(lo, ex_ref[...])
    o_nsa = (e[:, :NSA_Q] * oc_ref[...].astype(jnp.float32)
             + e[:, NSA_Q:2 * NSA_Q] * os_ref[...].astype(jnp.float32)
             + e[:, 2 * NSA_Q:] * ow_ref[...].astype(jnp.float32))
    y = (_sigmoid(gn_ref[...].astype(jnp.float32)) * _dot(o_nsa.astype(CDT), wun_ref[...])
         + _sigmoid(gm_ref[...].astype(jnp.float32)) * _dot(om_ref[...], wum_ref[...]))
    o_ref[...] = x_ref[...] + ga_ref[0] * _dot(y.astype(CDT), wo_ref[...])


def _merge(o_c, o_s, o_w, o_m, proj, x2d, ga, w_up_nsa, w_up_moba, w_out, seq):
    n, d = x2d.shape
    tm = min(512, seq)
    per_seq = seq // tm
    ng_w = COL_GN - COL_NG
    ex = np.zeros((ng_w, 3 * NSA_Q), np.float32)
    for h in range(NSA_HEADS):
        for j in range(3):
            ex[h * 3 + j, j * NSA_Q + h * HEAD_DIM: j * NSA_Q + (h + 1) * HEAD_DIM] = 1.0
    row = lambda i: (i, 0)
    const = lambda i: (0, 0)
    return pl.pallas_call(
        _merge_kernel,
        out_shape=jax.ShapeDtypeStruct((n, d), jnp.float32),
        grid=(n // tm,),
        in_specs=[pl.BlockSpec((tm, NSA_Q), row), pl.BlockSpec((tm, NSA_Q), row),
                  pl.BlockSpec((tm, NSA_Q), row), pl.BlockSpec((tm, MOBA_W), row),
                  pl.BlockSpec((tm, ng_w), lambda i: (i, COL_NG // ng_w)),
                  pl.BlockSpec((tm, d), lambda i: (i, COL_GN // d)),
                  pl.BlockSpec((tm, d), lambda i: (i, COL_GM // d)),
                  pl.BlockSpec((tm, d), row),
                  pl.BlockSpec((1, 1, d), lambda i: (i // per_seq, 0, 0)),
                  pl.BlockSpec((ng_w, 3 * NSA_Q), const),
                  pl.BlockSpec((NSA_Q, d), const), pl.BlockSpec((MOBA_W, d), const),
                  pl.BlockSpec((d, d), const)],
        out_specs=pl.BlockSpec((tm, d), row),
        compiler_params=_params(("arbitrary",)),
        name="mixer_merge",
    )(o_c, o_s, o_w, o_m, proj, proj, proj, x2d, ga, jnp.asarray(ex, CDT),
      w_up_nsa.astype(CDT), w_up_moba.astype(CDT), w_out.astype(CDT))


def _sorted_topk_rows(x, k):
    rows = x.shape[0]
    ridx = lax.broadcasted_iota(jnp.int32, x.shape, 0).astype(jnp.float32)
    vals = []
    for _ in range(k):
        m = jnp.max(x, axis=0, keepdims=True)
        idx = jnp.min(jnp.where(x == m, ridx, float(rows)), axis=0, keepdims=True)
        x = jnp.where(ridx == idx, LOWEST, x)
        vals.append(m)
    return jnp.concatenate(vals, axis=0)


def _peer_score_kernel(q_ref, k1_ref, k2_ref, s1_ref, s2_ref, e1_ref, e2_ref, tau_ref):
    half = PEER_QDIM // 2
    q = q_ref[...]
    s1 = _dot_nt(k1_ref[...], q[:, :half])
    s2 = _dot_nt(k2_ref[...], q[:, half:])
    s1 = s1 - jnp.max(s1, axis=0, keepdims=True)
    s2 = s2 - jnp.max(s2, axis=0, keepdims=True)
    v1 = _sorted_topk_rows(s1, PEER_TOPK)
    v2 = _sorted_topk_rows(s2, PEER_TOPK)
    cand = jnp.concatenate([v1[r:r + 1] + v2 for r in range(PEER_TOPK)], axis=0)
    vals = _sorted_topk_rows(cand, PEER_TOPK)
    z = jnp.sum(jnp.exp(vals - vals[0:1]), axis=0, keepdims=True)
    s1_ref[0] = s1
    s2_ref[0] = s2
    e1_ref[0] = jnp.exp(s1 - vals[0:1]) / z
    e2_ref[0] = jnp.exp(s2)
    tau_ref[0] = vals[PEER_TOPK - 1:PEER_TOPK]


def _peer_scores(qp, k1, k2):
    n = qp.shape[0]
    tt = 256
    shp = jax.ShapeDtypeStruct((PEER_HEADS, PEER_NKEYS, n), jnp.float32)
    big = pl.BlockSpec((1, PEER_NKEYS, tt), lambda i, h: (h, 0, i))
    return pl.pallas_call(
        _peer_score_kernel,
        out_shape=[shp, shp, shp, shp, jax.ShapeDtypeStruct((PEER_HEADS, 1, n), jnp.float32)],
        grid=(n // tt, PEER_HEADS),
        in_specs=[pl.BlockSpec((tt, PEER_QDIM), lambda i, h: (i, h)),
                  pl.BlockSpec((PEER_NKEYS, PEER_QDIM // 2), lambda i, h: (0, 0)),
                  pl.BlockSpec((PEER_NKEYS, PEER_QDIM // 2), lambda i, h: (0, 0))],
        out_specs=[big, big, big, big, pl.BlockSpec((1, 1, tt), lambda i, h: (h, 0, i))],
        compiler_params=_params(("arbitrary", "arbitrary")),
        name="peer_scores",
    )(qp, k1.astype(CDT), k2.astype(CDT))


def _peer_expert_kernel(h_ref, u_ref, vt_ref, s1_ref, s2_ref, e1_ref, e2_ref, tau_ref, x_ref, ga_ref,
                        o_ref, acc_ref, *, eb):
    j = pl.program_id(1)

    @pl.when(j == 0)
    def _():
        acc_ref[...] = jnp.zeros(acc_ref.shape, jnp.float32)

    act = _gelu(_dot_nt(u_ref[...], h_ref[...]))
    parts = []
    for al in range(eb // PEER_NKEYS):
        a = j * (eb // PEER_NKEYS) + al
        w = None
        for hd in range(PEER_HEADS):
            s1row = s1_ref[hd, pl.ds(a, 1), :]
            e1row = e1_ref[hd, pl.ds(a, 1), :]
            ssum = s1row + s2_ref[hd]
            contrib = jnp.where(ssum >= tau_ref[hd], e1row * e2_ref[hd], 0.0)
            w = contrib if w is None else w + contrib
        parts.append(w)
    pw = (jnp.concatenate(parts, axis=0) * act).astype(CDT)
    acc_ref[...] += _dot(vt_ref[...], pw)

    @pl.when(j == pl.num_programs(1) - 1)
    def _():
        o_ref[...] = x_ref[...] + ga_ref[0] * acc_ref[...].T


def _peer_experts(h2, u, v_t, s1, s2, e1, e2, tau, x2d, ga, seq):
    n, d = x2d.shape
    tt = min(512, seq)
    per_seq = seq // tt
    eb = 512
    n_exp = u.shape[0]
    big = pl.BlockSpec((PEER_HEADS, PEER_NKEYS, tt), lambda i, j: (0, 0, i))
    return pl.pallas_call(
        functools.partial(_peer_expert_kernel, eb=eb),
        out_shape=jax.ShapeDtypeStruct((n, d), jnp.float32),
        grid=(n // tt, n_exp // eb),
        in_specs=[pl.BlockSpec((tt, d), lambda i, j: (i, 0)),
                  pl.BlockSpec((eb, d), lambda i, j: (j, 0)),
                  pl.BlockSpec((d, eb), lambda i, j: (0, j)),
                  big, big, big, big,
                  pl.BlockSpec((PEER_HEADS, 1, tt), lambda i, j: (0, 0, i)),
                  pl.BlockSpec((tt, d), lambda i, j: (i, 0)),
                  pl.BlockSpec((1, 1, d), lambda i, j: (i // per_seq, 0, 0))],
        out_specs=pl.BlockSpec((tt, d), lambda i, j: (i, 0)),
        scratch_shapes=[pltpu.VMEM((d, tt), jnp.float32)],
        compiler_params=_params(("arbitrary", "arbitrary")),
        name="peer_experts",
    )(h2, u, v_t, s1, s2, e1, e2, tau, x2d, ga)


def _rms_kernel(x_ref, g_ref, o_ref):
    x = x_ref[...]
    ms = jnp.mean(x * x, axis=-1, keepdims=True)
    o_ref[...] = x * lax.rsqrt(ms + RMS_EPS) * g_ref[...]


def _final_norm(x2d, g):
    n, d = x2d.shape
    tm = 512
    return pl.pallas_call(
        _rms_kernel,
        out_shape=jax.ShapeDtypeStruct((n, d), jnp.float32),
        grid=(n // tm,),
        in_specs=[pl.BlockSpec((tm, d), lambda i: (i, 0)), pl.BlockSpec((1, d), lambda i: (0, 0))],
        out_specs=pl.BlockSpec((tm, d), lambda i: (i, 0)),
        compiler_params=_params(("arbitrary",)),
        name="final_rmsnorm",
    )(x2d, g.reshape(1, d))


def _key_major(proj, col, heads, batch, seq):
    k = proj[:, col:col + heads * HEAD_DIM].reshape(batch, seq, heads, HEAD_DIM)
    return jnp.transpose(k, (0, 2, 3, 1))


def _value_with_ones(proj, col, heads, batch, seq):
    v = proj[:, col:col + heads * HEAD_DIM].reshape(batch, seq, heads, HEAD_DIM)
    v = jnp.transpose(v, (0, 2, 1, 3))
    return jnp.concatenate([v, jnp.ones_like(v)], axis=-1)


def kernel(x, c, w_ada, b_ada, g_attn, g_ffn, w_in, cmp_pe, cmp_w1, cmp_w2, w_up_nsa, w_up_moba, w_out,
           peer_wq, peer_k1, peer_k2, peer_u, peer_v, g_final):
    batch, seq, d = x.shape
    depth = w_ada.shape[0]
    n = batch * seq
    nbm = seq // MOBA_BLOCK
    x2d = x.reshape(n, d)
    mod = _adaln_mod(c, w_ada, b_ada)
    rope = _rope_tables(seq)
    blk_onehot = (jnp.arange(nbm)[:, None] == jnp.arange(seq)[None, :] // MOBA_BLOCK).astype(CDT)
    for l in range(depth):
        sh1, sc1, ga1, sh2, sc2, ga2 = [m.reshape(batch, 1, d) for m in jnp.split(mod[l], 6, axis=-1)]
        proj = _norm_mod_matmul(x2d, g_attn[l], sc1, sh1, _reorder_w_in(w_in[l]), seq,
                                rope=rope, n_rope=ROPE_COLS // PROJ_TN)
        cmp_kv = _nsa_compress(proj, cmp_pe[l], cmp_w1[l], cmp_w2[l], batch, seq)
        o_c, selb = _nsa_compressed(proj, cmp_kv, batch, seq)
        o_s = _nsa_selected(proj, _key_major(proj, COL_KS, NSA_KV_HEADS, batch, seq),
                            _value_with_ones(proj, COL_VS, NSA_KV_HEADS, batch, seq), selb, batch, seq)
        o_w = _nsa_window(proj, _key_major(proj, COL_KW, NSA_KV_HEADS, batch, seq), batch, seq)
        mk_t = _key_major(proj, COL_MK, MOBA_HEADS, batch, seq)
        mk_aug = jnp.concatenate(
            [mk_t, jnp.broadcast_to(blk_onehot, (batch, MOBA_HEADS, nbm, seq))], axis=2)
        o_m = _moba(proj, mk_aug, _value_with_ones(proj, COL_MV, MOBA_HEADS, batch, seq),
                    _moba_kmean(proj, batch, seq), batch, seq)
        x2d = _merge(o_c, o_s, o_w, o_m, proj, x2d, ga1, w_up_nsa[l], w_up_moba[l], w_out[l], seq)
        qp, h2 = _norm_mod_matmul(x2d, g_ffn[l], sc2, sh2, peer_wq[l].astype(CDT), seq, emit_h=True)
        s1, s2, e1, e2, tau = _peer_scores(qp, peer_k1[l], peer_k2[l])
        x2d = _peer_experts(h2, peer_u[l].astype(CDT), peer_v[l].T.astype(CDT), s1, s2, e1, e2, tau,
                            x2d, ga2, seq)
    return _final_norm(x2d, g_final).reshape(batch, seq, d)
```

```python
import functools

import jax
import jax.numpy as jnp
import numpy as np
from jax import lax
from jax.experimental import pallas as pl
from jax.experimental.pallas import tpu as pltpu

D_MODEL = 1024
HEAD_DIM = 64
ROT_DIM = HEAD_DIM // 4
ROPE_THETA = 500000.0
NSA_HEADS = 8
NSA_KV_HEADS = 2
NSA_GROUP = NSA_HEADS // NSA_KV_HEADS
CMP_LEN = 32
CMP_STRIDE = 16
CMP_HID = 2 * HEAD_DIM
SLC_LEN = 64
SLC_TOPN = 16
WIN = 512
MOBA_HEADS = 8
MOBA_BLOCK = 256
MOBA_TOPK = 3
PEER_HEADS = 8
PEER_NKEYS = 128
PEER_QDIM = 256
PEER_TOPK = 16
RMS_EPS = 1e-6
NEG = -1e30
SEL_FORCE = 1e4
LOWEST = -3.0e38
LOG2E = 1.4426950408889634

NSA_Q = NSA_HEADS * HEAD_DIM
NSA_KV = NSA_KV_HEADS * HEAD_DIM
MOBA_W = MOBA_HEADS * HEAD_DIM

CDT = jnp.bfloat16
V7X_VMEM_LIMIT = 56 * 1024 * 1024

COL_QROT = 0
COL_KS = 512
COL_KW = 640
COL_MQ = 768
COL_MK = 1280
ROPE_COLS = 1792
COL_NG = 1792
COL_GN = 2048
COL_GM = 3072
COL_QRAW = 4096
COL_KC = 4608
COL_VC = 4736
COL_VS = 4864
COL_VW = 4992
COL_MV = 5120
PROJ_COLS = 5632
PROJ_TN = 256


def _params(sem):
    return pltpu.CompilerParams(dimension_semantics=sem, vmem_limit_bytes=V7X_VMEM_LIMIT)


def _dot(a, b):
    return jnp.dot(a, b, preferred_element_type=jnp.float32)


def _dot_nt(a, b):
    return lax.dot_general(a, b, (((1,), (1,)), ((), ())), preferred_element_type=jnp.float32)


def _split_hi_lo(x):
    hi = x.astype(CDT)
    lo = (x - hi.astype(jnp.float32)).astype(CDT)
    return hi, lo


def _gelu(x):
    return 0.5 * x * (1.0 + jnp.tanh(0.7978845608028654 * (x + 0.044715 * (x * x * x))))


def _sigmoid(x):
    return 1.0 / (1.0 + jnp.exp(-x))


def _mod_kernel(c_ref, w_ref, b_ref, o_ref):
    c = c_ref[...]
    sc = c * _sigmoid(c)
    o_ref[0] = jnp.dot(sc, w_ref[0], preferred_element_type=jnp.float32,
                       precision=lax.Precision.HIGHEST) + b_ref[0]


def _adaln_mod(c, w_ada, b_ada):
    depth, d, six_d = w_ada.shape
    b = c.shape[0]
    rows = 8
    c_pad = jnp.zeros((rows, d), jnp.float32).at[:b].set(c)
    tn = 1024
    out = pl.pallas_call(
        _mod_kernel,
        out_shape=jax.ShapeDtypeStruct((depth, rows, six_d), jnp.float32),
        grid=(depth, six_d // tn),
        in_specs=[pl.BlockSpec((rows, d), lambda l, j: (0, 0)),
                  pl.BlockSpec((1, d, tn), lambda l, j: (l, 0, j)),
                  pl.BlockSpec((1, 1, tn), lambda l, j: (l, 0, j))],
        out_specs=pl.BlockSpec((1, rows, tn), lambda l, j: (l, 0, j)),
        compiler_params=_params(("arbitrary", "arbitrary")),
        name="adaln_mod",
    )(c_pad, w_ada, b_ada.reshape(depth, 1, six_d))
    return out[:, :b]


def _nmm_kernel(*refs, n_rope, emit_h):
    if n_rope:
        x_ref, g_ref, sc_ref, sh_ref, w_ref, cos_ref, sa_ref, sb_ref = refs[:8]
        rest = refs[8:]
    else:
        x_ref, g_ref, sc_ref, sh_ref, w_ref = refs[:5]
        rest = refs[5:]
    if emit_h:
        o_ref, ho_ref, h_ref = rest
    else:
        o_ref, h_ref = rest
    j = pl.program_id(1)

    @pl.when(j == 0)
    def _():
        x = x_ref[...]
        ms = jnp.mean(x * x, axis=-1, keepdims=True)
        y = x * lax.rsqrt(ms + RMS_EPS) * g_ref[...]
        h = (y * (1.0 + sc_ref[0]) + sh_ref[0]).astype(h_ref.dtype)
        h_ref[...] = h
        if emit_h:
            ho_ref[...] = h

    acc = _dot(h_ref[...], w_ref[...])
    if n_rope:
        @pl.when(j < n_rope)
        def _():
            cos, sa, sb = cos_ref[...], sa_ref[...], sb_ref[...]
            parts = []
            for c0 in range(0, acc.shape[1], 128):
                a = acc[:, c0:c0 + 128]
                parts.append(a * cos + pltpu.roll(a, 128 - ROT_DIM // 2, axis=1) * sa
                             + pltpu.roll(a, ROT_DIM // 2, axis=1) * sb)
            o_ref[...] = jnp.concatenate(parts, axis=1).astype(o_ref.dtype)

        @pl.when(j >= n_rope)
        def _():
            o_ref[...] = acc.astype(o_ref.dtype)
    else:
        o_ref[...] = acc.astype(o_ref.dtype)


def _norm_mod_matmul(x2d, g, sc, sh, w, seq, rope=None, n_rope=0, emit_h=False, tn=PROJ_TN):
    n, d = x2d.shape
    cols = w.shape[1]
    tm = min(1024, seq)
    per_seq = seq // tm
    in_specs = [pl.BlockSpec((tm, d), lambda i, j: (i, 0)),
                pl.BlockSpec((1, d), lambda i, j: (0, 0)),
                pl.BlockSpec((1, 1, d), lambda i, j: (i // per_seq, 0, 0)),
                pl.BlockSpec((1, 1, d), lambda i, j: (i // per_seq, 0, 0)),
                pl.BlockSpec((d, tn), lambda i, j: (0, j))]
    args = [x2d, g.reshape(1, d), sc, sh, w]
    if n_rope:
        in_specs += [pl.BlockSpec((tm, 128), lambda i, j: (i % per_seq, 0))] * 3
        args += list(rope)
    out_shape = [jax.ShapeDtypeStruct((n, cols), CDT)]
    out_specs = [pl.BlockSpec((tm, tn), lambda i, j: (i, j))]
    if emit_h:
        out_shape.append(jax.ShapeDtypeStruct((n, d), CDT))
        out_specs.append(pl.BlockSpec((tm, d), lambda i, j: (i, 0)))
    res = pl.pallas_call(
        functools.partial(_nmm_kernel, n_rope=n_rope, emit_h=emit_h),
        out_shape=out_shape,
        grid=(n // tm, cols // tn),
        in_specs=in_specs,
        out_specs=out_specs,
        scratch_shapes=[pltpu.VMEM((tm, d), CDT)],
        compiler_params=_params(("arbitrary", "arbitrary")),
        name="norm_mod_matmul",
    )(*args)
    return res if emit_h else res[0]


def _rope_tables(seq):
    half = ROT_DIM // 2
    inv = ROPE_THETA ** (-jnp.arange(half, dtype=jnp.float32) / half)
    ang = jnp.arange(seq, dtype=jnp.float32)[:, None] * inv[None, :]
    cos, sin = jnp.cos(ang), jnp.sin(ang)
    ones = jnp.ones((seq, HEAD_DIM - ROT_DIM), jnp.float32)
    zeros = jnp.zeros((seq, HEAD_DIM - ROT_DIM), jnp.float32)
    zh = jnp.zeros((seq, half), jnp.float32)
    c64 = jnp.concatenate([cos, cos, ones], axis=1)
    sa64 = jnp.concatenate([-sin, zh, zeros], axis=1)
    sb64 = jnp.concatenate([zh, sin, zeros], axis=1)
    return tuple(jnp.concatenate([t, t], axis=1) for t in (c64, sa64, sb64))


def _reorder_w_in(w_in):
    o = 0
    pieces = {}
    for name, size in (("nq", NSA_Q), ("kc", NSA_KV), ("vc", NSA_KV), ("ks", NSA_KV), ("vs", NSA_KV),
                       ("kw", NSA_KV), ("vw", NSA_KV), ("ng", 3 * NSA_HEADS), ("mq", MOBA_W),
                       ("mk", MOBA_W), ("mv", MOBA_W), ("gn", D_MODEL), ("gm", D_MODEL)):
        pieces[name] = w_in[:, o:o + size]
        o += size
    scale = HEAD_DIM ** -0.5 * LOG2E
    pad = jnp.zeros((w_in.shape[0], COL_GN - COL_NG - 3 * NSA_HEADS), w_in.dtype)
    w = jnp.concatenate([pieces["nq"] * scale, pieces["ks"], pieces["kw"], pieces["mq"] * scale,
                         pieces["mk"], pieces["ng"], pad, pieces["gn"], pieces["gm"],
                         pieces["nq"] * scale, pieces["kc"], pieces["vc"], pieces["vs"],
                         pieces["vw"], pieces["mv"]], axis=1)
    assert w.shape[1] == PROJ_COLS
    return w.astype(CDT)


def _compress_kernel(x_ref, w1c_ref, pe_ref, w1_ref, w2_ref, o_ref):
    nb = x_ref.shape[3]
    ab = _dot(x_ref[0, 0, 0], w1c_ref[0])
    a = ab[:, :CMP_HID]
    b_next = pltpu.roll(ab[:, CMP_HID:], nb - 1, axis=0)
    bias = _dot(pe_ref[0], w1_ref[0])[0:1]
    hid = _gelu(a + b_next + bias)
    o_ref[0, 0, 0] = _dot(hid.astype(CDT), w2_ref[0]).astype(o_ref.dtype)


def _nsa_compress(proj, cmp_pe, cmp_w1, cmp_w2, batch, seq):
    nb = seq // CMP_STRIDE
    half = CMP_STRIDE * HEAD_DIM
    kcvc = proj[:, COL_KC:COL_KC + 2 * NSA_KV].reshape(batch, nb, CMP_STRIDE, 2, NSA_KV_HEADS, HEAD_DIM)
    x = jnp.transpose(kcvc, (3, 0, 4, 1, 2, 5)).reshape(2, batch, NSA_KV_HEADS, nb, half)
    w1c = jnp.concatenate([cmp_w1[:, :half], cmp_w1[:, half:]], axis=2).astype(CDT)
    pe = jnp.broadcast_to(cmp_pe.reshape(2, 1, CMP_LEN * HEAD_DIM), (2, 8, CMP_LEN * HEAD_DIM)).astype(CDT)
    return pl.pallas_call(
        _compress_kernel,
        out_shape=jax.ShapeDtypeStruct((2, batch, NSA_KV_HEADS, nb, HEAD_DIM), CDT),
        grid=(2, batch, NSA_KV_HEADS),
        in_specs=[pl.BlockSpec((1, 1, 1, nb, half), lambda w, b, k: (w, b, k, 0, 0)),
                  pl.BlockSpec((1, half, 2 * CMP_HID), lambda w, b, k: (w, 0, 0)),
                  pl.BlockSpec((1, 8, 2 * half), lambda w, b, k: (w, 0, 0)),
                  pl.BlockSpec((1, 2 * half, CMP_HID), lambda w, b, k: (w, 0, 0)),
                  pl.BlockSpec((1, CMP_HID, HEAD_DIM), lambda w, b, k: (w, 0, 0))],
        out_specs=pl.BlockSpec((1, 1, 1, nb, HEAD_DIM), lambda w, b, k: (w, b, k, 0, 0)),
        compiler_params=_params(("arbitrary", "arbitrary", "arbitrary")),
        name="nsa_compress",
    )(x, w1c, pe, cmp_w1.astype(CDT), cmp_w2.astype(CDT))


def _stack_heads(qblk):
    return jnp.concatenate([qblk[:, g * HEAD_DIM:(g + 1) * HEAD_DIM] for g in range(NSA_GROUP)], axis=0)


def _unstack_heads(o, tq):
    return jnp.concatenate([o[g * tq:(g + 1) * tq] for g in range(NSA_GROUP)], axis=1)


def _topk_mask(score, n_sel):
    shape = score.shape
    lane = lax.broadcasted_iota(jnp.int32, shape, 1).astype(jnp.float32)
    width = float(shape[1])

    def body(_, carry):
        sc, sel = carry
        m = jnp.max(sc, axis=-1, keepdims=True)
        idx = jnp.min(jnp.where(sc == m, lane, width), axis=-1, keepdims=True)
        pick = lane == idx
        return jnp.where(pick, LOWEST, sc), jnp.where(pick, 1.0, sel)

    _, sel = lax.fori_loop(0, n_sel, body, (score, jnp.zeros(shape, jnp.float32)))
    return sel


def _nsa_cmp_kernel(q_ref, kc_ref, vc_ref, ov_ref, oc_ref, sb_ref, *, tq, n_sel):
    i = pl.program_id(2)
    s0 = i * tq
    nb = kc_ref.shape[3]
    ns = ov_ref.shape[1]
    q4 = _stack_heads(q_ref[...])
    s = _dot_nt(q4, kc_ref[0, 0, 0])
    tq_col = s0 + lax.broadcasted_iota(jnp.int32, (tq, 1), 0)
    t4 = jnp.concatenate([tq_col] * NSA_GROUP, axis=0)
    cend = lax.broadcasted_iota(jnp.int32, (1, nb), 1) * CMP_STRIDE + (CMP_LEN - 1)
    mask = cend <= t4
    s = jnp.where(mask, s, NEG)
    m = jnp.max(s, axis=-1, keepdims=True)
    p = jnp.where(mask, jnp.exp2(s - m), 0.0)
    p = p / jnp.maximum(jnp.sum(p, axis=-1, keepdims=True), 1e-30)
    o = _dot(p.astype(CDT), vc_ref[0, 0, 0])
    oc_ref[...] = _unstack_heads(o, tq).astype(oc_ref.dtype)

    psum = p[0:tq]
    for g in range(1, NSA_GROUP):
        psum = psum + p[g * tq:(g + 1) * tq]
    hi, lo = _split_hi_lo(psum)
    imp = _dot(hi, ov_ref[...]) + _dot(lo, ov_ref[...])

    blk = lax.broadcasted_iota(jnp.int32, (ns, 1), 0)
    cur = (s0 + lax.broadcasted_iota(jnp.int32, (1, tq), 1)) // SLC_LEN
    valid = blk <= cur
    forced = (blk == 0) | (blk == cur) | (blk == cur - 1)
    score = jnp.where(valid, jnp.where(forced, SEL_FORCE, imp.T), NEG)
    ridx = lax.broadcasted_iota(jnp.int32, (ns, tq), 0).astype(jnp.float32)

    def body(_, carry):
        sc, sel = carry
        _, pick = _extract_max(sc, ridx)
        return jnp.where(pick, LOWEST, sc), jnp.where(pick, 1.0, sel)

    _, sel = lax.fori_loop(0, n_sel, body, (score, jnp.zeros((ns, tq), jnp.float32)))
    sb_ref[0, 0] = jnp.where((sel > 0.5) & valid, 0.0, NEG).T.astype(sb_ref.dtype)


def _nsa_compressed(proj, cmp_kv, batch, seq):
    tq = 128
    nq = seq // tq
    nb = seq // CMP_STRIDE
    ns = seq // SLC_LEN
    n_sel = min(SLC_TOPN, ns)
    c_start = np.arange(nb) * CMP_STRIDE
    s_start = np.arange(ns) * SLC_LEN
    ov = np.maximum(np.minimum(c_start[:, None] + CMP_LEN, s_start[None, :] + SLC_LEN)
                    - np.maximum(c_start[:, None], s_start[None, :]), 0).astype(np.float32) / CMP_LEN
    ov[nb - 1] = 0.0
    qb = COL_QRAW // (NSA_GROUP * HEAD_DIM)
    return pl.pallas_call(
        functools.partial(_nsa_cmp_kernel, tq=tq, n_sel=n_sel),
        out_shape=[jax.ShapeDtypeStruct((batch * seq, NSA_Q), CDT),
                   jax.ShapeDtypeStruct((batch, NSA_KV_HEADS, seq, ns), CDT)],
        grid=(batch, NSA_KV_HEADS, nq),
        in_specs=[pl.BlockSpec((tq, NSA_GROUP * HEAD_DIM), lambda b, k, i: (b * nq + i, qb + k)),
                  pl.BlockSpec((1, 1, 1, nb, HEAD_DIM), lambda b, k, i: (0, b, k, 0, 0)),
                  pl.BlockSpec((1, 1, 1, nb, HEAD_DIM), lambda b, k, i: (1, b, k, 0, 0)),
                  pl.BlockSpec((nb, ns), lambda b, k, i: (0, 0))],
        out_specs=[pl.BlockSpec((tq, NSA_GROUP * HEAD_DIM), lambda b, k, i: (b * nq + i, k)),
                   pl.BlockSpec((1, 1, tq, ns), lambda b, k, i: (b, k, i, 0))],
        compiler_params=_params(("arbitrary", "arbitrary", "arbitrary")),
        name="nsa_compressed",
    )(proj, cmp_kv, cmp_kv, jnp.asarray(ov, CDT))


def _flash_tile(s_ref, p_ref, m_ref, acc_ref, rc, tq, bias_ref=None, causal=None):
    rows, tk = s_ref.shape

    def tile(r0, c0):
        s = s_ref[r0:r0 + rc, c0:c0 + 128]
        if bias_ref is not None:
            s = s + bias_ref[r0 % tq:r0 % tq + rc, c0:c0 + 128]
        if causal is not None:
            kpos, t0 = causal
            t = t0 + (r0 % tq) + lax.broadcasted_iota(jnp.int32, (rc, 1), 0)
            s = jnp.where(kpos[:, c0:c0 + 128] <= t, s, NEG)
        return s

    for r0 in range(0, rows, rc):
        mx = tile(r0, 0)
        for c0 in range(128, tk, 128):
            mx = jnp.maximum(mx, tile(r0, c0))
        m_prev = m_ref[r0:r0 + rc, :]
        m_new = jnp.maximum(m_prev, jnp.max(mx, axis=-1, keepdims=True))
        acc_ref[r0:r0 + rc, :] = acc_ref[r0:r0 + rc, :] * jnp.exp2(m_prev - m_new)
        m_ref[r0:r0 + rc, :] = m_new
    for r0 in range(0, rows, rc):
        m_new = m_ref[r0:r0 + rc, :]
        for c0 in range(0, tk, 128):
            p_ref[r0:r0 + rc, c0:c0 + 128] = jnp.exp2(tile(r0, c0) - m_new).astype(p_ref.dtype)


def _nsa_sel_kernel(q_ref, kt_ref, va_ref, sb_ref, ex_ref, o_ref,
                    q4_ref, s_ref, p_ref, b_ref, m_ref, acc_ref, *, tq, tk, rc):
    i = pl.program_id(2)
    s0 = i * tq
    ns = sb_ref.shape[3]
    per_tile = tk // SLC_LEN
    q4_ref[...] = _stack_heads(q_ref[...])
    tile_of_blk = lax.broadcasted_iota(jnp.int32, (1, ns), 1) // per_tile
    m_ref[...] = jnp.full(m_ref.shape, NEG, jnp.float32)
    acc_ref[...] = jnp.zeros(acc_ref.shape, jnp.float32)

    def step(kt, masked):
        start = pl.multiple_of(kt * tk, tk)
        s_ref[...] = _dot(q4_ref[...], kt_ref[0, 0, :, pl.ds(start, tk)])
        sb = sb_ref[0, 0]
        b_ref[...] = _dot(jnp.where(tile_of_blk == kt, sb, jnp.zeros_like(sb)), ex_ref[...])
        causal = (start + lax.broadcasted_iota(jnp.int32, (1, tk), 1), s0) if masked else None
        _flash_tile(s_ref, p_ref, m_ref, acc_ref, rc, tq, bias_ref=b_ref, causal=causal)
        acc_ref[...] += _dot(p_ref[...], va_ref[0, 0, pl.ds(start, tk), :])

    def body(kt, carry):
        step(kt, False)
        return carry

    n_full = s0 // tk
    lax.fori_loop(0, n_full, body, 0)
    step(n_full, True)
    acc = acc_ref[...]
    o = acc[:, :HEAD_DIM] / acc[:, HEAD_DIM:]
    o_ref[...] = _unstack_heads(o, tq).astype(o_ref.dtype)


def _nsa_selected(proj, ks_t, vs_aug, selb, batch, seq):
    tq = 256
    tk = 512
    nq = seq // tq
    ns = seq // SLC_LEN
    per_tile = tk // SLC_LEN
    ex = (np.arange(ns)[:, None] % per_tile == np.arange(tk)[None, :] // SLC_LEN).astype(np.float32)
    qb = COL_QROT // (NSA_GROUP * HEAD_DIM)
    rows = NSA_GROUP * tq
    return pl.pallas_call(
        functools.partial(_nsa_sel_kernel, tq=tq, tk=tk, rc=64),
        out_shape=jax.ShapeDtypeStruct((batch * seq, NSA_Q), CDT),
        grid=(batch, NSA_KV_HEADS, nq),
        in_specs=[pl.BlockSpec((tq, NSA_GROUP * HEAD_DIM), lambda b, k, i: (b * nq + i, qb + k)),
                  pl.BlockSpec((1, 1, HEAD_DIM, seq), lambda b, k, i: (b, k, 0, 0)),
                  pl.BlockSpec((1, 1, seq, 128), lambda b, k, i: (b, k, 0, 0)),
                  pl.BlockSpec((1, 1, tq, ns), lambda b, k, i: (b, k, i, 0)),
                  pl.BlockSpec((ns, tk), lambda b, k, i: (0, 0))],
        out_specs=pl.BlockSpec((tq, NSA_GROUP * HEAD_DIM), lambda b, k, i: (b * nq + i, k)),
        scratch_shapes=[pltpu.VMEM((rows, HEAD_DIM), CDT), pltpu.VMEM((rows, tk), jnp.float32),
                        pltpu.VMEM((rows, tk), CDT), pltpu.VMEM((tq, tk), jnp.float32),
                        pltpu.VMEM((rows, 128), jnp.float32), pltpu.VMEM((rows, 128), jnp.float32)],
        compiler_params=_params(("arbitrary", "arbitrary", "arbitrary")),
        name="nsa_selected",
    )(proj, ks_t, vs_aug, selb, jnp.asarray(ex, CDT))


def _nsa_win_kernel(q_ref, kt_ref, v_ref, o_ref, *, tq):
    k = pl.program_id(1)
    i = pl.program_id(2)
    s0 = i * tq
    span = WIN + tq
    start = pl.multiple_of(jnp.maximum(s0 - WIN, 0), tq)
    q4 = _stack_heads(q_ref[...])
    s = _dot(q4, kt_ref[0, 0, :, pl.ds(start, span)])
    tq_col = s0 + lax.broadcasted_iota(jnp.int32, (tq, 1), 0)
    t4 = jnp.concatenate([tq_col] * NSA_GROUP, axis=0)
    kpos = start + lax.broadcasted_iota(jnp.int32, (1, span), 1)
    mask = (kpos <= t4) & (kpos > t4 - WIN)
    s = jnp.where(mask, s, NEG)
    m = jnp.max(s, axis=-1, keepdims=True)
    p = jnp.where(mask, jnp.exp2(s - m), 0.0)
    l = jnp.maximum(jnp.sum(p, axis=-1, keepdims=True), 1e-30)
    o = _dot(p.astype(CDT), v_ref[pl.ds(start, span), :]) / l
    o = jnp.where(k == 0, o[:, :HEAD_DIM], o[:, HEAD_DIM:])
    o_ref[...] = _unstack_heads(o, tq).astype(o_ref.dtype)


def _nsa_window(proj, kw_t, batch, seq):
    tq = 256
    nq = seq // tq
    qb = COL_QROT // (NSA_GROUP * HEAD_DIM)
    vb = COL_VW // 128
    return pl.pallas_call(
        functools.partial(_nsa_win_kernel, tq=tq),
        out_shape=jax.ShapeDtypeStruct((batch * seq, NSA_Q), CDT),
        grid=(batch, NSA_KV_HEADS, nq),
        in_specs=[pl.BlockSpec((tq, NSA_GROUP * HEAD_DIM), lambda b, k, i: (b * nq + i, qb + k)),
                  pl.BlockSpec((1, 1, HEAD_DIM, seq), lambda b, k, i: (b, k, 0, 0)),
                  pl.BlockSpec((seq, 128), lambda b, k, i: (b, vb))],
        out_specs=pl.BlockSpec((tq, NSA_GROUP * HEAD_DIM), lambda b, k, i: (b * nq + i, k)),
        compiler_params=_params(("arbitrary", "arbitrary", "arbitrary")),
        name="nsa_window",
    )(proj, kw_t, proj)


def _moba_mean_kernel(k_ref, o_ref):
    seq, w = k_ref.shape
    nbm = seq // MOBA_BLOCK
    k = k_ref[...].astype(jnp.float32).reshape(nbm, MOBA_BLOCK, w)
    o_ref[0] = jnp.sum(k, axis=1) * (1.0 / MOBA_BLOCK)


def _moba_kmean(proj, batch, seq):
    nbm = seq // MOBA_BLOCK
    kb = COL_MK // 128
    return pl.pallas_call(
        _moba_mean_kernel,
        out_shape=jax.ShapeDtypeStruct((batch, nbm, MOBA_W), jnp.float32),
        grid=(batch, MOBA_W // 128),
        in_specs=[pl.BlockSpec((seq, 128), lambda b, j: (b, kb + j))],
        out_specs=pl.BlockSpec((1, nbm, 128), lambda b, j: (b, 0, j)),
        compiler_params=_params(("arbitrary", "arbitrary")),
        name="moba_kmean",
    )(proj)


def _moba_kernel(q_ref, ka_ref, va_ref, km_ref, o_ref, qa_ref, s_ref, p_ref, m_ref, acc_ref,
                 *, tq, tk, n_top, rc):
    i = pl.program_id(2)
    s0 = i * tq
    nbm = km_ref.shape[1]
    blk = lax.broadcasted_iota(jnp.int32, (1, nbm), 1)
    cur = (s0 + lax.broadcasted_iota(jnp.int32, (tq, 1), 0)) // MOBA_BLOCK
    for hh in range(2):
        q = q_ref[:, hh * HEAD_DIM:(hh + 1) * HEAD_DIM]
        km_hi, km_lo = _split_hi_lo(km_ref[0][:, hh * HEAD_DIM:(hh + 1) * HEAD_DIM])
        gs = _dot_nt(q, km_hi) + _dot_nt(q, km_lo)
        gs = jnp.where(blk < cur, gs, NEG)
        sel = _topk_mask(gs, n_top)
        open_blk = ((sel > 0.5) & (gs > NEG * 0.5)) | (blk == cur)
        bias = jnp.where(open_blk, 0.0, NEG).astype(CDT)
        qa_ref[hh] = jnp.concatenate([q, bias], axis=1)
    m_ref[...] = jnp.full(m_ref.shape, NEG, jnp.float32)
    acc_ref[...] = jnp.zeros(acc_ref.shape, jnp.float32)

    def step(kt, masked):
        start = pl.multiple_of(kt * tk, tk)
        for hh in range(2):
            s_ref[hh] = _dot(qa_ref[hh], ka_ref[0, hh, :, pl.ds(start, tk)])
        causal = (start + lax.broadcasted_iota(jnp.int32, (1, tk), 1), s0) if masked else None
        for hh in range(2):
            _flash_tile(s_ref.at[hh], p_ref.at[hh], m_ref.at[hh], acc_ref.at[hh], rc, tq, causal=causal)
            acc_ref[hh] += _dot(p_ref[hh], va_ref[0, hh, pl.ds(start, tk), :])

    def body(kt, carry):
        step(kt, False)
        return carry

    n_full = s0 // tk
    lax.fori_loop(0, n_full, body, 0)
    for d in range(tq // tk):
        step(n_full + d, True)
    outs = []
    for hh in range(2):
        acc = acc_ref[hh]
        outs.append(acc[:, :HEAD_DIM] / acc[:, HEAD_DIM:])
    o_ref[...] = jnp.concatenate(outs, axis=1).astype(o_ref.dtype)


def _moba(proj, mk_aug_t, mv_aug, kmean, batch, seq):
    tq = min(1024, seq)
    tk = 512
    nq = seq // tq
    nbm = seq // MOBA_BLOCK
    n_top = min(MOBA_TOPK, nbm)
    qb = COL_MQ // 128
    aug = HEAD_DIM + nbm
    return pl.pallas_call(
        functools.partial(_moba_kernel, tq=tq, tk=tk, n_top=n_top, rc=64),
        out_shape=jax.ShapeDtypeStruct((batch * seq, MOBA_W), CDT),
        grid=(batch, MOBA_HEADS // 2, nq),
        in_specs=[pl.BlockSpec((tq, 128), lambda b, p, i: (b * nq + i, qb + p)),
                  pl.BlockSpec((1, 2, aug, seq), lambda b, p, i: (b, p, 0, 0)),
                  pl.BlockSpec((1, 2, seq, 128), lambda b, p, i: (b, p, 0, 0)),
                  pl.BlockSpec((1, nbm, 128), lambda b, p, i: (b, 0, p))],
        out_specs=pl.BlockSpec((tq, 128), lambda b, p, i: (b * nq + i, p)),
        scratch_shapes=[pltpu.VMEM((2, tq, aug), CDT), pltpu.VMEM((2, tq, tk), jnp.float32),
                        pltpu.VMEM((2, tq, tk), CDT), pltpu.VMEM((2, tq, 128), jnp.float32),
                        pltpu.VMEM((2, tq, 128), jnp.float32)],
        compiler_params=_params(("arbitrary", "arbitrary", "arbitrary")),
        name="moba",
    )(proj, mk_aug_t, mv_aug, kmean)


def _merge_kernel(oc_ref, os_ref, ow_ref, om_ref, ng_ref, gn_ref, gm_ref, x_ref, ga_ref,
                  ex_ref, wun_ref, wum_ref, wo_ref, o_ref):
    gates = _sigmoid(ng_ref[...].astype(jnp.float32))
    hi, lo = _split_hi_lo(gates)
    e = _dot(hi, ex_ref[...]) + _dot(lo, ex_ref[...])
    o_nsa = (e[:, :NSA_Q] * oc_ref[...].astype(jnp.float32)
             + e[:, NSA_Q:2 * NSA_Q] * os_ref[...].astype(jnp.float32)
             + e[:, 2 * NSA_Q:] * ow_ref[...].astype(jnp.float32))
    y = (_sigmoid(gn_ref[...].astype(jnp.float32)) * _dot(o_nsa.astype(CDT), wun_ref[...])
         + _sigmoid(gm_ref[...].astype(jnp.float32)) * _dot(om_ref[...], wum_ref[...]))
    o_ref[...] = x_ref[...] + ga_ref[0] * _dot(y.astype(CDT), wo_ref[...])


def _merge(o_c, o_s, o_w, o_m, proj, x2d, ga, w_up_nsa, w_up_moba, w_out, seq):
    n, d = x2d.shape
    tm = min(512, seq)
    per_seq = seq // tm
    ng_w = COL_GN - COL_NG
    ex = np.zeros((ng_w, 3 * NSA_Q), np.float32)
    for h in range(NSA_HEADS):
        for j in range(3):
            ex[h * 3 + j, j * NSA_Q + h * HEAD_DIM: j * NSA_Q + (h + 1) * HEAD_DIM] = 1.0
    row = lambda i: (i, 0)
    const = lambda i: (0, 0)
    return pl.pallas_call(
        _merge_kernel,
        out_shape=jax.ShapeDtypeStruct((n, d), jnp.float32),
        grid=(n // tm,),
        in_specs=[pl.BlockSpec((tm, NSA_Q), row), pl.BlockSpec((tm, NSA_Q), row),
                  pl.BlockSpec((tm, NSA_Q), row), pl.BlockSpec((tm, MOBA_W), row),
                  pl.BlockSpec((tm, ng_w), lambda i: (i, COL_NG // ng_w)),
                  pl.BlockSpec((tm, d), lambda i: (i, COL_GN // d)),
                  pl.BlockSpec((tm, d), lambda i: (i, COL_GM // d)),
                  pl.BlockSpec((tm, d), row),
                  pl.BlockSpec((1, 1, d), lambda i: (i // per_seq, 0, 0)),
                  pl.BlockSpec((ng_w, 3 * NSA_Q), const),
                  pl.BlockSpec((NSA_Q, d), const), pl.BlockSpec((MOBA_W, d), const),
                  pl.BlockSpec((d, d), const)],
        out_specs=pl.BlockSpec((tm, d), row),
        compiler_params=_params(("arbitrary",)),
        name="mixer_merge",
    )(o_c, o_s, o_w, o_m, proj, proj, proj, x2d, ga, jnp.asarray(ex, CDT),
      w_up_nsa.astype(CDT), w_up_moba.astype(CDT), w_out.astype(CDT))


def _extract_max(x, ridx):
    m = jnp.max(x, axis=0, keepdims=True)
    idx = jnp.min(jnp.where(x == m, ridx, float(x.shape[0])), axis=0, keepdims=True)
    return m, ridx == idx


def _peer_score_kernel(q_ref, k1_ref, k2_ref, cnt_ref, rk_ref, e1_ref, e2_ref):
    half = PEER_QDIM // 2
    k = PEER_TOPK
    q = q_ref[...]
    s1 = _dot_nt(k1_ref[...], q[:, :half])
    s2 = _dot_nt(k2_ref[...], q[:, half:])
    s1 = s1 - jnp.max(s1, axis=0, keepdims=True)
    s2 = s2 - jnp.max(s2, axis=0, keepdims=True)
    t = s1.shape[1]
    ridx = lax.broadcasted_iota(jnp.int32, s1.shape, 0).astype(jnp.float32)
    slot = lax.broadcasted_iota(jnp.int32, (k, t), 0)

    def rank_body(i, carry):
        x1, x2, rk1, rk2, v1, v2 = carry
        fi = i.astype(jnp.float32)
        m1, pick1 = _extract_max(x1, ridx)
        m2, pick2 = _extract_max(x2, ridx)
        return (jnp.where(pick1, LOWEST, x1), jnp.where(pick2, LOWEST, x2),
                jnp.where(pick1, fi, rk1), jnp.where(pick2, fi, rk2),
                jnp.where(slot == i, m1, v1), jnp.where(slot == i, m2, v2))

    unranked = jnp.full(s1.shape, float(k), jnp.float32)
    zeros = jnp.zeros((k, t), jnp.float32)
    _, _, rk1, rk2, v1, v2 = lax.fori_loop(0, k, rank_body, (s1, s2, unranked, unranked, zeros, zeros))

    pieces = [v1[i:i + 1] + v2[0:k // (i + 1)] for i in range(k)]
    n_cand = sum(p.shape[0] for p in pieces)
    pad = -n_cand % 8
    cand = jnp.concatenate(pieces + [jnp.full((pad, t), LOWEST, jnp.float32)], axis=0)
    cidx = lax.broadcasted_iota(jnp.int32, cand.shape, 0).astype(jnp.float32)

    def cand_body(i, carry):
        x, vals = carry
        m, pick = _extract_max(x, cidx)
        return jnp.where(pick, LOWEST, x), jnp.where(slot == i, m, vals)

    _, vals = lax.fori_loop(0, k, cand_body, (cand, zeros))
    tau = vals[k - 1:k]
    z = jnp.sum(jnp.exp(vals - vals[0:1]), axis=0, keepdims=True)
    cnt = jnp.zeros(s1.shape, jnp.float32)
    for i in range(k):
        n_i = jnp.sum(jnp.where(v1[i:i + 1] + v2 >= tau, 1.0, 0.0), axis=0, keepdims=True)
        cnt = jnp.where(rk1 == float(i), n_i, cnt)
    cnt_ref[0] = cnt
    rk_ref[0] = rk2.astype(rk_ref.dtype)
    e1_ref[0] = jnp.exp(s1 - vals[0:1]) / z
    e2_ref[0] = jnp.exp(s2).astype(e2_ref.dtype)


def _peer_scores(qp, k1, k2):
    n = qp.shape[0]
    tt = 256
    f32 = jax.ShapeDtypeStruct((PEER_HEADS, PEER_NKEYS, n), jnp.float32)
    cdt = jax.ShapeDtypeStruct((PEER_HEADS, PEER_NKEYS, n), CDT)
    big = pl.BlockSpec((1, PEER_NKEYS, tt), lambda i, h: (h, 0, i))
    return pl.pallas_call(
        _peer_score_kernel,
        out_shape=[f32, cdt, f32, cdt],
        grid=(n // tt, PEER_HEADS),
        in_specs=[pl.BlockSpec((tt, PEER_QDIM), lambda i, h: (i, h)),
                  pl.BlockSpec((PEER_NKEYS, PEER_QDIM // 2), lambda i, h: (0, 0)),
                  pl.BlockSpec((PEER_NKEYS, PEER_QDIM // 2), lambda i, h: (0, 0))],
        out_specs=[big, big, big, big],
        compiler_params=_params(("arbitrary", "arbitrary")),
        name="peer_scores",
    )(qp, k1.astype(CDT), k2.astype(CDT))


def _peer_expert_kernel(h_ref, u_ref, vt_ref, cnt_ref, rk_ref, e1_ref, e2_ref, x_ref, ga_ref,
                        o_ref, acc_ref, *, eb):
    j = pl.program_id(1)
    tt = h_ref.shape[0]

    @pl.when(j == 0)
    def _():
        acc_ref[...] = jnp.zeros(acc_ref.shape, jnp.float32)

    act = _gelu(_dot_nt(u_ref[...], h_ref[...])).astype(CDT)
    parts = []
    for al in range(eb // PEER_NKEYS):
        a = j * (eb // PEER_NKEYS) + al
        w = None
        for hd in range(PEER_HEADS):
            cnt_a = jnp.broadcast_to(cnt_ref[hd, pl.ds(a, 1), :].astype(CDT), (PEER_NKEYS, tt))
            e1_a = jnp.broadcast_to(e1_ref[hd, pl.ds(a, 1), :].astype(CDT), (PEER_NKEYS, tt))
            contrib = jnp.where(rk_ref[hd] < cnt_a, e2_ref[hd] * e1_a, jnp.zeros((), CDT))
            w = contrib if w is None else w + contrib
        parts.append(w)
    pw = jnp.concatenate(parts, axis=0) * act
    acc_ref[...] += _dot(vt_ref[...], pw)

    @pl.when(j == pl.num_programs(1) - 1)
    def _():
        o_ref[...] = x_ref[...] + ga_ref[0] * acc_ref[...].T


def _peer_experts(h2, u, v_t, cnt, rk2, e1, e2, x2d, ga, seq):
    n, d = x2d.shape
    tt = min(512, seq)
    per_seq = seq // tt
    eb = 512
    n_blk = u.shape[0] // eb
    big = pl.BlockSpec((PEER_HEADS, PEER_NKEYS, tt), lambda i, j: (0, 0, i))
    return pl.pallas_call(
        functools.partial(_peer_expert_kernel, eb=eb),
        out_shape=jax.ShapeDtypeStruct((n, d), jnp.float32),
        grid=(n // tt, n_blk),
        in_specs=[pl.BlockSpec((tt, d), lambda i, j: (i, 0)),
                  pl.BlockSpec((eb, d), lambda i, j: (j, 0)),
                  pl.BlockSpec((d, eb), lambda i, j: (0, j)),
                  big, big, big, big,
                  pl.BlockSpec((tt, d), lambda i, j: (i, 0)),
                  pl.BlockSpec((1, 1, d), lambda i, j: (i // per_seq, 0, 0))],
        out_specs=pl.BlockSpec((tt, d), lambda i, j: (i, 0)),
        scratch_shapes=[pltpu.VMEM((d, tt), jnp.float32)],
        compiler_params=_params(("arbitrary", "arbitrary")),
        name="peer_experts",
    )(h2, u, v_t, cnt, rk2, e1, e2, x2d, ga)


def _rms_kernel(x_ref, g_ref, o_ref):
    x = x_ref[...]
    ms = jnp.mean(x * x, axis=-1, keepdims=True)
    o_ref[...] = x * lax.rsqrt(ms + RMS_EPS) * g_ref[...]


def _final_norm(x2d, g):
    n, d = x2d.shape
    tm = 512
    return pl.pallas_call(
        _rms_kernel,
        out_shape=jax.ShapeDtypeStruct((n, d), jnp.float32),
        grid=(n // tm,),
        in_specs=[pl.BlockSpec((tm, d), lambda i: (i, 0)), pl.BlockSpec((1, d), lambda i: (0, 0))],
        out_specs=pl.BlockSpec((tm, d), lambda i: (i, 0)),
        compiler_params=_params(("arbitrary",)),
        name="final_rmsnorm",
    )(x2d, g.reshape(1, d))


def _key_major(proj, col, heads, batch, seq):
    k = proj[:, col:col + heads * HEAD_DIM].reshape(batch, seq, heads, HEAD_DIM)
    return jnp.transpose(k, (0, 2, 3, 1))


def _value_with_ones(proj, col, heads, batch, seq):
    v = proj[:, col:col + heads * HEAD_DIM].reshape(batch, seq, heads, HEAD_DIM)
    v = jnp.transpose(v, (0, 2, 1, 3))
    return jnp.concatenate([v, jnp.ones_like(v)], axis=-1)


def kernel(x, c, w_ada, b_ada, g_attn, g_ffn, w_in, cmp_pe, cmp_w1, cmp_w2, w_up_nsa, w_up_moba, w_out,
           peer_wq, peer_k1, peer_k2, peer_u, peer_v, g_final):
    batch, seq, d = x.shape
    depth = w_ada.shape[0]
    n = batch * seq
    nbm = seq // MOBA_BLOCK
    x2d = x.reshape(n, d)
    mod = _adaln_mod(c, w_ada, b_ada)
    rope = _rope_tables(seq)
    blk_onehot = (jnp.arange(nbm)[:, None] == jnp.arange(seq)[None, :] // MOBA_BLOCK).astype(CDT)
    for l in range(depth):
        sh1, sc1, ga1, sh2, sc2, ga2 = [m.reshape(batch, 1, d) for m in jnp.split(mod[l], 6, axis=-1)]
        proj = _norm_mod_matmul(x2d, g_attn[l], sc1, sh1, _reorder_w_in(w_in[l]), seq,
                                rope=rope, n_rope=ROPE_COLS // PROJ_TN)
        cmp_kv = _nsa_compress(proj, cmp_pe[l], cmp_w1[l], cmp_w2[l], batch, seq)
        o_c, selb = _nsa_compressed(proj, cmp_kv, batch, seq)
        o_s = _nsa_selected(proj, _key_major(proj, COL_KS, NSA_KV_HEADS, batch, seq),
                            _value_with_ones(proj, COL_VS, NSA_KV_HEADS, batch, seq), selb, batch, seq)
        o_w = _nsa_window(proj, _key_major(proj, COL_KW, NSA_KV_HEADS, batch, seq), batch, seq)
        mk_t = _key_major(proj, COL_MK, MOBA_HEADS, batch, seq)
        mk_aug = jnp.concatenate(
            [mk_t, jnp.broadcast_to(blk_onehot, (batch, MOBA_HEADS, nbm, seq))], axis=2)
        o_m = _moba(proj, mk_aug, _value_with_ones(proj, COL_MV, MOBA_HEADS, batch, seq),
                    _moba_kmean(proj, batch, seq), batch, seq)
        x2d = _merge(o_c, o_s, o_w, o_m, proj, x2d, ga1, w_up_nsa[l], w_up_moba[l], w_out[l], seq)
        qp, h2 = _norm_mod_matmul(x2d, g_ffn[l], sc2, sh2, peer_wq[l].astype(CDT), seq, emit_h=True)
        cnt, rk2, e1, e2 = _peer_scores(qp, peer_k1[l], peer_k2[l])
        x2d = _peer_experts(h2, peer_u[l].astype(CDT), peer_v[l].T.astype(CDT), cnt, rk2, e1, e2,
                            x2d, ga2, seq)
    return _final_norm(x2d, g_final).reshape(batch, seq, d)
```

```python
import functools

import jax
import jax.numpy as jnp
import numpy as np
from jax import lax
from jax.experimental import pallas as pl
from jax.experimental.pallas import tpu as pltpu

D_MODEL = 1024
HEAD_DIM = 64
ROT_DIM = HEAD_DIM // 4
ROPE_THETA = 500000.0
NSA_HEADS = 8
NSA_KV_HEADS = 2
NSA_GROUP = NSA_HEADS // NSA_KV_HEADS
CMP_LEN = 32
CMP_STRIDE = 16
CMP_HID = 2 * HEAD_DIM
SLC_LEN = 64
SLC_TOPN = 16
WIN = 512
MOBA_HEADS = 8
MOBA_BLOCK = 256
MOBA_TOPK = 3
PEER_HEADS = 8
PEER_NKEYS = 128
PEER_QDIM = 256
PEER_TOPK = 16
RMS_EPS = 1e-6
NEG = -1e30
SEL_FORCE = 1e4
LOWEST = -3.0e38
LOG2E = 1.4426950408889634

NSA_Q = NSA_HEADS * HEAD_DIM
NSA_KV = NSA_KV_HEADS * HEAD_DIM
MOBA_W = MOBA_HEADS * HEAD_DIM

CDT = jnp.bfloat16
V7X_VMEM_LIMIT = 56 * 1024 * 1024

COL_QROT = 0
COL_KS = 512
COL_KW = 640
COL_MQ = 768
COL_MK = 1280
ROPE_COLS = 1792
COL_NG = 1792
COL_GN = 2048
COL_GM = 3072
COL_QRAW = 4096
COL_KC = 4608
COL_VC = 4736
COL_VS = 4864
COL_VW = 4992
COL_MV = 5120
PROJ_COLS = 5632
PROJ_TN = 256


def _params(sem):
    return pltpu.CompilerParams(dimension_semantics=sem, vmem_limit_bytes=V7X_VMEM_LIMIT)


def _dot(a, b):
    return jnp.dot(a, b, preferred_element_type=jnp.float32)


def _dot_nt(a, b):
    return lax.dot_general(a, b, (((1,), (1,)), ((), ())), preferred_element_type=jnp.float32)


def _split_hi_lo(x):
    hi = x.astype(CDT)
    lo = (x - hi.astype(jnp.float32)).astype(CDT)
    return hi, lo


def _gelu(x):
    return 0.5 * x * (1.0 + jnp.tanh(0.7978845608028654 * (x + 0.044715 * (x * x * x))))


def _sigmoid(x):
    return 1.0 / (1.0 + jnp.exp(-x))


def _mod_kernel(c_ref, w_ref, b_ref, o_ref):
    c = c_ref[...]
    sc = c * _sigmoid(c)
    o_ref[0] = jnp.dot(sc, w_ref[0], preferred_element_type=jnp.float32,
                       precision=lax.Precision.HIGHEST) + b_ref[0]


def _adaln_mod(c, w_ada, b_ada):
    depth, d, six_d = w_ada.shape
    b = c.shape[0]
    rows = 8
    c_pad = jnp.zeros((rows, d), jnp.float32).at[:b].set(c)
    tn = 1024
    out = pl.pallas_call(
        _mod_kernel,
        out_shape=jax.ShapeDtypeStruct((depth, rows, six_d), jnp.float32),
        grid=(depth, six_d // tn),
        in_specs=[pl.BlockSpec((rows, d), lambda l, j: (0, 0)),
                  pl.BlockSpec((1, d, tn), lambda l, j: (l, 0, j)),
                  pl.BlockSpec((1, 1, tn), lambda l, j: (l, 0, j))],
        out_specs=pl.BlockSpec((1, rows, tn), lambda l, j: (l, 0, j)),
        compiler_params=_params(("arbitrary", "arbitrary")),
        name="adaln_mod",
    )(c_pad, w_ada, b_ada.reshape(depth, 1, six_d))
    return out[:, :b]


def _nmm_kernel(*refs, n_rope, emit_h):
    if n_rope:
        x_ref, g_ref, sc_ref, sh_ref, w_ref, cos_ref, sa_ref, sb_ref = refs[:8]
        rest = refs[8:]
    else:
        x_ref, g_ref, sc_ref, sh_ref, w_ref = refs[:5]
        rest = refs[5:]
    if emit_h:
        o_ref, ho_ref, h_ref = rest
    else:
        o_ref, h_ref = rest
    j = pl.program_id(1)

    @pl.when(j == 0)
    def _():
        x = x_ref[...]
        ms = jnp.mean(x * x, axis=-1, keepdims=True)
        y = x * lax.rsqrt(ms + RMS_EPS) * g_ref[...]
        h = (y * (1.0 + sc_ref[0]) + sh_ref[0]).astype(h_ref.dtype)
        h_ref[...] = h
        if emit_h:
            ho_ref[...] = h

    acc = _dot(h_ref[...], w_ref[...])
    if n_rope:
        @pl.when(j < n_rope)
        def _():
            cos, sa, sb = cos_ref[...], sa_ref[...], sb_ref[...]
            parts = []
            for c0 in range(0, acc.shape[1], 128):
                a = acc[:, c0:c0 + 128]
                parts.append(a * cos + pltpu.roll(a, 128 - ROT_DIM // 2, axis=1) * sa
                             + pltpu.roll(a, ROT_DIM // 2, axis=1) * sb)
            o_ref[...] = jnp.concatenate(parts, axis=1).astype(o_ref.dtype)

        @pl.when(j >= n_rope)
        def _():
            o_ref[...] = acc.astype(o_ref.dtype)
    else:
        o_ref[...] = acc.astype(o_ref.dtype)


def _norm_mod_matmul(x2d, g, sc, sh, w, seq, rope=None, n_rope=0, emit_h=False, tn=PROJ_TN):
    n, d = x2d.shape
    cols = w.shape[1]
    tm = min(1024, seq)
    per_seq = seq // tm
    in_specs = [pl.BlockSpec((tm, d), lambda i, j: (i, 0)),
                pl.BlockSpec((1, d), lambda i, j: (0, 0)),
                pl.BlockSpec((1, 1, d), lambda i, j: (i // per_seq, 0, 0)),
                pl.BlockSpec((1, 1, d), lambda i, j: (i // per_seq, 0, 0)),
                pl.BlockSpec((d, tn), lambda i, j: (0, j))]
    args = [x2d, g.reshape(1, d), sc, sh, w]
    if n_rope:
        in_specs += [pl.BlockSpec((tm, 128), lambda i, j: (i % per_seq, 0))] * 3
        args += list(rope)
    out_shape = [jax.ShapeDtypeStruct((n, cols), CDT)]
    out_specs = [pl.BlockSpec((tm, tn), lambda i, j: (i, j))]
    if emit_h:
        out_shape.append(jax.ShapeDtypeStruct((n, d), CDT))
        out_specs.append(pl.BlockSpec((tm, d), lambda i, j: (i, 0)))
    res = pl.pallas_call(
        functools.partial(_nmm_kernel, n_rope=n_rope, emit_h=emit_h),
        out_shape=out_shape,
        grid=(n // tm, cols // tn),
        in_specs=in_specs,
        out_specs=out_specs,
        scratch_shapes=[pltpu.VMEM((tm, d), CDT)],
        compiler_params=_params(("arbitrary", "arbitrary")),
        name="norm_mod_matmul",
    )(*args)
    return res if emit_h else res[0]


def _rope_tables(seq):
    half = ROT_DIM // 2
    inv = ROPE_THETA ** (-jnp.arange(half, dtype=jnp.float32) / half)
    ang = jnp.arange(seq, dtype=jnp.float32)[:, None] * inv[None, :]
    cos, sin = jnp.cos(ang), jnp.sin(ang)
    ones = jnp.ones((seq, HEAD_DIM - ROT_DIM), jnp.float32)
    zeros = jnp.zeros((seq, HEAD_DIM - ROT_DIM), jnp.float32)
    zh = jnp.zeros((seq, half), jnp.float32)
    c64 = jnp.concatenate([cos, cos, ones], axis=1)
    sa64 = jnp.concatenate([-sin, zh, zeros], axis=1)
    sb64 = jnp.concatenate([zh, sin, zeros], axis=1)
    return tuple(jnp.concatenate([t, t], axis=1) for t in (c64, sa64, sb64))


def _reorder_w_in(w_in):
    o = 0
    pieces = {}
    for name, size in (("nq", NSA_Q), ("kc", NSA_KV), ("vc", NSA_KV), ("ks", NSA_KV), ("vs", NSA_KV),
                       ("kw", NSA_KV), ("vw", NSA_KV), ("ng", 3 * NSA_HEADS), ("mq", MOBA_W),
                       ("mk", MOBA_W), ("mv", MOBA_W), ("gn", D_MODEL), ("gm", D_MODEL)):
        pieces[name] = w_in[:, o:o + size]
        o += size
    scale = HEAD_DIM ** -0.5 * LOG2E
    pad = jnp.zeros((w_in.shape[0], COL_GN - COL_NG - 3 * NSA_HEADS), w_in.dtype)
    w = jnp.concatenate([pieces["nq"] * scale, pieces["ks"], pieces["kw"], pieces["mq"] * scale,
                         pieces["mk"], pieces["ng"], pad, pieces["gn"], pieces["gm"],
                         pieces["nq"] * scale, pieces["kc"], pieces["vc"], pieces["vs"],
                         pieces["vw"], pieces["mv"]], axis=1)
    assert w.shape[1] == PROJ_COLS
    return w.astype(CDT)


def _compress_kernel(x_ref, w1c_ref, pe_ref, w1_ref, w2_ref, o_ref):
    nb = x_ref.shape[3]
    ab = _dot(x_ref[0, 0, 0], w1c_ref[0])
    a = ab[:, :CMP_HID]
    b_next = pltpu.roll(ab[:, CMP_HID:], nb - 1, axis=0)
    bias = _dot(pe_ref[0], w1_ref[0])[0:1]
    hid = _gelu(a + b_next + bias)
    o_ref[0, 0, 0] = _dot(hid.astype(CDT), w2_ref[0]).astype(o_ref.dtype)


def _nsa_compress(proj, cmp_pe, cmp_w1, cmp_w2, batch, seq):
    nb = seq // CMP_STRIDE
    half = CMP_STRIDE * HEAD_DIM
    kcvc = proj[:, COL_KC:COL_KC + 2 * NSA_KV].reshape(batch, nb, CMP_STRIDE, 2, NSA_KV_HEADS, HEAD_DIM)
    x = jnp.transpose(kcvc, (3, 0, 4, 1, 2, 5)).reshape(2, batch, NSA_KV_HEADS, nb, half)
    w1c = jnp.concatenate([cmp_w1[:, :half], cmp_w1[:, half:]], axis=2).astype(CDT)
    pe = jnp.broadcast_to(cmp_pe.reshape(2, 1, CMP_LEN * HEAD_DIM), (2, 8, CMP_LEN * HEAD_DIM)).astype(CDT)
    return pl.pallas_call(
        _compress_kernel,
        out_shape=jax.ShapeDtypeStruct((2, batch, NSA_KV_HEADS, nb, HEAD_DIM), CDT),
        grid=(2, batch, NSA_KV_HEADS),
        in_specs=[pl.BlockSpec((1, 1, 1, nb, half), lambda w, b, k: (w, b, k, 0, 0)),
                  pl.BlockSpec((1, half, 2 * CMP_HID), lambda w, b, k: (w, 0, 0)),
                  pl.BlockSpec((1, 8, 2 * half), lambda w, b, k: (w, 0, 0)),
                  pl.BlockSpec((1, 2 * half, CMP_HID), lambda w, b, k: (w, 0, 0)),
                  pl.BlockSpec((1, CMP_HID, HEAD_DIM), lambda w, b, k: (w, 0, 0))],
        out_specs=pl.BlockSpec((1, 1, 1, nb, HEAD_DIM), lambda w, b, k: (w, b, k, 0, 0)),
        compiler_params=_params(("arbitrary", "arbitrary", "arbitrary")),
        name="nsa_compress",
    )(x, w1c, pe, cmp_w1.astype(CDT), cmp_w2.astype(CDT))


def _stack_heads(qblk):
    return jnp.concatenate([qblk[:, g * HEAD_DIM:(g + 1) * HEAD_DIM] for g in range(NSA_GROUP)], axis=0)


def _unstack_heads(o, tq):
    return jnp.concatenate([o[g * tq:(g + 1) * tq] for g in range(NSA_GROUP)], axis=1)


def _topk_mask(score, n_sel):
    shape = score.shape
    lane = lax.broadcasted_iota(jnp.int32, shape, 1).astype(jnp.float32)
    width = float(shape[1])

    def body(_, carry):
        sc, sel = carry
        m = jnp.max(sc, axis=-1, keepdims=True)
        idx = jnp.min(jnp.where(sc == m, lane, width), axis=-1, keepdims=True)
        pick = lane == idx
        return jnp.where(pick, LOWEST, sc), jnp.where(pick, 1.0, sel)

    _, sel = lax.fori_loop(0, n_sel, body, (score, jnp.zeros(shape, jnp.float32)))
    return sel


def _nsa_cmp_kernel(q_ref, kc_ref, vc_ref, ov_ref, oc_ref, sb_ref, *, tq, n_sel):
    i = pl.program_id(2)
    s0 = i * tq
    nb = kc_ref.shape[3]
    ns = ov_ref.shape[1]
    q4 = _stack_heads(q_ref[...])
    s = _dot_nt(q4, kc_ref[0, 0, 0])
    tq_col = s0 + lax.broadcasted_iota(jnp.int32, (tq, 1), 0)
    t4 = jnp.concatenate([tq_col] * NSA_GROUP, axis=0)
    cend = lax.broadcasted_iota(jnp.int32, (1, nb), 1) * CMP_STRIDE + (CMP_LEN - 1)
    mask = cend <= t4
    s = jnp.where(mask, s, NEG)
    m = jnp.max(s, axis=-1, keepdims=True)
    p = jnp.where(mask, jnp.exp2(s - m), 0.0)
    p = p / jnp.maximum(jnp.sum(p, axis=-1, keepdims=True), 1e-30)
    o = _dot(p.astype(CDT), vc_ref[0, 0, 0])
    oc_ref[...] = _unstack_heads(o, tq).astype(oc_ref.dtype)

    psum = p[0:tq]
    for g in range(1, NSA_GROUP):
        psum = psum + p[g * tq:(g + 1) * tq]
    hi, lo = _split_hi_lo(psum)
    imp = _dot(hi, ov_ref[...]) + _dot(lo, ov_ref[...])

    blk = lax.broadcasted_iota(jnp.int32, (ns, 1), 0)
    cur = (s0 + lax.broadcasted_iota(jnp.int32, (1, tq), 1)) // SLC_LEN
    valid = blk <= cur
    forced = (blk == 0) | (blk == cur) | (blk == cur - 1)
    score = jnp.where(valid, jnp.where(forced, SEL_FORCE, imp.T), NEG)
    ridx = lax.broadcasted_iota(jnp.int32, (ns, tq), 0).astype(jnp.float32)

    def body(_, carry):
        sc, sel = carry
        _, pick = _extract_max(sc, ridx)
        return jnp.where(pick, LOWEST, sc), jnp.where(pick, 1.0, sel)

    _, sel = lax.fori_loop(0, n_sel, body, (score, jnp.zeros((ns, tq), jnp.float32)))
    bias = jnp.where((sel > 0.5) & valid, 0.0, NEG).T
    extra = sb_ref.shape[3] - ns
    if extra:
        bias = jnp.concatenate([bias, jnp.full((tq, extra), NEG, jnp.float32)], axis=1)
    sb_ref[0, 0] = bias.astype(sb_ref.dtype)


def _nsa_compressed(proj, cmp_kv, batch, seq):
    tq = 128
    nq = seq // tq
    nb = seq // CMP_STRIDE
    ns = seq // SLC_LEN
    n_sel = min(SLC_TOPN, ns)
    c_start = np.arange(nb) * CMP_STRIDE
    s_start = np.arange(ns) * SLC_LEN
    ov = np.maximum(np.minimum(c_start[:, None] + CMP_LEN, s_start[None, :] + SLC_LEN)
                    - np.maximum(c_start[:, None], s_start[None, :]), 0).astype(np.float32) / CMP_LEN
    ov[nb - 1] = 0.0
    qb = COL_QRAW // (NSA_GROUP * HEAD_DIM)
    return pl.pallas_call(
        functools.partial(_nsa_cmp_kernel, tq=tq, n_sel=n_sel),
        out_shape=[jax.ShapeDtypeStruct((batch * seq, NSA_Q), CDT),
                   jax.ShapeDtypeStruct((batch, NSA_KV_HEADS, seq, max(ns, 128)), CDT)],
        grid=(batch, NSA_KV_HEADS, nq),
        in_specs=[pl.BlockSpec((tq, NSA_GROUP * HEAD_DIM), lambda b, k, i: (b * nq + i, qb + k)),
                  pl.BlockSpec((1, 1, 1, nb, HEAD_DIM), lambda b, k, i: (0, b, k, 0, 0)),
                  pl.BlockSpec((1, 1, 1, nb, HEAD_DIM), lambda b, k, i: (1, b, k, 0, 0)),
                  pl.BlockSpec((nb, ns), lambda b, k, i: (0, 0))],
        out_specs=[pl.BlockSpec((tq, NSA_GROUP * HEAD_DIM), lambda b, k, i: (b * nq + i, k)),
                   pl.BlockSpec((1, 1, tq, max(ns, 128)), lambda b, k, i: (b, k, i, 0))],
        compiler_params=_params(("arbitrary", "arbitrary", "arbitrary")),
        name="nsa_compressed",
    )(proj, cmp_kv, cmp_kv, jnp.asarray(ov, CDT))


def _flash_tile(s_ref, p_ref, m_ref, acc_ref, rc, tq, bias_ref=None, causal=None):
    rows, tk = s_ref.shape

    def tile(r0, c0):
        s = s_ref[r0:r0 + rc, c0:c0 + 128]
        if bias_ref is not None:
            s = s + bias_ref[r0 % tq:r0 % tq + rc, c0:c0 + 128]
        if causal is not None:
            kpos, t0 = causal
            t = t0 + (r0 % tq) + lax.broadcasted_iota(jnp.int32, (rc, 1), 0)
            s = jnp.where(kpos[:, c0:c0 + 128] <= t, s, NEG)
        return s

    for r0 in range(0, rows, rc):
        mx = tile(r0, 0)
        for c0 in range(128, tk, 128):
            mx = jnp.maximum(mx, tile(r0, c0))
        m_prev = m_ref[r0:r0 + rc, :]
        m_new = jnp.maximum(m_prev, jnp.max(mx, axis=-1, keepdims=True))
        acc_ref[r0:r0 + rc, :] = acc_ref[r0:r0 + rc, :] * jnp.exp2(m_prev - m_new)
        m_ref[r0:r0 + rc, :] = m_new
    for r0 in range(0, rows, rc):
        m_new = m_ref[r0:r0 + rc, :]
        for c0 in range(0, tk, 128):
            p_ref[r0:r0 + rc, c0:c0 + 128] = jnp.exp2(tile(r0, c0) - m_new).astype(p_ref.dtype)


def _nsa_sel_kernel(q_ref, kt_ref, va_ref, sb_ref, o_ref,
                    qp_ref, q4_ref, sbf_ref, s_ref, p_ref, m_ref, acc_ref, *, tq, tk, rc):
    i = pl.program_id(2)
    s0 = i * tq
    nsp = sb_ref.shape[3]
    per_tile = tk // SLC_LEN
    q4 = _stack_heads(q_ref[...])
    qp_ref[...] = jnp.concatenate([q4, jnp.zeros_like(q4)], axis=1)
    sbf_ref[...] = sb_ref[0, 0].astype(jnp.float32)
    lane = lax.broadcasted_iota(jnp.int32, (tq, 128), 1)
    bias_lanes = (lane >= HEAD_DIM) & (lane < HEAD_DIM + per_tile)
    m_ref[...] = jnp.full(m_ref.shape, NEG, jnp.float32)
    acc_ref[...] = jnp.zeros(acc_ref.shape, jnp.float32)

    def step(kt, masked):
        start = pl.multiple_of(kt * tk, tk)
        shift = (HEAD_DIM - kt * per_tile + nsp) % nsp
        rolled = pltpu.roll(sbf_ref[...], shift, axis=1)[:, :128].astype(q4_ref.dtype)
        for g in range(NSA_GROUP):
            q4_ref[g * tq:(g + 1) * tq, :] = jnp.where(bias_lanes, rolled, qp_ref[g * tq:(g + 1) * tq, :])
        s_ref[...] = _dot(q4_ref[...], kt_ref[0, 0, :, pl.ds(start, tk)])
        causal = (start + lax.broadcasted_iota(jnp.int32, (1, tk), 1), s0) if masked else None
        _flash_tile(s_ref, p_ref, m_ref, acc_ref, rc, tq, causal=causal)
        acc_ref[...] += _dot(p_ref[...], va_ref[0, 0, pl.ds(start, tk), :])

    def body(kt, carry):
        step(kt, False)
        return carry

    n_full = s0 // tk
    lax.fori_loop(0, n_full, body, 0)
    step(n_full, True)
    acc = acc_ref[...]
    o = acc[:, :HEAD_DIM] / acc[:, HEAD_DIM:]
    o_ref[...] = _unstack_heads(o, tq).astype(o_ref.dtype)


def _nsa_selected(proj, ks_t, vs_aug, selb, batch, seq):
    tq = 256
    tk = 512
    nq = seq // tq
    nsp = selb.shape[3]
    per_tile = tk // SLC_LEN
    blk_in_tile = (jnp.arange(seq) // SLC_LEN) % per_tile
    onehot = (jnp.arange(128 - HEAD_DIM)[:, None] == blk_in_tile[None, :]).astype(CDT)
    k_aug = jnp.concatenate(
        [ks_t, jnp.broadcast_to(onehot, (batch, NSA_KV_HEADS, 128 - HEAD_DIM, seq))], axis=2)
    qb = COL_QROT // (NSA_GROUP * HEAD_DIM)
    rows = NSA_GROUP * tq
    return pl.pallas_call(
        functools.partial(_nsa_sel_kernel, tq=tq, tk=tk, rc=64),
        out_shape=jax.ShapeDtypeStruct((batch * seq, NSA_Q), CDT),
        grid=(batch, NSA_KV_HEADS, nq),
        in_specs=[pl.BlockSpec((tq, NSA_GROUP * HEAD_DIM), lambda b, k, i: (b * nq + i, qb + k)),
                  pl.BlockSpec((1, 1, 128, seq), lambda b, k, i: (b, k, 0, 0)),
                  pl.BlockSpec((1, 1, seq, 128), lambda b, k, i: (b, k, 0, 0)),
                  pl.BlockSpec((1, 1, tq, nsp), lambda b, k, i: (b, k, i, 0))],
        out_specs=pl.BlockSpec((tq, NSA_GROUP * HEAD_DIM), lambda b, k, i: (b * nq + i, k)),
        scratch_shapes=[pltpu.VMEM((rows, 128), CDT), pltpu.VMEM((rows, 128), CDT),
                        pltpu.VMEM((tq, nsp), jnp.float32), pltpu.VMEM((rows, tk), jnp.float32),
                        pltpu.VMEM((rows, tk), CDT),
                        pltpu.VMEM((rows, 128), jnp.float32), pltpu.VMEM((rows, 128), jnp.float32)],
        compiler_params=_params(("arbitrary", "arbitrary", "arbitrary")),
        name="nsa_selected",
    )(proj, k_aug, vs_aug, selb)


def _nsa_win_kernel(q_ref, kt_ref, v_ref, o_ref, *, tq):
    k = pl.program_id(1)
    i = pl.program_id(2)
    s0 = i * tq
    span = WIN + tq
    start = pl.multiple_of(jnp.maximum(s0 - WIN, 0), tq)
    q4 = _stack_heads(q_ref[...])
    s = _dot(q4, kt_ref[0, 0, :, pl.ds(start, span)])
    tq_col = s0 + lax.broadcasted_iota(jnp.int32, (tq, 1), 0)
    t4 = jnp.concatenate([tq_col] * NSA_GROUP, axis=0)
    kpos = start + lax.broadcasted_iota(jnp.int32, (1, span), 1)
    mask = (kpos <= t4) & (kpos > t4 - WIN)
    s = jnp.where(mask, s, NEG)
    m = jnp.max(s, axis=-1, keepdims=True)
    p = jnp.where(mask, jnp.exp2(s - m), 0.0)
    l = jnp.maximum(jnp.sum(p, axis=-1, keepdims=True), 1e-30)
    o = _dot(p.astype(CDT), v_ref[pl.ds(start, span), :]) / l
    o = jnp.where(k == 0, o[:, :HEAD_DIM], o[:, HEAD_DIM:])
    o_ref[...] = _unstack_heads(o, tq).astype(o_ref.dtype)


def _nsa_window(proj, kw_t, batch, seq):
    tq = 256
    nq = seq // tq
    qb = COL_QROT // (NSA_GROUP * HEAD_DIM)
    vb = COL_VW // 128
    return pl.pallas_call(
        functools.partial(_nsa_win_kernel, tq=tq),
        out_shape=jax.ShapeDtypeStruct((batch * seq, NSA_Q), CDT),
        grid=(batch, NSA_KV_HEADS, nq),
        in_specs=[pl.BlockSpec((tq, NSA_GROUP * HEAD_DIM), lambda b, k, i: (b * nq + i, qb + k)),
                  pl.BlockSpec((1, 1, HEAD_DIM, seq), lambda b, k, i: (b, k, 0, 0)),
                  pl.BlockSpec((seq, 128), lambda b, k, i: (b, vb))],
        out_specs=pl.BlockSpec((tq, NSA_GROUP * HEAD_DIM), lambda b, k, i: (b * nq + i, k)),
        compiler_params=_params(("arbitrary", "arbitrary", "arbitrary")),
        name="nsa_window",
    )(proj, kw_t, proj)


def _moba_mean_kernel(k_ref, o_ref):
    seq, w = k_ref.shape
    nbm = seq // MOBA_BLOCK
    k = k_ref[...].astype(jnp.float32).reshape(nbm, MOBA_BLOCK, w)
    o_ref[0] = jnp.sum(k, axis=1) * (1.0 / MOBA_BLOCK)


def _moba_kmean(proj, batch, seq):
    nbm = seq // MOBA_BLOCK
    kb = COL_MK // 128
    return pl.pallas_call(
        _moba_mean_kernel,
        out_shape=jax.ShapeDtypeStruct((batch, nbm, MOBA_W), jnp.float32),
        grid=(batch, MOBA_W // 128),
        in_specs=[pl.BlockSpec((seq, 128), lambda b, j: (b, kb + j))],
        out_specs=pl.BlockSpec((1, nbm, 128), lambda b, j: (b, 0, j)),
        compiler_params=_params(("arbitrary", "arbitrary")),
        name="moba_kmean",
    )(proj)


def _moba_kernel(q_ref, ka_ref, va_ref, km_ref, o_ref, qa_ref, s_ref, p_ref, m_ref, acc_ref,
                 *, tq, tk, n_top, rc):
    i = pl.program_id(2)
    s0 = i * tq
    nbm = km_ref.shape[1]
    blk = lax.broadcasted_iota(jnp.int32, (1, nbm), 1)
    cur = (s0 + lax.broadcasted_iota(jnp.int32, (tq, 1), 0)) // MOBA_BLOCK
    for hh in range(2):
        q = q_ref[:, hh * HEAD_DIM:(hh + 1) * HEAD_DIM]
        km_hi, km_lo = _split_hi_lo(km_ref[0][:, hh * HEAD_DIM:(hh + 1) * HEAD_DIM])
        gs = _dot_nt(q, km_hi) + _dot_nt(q, km_lo)
        gs = jnp.where(blk < cur, gs, NEG)
        sel = _topk_mask(gs, n_top)
        open_blk = ((sel > 0.5) & (gs > NEG * 0.5)) | (blk == cur)
        bias = jnp.where(open_blk, 0.0, NEG).astype(CDT)
        qa_ref[hh] = jnp.concatenate([q, bias], axis=1)
    m_ref[...] = jnp.full(m_ref.shape, NEG, jnp.float32)
    acc_ref[...] = jnp.zeros(acc_ref.shape, jnp.float32)

    def step(kt, masked):
        start = pl.multiple_of(kt * tk, tk)
        for hh in range(2):
            s_ref[hh] = _dot(qa_ref[hh], ka_ref[0, hh, :, pl.ds(start, tk)])
        causal = (start + lax.broadcasted_iota(jnp.int32, (1, tk), 1), s0) if masked else None
        for hh in range(2):
            _flash_tile(s_ref.at[hh], p_ref.at[hh], m_ref.at[hh], acc_ref.at[hh], rc, tq, causal=causal)
            acc_ref[hh] += _dot(p_ref[hh], va_ref[0, hh, pl.ds(start, tk), :])

    def body(kt, carry):
        step(kt, False)
        return carry

    n_full = s0 // tk
    lax.fori_loop(0, n_full, body, 0)
    for d in range(tq // tk):
        step(n_full + d, True)
    outs = []
    for hh in range(2):
        acc = acc_ref[hh]
        outs.append(acc[:, :HEAD_DIM] / acc[:, HEAD_DIM:])
    o_ref[...] = jnp.concatenate(outs, axis=1).astype(o_ref.dtype)


def _moba(proj, mk_aug_t, mv_aug, kmean, batch, seq):
    tq = min(1024, seq)
    tk = 512
    nq = seq // tq
    nbm = seq // MOBA_BLOCK
    n_top = min(MOBA_TOPK, nbm)
    qb = COL_MQ // 128
    aug = HEAD_DIM + nbm
    return pl.pallas_call(
        functools.partial(_moba_kernel, tq=tq, tk=tk, n_top=n_top, rc=64),
        out_shape=jax.ShapeDtypeStruct((batch * seq, MOBA_W), CDT),
        grid=(batch, MOBA_HEADS // 2, nq),
        in_specs=[pl.BlockSpec((tq, 128), lambda b, p, i: (b * nq + i, qb + p)),
                  pl.BlockSpec((1, 2, aug, seq), lambda b, p, i: (b, p, 0, 0)),
                  pl.BlockSpec((1, 2, seq, 128), lambda b, p, i: (b, p, 0, 0)),
                  pl.BlockSpec((1, nbm, 128), lambda b, p, i: (b, 0, p))],
        out_specs=pl.BlockSpec((tq, 128), lambda b, p, i: (b * nq + i, p)),
        scratch_shapes=[pltpu.VMEM((2, tq, aug), CDT), pltpu.VMEM((2, tq, tk), jnp.float32),
                        pltpu.VMEM((2, tq, tk), CDT), pltpu.VMEM((2, tq, 128), jnp.float32),
                        pltpu.VMEM((2, tq, 128), jnp.float32)],
        compiler_params=_params(("arbitrary", "arbitrary", "arbitrary")),
        name="moba",
    )(proj, mk_aug_t, mv_aug, kmean)


def _merge_kernel(oc_ref, os_ref, ow_ref, om_ref, ng_ref, gn_ref, gm_ref, x_ref, ga_ref,
                  ex_ref, wun_ref, wum_ref, wo_ref, o_ref):
    gates = _sigmoid(ng_ref[...].astype(jnp.float32))
    hi, lo = _split_hi_lo(gates)
    e = _dot(hi, ex_ref[...]) + _dot(lo, ex_ref[...])
    o_nsa = (e[:, :NSA_Q] * oc_ref[...].astype(jnp.float32)
             + e[:, NSA_Q:2 * NSA_Q] * os_ref[...].astype(jnp.float32)
             + e[:, 2 * NSA_Q:] * ow_ref[...].astype(jnp.float32))
    y = (_sigmoid(gn_ref[...].astype(jnp.float32)) * _dot(o_nsa.astype(CDT), wun_ref[...])
         + _sigmoid(gm_ref[...].astype(jnp.float32)) * _dot(om_ref[...], wum_ref[...]))
    o_ref[...] = x_ref[...] + ga_ref[0] * _dot(y.astype(CDT), wo_ref[...])


def _merge(o_c, o_s, o_w, o_m, proj, x2d, ga, w_up_nsa, w_up_moba, w_out, seq):
    n, d = x2d.shape
    tm = min(512, seq)
    per_seq = seq // tm
    ng_w = COL_GN - COL_NG
    ex = np.zeros((ng_w, 3 * NSA_Q), np.float32)
    for h in range(NSA_HEADS):
        for j in range(3):
            ex[h * 3 + j, j * NSA_Q + h * HEAD_DIM: j * NSA_Q + (h + 1) * HEAD_DIM] = 1.0
    row = lambda i: (i, 0)
    const = lambda i: (0, 0)
    return pl.pallas_call(
        _merge_kernel,
        out_shape=jax.ShapeDtypeStruct((n, d), jnp.float32),
        grid=(n // tm,),
        in_specs=[pl.BlockSpec((tm, NSA_Q), row), pl.BlockSpec((tm, NSA_Q), row),
                  pl.BlockSpec((tm, NSA_Q), row), pl.BlockSpec((tm, MOBA_W), row),
                  pl.BlockSpec((tm, ng_w), lambda i: (i, COL_NG // ng_w)),
                  pl.BlockSpec((tm, d), lambda i: (i, COL_GN // d)),
                  pl.BlockSpec((tm, d), lambda i: (i, COL_GM // d)),
                  pl.BlockSpec((tm, d), row),
                  pl.BlockSpec((1, 1, d), lambda i: (i // per_seq, 0, 0)),
                  pl.BlockSpec((ng_w, 3 * NSA_Q), const),
                  pl.BlockSpec((NSA_Q, d), const), pl.BlockSpec((MOBA_W, d), const),
                  pl.BlockSpec((d, d), const)],
        out_specs=pl.BlockSpec((tm, d), row),
        compiler_params=_params(("arbitrary",)),
        name="mixer_merge",
    )(o_c, o_s, o_w, o_m, proj, proj, proj, x2d, ga, jnp.asarray(ex, CDT),
      w_up_nsa.astype(CDT), w_up_moba.astype(CDT), w_out.astype(CDT))


def _extract_max(x, ridx):
    m = jnp.max(x, axis=0, keepdims=True)
    idx = jnp.min(jnp.where(x == m, ridx, float(x.shape[0])), axis=0, keepdims=True)
    return m, ridx == idx


def _peer_score_kernel(q_ref, k1_ref, k2_ref, cnt_ref, rk_ref, e1_ref, e2_ref):
    half = PEER_QDIM // 2
    k = PEER_TOPK
    q = q_ref[...]
    s1_all = _dot_nt(k1_ref[...], q[:, :half])
    s2_all = _dot_nt(k2_ref[...], q[:, half:])
    t = 128
    ridx = lax.broadcasted_iota(jnp.int32, (PEER_NKEYS, t), 0).astype(jnp.float32)
    unranked = jnp.full((PEER_NKEYS, t), float(k), jnp.float32)
    for c0 in range(0, s1_all.shape[1], t):
        s1 = s1_all[:, c0:c0 + t]
        s2 = s2_all[:, c0:c0 + t]
        s1 = s1 - jnp.max(s1, axis=0, keepdims=True)
        s2 = s2 - jnp.max(s2, axis=0, keepdims=True)
        x1, x2, rk1, rk2, v1, v2 = s1, s2, unranked, unranked, [], []
        for i in range(k):
            m1, pick1 = _extract_max(x1, ridx)
            m2, pick2 = _extract_max(x2, ridx)
            x1, x2 = jnp.where(pick1, LOWEST, x1), jnp.where(pick2, LOWEST, x2)
            rk1, rk2 = jnp.where(pick1, float(i), rk1), jnp.where(pick2, float(i), rk2)
            v1.append(m1)
            v2.append(m2)
        v2_all = jnp.concatenate(v2, axis=0)

        pieces = [v1[i] + v2_all[0:k // (i + 1)] for i in range(k)]
        pad = -sum(p.shape[0] for p in pieces) % 8
        x = jnp.concatenate(pieces + [jnp.full((pad, t), LOWEST, jnp.float32)], axis=0)
        cidx = lax.broadcasted_iota(jnp.int32, x.shape, 0).astype(jnp.float32)
        vals = []
        for i in range(k):
            m, pick = _extract_max(x, cidx)
            x = jnp.where(pick, LOWEST, x)
            vals.append(m)
        tau = vals[k - 1]
        z = vals[0] - vals[0] + 1.0
        for i in range(1, k):
            z = z + jnp.exp(vals[i] - vals[0])
        cnt = jnp.zeros((PEER_NKEYS, t), jnp.float32)
        for i in range(k):
            n_i = jnp.sum(jnp.where(v1[i] + v2_all >= tau, 1.0, 0.0), axis=0, keepdims=True)
            cnt = jnp.where(rk1 == float(i), n_i, cnt)
        cnt_ref[0, :, c0:c0 + t] = cnt
        rk_ref[0, :, c0:c0 + t] = rk2.astype(rk_ref.dtype)
        e1_ref[0, :, c0:c0 + t] = jnp.exp(s1 - vals[0]) / z
        e2_ref[0, :, c0:c0 + t] = jnp.exp(s2).astype(e2_ref.dtype)


def _peer_scores(qp, k1, k2):
    n = qp.shape[0]
    tt = 256
    f32 = jax.ShapeDtypeStruct((PEER_HEADS, PEER_NKEYS, n), jnp.float32)
    cdt = jax.ShapeDtypeStruct((PEER_HEADS, PEER_NKEYS, n), CDT)
    big = pl.BlockSpec((1, PEER_NKEYS, tt), lambda i, h: (h, 0, i))
    return pl.pallas_call(
        _peer_score_kernel,
        out_shape=[f32, cdt, f32, cdt],
        grid=(n // tt, PEER_HEADS),
        in_specs=[pl.BlockSpec((tt, PEER_QDIM), lambda i, h: (i, h)),
                  pl.BlockSpec((PEER_NKEYS, PEER_QDIM // 2), lambda i, h: (0, 0)),
                  pl.BlockSpec((PEER_NKEYS, PEER_QDIM // 2), lambda i, h: (0, 0))],
        out_specs=[big, big, big, big],
        compiler_params=_params(("arbitrary", "arbitrary")),
        name="peer_scores",
    )(qp, k1.astype(CDT), k2.astype(CDT))


def _peer_expert_kernel(h_ref, u_ref, vt_ref, cnt_ref, rk_ref, e1_ref, e2_ref, x_ref, ga_ref,
                        o_ref, acc_ref, *, eb):
    j = pl.program_id(1)
    tt = h_ref.shape[0]

    @pl.when(j == 0)
    def _():
        acc_ref[...] = jnp.zeros(acc_ref.shape, jnp.float32)

    parts = []
    for al in range(eb // PEER_NKEYS):
        a = j * (eb // PEER_NKEYS) + al
        w = None
        for hd in range(PEER_HEADS):
            cnt_a = jnp.broadcast_to(cnt_ref[hd, pl.ds(a, 1), :].astype(CDT), (PEER_NKEYS, tt))
            e1_a = jnp.broadcast_to(e1_ref[hd, pl.ds(a, 1), :].astype(CDT), (PEER_NKEYS, tt))
            contrib = jnp.where(rk_ref[hd] < cnt_a, e2_ref[hd] * e1_a, jnp.zeros((), CDT))
            w = contrib if w is None else w + contrib
        parts.append(w)
    act = _gelu(_dot_nt(u_ref[...], h_ref[...])).astype(CDT)
    pw = jnp.concatenate(parts, axis=0) * act
    acc_ref[...] += _dot(vt_ref[...], pw)

    @pl.when(j == pl.num_programs(1) - 1)
    def _():
        o_ref[...] = x_ref[...] + ga_ref[0] * acc_ref[...].T


def _peer_experts(h2, u, v_t, cnt, rk2, e1, e2, x2d, ga, seq):
    n, d = x2d.shape
    tt = min(512, seq)
    per_seq = seq // tt
    eb = 512
    n_blk = u.shape[0] // eb
    big = pl.BlockSpec((PEER_HEADS, PEER_NKEYS, tt), lambda i, j: (0, 0, i))
    return pl.pallas_call(
        functools.partial(_peer_expert_kernel, eb=eb),
        out_shape=jax.ShapeDtypeStruct((n, d), jnp.float32),
        grid=(n // tt, n_blk),
        in_specs=[pl.BlockSpec((tt, d), lambda i, j: (i, 0)),
                  pl.BlockSpec((eb, d), lambda i, j: (j, 0)),
                  pl.BlockSpec((d, eb), lambda i, j: (0, j)),
                  big, big, big, big,
                  pl.BlockSpec((tt, d), lambda i, j: (i, 0)),
                  pl.BlockSpec((1, 1, d), lambda i, j: (i // per_seq, 0, 0))],
        out_specs=pl.BlockSpec((tt, d), lambda i, j: (i, 0)),
        scratch_shapes=[pltpu.VMEM((d, tt), jnp.float32)],
        compiler_params=_params(("arbitrary", "arbitrary")),
        name="peer_experts",
    )(h2, u, v_t, cnt, rk2, e1, e2, x2d, ga)


def _rms_kernel(x_ref, g_ref, o_ref):
    x = x_ref[...]
    ms = jnp.mean(x * x, axis=-1, keepdims=True)
    o_ref[...] = x * lax.rsqrt(ms + RMS_EPS) * g_ref[...]


def _final_norm(x2d, g):
    n, d = x2d.shape
    tm = 512
    return pl.pallas_call(
        _rms_kernel,
        out_shape=jax.ShapeDtypeStruct((n, d), jnp.float32),
        grid=(n // tm,),
        in_specs=[pl.BlockSpec((tm, d), lambda i: (i, 0)), pl.BlockSpec((1, d), lambda i: (0, 0))],
        out_specs=pl.BlockSpec((tm, d), lambda i: (i, 0)),
        compiler_params=_params(("arbitrary",)),
        name="final_rmsnorm",
    )(x2d, g.reshape(1, d))


def _key_major(proj, col, heads, batch, seq):
    k = proj[:, col:col + heads * HEAD_DIM].reshape(batch, seq, heads, HEAD_DIM)
    return jnp.transpose(k, (0, 2, 3, 1))


def _value_with_ones(proj, col, heads, batch, seq):
    v = proj[:, col:col + heads * HEAD_DIM].reshape(batch, seq, heads, HEAD_DIM)
    v = jnp.transpose(v, (0, 2, 1, 3))
    return jnp.concatenate([v, jnp.ones_like(v)], axis=-1)


def kernel(x, c, w_ada, b_ada, g_attn, g_ffn, w_in, cmp_pe, cmp_w1, cmp_w2, w_up_nsa, w_up_moba, w_out,
           peer_wq, peer_k1, peer_k2, peer_u, peer_v, g_final):
    batch, seq, d = x.shape
    depth = w_ada.shape[0]
    n = batch * seq
    nbm = seq // MOBA_BLOCK
    x2d = x.reshape(n, d)
    mod = _adaln_mod(c, w_ada, b_ada)
    rope = _rope_tables(seq)
    blk_onehot = (jnp.arange(nbm)[:, None] == jnp.arange(seq)[None, :] // MOBA_BLOCK).astype(CDT)
    for l in range(depth):
        sh1, sc1, ga1, sh2, sc2, ga2 = [m.reshape(batch, 1, d) for m in jnp.split(mod[l], 6, axis=-1)]
        proj = _norm_mod_matmul(x2d, g_attn[l], sc1, sh1, _reorder_w_in(w_in[l]), seq,
                                rope=rope, n_rope=ROPE_COLS // PROJ_TN)
        cmp_kv = _nsa_compress(proj, cmp_pe[l], cmp_w1[l], cmp_w2[l], batch, seq)
        o_c, selb = _nsa_compressed(proj, cmp_kv, batch, seq)
        o_s = _nsa_selected(proj, _key_major(proj, COL_KS, NSA_KV_HEADS, batch, seq),
                            _value_with_ones(proj, COL_VS, NSA_KV_HEADS, batch, seq), selb, batch, seq)
        o_w = _nsa_window(proj, _key_major(proj, COL_KW, NSA_KV_HEADS, batch, seq), batch, seq)
        mk_t = _key_major(proj, COL_MK, MOBA_HEADS, batch, seq)
        mk_aug = jnp.concatenate(
            [mk_t, jnp.broadcast_to(blk_onehot, (batch, MOBA_HEADS, nbm, seq))], axis=2)
        o_m = _moba(proj, mk_aug, _value_with_ones(proj, COL_MV, MOBA_HEADS, batch, seq),
                    _moba_kmean(proj, batch, seq), batch, seq)
        x2d = _merge(o_c, o_s, o_w, o_m, proj, x2d, ga1, w_up_nsa[l], w_up_moba[l], w_out[l], seq)
        qp, h2 = _norm_mod_matmul(x2d, g_ffn[l], sc2, sh2, peer_wq[l].astype(CDT), seq, emit_h=True)
        cnt, rk2, e1, e2 = _peer_scores(qp, peer_k1[l], peer_k2[l])
        x2d = _peer_experts(h2, peer_u[l].astype(CDT), peer_v[l].T.astype(CDT), cnt, rk2, e1, e2,
                            x2d, ga2, seq)
    return _final_norm(x2d, g_final).reshape(batch, seq, d)
```

```python
import functools

import jax
import jax.numpy as jnp
import numpy as np
from jax import lax
from jax.experimental import pallas as pl
from jax.experimental.pallas import tpu as pltpu

D_MODEL = 1024
HEAD_DIM = 64
ROT_DIM = HEAD_DIM // 4
ROPE_THETA = 500000.0
NSA_HEADS = 8
NSA_KV_HEADS = 2
NSA_GROUP = NSA_HEADS // NSA_KV_HEADS
CMP_LEN = 32
CMP_STRIDE = 16
CMP_HID = 2 * HEAD_DIM
SLC_LEN = 64
SLC_TOPN = 16
WIN = 512
MOBA_HEADS = 8
MOBA_BLOCK = 256
MOBA_TOPK = 3
PEER_HEADS = 8
PEER_NKEYS = 128
PEER_QDIM = 256
PEER_TOPK = 16
RMS_EPS = 1e-6
NEG = -1e30
SEL_FORCE = 1e4
LOWEST = -3.0e38
LOG2E = 1.4426950408889634

NSA_Q = NSA_HEADS * HEAD_DIM
NSA_KV = NSA_KV_HEADS * HEAD_DIM
MOBA_W = MOBA_HEADS * HEAD_DIM

CDT = jnp.bfloat16
V7X_VMEM_LIMIT = 56 * 1024 * 1024

COL_QROT = 0
COL_KS = 512
COL_KW = 640
COL_MQ = 768
COL_MK = 1280
ROPE_COLS = 1792
COL_NG = 1792
COL_GN = 2048
COL_GM = 3072
COL_QRAW = 4096
COL_KC = 4608
COL_VC = 4736
COL_VS = 4864
COL_VW = 4992
COL_MV = 5120
PROJ_COLS = 5632
PROJ_TN = 256


def _params(sem):
    return pltpu.CompilerParams(dimension_semantics=sem, vmem_limit_bytes=V7X_VMEM_LIMIT)


def _dot(a, b):
    return jnp.dot(a, b, preferred_element_type=jnp.float32)


def _dot_nt(a, b):
    return lax.dot_general(a, b, (((1,), (1,)), ((), ())), preferred_element_type=jnp.float32)


def _split_hi_lo(x):
    hi = x.astype(CDT)
    lo = (x - hi.astype(jnp.float32)).astype(CDT)
    return hi, lo


def _gelu(x):
    return 0.5 * x * (1.0 + jnp.tanh(0.7978845608028654 * (x + 0.044715 * (x * x * x))))


def _sigmoid(x):
    return 1.0 / (1.0 + jnp.exp(-x))


def _mod_kernel(c_ref, w_ref, b_ref, o_ref):
    c = c_ref[...]
    sc = c * _sigmoid(c)
    o_ref[0] = jnp.dot(sc, w_ref[0], preferred_element_type=jnp.float32,
                       precision=lax.Precision.HIGHEST) + b_ref[0]


def _adaln_mod(c, w_ada, b_ada):
    depth, d, six_d = w_ada.shape
    b = c.shape[0]
    rows = 8
    c_pad = jnp.zeros((rows, d), jnp.float32).at[:b].set(c)
    tn = 1024
    out = pl.pallas_call(
        _mod_kernel,
        out_shape=jax.ShapeDtypeStruct((depth, rows, six_d), jnp.float32),
        grid=(depth, six_d // tn),
        in_specs=[pl.BlockSpec((rows, d), lambda l, j: (0, 0)),
                  pl.BlockSpec((1, d, tn), lambda l, j: (l, 0, j)),
                  pl.BlockSpec((1, 1, tn), lambda l, j: (l, 0, j))],
        out_specs=pl.BlockSpec((1, rows, tn), lambda l, j: (l, 0, j)),
        compiler_params=_params(("arbitrary", "arbitrary")),
        name="adaln_mod",
    )(c_pad, w_ada, b_ada.reshape(depth, 1, six_d))
    return out[:, :b]


def _nmm_kernel(*refs, n_rope, emit_h):
    if n_rope:
        x_ref, g_ref, sc_ref, sh_ref, w_ref, cos_ref, sa_ref, sb_ref = refs[:8]
        rest = refs[8:]
    else:
        x_ref, g_ref, sc_ref, sh_ref, w_ref = refs[:5]
        rest = refs[5:]
    if emit_h:
        o_ref, ho_ref, h_ref = rest
    else:
        o_ref, h_ref = rest
    j = pl.program_id(1)

    @pl.when(j == 0)
    def _():
        x = x_ref[...]
        ms = jnp.mean(x * x, axis=-1, keepdims=True)
        y = x * lax.rsqrt(ms + RMS_EPS) * g_ref[...]
        h = (y * (1.0 + sc_ref[0]) + sh_ref[0]).astype(h_ref.dtype)
        h_ref[...] = h
        if emit_h:
            ho_ref[...] = h

    acc = _dot(h_ref[...], w_ref[...])
    if n_rope:
        @pl.when(j < n_rope)
        def _():
            cos, sa, sb = cos_ref[...], sa_ref[...], sb_ref[...]
            parts = []
            for c0 in range(0, acc.shape[1], 128):
                a = acc[:, c0:c0 + 128]
                parts.append(a * cos + pltpu.roll(a, 128 - ROT_DIM // 2, axis=1) * sa
                             + pltpu.roll(a, ROT_DIM // 2, axis=1) * sb)
            o_ref[...] = jnp.concatenate(parts, axis=1).astype(o_ref.dtype)

        @pl.when(j >= n_rope)
        def _():
            o_ref[...] = acc.astype(o_ref.dtype)
    else:
        o_ref[...] = acc.astype(o_ref.dtype)


def _norm_mod_matmul(x2d, g, sc, sh, w, seq, rope=None, n_rope=0, emit_h=False, tn=PROJ_TN):
    n, d = x2d.shape
    cols = w.shape[1]
    tm = min(1024, seq)
    per_seq = seq // tm
    in_specs = [pl.BlockSpec((tm, d), lambda i, j: (i, 0)),
                pl.BlockSpec((1, d), lambda i, j: (0, 0)),
                pl.BlockSpec((1, 1, d), lambda i, j: (i // per_seq, 0, 0)),
                pl.BlockSpec((1, 1, d), lambda i, j: (i // per_seq, 0, 0)),
                pl.BlockSpec((d, tn), lambda i, j: (0, j))]
    args = [x2d, g.reshape(1, d), sc, sh, w]
    if n_rope:
        in_specs += [pl.BlockSpec((tm, 128), lambda i, j: (i % per_seq, 0))] * 3
        args += list(rope)
    out_shape = [jax.ShapeDtypeStruct((n, cols), CDT)]
    out_specs = [pl.BlockSpec((tm, tn), lambda i, j: (i, j))]
    if emit_h:
        out_shape.append(jax.ShapeDtypeStruct((n, d), CDT))
        out_specs.append(pl.BlockSpec((tm, d), lambda i, j: (i, 0)))
    res = pl.pallas_call(
        functools.partial(_nmm_kernel, n_rope=n_rope, emit_h=emit_h),
        out_shape=out_shape,
        grid=(n // tm, cols // tn),
        in_specs=in_specs,
        out_specs=out_specs,
        scratch_shapes=[pltpu.VMEM((tm, d), CDT)],
        compiler_params=_params(("arbitrary", "arbitrary")),
        name="norm_mod_matmul",
    )(*args)
    return res if emit_h else res[0]


def _rope_tables(seq):
    half = ROT_DIM // 2
    inv = ROPE_THETA ** (-jnp.arange(half, dtype=jnp.float32) / half)
    ang = jnp.arange(seq, dtype=jnp.float32)[:, None] * inv[None, :]
    cos, sin = jnp.cos(ang), jnp.sin(ang)
    ones = jnp.ones((seq, HEAD_DIM - ROT_DIM), jnp.float32)
    zeros = jnp.zeros((seq, HEAD_DIM - ROT_DIM), jnp.float32)
    zh = jnp.zeros((seq, half), jnp.float32)
    c64 = jnp.concatenate([cos, cos, ones], axis=1)
    sa64 = jnp.concatenate([-sin, zh, zeros], axis=1)
    sb64 = jnp.concatenate([zh, sin, zeros], axis=1)
    return tuple(jnp.concatenate([t, t], axis=1) for t in (c64, sa64, sb64))


def _reorder_w_in(w_in):
    o = 0
    pieces = {}
    for name, size in (("nq", NSA_Q), ("kc", NSA_KV), ("vc", NSA_KV), ("ks", NSA_KV), ("vs", NSA_KV),
                       ("kw", NSA_KV), ("vw", NSA_KV), ("ng", 3 * NSA_HEADS), ("mq", MOBA_W),
                       ("mk", MOBA_W), ("mv", MOBA_W), ("gn", D_MODEL), ("gm", D_MODEL)):
        pieces[name] = w_in[:, o:o + size]
        o += size
    scale = HEAD_DIM ** -0.5 * LOG2E
    pad = jnp.zeros((w_in.shape[0], COL_GN - COL_NG - 3 * NSA_HEADS), w_in.dtype)
    w = jnp.concatenate([pieces["nq"] * scale, pieces["ks"], pieces["kw"], pieces["mq"] * scale,
                         pieces["mk"], pieces["ng"], pad, pieces["gn"], pieces["gm"],
                         pieces["nq"] * scale, pieces["kc"], pieces["vc"], pieces["vs"],
                         pieces["vw"], pieces["mv"]], axis=1)
    assert w.shape[1] == PROJ_COLS
    return w.astype(CDT)


def _compress_kernel(x_ref, w1c_ref, pe_ref, w1_ref, w2_ref, o_ref):
    nb = x_ref.shape[3]
    ab = _dot(x_ref[0, 0, 0], w1c_ref[0])
    a = ab[:, :CMP_HID]
    b_next = pltpu.roll(ab[:, CMP_HID:], nb - 1, axis=0)
    bias = _dot(pe_ref[0], w1_ref[0])[0:1]
    hid = _gelu(a + b_next + bias)
    o_ref[0, 0, 0] = _dot(hid.astype(CDT), w2_ref[0]).astype(o_ref.dtype)


def _nsa_compress(proj, cmp_pe, cmp_w1, cmp_w2, batch, seq):
    nb = seq // CMP_STRIDE
    half = CMP_STRIDE * HEAD_DIM
    kcvc = proj[:, COL_KC:COL_KC + 2 * NSA_KV].reshape(batch, nb, CMP_STRIDE, 2, NSA_KV_HEADS, HEAD_DIM)
    x = jnp.transpose(kcvc, (3, 0, 4, 1, 2, 5)).reshape(2, batch, NSA_KV_HEADS, nb, half)
    w1c = jnp.concatenate([cmp_w1[:, :half], cmp_w1[:, half:]], axis=2).astype(CDT)
    pe = jnp.broadcast_to(cmp_pe.reshape(2, 1, CMP_LEN * HEAD_DIM), (2, 8, CMP_LEN * HEAD_DIM)).astype(CDT)
    return pl.pallas_call(
        _compress_kernel,
        out_shape=jax.ShapeDtypeStruct((2, batch, NSA_KV_HEADS, nb, HEAD_DIM), CDT),
        grid=(2, batch, NSA_KV_HEADS),
        in_specs=[pl.BlockSpec((1, 1, 1, nb, half), lambda w, b, k: (w, b, k, 0, 0)),
                  pl.BlockSpec((1, half, 2 * CMP_HID), lambda w, b, k: (w, 0, 0)),
                  pl.BlockSpec((1, 8, 2 * half), lambda w, b, k: (w, 0, 0)),
                  pl.BlockSpec((1, 2 * half, CMP_HID), lambda w, b, k: (w, 0, 0)),
                  pl.BlockSpec((1, CMP_HID, HEAD_DIM), lambda w, b, k: (w, 0, 0))],
        out_specs=pl.BlockSpec((1, 1, 1, nb, HEAD_DIM), lambda w, b, k: (w, b, k, 0, 0)),
        compiler_params=_params(("arbitrary", "arbitrary", "arbitrary")),
        name="nsa_compress",
    )(x, w1c, pe, cmp_w1.astype(CDT), cmp_w2.astype(CDT))


def _stack_heads(qblk):
    return jnp.concatenate([qblk[:, g * HEAD_DIM:(g + 1) * HEAD_DIM] for g in range(NSA_GROUP)], axis=0)


def _unstack_heads(o, tq):
    return jnp.concatenate([o[g * tq:(g + 1) * tq] for g in range(NSA_GROUP)], axis=1)


def _topk_mask(score, n_sel):
    shape = score.shape
    lane = lax.broadcasted_iota(jnp.int32, shape, 1).astype(jnp.float32)
    width = float(shape[1])

    def body(_, carry):
        sc, sel = carry
        m = jnp.max(sc, axis=-1, keepdims=True)
        idx = jnp.min(jnp.where(sc == m, lane, width), axis=-1, keepdims=True)
        pick = lane == idx
        return jnp.where(pick, LOWEST, sc), jnp.where(pick, 1.0, sel)

    _, sel = lax.fori_loop(0, n_sel, body, (score, jnp.zeros(shape, jnp.float32)))
    return sel


def _nsa_cmp_kernel(q_ref, kc_ref, vc_ref, ov_ref, oc_ref, sb_ref, imp_ref, *, tq, n_sel, n_var):
    i = pl.program_id(2)
    s0 = i * tq
    nb = kc_ref.shape[3]
    ns = ov_ref.shape[1]
    tq_col = s0 + lax.broadcasted_iota(jnp.int32, (tq, 1), 0)
    t4 = jnp.concatenate([tq_col] * NSA_GROUP, axis=0)

    def attend(nk):
        q4 = _stack_heads(q_ref[...])
        s = _dot_nt(q4, kc_ref[0, 0, 0, :nk, :])
        cend = lax.broadcasted_iota(jnp.int32, (1, nk), 1) * CMP_STRIDE + (CMP_LEN - 1)
        mask = cend <= t4
        s = jnp.where(mask, s, NEG)
        m = jnp.max(s, axis=-1, keepdims=True)
        p = jnp.where(mask, jnp.exp2(s - m), 0.0)
        p = p / jnp.maximum(jnp.sum(p, axis=-1, keepdims=True), 1e-30)
        o = _dot(p.astype(CDT), vc_ref[0, 0, 0, :nk, :])
        oc_ref[...] = _unstack_heads(o, tq).astype(oc_ref.dtype)
        psum = p[0:tq]
        for g in range(1, NSA_GROUP):
            psum = psum + p[g * tq:(g + 1) * tq]
        hi, lo = _split_hi_lo(psum)
        imp_ref[...] = _dot(hi, ov_ref[:nk, :]) + _dot(lo, ov_ref[:nk, :])

    tiles_per_var = pl.num_programs(2) // n_var
    for v in range(n_var):
        @pl.when(i // tiles_per_var == v)
        def _(v=v):
            attend(nb * (v + 1) // n_var)

    blk = lax.broadcasted_iota(jnp.int32, (ns, 1), 0)
    cur = (s0 + lax.broadcasted_iota(jnp.int32, (1, tq), 1)) // SLC_LEN
    valid = blk <= cur
    forced = (blk == 0) | (blk == cur) | (blk == cur - 1)
    score = jnp.where(valid, jnp.where(forced, SEL_FORCE, imp_ref[...].T), NEG)
    ridx = lax.broadcasted_iota(jnp.int32, (ns, tq), 0).astype(jnp.float32)

    def body(_, carry):
        sc, sel = carry
        _, pick = _extract_max(sc, ridx)
        return jnp.where(pick, LOWEST, sc), jnp.where(pick, 1.0, sel)

    _, sel = lax.fori_loop(0, n_sel, body, (score, jnp.zeros((ns, tq), jnp.float32)))
    bias = jnp.where((sel > 0.5) & valid, 0.0, NEG).T
    extra = sb_ref.shape[3] - ns
    if extra:
        bias = jnp.concatenate([bias, jnp.full((tq, extra), NEG, jnp.float32)], axis=1)
    sb_ref[0, 0] = bias.astype(sb_ref.dtype)


def _nsa_compressed(proj, cmp_kv, batch, seq):
    tq = 128
    nq = seq // tq
    nb = seq // CMP_STRIDE
    ns = seq // SLC_LEN
    n_sel = min(SLC_TOPN, ns)
    c_start = np.arange(nb) * CMP_STRIDE
    s_start = np.arange(ns) * SLC_LEN
    ov = np.maximum(np.minimum(c_start[:, None] + CMP_LEN, s_start[None, :] + SLC_LEN)
                    - np.maximum(c_start[:, None], s_start[None, :]), 0).astype(np.float32) / CMP_LEN
    ov[nb - 1] = 0.0
    qb = COL_QRAW // (NSA_GROUP * HEAD_DIM)
    return pl.pallas_call(
        functools.partial(_nsa_cmp_kernel, tq=tq, n_sel=n_sel,
                          n_var=4 if (nb % (4 * 128) == 0 and nq % 4 == 0) else 1),
        out_shape=[jax.ShapeDtypeStruct((batch * seq, NSA_Q), CDT),
                   jax.ShapeDtypeStruct((batch, NSA_KV_HEADS, seq, max(ns, 128)), CDT)],
        grid=(batch, NSA_KV_HEADS, nq),
        in_specs=[pl.BlockSpec((tq, NSA_GROUP * HEAD_DIM), lambda b, k, i: (b * nq + i, qb + k)),
                  pl.BlockSpec((1, 1, 1, nb, HEAD_DIM), lambda b, k, i: (0, b, k, 0, 0)),
                  pl.BlockSpec((1, 1, 1, nb, HEAD_DIM), lambda b, k, i: (1, b, k, 0, 0)),
                  pl.BlockSpec((nb, ns), lambda b, k, i: (0, 0))],
        out_specs=[pl.BlockSpec((tq, NSA_GROUP * HEAD_DIM), lambda b, k, i: (b * nq + i, k)),
                   pl.BlockSpec((1, 1, tq, max(ns, 128)), lambda b, k, i: (b, k, i, 0))],
        scratch_shapes=[pltpu.VMEM((tq, ns), jnp.float32)],
        compiler_params=_params(("arbitrary", "arbitrary", "arbitrary")),
        name="nsa_compressed",
    )(proj, cmp_kv, cmp_kv, jnp.asarray(ov, CDT))


def _flash_tile(s_ref, p_ref, m_ref, acc_ref, rc, tq, bias_ref=None, causal=None):
    rows, tk = s_ref.shape

    def tile(r0, c0):
        s = s_ref[r0:r0 + rc, c0:c0 + 128]
        if bias_ref is not None:
            s = s + bias_ref[r0 % tq:r0 % tq + rc, c0:c0 + 128]
        if causal is not None:
            kpos, t0 = causal
            t = t0 + (r0 % tq) + lax.broadcasted_iota(jnp.int32, (rc, 1), 0)
            s = jnp.where(kpos[:, c0:c0 + 128] <= t, s, NEG)
        return s

    for r0 in range(0, rows, rc):
        mx = tile(r0, 0)
        for c0 in range(128, tk, 128):
            mx = jnp.maximum(mx, tile(r0, c0))
        m_prev = m_ref[r0:r0 + rc, :]
        m_new = jnp.maximum(m_prev, jnp.max(mx, axis=-1, keepdims=True))
        acc_ref[r0:r0 + rc, :] = acc_ref[r0:r0 + rc, :] * jnp.exp2(m_prev - m_new)
        m_ref[r0:r0 + rc, :] = m_new
    for r0 in range(0, rows, rc):
        m_new = m_ref[r0:r0 + rc, :]
        for c0 in range(0, tk, 128):
            p_ref[r0:r0 + rc, c0:c0 + 128] = jnp.exp2(tile(r0, c0) - m_new).astype(p_ref.dtype)


def _nsa_sel_kernel(q_ref, kt_ref, va_ref, sb_ref, o_ref,
                    qp_ref, q4_ref, sbf_ref, s_ref, p_ref, m_ref, acc_ref, *, tq, tk, rc):
    i = pl.program_id(2)
    s0 = i * tq
    nsp = sb_ref.shape[3]
    per_tile = tk // SLC_LEN
    q4 = _stack_heads(q_ref[...])
    qp_ref[...] = jnp.concatenate([q4, jnp.zeros_like(q4)], axis=1)
    sbf_ref[...] = sb_ref[0, 0].astype(jnp.float32)
    lane = lax.broadcasted_iota(jnp.int32, (tq, 128), 1)
    bias_lanes = (lane >= HEAD_DIM) & (lane < HEAD_DIM + per_tile)
    m_ref[...] = jnp.full(m_ref.shape, NEG, jnp.float32)
    acc_ref[...] = jnp.zeros(acc_ref.shape, jnp.float32)

    def step(kt, masked):
        start = pl.multiple_of(kt * tk, tk)
        shift = (HEAD_DIM - kt * per_tile + nsp) % nsp
        rolled = pltpu.roll(sbf_ref[...], shift, axis=1)[:, :128].astype(q4_ref.dtype)
        for g in range(NSA_GROUP):
            q4_ref[g * tq:(g + 1) * tq, :] = jnp.where(bias_lanes, rolled, qp_ref[g * tq:(g + 1) * tq, :])
        s_ref[...] = _dot(q4_ref[...], kt_ref[0, 0, :, pl.ds(start, tk)])
        causal = (start + lax.broadcasted_iota(jnp.int32, (1, tk), 1), s0) if masked else None
        _flash_tile(s_ref, p_ref, m_ref, acc_ref, rc, tq, causal=causal)
        acc_ref[...] += _dot(p_ref[...], va_ref[0, 0, pl.ds(start, tk), :])

    def body(kt, carry):
        step(kt, False)
        return carry

    n_full = s0 // tk
    lax.fori_loop(0, n_full, body, 0)
    step(n_full, True)
    acc = acc_ref[...]
    o = acc[:, :HEAD_DIM] / acc[:, HEAD_DIM:]
    o_ref[...] = _unstack_heads(o, tq).astype(o_ref.dtype)


def _nsa_selected(proj, ks_t, vs_aug, selb, batch, seq):
    tq = 256
    tk = 512
    nq = seq // tq
    nsp = selb.shape[3]
    per_tile = tk // SLC_LEN
    blk_in_tile = (jnp.arange(seq) // SLC_LEN) % per_tile
    onehot = (jnp.arange(128 - HEAD_DIM)[:, None] == blk_in_tile[None, :]).astype(CDT)
    k_aug = jnp.concatenate(
        [ks_t, jnp.broadcast_to(onehot, (batch, NSA_KV_HEADS, 128 - HEAD_DIM, seq))], axis=2)
    qb = COL_QROT // (NSA_GROUP * HEAD_DIM)
    rows = NSA_GROUP * tq
    return pl.pallas_call(
        functools.partial(_nsa_sel_kernel, tq=tq, tk=tk, rc=64),
        out_shape=jax.ShapeDtypeStruct((batch * seq, NSA_Q), CDT),
        grid=(batch, NSA_KV_HEADS, nq),
        in_specs=[pl.BlockSpec((tq, NSA_GROUP * HEAD_DIM), lambda b, k, i: (b * nq + i, qb + k)),
                  pl.BlockSpec((1, 1, 128, seq), lambda b, k, i: (b, k, 0, 0)),
                  pl.BlockSpec((1, 1, seq, 128), lambda b, k, i: (b, k, 0, 0)),
                  pl.BlockSpec((1, 1, tq, nsp), lambda b, k, i: (b, k, i, 0))],
        out_specs=pl.BlockSpec((tq, NSA_GROUP * HEAD_DIM), lambda b, k, i: (b * nq + i, k)),
        scratch_shapes=[pltpu.VMEM((rows, 128), CDT), pltpu.VMEM((rows, 128), CDT),
                        pltpu.VMEM((tq, nsp), jnp.float32), pltpu.VMEM((rows, tk), jnp.float32),
                        pltpu.VMEM((rows, tk), CDT),
                        pltpu.VMEM((rows, 128), jnp.float32), pltpu.VMEM((rows, 128), jnp.float32)],
        compiler_params=_params(("arbitrary", "arbitrary", "arbitrary")),
        name="nsa_selected",
    )(proj, k_aug, vs_aug, selb)


def _nsa_win_kernel(q_ref, kt_ref, v_ref, o_ref, *, tq):
    k = pl.program_id(1)
    i = pl.program_id(2)
    s0 = i * tq
    span = WIN + tq
    start = pl.multiple_of(jnp.maximum(s0 - WIN, 0), tq)
    q4 = _stack_heads(q_ref[...])
    s = _dot(q4, kt_ref[0, 0, :, pl.ds(start, span)])
    tq_col = s0 + lax.broadcasted_iota(jnp.int32, (tq, 1), 0)
    t4 = jnp.concatenate([tq_col] * NSA_GROUP, axis=0)
    kpos = start + lax.broadcasted_iota(jnp.int32, (1, span), 1)
    mask = (kpos <= t4) & (kpos > t4 - WIN)
    s = jnp.where(mask, s, NEG)
    m = jnp.max(s, axis=-1, keepdims=True)
    p = jnp.where(mask, jnp.exp2(s - m), 0.0)
    l = jnp.maximum(jnp.sum(p, axis=-1, keepdims=True), 1e-30)
    o = _dot(p.astype(CDT), v_ref[pl.ds(start, span), :]) / l
    o = jnp.where(k == 0, o[:, :HEAD_DIM], o[:, HEAD_DIM:])
    o_ref[...] = _unstack_heads(o, tq).astype(o_ref.dtype)


def _nsa_window(proj, kw_t, batch, seq):
    tq = 256
    nq = seq // tq
    qb = COL_QROT // (NSA_GROUP * HEAD_DIM)
    vb = COL_VW // 128
    return pl.pallas_call(
        functools.partial(_nsa_win_kernel, tq=tq),
        out_shape=jax.ShapeDtypeStruct((batch * seq, NSA_Q), CDT),
        grid=(batch, NSA_KV_HEADS, nq),
        in_specs=[pl.BlockSpec((tq, NSA_GROUP * HEAD_DIM), lambda b, k, i: (b * nq + i, qb + k)),
                  pl.BlockSpec((1, 1, HEAD_DIM, seq), lambda b, k, i: (b, k, 0, 0)),
                  pl.BlockSpec((seq, 128), lambda b, k, i: (b, vb))],
        out_specs=pl.BlockSpec((tq, NSA_GROUP * HEAD_DIM), lambda b, k, i: (b * nq + i, k)),
        compiler_params=_params(("arbitrary", "arbitrary", "arbitrary")),
        name="nsa_window",
    )(proj, kw_t, proj)


def _moba_mean_kernel(k_ref, o_ref):
    seq, w = k_ref.shape
    nbm = seq // MOBA_BLOCK
    k = k_ref[...].astype(jnp.float32).reshape(nbm, MOBA_BLOCK, w)
    o_ref[0] = jnp.sum(k, axis=1) * (1.0 / MOBA_BLOCK)


def _moba_kmean(proj, batch, seq):
    nbm = seq // MOBA_BLOCK
    kb = COL_MK // 128
    return pl.pallas_call(
        _moba_mean_kernel,
        out_shape=jax.ShapeDtypeStruct((batch, nbm, MOBA_W), jnp.float32),
        grid=(batch, MOBA_W // 128),
        in_specs=[pl.BlockSpec((seq, 128), lambda b, j: (b, kb + j))],
        out_specs=pl.BlockSpec((1, nbm, 128), lambda b, j: (b, 0, j)),
        compiler_params=_params(("arbitrary", "arbitrary")),
        name="moba_kmean",
    )(proj)


def _moba_kernel(q_ref, ka_ref, va_ref, km_ref, o_ref, qa_ref, s_ref, p_ref, m_ref, acc_ref,
                 *, tq, tk, n_top, rc):
    i = pl.program_id(2)
    s0 = i * tq
    nbm = km_ref.shape[1]
    blk = lax.broadcasted_iota(jnp.int32, (nbm, 1), 0)
    cur = (s0 + lax.broadcasted_iota(jnp.int32, (1, tq), 1)) // MOBA_BLOCK
    ridx = lax.broadcasted_iota(jnp.int32, (nbm, tq), 0).astype(jnp.float32)
    for hh in range(2):
        q = q_ref[:, hh * HEAD_DIM:(hh + 1) * HEAD_DIM]
        km_hi, km_lo = _split_hi_lo(km_ref[0][:, hh * HEAD_DIM:(hh + 1) * HEAD_DIM])
        gs = _dot_nt(km_hi, q) + _dot_nt(km_lo, q)
        gs = jnp.where(blk < cur, gs, NEG)

        def body(_, carry):
            sc, sel = carry
            _, pick = _extract_max(sc, ridx)
            return jnp.where(pick, LOWEST, sc), jnp.where(pick, 1.0, sel)

        _, sel = lax.fori_loop(0, n_top, body, (gs, jnp.zeros((nbm, tq), jnp.float32)))
        open_blk = ((sel > 0.5) & (gs > NEG * 0.5)) | (blk == cur)
        bias_t = jnp.where(open_blk, 0.0, NEG)
        lead = -nbm % 128
        if lead:
            bias_t = jnp.concatenate([jnp.zeros((lead, tq), jnp.float32), bias_t], axis=0)
        bias = bias_t.T[:, lead:].astype(CDT)
        qa_ref[hh] = jnp.concatenate([q, bias], axis=1)
    m_ref[...] = jnp.full(m_ref.shape, NEG, jnp.float32)
    acc_ref[...] = jnp.zeros(acc_ref.shape, jnp.float32)

    def step(kt, masked):
        start = pl.multiple_of(kt * tk, tk)
        for hh in range(2):
            s_ref[hh] = _dot(qa_ref[hh], ka_ref[0, hh, :, pl.ds(start, tk)])
        causal = (start + lax.broadcasted_iota(jnp.int32, (1, tk), 1), s0) if masked else None
        for hh in range(2):
            _flash_tile(s_ref.at[hh], p_ref.at[hh], m_ref.at[hh], acc_ref.at[hh], rc, tq, causal=causal)
            acc_ref[hh] += _dot(p_ref[hh], va_ref[0, hh, pl.ds(start, tk), :])

    def body(kt, carry):
        step(kt, False)
        return carry

    n_full = s0 // tk
    lax.fori_loop(0, n_full, body, 0)
    for d in range(tq // tk):
        step(n_full + d, True)
    outs = []
    for hh in range(2):
        acc = acc_ref[hh]
        outs.append(acc[:, :HEAD_DIM] / acc[:, HEAD_DIM:])
    o_ref[...] = jnp.concatenate(outs, axis=1).astype(o_ref.dtype)


def _moba(proj, mk_aug_t, mv_aug, kmean, batch, seq):
    tq = min(1024, seq)
    tk = 512
    nq = seq // tq
    nbm = seq // MOBA_BLOCK
    n_top = min(MOBA_TOPK, nbm)
    qb = COL_MQ // 128
    aug = HEAD_DIM + nbm
    return pl.pallas_call(
        functools.partial(_moba_kernel, tq=tq, tk=tk, n_top=n_top, rc=64),
        out_shape=jax.ShapeDtypeStruct((batch * seq, MOBA_W), CDT),
        grid=(batch, MOBA_HEADS // 2, nq),
        in_specs=[pl.BlockSpec((tq, 128), lambda b, p, i: (b * nq + i, qb + p)),
                  pl.BlockSpec((1, 2, aug, seq), lambda b, p, i: (b, p, 0, 0)),
                  pl.BlockSpec((1, 2, seq, 128), lambda b, p, i: (b, p, 0, 0)),
                  pl.BlockSpec((1, nbm, 128), lambda b, p, i: (b, 0, p))],
        out_specs=pl.BlockSpec((tq, 128), lambda b, p, i: (b * nq + i, p)),
        scratch_shapes=[pltpu.VMEM((2, tq, aug), CDT), pltpu.VMEM((2, tq, tk), jnp.float32),
                        pltpu.VMEM((2, tq, tk), CDT), pltpu.VMEM((2, tq, 128), jnp.float32),
                        pltpu.VMEM((2, tq, 128), jnp.float32)],
        compiler_params=_params(("arbitrary", "arbitrary", "arbitrary")),
        name="moba",
    )(proj, mk_aug_t, mv_aug, kmean)


def _merge_kernel(oc_ref, os_ref, ow_ref, om_ref, ng_ref, gn_ref, gm_ref, x_ref, ga_ref,
                  ex_ref, wun_ref, wum_ref, wo_ref, o_ref):
    gates = _sigmoid(ng_ref[...].astype(jnp.float32))
    hi, lo = _split_hi_lo(gates)
    e = _dot(hi, ex_ref[...]) + _dot(lo, ex_ref[...])
    o_nsa = (e[:, :NSA_Q] * oc_ref[...].astype(jnp.float32)
             + e[:, NSA_Q:2 * NSA_Q] * os_ref[...].astype(jnp.float32)
             + e[:, 2 * NSA_Q:] * ow_ref[...].astype(jnp.float32))
    y = (_sigmoid(gn_ref[...].astype(jnp.float32)) * _dot(o_nsa.astype(CDT), wun_ref[...])
         + _sigmoid(gm_ref[...].astype(jnp.float32)) * _dot(om_ref[...], wum_ref[...]))
    o_ref[...] = x_ref[...] + ga_ref[0] * _dot(y.astype(CDT), wo_ref[...])


def _merge(o_c, o_s, o_w, o_m, proj, x2d, ga, w_up_nsa, w_up_moba, w_out, seq):
    n, d = x2d.shape
    tm = min(512, seq)
    per_seq = seq // tm
    ng_w = COL_GN - COL_NG
    ex = np.zeros((ng_w, 3 * NSA_Q), np.float32)
    for h in range(NSA_HEADS):
        for j in range(3):
            ex[h * 3 + j, j * NSA_Q + h * HEAD_DIM: j * NSA_Q + (h + 1) * HEAD_DIM] = 1.0
    row = lambda i: (i, 0)
    const = lambda i: (0, 0)
    return pl.pallas_call(
        _merge_kernel,
        out_shape=jax.ShapeDtypeStruct((n, d), jnp.float32),
        grid=(n // tm,),
        in_specs=[pl.BlockSpec((tm, NSA_Q), row), pl.BlockSpec((tm, NSA_Q), row),
                  pl.BlockSpec((tm, NSA_Q), row), pl.BlockSpec((tm, MOBA_W), row),
                  pl.BlockSpec((tm, ng_w), lambda i: (i, COL_NG // ng_w)),
                  pl.BlockSpec((tm, d), lambda i: (i, COL_GN // d)),
                  pl.BlockSpec((tm, d), lambda i: (i, COL_GM // d)),
                  pl.BlockSpec((tm, d), row),
                  pl.BlockSpec((1, 1, d), lambda i: (i // per_seq, 0, 0)),
                  pl.BlockSpec((ng_w, 3 * NSA_Q), const),
                  pl.BlockSpec((NSA_Q, d), const), pl.BlockSpec((MOBA_W, d), const),
                  pl.BlockSpec((d, d), const)],
        out_specs=pl.BlockSpec((tm, d), row),
        compiler_params=_params(("arbitrary",)),
        name="mixer_merge",
    )(o_c, o_s, o_w, o_m, proj, proj, proj, x2d, ga, jnp.asarray(ex, CDT),
      w_up_nsa.astype(CDT), w_up_moba.astype(CDT), w_out.astype(CDT))


def _extract_max(x, ridx):
    m = jnp.max(x, axis=0, keepdims=True)
    idx = jnp.min(jnp.where(x == m, ridx, float(x.shape[0])), axis=0, keepdims=True)
    return m, ridx == idx


def _peer_score_kernel(q_ref, k1_ref, k2_ref, cnt_ref, rk_ref, e1_ref, e2_ref):
    half = PEER_QDIM // 2
    k = PEER_TOPK
    q = q_ref[...]
    s1_all = _dot_nt(k1_ref[...], q[:, :half])
    s2_all = _dot_nt(k2_ref[...], q[:, half:])
    t = 128
    ridx = lax.broadcasted_iota(jnp.int32, (PEER_NKEYS, t), 0).astype(jnp.float32)
    unranked = jnp.full((PEER_NKEYS, t), float(k), jnp.float32)
    for c0 in range(0, s1_all.shape[1], t):
        s1 = s1_all[:, c0:c0 + t]
        s2 = s2_all[:, c0:c0 + t]
        s1 = s1 - jnp.max(s1, axis=0, keepdims=True)
        s2 = s2 - jnp.max(s2, axis=0, keepdims=True)
        x1, x2, rk1, rk2, v1, v2 = s1, s2, unranked, unranked, [], []
        for i in range(k):
            m1, pick1 = _extract_max(x1, ridx)
            m2, pick2 = _extract_max(x2, ridx)
            x1, x2 = jnp.where(pick1, LOWEST, x1), jnp.where(pick2, LOWEST, x2)
            rk1, rk2 = jnp.where(pick1, float(i), rk1), jnp.where(pick2, float(i), rk2)
            v1.append(m1)
            v2.append(m2)
        v2_all = jnp.concatenate(v2, axis=0)

        pieces = [v1[i] + v2_all[0:k // (i + 1)] for i in range(k)]
        pad = -sum(p.shape[0] for p in pieces) % 8
        x = jnp.concatenate(pieces + [jnp.full((pad, t), LOWEST, jnp.float32)], axis=0)
        cidx = lax.broadcasted_iota(jnp.int32, x.shape, 0).astype(jnp.float32)
        vals = []
        for i in range(k):
            m, pick = _extract_max(x, cidx)
            x = jnp.where(pick, LOWEST, x)
            vals.append(m)
        tau = vals[k - 1]
        z = vals[0] - vals[0] + 1.0
        for i in range(1, k):
            z = z + jnp.exp(vals[i] - vals[0])
        cnt = jnp.zeros((PEER_NKEYS, t), jnp.float32)
        for i in range(k):
            n_i = jnp.sum(jnp.where(v1[i] + v2_all >= tau, 1.0, 0.0), axis=0, keepdims=True)
            cnt = jnp.where(rk1 == float(i), n_i, cnt)
        cnt_ref[0, :, c0:c0 + t] = cnt
        rk_ref[0, :, c0:c0 + t] = rk2.astype(rk_ref.dtype)
        e1_ref[0, :, c0:c0 + t] = jnp.exp(s1 - vals[0]) / z
        e2_ref[0, :, c0:c0 + t] = jnp.exp(s2).astype(e2_ref.dtype)


def _peer_scores(qp, k1, k2):
    n = qp.shape[0]
    tt = 256
    f32 = jax.ShapeDtypeStruct((PEER_HEADS, PEER_NKEYS, n), jnp.float32)
    cdt = jax.ShapeDtypeStruct((PEER_HEADS, PEER_NKEYS, n), CDT)
    big = pl.BlockSpec((1, PEER_NKEYS, tt), lambda i, h: (h, 0, i))
    return pl.pallas_call(
        _peer_score_kernel,
        out_shape=[f32, cdt, f32, cdt],
        grid=(n // tt, PEER_HEADS),
        in_specs=[pl.BlockSpec((tt, PEER_QDIM), lambda i, h: (i, h)),
                  pl.BlockSpec((PEER_NKEYS, PEER_QDIM // 2), lambda i, h: (0, 0)),
                  pl.BlockSpec((PEER_NKEYS, PEER_QDIM // 2), lambda i, h: (0, 0))],
        out_specs=[big, big, big, big],
        compiler_params=_params(("arbitrary", "arbitrary")),
        name="peer_scores",
    )(qp, k1.astype(CDT), k2.astype(CDT))


def _row_to_rows(row, n):
    tile_rows = 16
    tile = jnp.broadcast_to(row, (tile_rows, row.shape[1])).astype(CDT)
    return jnp.concatenate([tile] * (n // tile_rows), axis=0)


def _peer_expert_kernel(h_ref, u_ref, vt_ref, cnt_ref, rk_ref, e1_ref, e2_ref, x_ref, ga_ref,
                        o_ref, acc_ref, *, eb):
    j = pl.program_id(1)
    tt = h_ref.shape[0]

    @pl.when(j == 0)
    def _():
        acc_ref[...] = jnp.zeros(acc_ref.shape, jnp.float32)

    parts = []
    for al in range(eb // PEER_NKEYS):
        a = j * (eb // PEER_NKEYS) + al
        w = None
        for hd in range(PEER_HEADS):
            cnt_a = _row_to_rows(cnt_ref[hd, pl.ds(a, 1), :], PEER_NKEYS)
            e1_a = _row_to_rows(e1_ref[hd, pl.ds(a, 1), :], PEER_NKEYS)
            contrib = jnp.where(rk_ref[hd] < cnt_a, e2_ref[hd] * e1_a, jnp.zeros((), CDT))
            w = contrib if w is None else w + contrib
        parts.append(w)
    act = _gelu(_dot_nt(u_ref[...], h_ref[...]).astype(CDT))
    pw = jnp.concatenate(parts, axis=0) * act
    acc_ref[...] += _dot(vt_ref[...], pw)

    @pl.when(j == pl.num_programs(1) - 1)
    def _():
        o_ref[...] = x_ref[...] + ga_ref[0] * acc_ref[...].T


def _peer_experts(h2, u, v_t, cnt, rk2, e1, e2, x2d, ga, seq):
    n, d = x2d.shape
    tt = min(512, seq)
    per_seq = seq // tt
    eb = 512
    n_blk = u.shape[0] // eb
    big = pl.BlockSpec((PEER_HEADS, PEER_NKEYS, tt), lambda i, j: (0, 0, i))
    return pl.pallas_call(
        functools.partial(_peer_expert_kernel, eb=eb),
        out_shape=jax.ShapeDtypeStruct((n, d), jnp.float32),
        grid=(n // tt, n_blk),
        in_specs=[pl.BlockSpec((tt, d), lambda i, j: (i, 0)),
                  pl.BlockSpec((eb, d), lambda i, j: (j, 0)),
                  pl.BlockSpec((d, eb), lambda i, j: (0, j)),
                  big, big, big, big,
                  pl.BlockSpec((tt, d), lambda i, j: (i, 0)),
                  pl.BlockSpec((1, 1, d), lambda i, j: (i // per_seq, 0, 0))],
        out_specs=pl.BlockSpec((tt, d), lambda i, j: (i, 0)),
        scratch_shapes=[pltpu.VMEM((d, tt), jnp.float32)],
        compiler_params=_params(("arbitrary", "arbitrary")),
        name="peer_experts",
    )(h2, u, v_t, cnt, rk2, e1, e2, x2d, ga)


def _rms_kernel(x_ref, g_ref, o_ref):
    x = x_ref[...]
    ms = jnp.mean(x * x, axis=-1, keepdims=True)
    o_ref[...] = x * lax.rsqrt(ms + RMS_EPS) * g_ref[...]


def _final_norm(x2d, g):
    n, d = x2d.shape
    tm = 512
    return pl.pallas_call(
        _rms_kernel,
        out_shape=jax.ShapeDtypeStruct((n, d), jnp.float32),
        grid=(n // tm,),
        in_specs=[pl.BlockSpec((tm, d), lambda i: (i, 0)), pl.BlockSpec((1, d), lambda i: (0, 0))],
        out_specs=pl.BlockSpec((tm, d), lambda i: (i, 0)),
        compiler_params=_params(("arbitrary",)),
        name="final_rmsnorm",
    )(x2d, g.reshape(1, d))


def _key_major(proj, col, heads, batch, seq):
    k = proj[:, col:col + heads * HEAD_DIM].reshape(batch, seq, heads, HEAD_DIM)
    return jnp.transpose(k, (0, 2, 3, 1))


def _value_with_ones(proj, col, heads, batch, seq):
    v = proj[:, col:col + heads * HEAD_DIM].reshape(batch, seq, heads, HEAD_DIM)
    v = jnp.transpose(v, (0, 2, 1, 3))
    return jnp.concatenate([v, jnp.ones_like(v)], axis=-1)


def kernel(x, c, w_ada, b_ada, g_attn, g_ffn, w_in, cmp_pe, cmp_w1, cmp_w2, w_up_nsa, w_up_moba, w_out,
           peer_wq, peer_k1, peer_k2, peer_u, peer_v, g_final):
    batch, seq, d = x.shape
    depth = w_ada.shape[0]
    n = batch * seq
    nbm = seq // MOBA_BLOCK
    x2d = x.reshape(n, d)
    mod = _adaln_mod(c, w_ada, b_ada)
    rope = _rope_tables(seq)
    blk_onehot = (jnp.arange(nbm)[:, None] == jnp.arange(seq)[None, :] // MOBA_BLOCK).astype(CDT)
    for l in range(depth):
        sh1, sc1, ga1, sh2, sc2, ga2 = [m.reshape(batch, 1, d) for m in jnp.split(mod[l], 6, axis=-1)]
        proj = _norm_mod_matmul(x2d, g_attn[l], sc1, sh1, _reorder_w_in(w_in[l]), seq,
                                rope=rope, n_rope=ROPE_COLS // PROJ_TN)
        cmp_kv = _nsa_compress(proj, cmp_pe[l], cmp_w1[l], cmp_w2[l], batch, seq)
        o_c, selb = _nsa_compressed(proj, cmp_kv, batch, seq)
        o_s = _nsa_selected(proj, _key_major(proj, COL_KS, NSA_KV_HEADS, batch, seq),
                            _value_with_ones(proj, COL_VS, NSA_KV_HEADS, batch, seq), selb, batch, seq)
        o_w = _nsa_window(proj, _key_major(proj, COL_KW, NSA_KV_HEADS, batch, seq), batch, seq)
        mk_t = _key_major(proj, COL_MK, MOBA_HEADS, batch, seq)
        mk_aug = jnp.concatenate(
            [mk_t, jnp.broadcast_to(blk_onehot, (batch, MOBA_HEADS, nbm, seq))], axis=2)
        o_m = _moba(proj, mk_aug, _value_with_ones(proj, COL_MV, MOBA_HEADS, batch, seq),
                    _moba_kmean(proj, batch, seq), batch, seq)
        x2d = _merge(o_c, o_s, o_w, o_m, proj, x2d, ga1, w_up_nsa[l], w_up_moba[l], w_out[l], seq)
        qp, h2 = _norm_mod_matmul(x2d, g_ffn[l], sc2, sh2, peer_wq[l].astype(CDT), seq, emit_h=True)
        cnt, rk2, e1, e2 = _peer_scores(qp, peer_k1[l], peer_k2[l])
        x2d = _peer_experts(h2, peer_u[l].astype(CDT), peer_v[l].T.astype(CDT), cnt, rk2, e1, e2,
                            x2d, ga2, seq)
    return _final_norm(x2d, g_final).reshape(batch, seq, d)
```

```python
import functools

import jax
import jax.numpy as jnp
import numpy as np
from jax import lax
from jax.experimental import pallas as pl
from jax.experimental.pallas import tpu as pltpu

D_MODEL = 1024
HEAD_DIM = 64
ROT_DIM = HEAD_DIM // 4
ROPE_THETA = 500000.0
NSA_HEADS = 8
NSA_KV_HEADS = 2
NSA_GROUP = NSA_HEADS // NSA_KV_HEADS
CMP_LEN = 32
CMP_STRIDE = 16
CMP_HID = 2 * HEAD_DIM
SLC_LEN = 64
SLC_TOPN = 16
WIN = 512
MOBA_HEADS = 8
MOBA_BLOCK = 256
MOBA_TOPK = 3
PEER_HEADS = 8
PEER_NKEYS = 128
PEER_QDIM = 256
PEER_TOPK = 16
RMS_EPS = 1e-6
NEG = -1e30
SEL_FORCE = 1e4
LOWEST = -3.0e38
LOG2E = 1.4426950408889634

NSA_Q = NSA_HEADS * HEAD_DIM
NSA_KV = NSA_KV_HEADS * HEAD_DIM
MOBA_W = MOBA_HEADS * HEAD_DIM

CDT = jnp.bfloat16
V7X_VMEM_LIMIT = 56 * 1024 * 1024

COL_QROT = 0
COL_KS = 512
COL_KW = 640
COL_MQ = 768
COL_MK = 1280
ROPE_COLS = 1792
COL_NG = 1792
COL_GN = 2048
COL_GM = 3072
COL_QRAW = 4096
COL_KC = 4608
COL_VC = 4736
COL_VS = 4864
COL_VW = 4992
COL_MV = 5120
PROJ_COLS = 5632
PROJ_TN = 256


def _params(sem):
    return pltpu.CompilerParams(dimension_semantics=sem, vmem_limit_bytes=V7X_VMEM_LIMIT)


def _dot(a, b):
    return jnp.dot(a, b, preferred_element_type=jnp.float32)


def _dot_nt(a, b):
    return lax.dot_general(a, b, (((1,), (1,)), ((), ())), preferred_element_type=jnp.float32)


def _split_hi_lo(x):
    hi = x.astype(CDT)
    lo = (x - hi.astype(jnp.float32)).astype(CDT)
    return hi, lo


def _gelu(x):
    return 0.5 * x * (1.0 + jnp.tanh(0.7978845608028654 * (x + 0.044715 * (x * x * x))))


def _sigmoid(x):
    return 1.0 / (1.0 + jnp.exp(-x))


def _mod_kernel(c_ref, w_ref, b_ref, o_ref):
    c = c_ref[...]
    sc = c * _sigmoid(c)
    o_ref[0] = jnp.dot(sc, w_ref[0], preferred_element_type=jnp.float32,
                       precision=lax.Precision.HIGHEST) + b_ref[0]


def _adaln_mod(c, w_ada, b_ada):
    depth, d, six_d = w_ada.shape
    b = c.shape[0]
    rows = 8
    c_pad = jnp.zeros((rows, d), jnp.float32).at[:b].set(c)
    tn = 1024
    out = pl.pallas_call(
        _mod_kernel,
        out_shape=jax.ShapeDtypeStruct((depth, rows, six_d), jnp.float32),
        grid=(depth, six_d // tn),
        in_specs=[pl.BlockSpec((rows, d), lambda l, j: (0, 0)),
                  pl.BlockSpec((1, d, tn), lambda l, j: (l, 0, j)),
                  pl.BlockSpec((1, 1, tn), lambda l, j: (l, 0, j))],
        out_specs=pl.BlockSpec((1, rows, tn), lambda l, j: (l, 0, j)),
        compiler_params=_params(("arbitrary", "arbitrary")),
        name="adaln_mod",
    )(c_pad, w_ada, b_ada.reshape(depth, 1, six_d))
    return out[:, :b]


def _nmm_kernel(*refs, n_rope, emit_h):
    if n_rope:
        x_ref, g_ref, sc_ref, sh_ref, w_ref, cos_ref, sa_ref, sb_ref = refs[:8]
        rest = refs[8:]
    else:
        x_ref, g_ref, sc_ref, sh_ref, w_ref = refs[:5]
        rest = refs[5:]
    if emit_h:
        o_ref, ho_ref, h_ref = rest
    else:
        o_ref, h_ref = rest
    j = pl.program_id(1)

    @pl.when(j == 0)
    def _():
        x = x_ref[...]
        ms = jnp.mean(x * x, axis=-1, keepdims=True)
        y = x * lax.rsqrt(ms + RMS_EPS) * g_ref[...]
        h = (y * (1.0 + sc_ref[0]) + sh_ref[0]).astype(h_ref.dtype)
        h_ref[...] = h
        if emit_h:
            ho_ref[...] = h

    acc = _dot(h_ref[...], w_ref[...])
    if n_rope:
        @pl.when(j < n_rope)
        def _():
            cos, sa, sb = cos_ref[...], sa_ref[...], sb_ref[...]
            parts = []
            for c0 in range(0, acc.shape[1], 128):
                a = acc[:, c0:c0 + 128]
                parts.append(a * cos + pltpu.roll(a, 128 - ROT_DIM // 2, axis=1) * sa
                             + pltpu.roll(a, ROT_DIM // 2, axis=1) * sb)
            o_ref[...] = jnp.concatenate(parts, axis=1).astype(o_ref.dtype)

        @pl.when(j >= n_rope)
        def _():
            o_ref[...] = acc.astype(o_ref.dtype)
    else:
        o_ref[...] = acc.astype(o_ref.dtype)


def _norm_mod_matmul(x2d, g, sc, sh, w, seq, rope=None, n_rope=0, emit_h=False, tn=PROJ_TN):
    n, d = x2d.shape
    cols = w.shape[1]
    tm = min(1024, seq)
    per_seq = seq // tm
    in_specs = [pl.BlockSpec((tm, d), lambda i, j: (i, 0)),
                pl.BlockSpec((1, d), lambda i, j: (0, 0)),
                pl.BlockSpec((1, 1, d), lambda i, j: (i // per_seq, 0, 0)),
                pl.BlockSpec((1, 1, d), lambda i, j: (i // per_seq, 0, 0)),
                pl.BlockSpec((d, tn), lambda i, j: (0, j))]
    args = [x2d, g.reshape(1, d), sc, sh, w]
    if n_rope:
        in_specs += [pl.BlockSpec((tm, 128), lambda i, j: (i % per_seq, 0))] * 3
        args += list(rope)
    out_shape = [jax.ShapeDtypeStruct((n, cols), CDT)]
    out_specs = [pl.BlockSpec((tm, tn), lambda i, j: (i, j))]
    if emit_h:
        out_shape.append(jax.ShapeDtypeStruct((n, d), CDT))
        out_specs.append(pl.BlockSpec((tm, d), lambda i, j: (i, 0)))
    res = pl.pallas_call(
        functools.partial(_nmm_kernel, n_rope=n_rope, emit_h=emit_h),
        out_shape=out_shape,
        grid=(n // tm, cols // tn),
        in_specs=in_specs,
        out_specs=out_specs,
        scratch_shapes=[pltpu.VMEM((tm, d), CDT)],
        compiler_params=_params(("arbitrary", "arbitrary")),
        name="norm_mod_matmul",
    )(*args)
    return res if emit_h else res[0]


def _rope_tables(seq):
    half = ROT_DIM // 2
    inv = ROPE_THETA ** (-jnp.arange(half, dtype=jnp.float32) / half)
    ang = jnp.arange(seq, dtype=jnp.float32)[:, None] * inv[None, :]
    cos, sin = jnp.cos(ang), jnp.sin(ang)
    ones = jnp.ones((seq, HEAD_DIM - ROT_DIM), jnp.float32)
    zeros = jnp.zeros((seq, HEAD_DIM - ROT_DIM), jnp.float32)
    zh = jnp.zeros((seq, half), jnp.float32)
    c64 = jnp.concatenate([cos, cos, ones], axis=1)
    sa64 = jnp.concatenate([-sin, zh, zeros], axis=1)
    sb64 = jnp.concatenate([zh, sin, zeros], axis=1)
    return tuple(jnp.concatenate([t, t], axis=1) for t in (c64, sa64, sb64))


def _reorder_w_in(w_in):
    o = 0
    pieces = {}
    for name, size in (("nq", NSA_Q), ("kc", NSA_KV), ("vc", NSA_KV), ("ks", NSA_KV), ("vs", NSA_KV),
                       ("kw", NSA_KV), ("vw", NSA_KV), ("ng", 3 * NSA_HEADS), ("mq", MOBA_W),
                       ("mk", MOBA_W), ("mv", MOBA_W), ("gn", D_MODEL), ("gm", D_MODEL)):
        pieces[name] = w_in[:, o:o + size]
        o += size
    scale = HEAD_DIM ** -0.5 * LOG2E
    pad = jnp.zeros((w_in.shape[0], COL_GN - COL_NG - 3 * NSA_HEADS), w_in.dtype)
    w = jnp.concatenate([pieces["nq"] * scale, pieces["ks"], pieces["kw"], pieces["mq"] * scale,
                         pieces["mk"], pieces["ng"], pad, pieces["gn"], pieces["gm"],
                         pieces["nq"] * scale, pieces["kc"], pieces["vc"], pieces["vs"],
                         pieces["vw"], pieces["mv"]], axis=1)
    assert w.shape[1] == PROJ_COLS
    return w.astype(CDT)


def _compress_kernel(x_ref, w1c_ref, pe_ref, w1_ref, w2_ref, o_ref):
    nb = x_ref.shape[3]
    ab = _dot(x_ref[0, 0, 0], w1c_ref[0])
    a = ab[:, :CMP_HID]
    b_next = pltpu.roll(ab[:, CMP_HID:], nb - 1, axis=0)
    bias = _dot(pe_ref[0], w1_ref[0])[0:1]
    hid = _gelu(a + b_next + bias)
    o_ref[0, 0, 0] = _dot(hid.astype(CDT), w2_ref[0]).astype(o_ref.dtype)


def _nsa_compress(proj, cmp_pe, cmp_w1, cmp_w2, batch, seq):
    nb = seq // CMP_STRIDE
    half = CMP_STRIDE * HEAD_DIM
    kcvc = proj[:, COL_KC:COL_KC + 2 * NSA_KV].reshape(batch, nb, CMP_STRIDE, 2, NSA_KV_HEADS, HEAD_DIM)
    x = jnp.transpose(kcvc, (3, 0, 4, 1, 2, 5)).reshape(2, batch, NSA_KV_HEADS, nb, half)
    w1c = jnp.concatenate([cmp_w1[:, :half], cmp_w1[:, half:]], axis=2).astype(CDT)
    pe = jnp.broadcast_to(cmp_pe.reshape(2, 1, CMP_LEN * HEAD_DIM), (2, 8, CMP_LEN * HEAD_DIM)).astype(CDT)
    return pl.pallas_call(
        _compress_kernel,
        out_shape=jax.ShapeDtypeStruct((2, batch, NSA_KV_HEADS, nb, HEAD_DIM), CDT),
        grid=(2, batch, NSA_KV_HEADS),
        in_specs=[pl.BlockSpec((1, 1, 1, nb, half), lambda w, b, k: (w, b, k, 0, 0)),
                  pl.BlockSpec((1, half, 2 * CMP_HID), lambda w, b, k: (w, 0, 0)),
                  pl.BlockSpec((1, 8, 2 * half), lambda w, b, k: (w, 0, 0)),
                  pl.BlockSpec((1, 2 * half, CMP_HID), lambda w, b, k: (w, 0, 0)),
                  pl.BlockSpec((1, CMP_HID, HEAD_DIM), lambda w, b, k: (w, 0, 0))],
        out_specs=pl.BlockSpec((1, 1, 1, nb, HEAD_DIM), lambda w, b, k: (w, b, k, 0, 0)),
        compiler_params=_params(("arbitrary", "arbitrary", "arbitrary")),
        name="nsa_compress",
    )(x, w1c, pe, cmp_w1.astype(CDT), cmp_w2.astype(CDT))


def _stack_heads(qblk):
    return jnp.concatenate([qblk[:, g * HEAD_DIM:(g + 1) * HEAD_DIM] for g in range(NSA_GROUP)], axis=0)


def _unstack_heads(o, tq):
    return jnp.concatenate([o[g * tq:(g + 1) * tq] for g in range(NSA_GROUP)], axis=1)


def _topk_mask(score, n_sel):
    shape = score.shape
    lane = lax.broadcasted_iota(jnp.int32, shape, 1).astype(jnp.float32)
    width = float(shape[1])

    def body(_, carry):
        sc, sel = carry
        m = jnp.max(sc, axis=-1, keepdims=True)
        idx = jnp.min(jnp.where(sc == m, lane, width), axis=-1, keepdims=True)
        pick = lane == idx
        return jnp.where(pick, LOWEST, sc), jnp.where(pick, 1.0, sel)

    _, sel = lax.fori_loop(0, n_sel, body, (score, jnp.zeros(shape, jnp.float32)))
    return sel


def _nsa_cmp_kernel(q_ref, kc_ref, vc_ref, ov_ref, oc_ref, sb_ref, imp_ref, *, tq, n_sel, n_var):
    i = pl.program_id(2)
    s0 = i * tq
    nb = kc_ref.shape[3]
    ns = ov_ref.shape[1]
    tq_col = s0 + lax.broadcasted_iota(jnp.int32, (tq, 1), 0)
    t4 = jnp.concatenate([tq_col] * NSA_GROUP, axis=0)

    def attend(nk):
        q4 = _stack_heads(q_ref[...])
        s = _dot_nt(q4, kc_ref[0, 0, 0, :nk, :])
        cend = lax.broadcasted_iota(jnp.int32, (1, nk), 1) * CMP_STRIDE + (CMP_LEN - 1)
        mask = cend <= t4
        s = jnp.where(mask, s, NEG)
        m = jnp.max(s, axis=-1, keepdims=True)
        p = jnp.where(mask, jnp.exp2(s - m), 0.0)
        p = p / jnp.maximum(jnp.sum(p, axis=-1, keepdims=True), 1e-30)
        o = _dot(p.astype(CDT), vc_ref[0, 0, 0, :nk, :])
        oc_ref[...] = _unstack_heads(o, tq).astype(oc_ref.dtype)
        psum = p[0:tq]
        for g in range(1, NSA_GROUP):
            psum = psum + p[g * tq:(g + 1) * tq]
        hi, lo = _split_hi_lo(psum)
        imp_ref[...] = _dot(hi, ov_ref[:nk, :]) + _dot(lo, ov_ref[:nk, :])

    tiles_per_var = pl.num_programs(2) // n_var
    for v in range(n_var):
        @pl.when(i // tiles_per_var == v)
        def _(v=v):
            attend(nb * (v + 1) // n_var)

    blk = lax.broadcasted_iota(jnp.int32, (ns, 1), 0)
    cur = (s0 + lax.broadcasted_iota(jnp.int32, (1, tq), 1)) // SLC_LEN
    valid = blk <= cur
    forced = (blk == 0) | (blk == cur) | (blk == cur - 1)
    score = jnp.where(valid, jnp.where(forced, SEL_FORCE, imp_ref[...].T), NEG)
    ridx = lax.broadcasted_iota(jnp.int32, (ns, tq), 0).astype(jnp.float32)

    def body(_, carry):
        sc, sel = carry
        _, pick = _extract_max(sc, ridx)
        return jnp.where(pick, LOWEST, sc), jnp.where(pick, 1.0, sel)

    _, sel = lax.fori_loop(0, n_sel, body, (score, jnp.zeros((ns, tq), jnp.float32)))
    bias = jnp.where((sel > 0.5) & valid, 0.0, NEG).T
    extra = sb_ref.shape[3] - ns
    if extra:
        bias = jnp.concatenate([bias, jnp.full((tq, extra), NEG, jnp.float32)], axis=1)
    sb_ref[0, 0] = bias.astype(sb_ref.dtype)


def _nsa_compressed(proj, cmp_kv, batch, seq):
    tq = 128
    nq = seq // tq
    nb = seq // CMP_STRIDE
    ns = seq // SLC_LEN
    n_sel = min(SLC_TOPN, ns)
    c_start = np.arange(nb) * CMP_STRIDE
    s_start = np.arange(ns) * SLC_LEN
    ov = np.maximum(np.minimum(c_start[:, None] + CMP_LEN, s_start[None, :] + SLC_LEN)
                    - np.maximum(c_start[:, None], s_start[None, :]), 0).astype(np.float32) / CMP_LEN
    ov[nb - 1] = 0.0
    qb = COL_QRAW // (NSA_GROUP * HEAD_DIM)
    return pl.pallas_call(
        functools.partial(_nsa_cmp_kernel, tq=tq, n_sel=n_sel,
                          n_var=4 if (nb % (4 * 128) == 0 and nq % 4 == 0) else 1),
        out_shape=[jax.ShapeDtypeStruct((batch * seq, NSA_Q), CDT),
                   jax.ShapeDtypeStruct((batch, NSA_KV_HEADS, seq, max(ns, 128)), CDT)],
        grid=(batch, NSA_KV_HEADS, nq),
        in_specs=[pl.BlockSpec((tq, NSA_GROUP * HEAD_DIM), lambda b, k, i: (b * nq + i, qb + k)),
                  pl.BlockSpec((1, 1, 1, nb, HEAD_DIM), lambda b, k, i: (0, b, k, 0, 0)),
                  pl.BlockSpec((1, 1, 1, nb, HEAD_DIM), lambda b, k, i: (1, b, k, 0, 0)),
                  pl.BlockSpec((nb, ns), lambda b, k, i: (0, 0))],
        out_specs=[pl.BlockSpec((tq, NSA_GROUP * HEAD_DIM), lambda b, k, i: (b * nq + i, k)),
                   pl.BlockSpec((1, 1, tq, max(ns, 128)), lambda b, k, i: (b, k, i, 0))],
        scratch_shapes=[pltpu.VMEM((tq, ns), jnp.float32)],
        compiler_params=_params(("arbitrary", "arbitrary", "arbitrary")),
        name="nsa_compressed",
    )(proj, cmp_kv, cmp_kv, jnp.asarray(ov, CDT))


def _flash_tile(s_ref, p_ref, m_ref, acc_ref, rc, tq, bias_ref=None, causal=None):
    rows, tk = s_ref.shape

    def tile(r0, c0):
        s = s_ref[r0:r0 + rc, c0:c0 + 128]
        if bias_ref is not None:
            s = s + bias_ref[r0 % tq:r0 % tq + rc, c0:c0 + 128]
        if causal is not None:
            kpos, t0 = causal
            t = t0 + (r0 % tq) + lax.broadcasted_iota(jnp.int32, (rc, 1), 0)
            s = jnp.where(kpos[:, c0:c0 + 128] <= t, s, NEG)
        return s

    for r0 in range(0, rows, rc):
        mx = tile(r0, 0)
        for c0 in range(128, tk, 128):
            mx = jnp.maximum(mx, tile(r0, c0))
        m_prev = m_ref[r0:r0 + rc, :]
        m_new = jnp.maximum(m_prev, jnp.max(mx, axis=-1, keepdims=True))
        acc_ref[r0:r0 + rc, :] = acc_ref[r0:r0 + rc, :] * jnp.exp2(m_prev - m_new)
        m_ref[r0:r0 + rc, :] = m_new
    for r0 in range(0, rows, rc):
        m_new = m_ref[r0:r0 + rc, :]
        for c0 in range(0, tk, 128):
            p_ref[r0:r0 + rc, c0:c0 + 128] = jnp.exp2(tile(r0, c0) - m_new).astype(p_ref.dtype)


def _nsa_sel_kernel(q_ref, kt_ref, va_ref, sb_ref, o_ref,
                    qp_ref, q4_ref, sbf_ref, s_ref, p_ref, m_ref, acc_ref, *, tq, tk, rc):
    i = pl.program_id(2)
    s0 = i * tq
    nsp = sb_ref.shape[3]
    per_tile = tk // SLC_LEN
    q4 = _stack_heads(q_ref[...])
    qp_ref[...] = jnp.concatenate([q4, jnp.zeros_like(q4)], axis=1)
    sbf_ref[...] = sb_ref[0, 0].astype(jnp.float32)
    lane = lax.broadcasted_iota(jnp.int32, (tq, 128), 1)
    bias_lanes = (lane >= HEAD_DIM) & (lane < HEAD_DIM + per_tile)
    m_ref[...] = jnp.full(m_ref.shape, NEG, jnp.float32)
    acc_ref[...] = jnp.zeros(acc_ref.shape, jnp.float32)

    def step(kt, masked):
        start = pl.multiple_of(kt * tk, tk)
        shift = (HEAD_DIM - kt * per_tile + nsp) % nsp
        rolled = pltpu.roll(sbf_ref[...], shift, axis=1)[:, :128].astype(q4_ref.dtype)
        for g in range(NSA_GROUP):
            q4_ref[g * tq:(g + 1) * tq, :] = jnp.where(bias_lanes, rolled, qp_ref[g * tq:(g + 1) * tq, :])
        s_ref[...] = _dot(q4_ref[...], kt_ref[0, 0, :, pl.ds(start, tk)])
        causal = (start + lax.broadcasted_iota(jnp.int32, (1, tk), 1), s0) if masked else None
        _flash_tile(s_ref, p_ref, m_ref, acc_ref, rc, tq, causal=causal)
        acc_ref[...] += _dot(p_ref[...], va_ref[0, 0, pl.ds(start, tk), :])

    def body(kt, carry):
        step(kt, False)
        return carry

    n_full = s0 // tk
    lax.fori_loop(0, n_full, body, 0)
    step(n_full, True)
    acc = acc_ref[...]
    o = acc[:, :HEAD_DIM] / acc[:, HEAD_DIM:]
    o_ref[...] = _unstack_heads(o, tq).astype(o_ref.dtype)


def _nsa_selected(proj, ks_t, vs_aug, selb, batch, seq):
    tq = 256
    tk = 512
    nq = seq // tq
    nsp = selb.shape[3]
    per_tile = tk // SLC_LEN
    blk_in_tile = (jnp.arange(seq) // SLC_LEN) % per_tile
    onehot = (jnp.arange(128 - HEAD_DIM)[:, None] == blk_in_tile[None, :]).astype(CDT)
    k_aug = jnp.concatenate(
        [ks_t, jnp.broadcast_to(onehot, (batch, NSA_KV_HEADS, 128 - HEAD_DIM, seq))], axis=2)
    qb = COL_QROT // (NSA_GROUP * HEAD_DIM)
    rows = NSA_GROUP * tq
    return pl.pallas_call(
        functools.partial(_nsa_sel_kernel, tq=tq, tk=tk, rc=64),
        out_shape=jax.ShapeDtypeStruct((batch * seq, NSA_Q), CDT),
        grid=(batch, NSA_KV_HEADS, nq),
        in_specs=[pl.BlockSpec((tq, NSA_GROUP * HEAD_DIM), lambda b, k, i: (b * nq + i, qb + k)),
                  pl.BlockSpec((1, 1, 128, seq), lambda b, k, i: (b, k, 0, 0)),
                  pl.BlockSpec((1, 1, seq, 128), lambda b, k, i: (b, k, 0, 0)),
                  pl.BlockSpec((1, 1, tq, nsp), lambda b, k, i: (b, k, i, 0))],
        out_specs=pl.BlockSpec((tq, NSA_GROUP * HEAD_DIM), lambda b, k, i: (b * nq + i, k)),
        scratch_shapes=[pltpu.VMEM((rows, 128), CDT), pltpu.VMEM((rows, 128), CDT),
                        pltpu.VMEM((tq, nsp), jnp.float32), pltpu.VMEM((rows, tk), jnp.float32),
                        pltpu.VMEM((rows, tk), CDT),
                        pltpu.VMEM((rows, 128), jnp.float32), pltpu.VMEM((rows, 128), jnp.float32)],
        compiler_params=_params(("arbitrary", "arbitrary", "arbitrary")),
        name="nsa_selected",
    )(proj, k_aug, vs_aug, selb)


def _nsa_win_kernel(q_ref, kt_ref, v_ref, o_ref, *, tq):
    k = pl.program_id(1)
    i = pl.program_id(2)
    s0 = i * tq
    span = WIN + tq
    start = pl.multiple_of(jnp.maximum(s0 - WIN, 0), tq)
    q4 = _stack_heads(q_ref[...])
    s = _dot(q4, kt_ref[0, 0, :, pl.ds(start, span)])
    tq_col = s0 + lax.broadcasted_iota(jnp.int32, (tq, 1), 0)
    t4 = jnp.concatenate([tq_col] * NSA_GROUP, axis=0)
    kpos = start + lax.broadcasted_iota(jnp.int32, (1, span), 1)
    mask = (kpos <= t4) & (kpos > t4 - WIN)
    s = jnp.where(mask, s, NEG)
    m = jnp.max(s, axis=-1, keepdims=True)
    p = jnp.where(mask, jnp.exp2(s - m), 0.0)
    l = jnp.maximum(jnp.sum(p, axis=-1, keepdims=True), 1e-30)
    o = _dot(p.astype(CDT), v_ref[pl.ds(start, span), :]) / l
    o = jnp.where(k == 0, o[:, :HEAD_DIM], o[:, HEAD_DIM:])
    o_ref[...] = _unstack_heads(o, tq).astype(o_ref.dtype)


def _nsa_window(proj, kw_t, batch, seq):
    tq = 256
    nq = seq // tq
    qb = COL_QROT // (NSA_GROUP * HEAD_DIM)
    vb = COL_VW // 128
    return pl.pallas_call(
        functools.partial(_nsa_win_kernel, tq=tq),
        out_shape=jax.ShapeDtypeStruct((batch * seq, NSA_Q), CDT),
        grid=(batch, NSA_KV_HEADS, nq),
        in_specs=[pl.BlockSpec((tq, NSA_GROUP * HEAD_DIM), lambda b, k, i: (b * nq + i, qb + k)),
                  pl.BlockSpec((1, 1, HEAD_DIM, seq), lambda b, k, i: (b, k, 0, 0)),
                  pl.BlockSpec((seq, 128), lambda b, k, i: (b, vb))],
        out_specs=pl.BlockSpec((tq, NSA_GROUP * HEAD_DIM), lambda b, k, i: (b * nq + i, k)),
        compiler_params=_params(("arbitrary", "arbitrary", "arbitrary")),
        name="nsa_window",
    )(proj, kw_t, proj)


def _moba_mean_kernel(k_ref, o_ref):
    seq, w = k_ref.shape
    nbm = seq // MOBA_BLOCK
    k = k_ref[...].astype(jnp.float32).reshape(nbm, MOBA_BLOCK, w)
    o_ref[0] = jnp.sum(k, axis=1) * (1.0 / MOBA_BLOCK)


def _moba_kmean(proj, batch, seq):
    nbm = seq // MOBA_BLOCK
    kb = COL_MK // 128
    return pl.pallas_call(
        _moba_mean_kernel,
        out_shape=jax.ShapeDtypeStruct((batch, nbm, MOBA_W), jnp.float32),
        grid=(batch, MOBA_W // 128),
        in_specs=[pl.BlockSpec((seq, 128), lambda b, j: (b, kb + j))],
        out_specs=pl.BlockSpec((1, nbm, 128), lambda b, j: (b, 0, j)),
        compiler_params=_params(("arbitrary", "arbitrary")),
        name="moba_kmean",
    )(proj)


def _moba_kernel(q_ref, ka_ref, va_ref, km_ref, o_ref, qa_ref, s_ref, p_ref, m_ref, acc_ref,
                 *, tq, tk, n_top, rc):
    i = pl.program_id(2)
    s0 = i * tq
    nbm = km_ref.shape[1]
    blk = lax.broadcasted_iota(jnp.int32, (nbm, 1), 0)
    cur = (s0 + lax.broadcasted_iota(jnp.int32, (1, tq), 1)) // MOBA_BLOCK
    ridx = lax.broadcasted_iota(jnp.int32, (nbm, tq), 0).astype(jnp.float32)
    for hh in range(2):
        q = q_ref[:, hh * HEAD_DIM:(hh + 1) * HEAD_DIM]
        km_hi, km_lo = _split_hi_lo(km_ref[0][:, hh * HEAD_DIM:(hh + 1) * HEAD_DIM])
        gs = _dot_nt(km_hi, q) + _dot_nt(km_lo, q)
        gs = jnp.where(blk < cur, gs, NEG)

        def body(_, carry):
            sc, sel = carry
            _, pick = _extract_max(sc, ridx)
            return jnp.where(pick, LOWEST, sc), jnp.where(pick, 1.0, sel)

        _, sel = lax.fori_loop(0, n_top, body, (gs, jnp.zeros((nbm, tq), jnp.float32)))
        open_blk = ((sel > 0.5) & (gs > NEG * 0.5)) | (blk == cur)
        bias_t = jnp.where(open_blk, 0.0, NEG)
        lead = -nbm % 128
        if lead:
            bias_t = jnp.concatenate([jnp.zeros((lead, tq), jnp.float32), bias_t], axis=0)
        bias = bias_t.T[:, lead:].astype(CDT)
        qa_ref[hh] = jnp.concatenate([q, bias], axis=1)
    m_ref[...] = jnp.full(m_ref.shape, NEG, jnp.float32)
    acc_ref[...] = jnp.zeros(acc_ref.shape, jnp.float32)

    def step(kt, masked):
        start = pl.multiple_of(kt * tk, tk)
        for hh in range(2):
            s_ref[hh] = _dot(qa_ref[hh], ka_ref[0, hh, :, pl.ds(start, tk)])
        causal = (start + lax.broadcasted_iota(jnp.int32, (1, tk), 1), s0) if masked else None
        for hh in range(2):
            _flash_tile(s_ref.at[hh], p_ref.at[hh], m_ref.at[hh], acc_ref.at[hh], rc, tq, causal=causal)
            acc_ref[hh] += _dot(p_ref[hh], va_ref[0, hh, pl.ds(start, tk), :])

    def body(kt, carry):
        step(kt, False)
        return carry

    n_full = s0 // tk
    lax.fori_loop(0, n_full, body, 0)
    for d in range(tq // tk):
        step(n_full + d, True)
    outs = []
    for hh in range(2):
        acc = acc_ref[hh]
        outs.append(acc[:, :HEAD_DIM] / acc[:, HEAD_DIM:])
    o_ref[...] = jnp.concatenate(outs, axis=1).astype(o_ref.dtype)


def _moba(proj, mk_aug_t, mv_aug, kmean, batch, seq):
    tq = min(1024, seq)
    tk = 512
    nq = seq // tq
    nbm = seq // MOBA_BLOCK
    n_top = min(MOBA_TOPK, nbm)
    qb = COL_MQ // 128
    aug = HEAD_DIM + nbm
    return pl.pallas_call(
        functools.partial(_moba_kernel, tq=tq, tk=tk, n_top=n_top, rc=64),
        out_shape=jax.ShapeDtypeStruct((batch * seq, MOBA_W), CDT),
        grid=(batch, MOBA_HEADS // 2, nq),
        in_specs=[pl.BlockSpec((tq, 128), lambda b, p, i: (b * nq + i, qb + p)),
                  pl.BlockSpec((1, 2, aug, seq), lambda b, p, i: (b, p, 0, 0)),
                  pl.BlockSpec((1, 2, seq, 128), lambda b, p, i: (b, p, 0, 0)),
                  pl.BlockSpec((1, nbm, 128), lambda b, p, i: (b, 0, p))],
        out_specs=pl.BlockSpec((tq, 128), lambda b, p, i: (b * nq + i, p)),
        scratch_shapes=[pltpu.VMEM((2, tq, aug), CDT), pltpu.VMEM((2, tq, tk), jnp.float32),
                        pltpu.VMEM((2, tq, tk), CDT), pltpu.VMEM((2, tq, 128), jnp.float32),
                        pltpu.VMEM((2, tq, 128), jnp.float32)],
        compiler_params=_params(("arbitrary", "arbitrary", "arbitrary")),
        name="moba",
    )(proj, mk_aug_t, mv_aug, kmean)


def _merge_kernel(oc_ref, os_ref, ow_ref, om_ref, ng_ref, gn_ref, gm_ref, x_ref, ga_ref,
                  ex_ref, wun_ref, wum_ref, wo_ref, o_ref):
    gates = _sigmoid(ng_ref[...].astype(jnp.float32))
    hi, lo = _split_hi_lo(gates)
    e = _dot(hi, ex_ref[...]) + _dot(lo, ex_ref[...])
    o_nsa = (e[:, :NSA_Q] * oc_ref[...].astype(jnp.float32)
             + e[:, NSA_Q:2 * NSA_Q] * os_ref[...].astype(jnp.float32)
             + e[:, 2 * NSA_Q:] * ow_ref[...].astype(jnp.float32))
    y = (_sigmoid(gn_ref[...].astype(jnp.float32)) * _dot(o_nsa.astype(CDT), wun_ref[...])
         + _sigmoid(gm_ref[...].astype(jnp.float32)) * _dot(om_ref[...], wum_ref[...]))
    o_ref[...] = x_ref[...] + ga_ref[0] * _dot(y.astype(CDT), wo_ref[...])


def _merge(o_c, o_s, o_w, o_m, proj, x2d, ga, w_up_nsa, w_up_moba, w_out, seq):
    n, d = x2d.shape
    tm = min(512, seq)
    per_seq = seq // tm
    ng_w = COL_GN - COL_NG
    ex = np.zeros((ng_w, 3 * NSA_Q), np.float32)
    for h in range(NSA_HEADS):
        for j in range(3):
            ex[h * 3 + j, j * NSA_Q + h * HEAD_DIM: j * NSA_Q + (h + 1) * HEAD_DIM] = 1.0
    row = lambda i: (i, 0)
    const = lambda i: (0, 0)
    return pl.pallas_call(
        _merge_kernel,
        out_shape=jax.ShapeDtypeStruct((n, d), jnp.float32),
        grid=(n // tm,),
        in_specs=[pl.BlockSpec((tm, NSA_Q), row), pl.BlockSpec((tm, NSA_Q), row),
                  pl.BlockSpec((tm, NSA_Q), row), pl.BlockSpec((tm, MOBA_W), row),
                  pl.BlockSpec((tm, ng_w), lambda i: (i, COL_NG // ng_w)),
                  pl.BlockSpec((tm, d), lambda i: (i, COL_GN // d)),
                  pl.BlockSpec((tm, d), lambda i: (i, COL_GM // d)),
                  pl.BlockSpec((tm, d), row),
                  pl.BlockSpec((1, 1, d), lambda i: (i // per_seq, 0, 0)),
                  pl.BlockSpec((ng_w, 3 * NSA_Q), const),
                  pl.BlockSpec((NSA_Q, d), const), pl.BlockSpec((MOBA_W, d), const),
                  pl.BlockSpec((d, d), const)],
        out_specs=pl.BlockSpec((tm, d), row),
        compiler_params=_params(("arbitrary",)),
        name="mixer_merge",
    )(o_c, o_s, o_w, o_m, proj, proj, proj, x2d, ga, jnp.asarray(ex, CDT),
      w_up_nsa.astype(CDT), w_up_moba.astype(CDT), w_out.astype(CDT))


def _extract_max(x, ridx):
    m = jnp.max(x, axis=0, keepdims=True)
    idx = jnp.min(jnp.where(x == m, ridx, float(x.shape[0])), axis=0, keepdims=True)
    return m, ridx == idx


def _peer_score_kernel(q_ref, k1_ref, k2_ref, cnt_ref, rk_ref, e1_ref, e2_ref):
    half = PEER_QDIM // 2
    k = PEER_TOPK
    q = q_ref[...]
    s1_all = _dot_nt(k1_ref[...], q[:, :half])
    s2_all = _dot_nt(k2_ref[...], q[:, half:])
    t = 128
    ridx = lax.broadcasted_iota(jnp.int32, (PEER_NKEYS, t), 0).astype(jnp.float32)
    unranked = jnp.full((PEER_NKEYS, t), float(k), jnp.float32)
    chunks = list(range(0, s1_all.shape[1], t))
    shifted = []
    for c0 in chunks:
        for s_all in (s1_all, s2_all):
            s = s_all[:, c0:c0 + t]
            shifted.append(s - jnp.max(s, axis=0, keepdims=True))
    xs = list(shifted)
    rks = [unranked] * len(xs)
    tops = [[] for _ in xs]
    for i in range(k):
        for n in range(len(xs)):
            m, pick = _extract_max(xs[n], ridx)
            xs[n] = jnp.where(pick, LOWEST, xs[n])
            rks[n] = jnp.where(pick, float(i), rks[n])
            tops[n].append(m)

    v2_alls, cands = [], []
    for c in range(len(chunks)):
        v1, v2_all = tops[2 * c], jnp.concatenate(tops[2 * c + 1], axis=0)
        pieces = [v1[i] + v2_all[0:k // (i + 1)] for i in range(k)]
        pad = -sum(p.shape[0] for p in pieces) % 8
        cands.append(jnp.concatenate(pieces + [jnp.full((pad, t), LOWEST, jnp.float32)], axis=0))
        v2_alls.append(v2_all)
    cidx = lax.broadcasted_iota(jnp.int32, cands[0].shape, 0).astype(jnp.float32)
    vals = [[] for _ in chunks]
    for i in range(k):
        for c in range(len(chunks)):
            m, pick = _extract_max(cands[c], cidx)
            cands[c] = jnp.where(pick, LOWEST, cands[c])
            vals[c].append(m)

    for c, c0 in enumerate(chunks):
        v1, v2_all, val = tops[2 * c], v2_alls[c], vals[c]
        tau = val[k - 1]
        z = val[0] - val[0] + 1.0
        for i in range(1, k):
            z = z + jnp.exp(val[i] - val[0])
        cnt = jnp.zeros((PEER_NKEYS, t), jnp.float32)
        for i in range(k):
            n_i = jnp.sum(jnp.where(v1[i] + v2_all >= tau, 1.0, 0.0), axis=0, keepdims=True)
            cnt = jnp.where(rks[2 * c] == float(i), n_i, cnt)
        cnt_ref[0, :, c0:c0 + t] = cnt
        rk_ref[0, :, c0:c0 + t] = rks[2 * c + 1].astype(rk_ref.dtype)
        e1_ref[0, :, c0:c0 + t] = jnp.exp(shifted[2 * c] - val[0]) / z
        e2_ref[0, :, c0:c0 + t] = jnp.exp(shifted[2 * c + 1]).astype(e2_ref.dtype)


def _peer_scores(qp, k1, k2):
    n = qp.shape[0]
    tt = 512
    f32 = jax.ShapeDtypeStruct((PEER_HEADS, PEER_NKEYS, n), jnp.float32)
    cdt = jax.ShapeDtypeStruct((PEER_HEADS, PEER_NKEYS, n), CDT)
    big = pl.BlockSpec((1, PEER_NKEYS, tt), lambda i, h: (h, 0, i))
    return pl.pallas_call(
        _peer_score_kernel,
        out_shape=[f32, cdt, f32, cdt],
        grid=(n // tt, PEER_HEADS),
        in_specs=[pl.BlockSpec((tt, PEER_QDIM), lambda i, h: (i, h)),
                  pl.BlockSpec((PEER_NKEYS, PEER_QDIM // 2), lambda i, h: (0, 0)),
                  pl.BlockSpec((PEER_NKEYS, PEER_QDIM // 2), lambda i, h: (0, 0))],
        out_specs=[big, big, big, big],
        compiler_params=_params(("arbitrary", "arbitrary")),
        name="peer_scores",
    )(qp, k1.astype(CDT), k2.astype(CDT))


def _row_to_rows(row, n):
    tile_rows = 16
    tile = jnp.broadcast_to(row, (tile_rows, row.shape[1])).astype(CDT)
    return jnp.concatenate([tile] * (n // tile_rows), axis=0)


def _peer_expert_kernel(h_ref, u_ref, vt_ref, cnt_ref, rk_ref, e1_ref, e2_ref, x_ref, ga_ref,
                        o_ref, acc_ref, *, eb):
    j = pl.program_id(1)
    tt = h_ref.shape[0]

    @pl.when(j == 0)
    def _():
        acc_ref[...] = jnp.zeros(acc_ref.shape, jnp.float32)

    parts = []
    piece = 2 * PEER_NKEYS
    for e0 in range(0, eb, piece):
        sc = _dot_nt(u_ref[e0:e0 + piece, :], h_ref[...])
        ws = []
        for al in range(e0 // PEER_NKEYS, (e0 + piece) // PEER_NKEYS):
            a = j * (eb // PEER_NKEYS) + al
            w = None
            for hd in range(PEER_HEADS):
                cnt_a = _row_to_rows(cnt_ref[hd, pl.ds(a, 1), :], PEER_NKEYS)
                e1_a = _row_to_rows(e1_ref[hd, pl.ds(a, 1), :], PEER_NKEYS)
                contrib = jnp.where(rk_ref[hd] < cnt_a, e2_ref[hd] * e1_a, jnp.zeros((), CDT))
                w = contrib if w is None else w + contrib
            ws.append(w)
        parts.append(jnp.concatenate(ws, axis=0) * _gelu(sc.astype(CDT)))
    pw = jnp.concatenate(parts, axis=0)
    acc_ref[...] += _dot(vt_ref[...], pw)

    @pl.when(j == pl.num_programs(1) - 1)
    def _():
        o_ref[...] = x_ref[...] + ga_ref[0] * acc_ref[...].T


def _peer_experts(h2, u, v_t, cnt, rk2, e1, e2, x2d, ga, seq):
    n, d = x2d.shape
    tt = min(512, seq)
    per_seq = seq // tt
    eb = 512
    n_blk = u.shape[0] // eb
    big = pl.BlockSpec((PEER_HEADS, PEER_NKEYS, tt), lambda i, j: (0, 0, i))
    return pl.pallas_call(
        functools.partial(_peer_expert_kernel, eb=eb),
        out_shape=jax.ShapeDtypeStruct((n, d), jnp.float32),
        grid=(n // tt, n_blk),
        in_specs=[pl.BlockSpec((tt, d), lambda i, j: (i, 0)),
                  pl.BlockSpec((eb, d), lambda i, j: (j, 0)),
                  pl.BlockSpec((d, eb), lambda i, j: (0, j)),
                  big, big, big, big,
                  pl.BlockSpec((tt, d), lambda i, j: (i, 0)),
                  pl.BlockSpec((1, 1, d), lambda i, j: (i // per_seq, 0, 0))],
        out_specs=pl.BlockSpec((tt, d), lambda i, j: (i, 0)),
        scratch_shapes=[pltpu.VMEM((d, tt), jnp.float32)],
        compiler_params=_params(("arbitrary", "arbitrary")),
        name="peer_experts",
    )(h2, u, v_t, cnt, rk2, e1, e2, x2d, ga)


def _rms_kernel(x_ref, g_ref, o_ref):
    x = x_ref[...]
    ms = jnp.mean(x * x, axis=-1, keepdims=True)
    o_ref[...] = x * lax.rsqrt(ms + RMS_EPS) * g_ref[...]


def _final_norm(x2d, g):
    n, d = x2d.shape
    tm = 512
    return pl.pallas_call(
        _rms_kernel,
        out_shape=jax.ShapeDtypeStruct((n, d), jnp.float32),
        grid=(n // tm,),
        in_specs=[pl.BlockSpec((tm, d), lambda i: (i, 0)), pl.BlockSpec((1, d), lambda i: (0, 0))],
        out_specs=pl.BlockSpec((tm, d), lambda i: (i, 0)),
        compiler_params=_params(("arbitrary",)),
        name="final_rmsnorm",
    )(x2d, g.reshape(1, d))


def _key_major(proj, col, heads, batch, seq):
    k = proj[:, col:col + heads * HEAD_DIM].reshape(batch, seq, heads, HEAD_DIM)
    return jnp.transpose(k, (0, 2, 3, 1))


def _value_with_ones(proj, col, heads, batch, seq):
    v = proj[:, col:col + heads * HEAD_DIM].reshape(batch, seq, heads, HEAD_DIM)
    v = jnp.transpose(v, (0, 2, 1, 3))
    return jnp.concatenate([v, jnp.ones_like(v)], axis=-1)


def kernel(x, c, w_ada, b_ada, g_attn, g_ffn, w_in, cmp_pe, cmp_w1, cmp_w2, w_up_nsa, w_up_moba, w_out,
           peer_wq, peer_k1, peer_k2, peer_u, peer_v, g_final):
    batch, seq, d = x.shape
    depth = w_ada.shape[0]
    n = batch * seq
    nbm = seq // MOBA_BLOCK
    x2d = x.reshape(n, d)
    mod = _adaln_mod(c, w_ada, b_ada)
    rope = _rope_tables(seq)
    blk_onehot = (jnp.arange(nbm)[:, None] == jnp.arange(seq)[None, :] // MOBA_BLOCK).astype(CDT)
    for l in range(depth):
        sh1, sc1, ga1, sh2, sc2, ga2 = [m.reshape(batch, 1, d) for m in jnp.split(mod[l], 6, axis=-1)]
        proj = _norm_mod_matmul(x2d, g_attn[l], sc1, sh1, _reorder_w_in(w_in[l]), seq,
                                rope=rope, n_rope=ROPE_COLS // PROJ_TN)
        cmp_kv = _nsa_compress(proj, cmp_pe[l], cmp_w1[l], cmp_w2[l], batch, seq)
        o_c, selb = _nsa_compressed(proj, cmp_kv, batch, seq)
        o_s = _nsa_selected(proj, _key_major(proj, COL_KS, NSA_KV_HEADS, batch, seq),
                            _value_with_ones(proj, COL_VS, NSA_KV_HEADS, batch, seq), selb, batch, seq)
        o_w = _nsa_window(proj, _key_major(proj, COL_KW, NSA_KV_HEADS, batch, seq), batch, seq)
        mk_t = _key_major(proj, COL_MK, MOBA_HEADS, batch, seq)
        mk_aug = jnp.concatenate(
            [mk_t, jnp.broadcast_to(blk_onehot, (batch, MOBA_HEADS, nbm, seq))], axis=2)
        o_m = _moba(proj, mk_aug, _value_with_ones(proj, COL_MV, MOBA_HEADS, batch, seq),
                    _moba_kmean(proj, batch, seq), batch, seq)
        x2d = _merge(o_c, o_s, o_w, o_m, proj, x2d, ga1, w_up_nsa[l], w_up_moba[l], w_out[l], seq)
        qp, h2 = _norm_mod_matmul(x2d, g_ffn[l], sc2, sh2, peer_wq[l].astype(CDT), seq, emit_h=True)
        cnt, rk2, e1, e2 = _peer_scores(qp, peer_k1[l], peer_k2[l])
        x2d = _peer_experts(h2, peer_u[l].astype(CDT), peer_v[l].T.astype(CDT), cnt, rk2, e1, e2,
                            x2d, ga2, seq)
    return _final_norm(x2d, g_final).reshape(batch, seq, d)
```

```python
import functools

import jax
import jax.numpy as jnp
import numpy as np
from jax import lax
from jax.experimental import pallas as pl
from jax.experimental.pallas import tpu as pltpu

D_MODEL = 1024
HEAD_DIM = 64
ROT_DIM = HEAD_DIM // 4
ROPE_THETA = 500000.0
NSA_HEADS = 8
NSA_KV_HEADS = 2
NSA_GROUP = NSA_HEADS // NSA_KV_HEADS
CMP_LEN = 32
CMP_STRIDE = 16
CMP_HID = 2 * HEAD_DIM
SLC_LEN = 64
SLC_TOPN = 16
WIN = 512
MOBA_HEADS = 8
MOBA_BLOCK = 256
MOBA_TOPK = 3
PEER_HEADS = 8
PEER_NKEYS = 128
PEER_QDIM = 256
PEER_TOPK = 16
RMS_EPS = 1e-6
NEG = -1e30
SEL_FORCE = 1e4
LOWEST = -3.0e38
LOG2E = 1.4426950408889634

NSA_Q = NSA_HEADS * HEAD_DIM
NSA_KV = NSA_KV_HEADS * HEAD_DIM
MOBA_W = MOBA_HEADS * HEAD_DIM

CDT = jnp.bfloat16
V7X_VMEM_LIMIT = 56 * 1024 * 1024

COL_QROT = 0
COL_KS = 512
COL_KW = 640
COL_MQ = 768
COL_MK = 1280
ROPE_COLS = 1792
COL_NG = 1792
COL_GN = 2048
COL_GM = 3072
COL_QRAW = 4096
COL_KC = 4608
COL_VC = 4736
COL_VS = 4864
COL_VW = 4992
COL_MV = 5120
PROJ_COLS = 5632
PROJ_TN = 256


def _params(sem):
    return pltpu.CompilerParams(dimension_semantics=sem, vmem_limit_bytes=V7X_VMEM_LIMIT)


def _dot(a, b):
    return jnp.dot(a, b, preferred_element_type=jnp.float32)


def _dot_nt(a, b):
    return lax.dot_general(a, b, (((1,), (1,)), ((), ())), preferred_element_type=jnp.float32)


def _split_hi_lo(x):
    hi = x.astype(CDT)
    lo = (x - hi.astype(jnp.float32)).astype(CDT)
    return hi, lo


def _gelu(x):
    return 0.5 * x * (1.0 + jnp.tanh(0.7978845608028654 * (x + 0.044715 * (x * x * x))))


def _sigmoid(x):
    return 1.0 / (1.0 + jnp.exp(-x))


def _mod_kernel(c_ref, w_ref, b_ref, o_ref):
    c = c_ref[...]
    sc = c * _sigmoid(c)
    o_ref[0] = jnp.dot(sc, w_ref[0], preferred_element_type=jnp.float32,
                       precision=lax.Precision.HIGHEST) + b_ref[0]


def _adaln_mod(c, w_ada, b_ada):
    depth, d, six_d = w_ada.shape
    b = c.shape[0]
    rows = 8
    c_pad = jnp.zeros((rows, d), jnp.float32).at[:b].set(c)
    tn = 1024
    out = pl.pallas_call(
        _mod_kernel,
        out_shape=jax.ShapeDtypeStruct((depth, rows, six_d), jnp.float32),
        grid=(depth, six_d // tn),
        in_specs=[pl.BlockSpec((rows, d), lambda l, j: (0, 0)),
                  pl.BlockSpec((1, d, tn), lambda l, j: (l, 0, j)),
                  pl.BlockSpec((1, 1, tn), lambda l, j: (l, 0, j))],
        out_specs=pl.BlockSpec((1, rows, tn), lambda l, j: (l, 0, j)),
        compiler_params=_params(("arbitrary", "arbitrary")),
        name="adaln_mod",
    )(c_pad, w_ada, b_ada.reshape(depth, 1, six_d))
    return out[:, :b]


def _nmm_kernel(*refs, n_rope, emit_h):
    if n_rope:
        x_ref, g_ref, sc_ref, sh_ref, w_ref, cos_ref, sa_ref, sb_ref = refs[:8]
        rest = refs[8:]
    else:
        x_ref, g_ref, sc_ref, sh_ref, w_ref = refs[:5]
        rest = refs[5:]
    if emit_h:
        o_ref, ho_ref, h_ref = rest
    else:
        o_ref, h_ref = rest
    j = pl.program_id(1)

    @pl.when(j == 0)
    def _():
        x = x_ref[...]
        ms = jnp.mean(x * x, axis=-1, keepdims=True)
        y = x * lax.rsqrt(ms + RMS_EPS) * g_ref[...]
        h = (y * (1.0 + sc_ref[0]) + sh_ref[0]).astype(h_ref.dtype)
        h_ref[...] = h
        if emit_h:
            ho_ref[...] = h

    acc = _dot(h_ref[...], w_ref[...])
    if n_rope:
        @pl.when(j < n_rope)
        def _():
            cos, sa, sb = cos_ref[...], sa_ref[...], sb_ref[...]
            parts = []
            for c0 in range(0, acc.shape[1], 128):
                a = acc[:, c0:c0 + 128]
                parts.append(a * cos + pltpu.roll(a, 128 - ROT_DIM // 2, axis=1) * sa
                             + pltpu.roll(a, ROT_DIM // 2, axis=1) * sb)
            o_ref[...] = jnp.concatenate(parts, axis=1).astype(o_ref.dtype)

        @pl.when(j >= n_rope)
        def _():
            o_ref[...] = acc.astype(o_ref.dtype)
    else:
        o_ref[...] = acc.astype(o_ref.dtype)


def _norm_mod_matmul(x2d, g, sc, sh, w, seq, rope=None, n_rope=0, emit_h=False, tn=PROJ_TN):
    n, d = x2d.shape
    cols = w.shape[1]
    tm = min(1024, seq)
    per_seq = seq // tm
    in_specs = [pl.BlockSpec((tm, d), lambda i, j: (i, 0)),
                pl.BlockSpec((1, d), lambda i, j: (0, 0)),
                pl.BlockSpec((1, 1, d), lambda i, j: (i // per_seq, 0, 0)),
                pl.BlockSpec((1, 1, d), lambda i, j: (i // per_seq, 0, 0)),
                pl.BlockSpec((d, tn), lambda i, j: (0, j))]
    args = [x2d, g.reshape(1, d), sc, sh, w]
    if n_rope:
        in_specs += [pl.BlockSpec((tm, 128), lambda i, j: (i % per_seq, 0))] * 3
        args += list(rope)
    out_shape = [jax.ShapeDtypeStruct((n, cols), CDT)]
    out_specs = [pl.BlockSpec((tm, tn), lambda i, j: (i, j))]
    if emit_h:
        out_shape.append(jax.ShapeDtypeStruct((n, d), CDT))
        out_specs.append(pl.BlockSpec((tm, d), lambda i, j: (i, 0)))
    res = pl.pallas_call(
        functools.partial(_nmm_kernel, n_rope=n_rope, emit_h=emit_h),
        out_shape=out_shape,
        grid=(n // tm, cols // tn),
        in_specs=in_specs,
        out_specs=out_specs,
        scratch_shapes=[pltpu.VMEM((tm, d), CDT)],
        compiler_params=_params(("arbitrary", "arbitrary")),
        name="norm_mod_matmul",
    )(*args)
    return res if emit_h else res[0]


def _rope_tables(seq):
    half = ROT_DIM // 2
    inv = ROPE_THETA ** (-jnp.arange(half, dtype=jnp.float32) / half)
    ang = jnp.arange(seq, dtype=jnp.float32)[:, None] * inv[None, :]
    cos, sin = jnp.cos(ang), jnp.sin(ang)
    ones = jnp.ones((seq, HEAD_DIM - ROT_DIM), jnp.float32)
    zeros = jnp.zeros((seq, HEAD_DIM - ROT_DIM), jnp.float32)
    zh = jnp.zeros((seq, half), jnp.float32)
    c64 = jnp.concatenate([cos, cos, ones], axis=1)
    sa64 = jnp.concatenate([-sin, zh, zeros], axis=1)
    sb64 = jnp.concatenate([zh, sin, zeros], axis=1)
    return tuple(jnp.concatenate([t, t], axis=1) for t in (c64, sa64, sb64))


def _reorder_w_in(w_in):
    o = 0
    pieces = {}
    for name, size in (("nq", NSA_Q), ("kc", NSA_KV), ("vc", NSA_KV), ("ks", NSA_KV), ("vs", NSA_KV),
                       ("kw", NSA_KV), ("vw", NSA_KV), ("ng", 3 * NSA_HEADS), ("mq", MOBA_W),
                       ("mk", MOBA_W), ("mv", MOBA_W), ("gn", D_MODEL), ("gm", D_MODEL)):
        pieces[name] = w_in[:, o:o + size]
        o += size
    scale = HEAD_DIM ** -0.5 * LOG2E
    pad = jnp.zeros((w_in.shape[0], COL_GN - COL_NG - 3 * NSA_HEADS), w_in.dtype)
    w = jnp.concatenate([pieces["nq"] * scale, pieces["ks"], pieces["kw"], pieces["mq"] * scale,
                         pieces["mk"], pieces["ng"], pad, pieces["gn"], pieces["gm"],
                         pieces["nq"] * scale, pieces["kc"], pieces["vc"], pieces["vs"],
                         pieces["vw"], pieces["mv"]], axis=1)
    assert w.shape[1] == PROJ_COLS
    return w.astype(CDT)


def _compress_kernel(x_ref, w1c_ref, pe_ref, w1_ref, w2_ref, o_ref):
    nb = x_ref.shape[3]
    ab = _dot(x_ref[0, 0, 0], w1c_ref[0])
    a = ab[:, :CMP_HID]
    b_next = pltpu.roll(ab[:, CMP_HID:], nb - 1, axis=0)
    bias = _dot(pe_ref[0], w1_ref[0])[0:1]
    hid = _gelu(a + b_next + bias)
    o_ref[0, 0, 0] = _dot(hid.astype(CDT), w2_ref[0]).astype(o_ref.dtype)


def _nsa_compress(proj, cmp_pe, cmp_w1, cmp_w2, batch, seq):
    nb = seq // CMP_STRIDE
    half = CMP_STRIDE * HEAD_DIM
    kcvc = proj[:, COL_KC:COL_KC + 2 * NSA_KV].reshape(batch, nb, CMP_STRIDE, 2, NSA_KV_HEADS, HEAD_DIM)
    x = jnp.transpose(kcvc, (3, 0, 4, 1, 2, 5)).reshape(2, batch, NSA_KV_HEADS, nb, half)
    w1c = jnp.concatenate([cmp_w1[:, :half], cmp_w1[:, half:]], axis=2).astype(CDT)
    pe = jnp.broadcast_to(cmp_pe.reshape(2, 1, CMP_LEN * HEAD_DIM), (2, 8, CMP_LEN * HEAD_DIM)).astype(CDT)
    return pl.pallas_call(
        _compress_kernel,
        out_shape=jax.ShapeDtypeStruct((2, batch, NSA_KV_HEADS, nb, HEAD_DIM), CDT),
        grid=(2, batch, NSA_KV_HEADS),
        in_specs=[pl.BlockSpec((1, 1, 1, nb, half), lambda w, b, k: (w, b, k, 0, 0)),
                  pl.BlockSpec((1, half, 2 * CMP_HID), lambda w, b, k: (w, 0, 0)),
                  pl.BlockSpec((1, 8, 2 * half), lambda w, b, k: (w, 0, 0)),
                  pl.BlockSpec((1, 2 * half, CMP_HID), lambda w, b, k: (w, 0, 0)),
                  pl.BlockSpec((1, CMP_HID, HEAD_DIM), lambda w, b, k: (w, 0, 0))],
        out_specs=pl.BlockSpec((1, 1, 1, nb, HEAD_DIM), lambda w, b, k: (w, b, k, 0, 0)),
        compiler_params=_params(("arbitrary", "arbitrary", "arbitrary")),
        name="nsa_compress",
    )(x, w1c, pe, cmp_w1.astype(CDT), cmp_w2.astype(CDT))


def _stack_heads(qblk):
    return jnp.concatenate([qblk[:, g * HEAD_DIM:(g + 1) * HEAD_DIM] for g in range(NSA_GROUP)], axis=0)


def _unstack_heads(o, tq):
    return jnp.concatenate([o[g * tq:(g + 1) * tq] for g in range(NSA_GROUP)], axis=1)


def _topk_mask(score, n_sel):
    shape = score.shape
    lane = lax.broadcasted_iota(jnp.int32, shape, 1).astype(jnp.float32)
    width = float(shape[1])

    def body(_, carry):
        sc, sel = carry
        m = jnp.max(sc, axis=-1, keepdims=True)
        idx = jnp.min(jnp.where(sc == m, lane, width), axis=-1, keepdims=True)
        pick = lane == idx
        return jnp.where(pick, LOWEST, sc), jnp.where(pick, 1.0, sel)

    _, sel = lax.fori_loop(0, n_sel, body, (score, jnp.zeros(shape, jnp.float32)))
    return sel


def _nsa_cmp_kernel(q_ref, kc_ref, vc_ref, ov_ref, oc_ref, sb_ref, imp_ref, *, tq, n_sel, n_var):
    i = pl.program_id(2)
    s0 = i * tq
    nb = kc_ref.shape[3]
    ns = ov_ref.shape[1]
    tq_col = s0 + lax.broadcasted_iota(jnp.int32, (tq, 1), 0)
    t4 = jnp.concatenate([tq_col] * NSA_GROUP, axis=0)

    def attend(nk):
        q4 = _stack_heads(q_ref[...])
        s = _dot_nt(q4, kc_ref[0, 0, 0, :nk, :])
        cend = lax.broadcasted_iota(jnp.int32, (1, nk), 1) * CMP_STRIDE + (CMP_LEN - 1)
        mask = cend <= t4
        s = jnp.where(mask, s, NEG)
        m = jnp.max(s, axis=-1, keepdims=True)
        p = jnp.where(mask, jnp.exp2(s - m), 0.0)
        p = p / jnp.maximum(jnp.sum(p, axis=-1, keepdims=True), 1e-30)
        o = _dot(p.astype(CDT), vc_ref[0, 0, 0, :nk, :])
        oc_ref[...] = _unstack_heads(o, tq).astype(oc_ref.dtype)
        psum = p[0:tq]
        for g in range(1, NSA_GROUP):
            psum = psum + p[g * tq:(g + 1) * tq]
        hi, lo = _split_hi_lo(psum)
        imp_ref[...] = _dot(hi, ov_ref[:nk, :]) + _dot(lo, ov_ref[:nk, :])

    tiles_per_var = pl.num_programs(2) // n_var
    for v in range(n_var):
        @pl.when(i // tiles_per_var == v)
        def _(v=v):
            attend(nb * (v + 1) // n_var)

    blk = lax.broadcasted_iota(jnp.int32, (ns, 1), 0)
    cur = (s0 + lax.broadcasted_iota(jnp.int32, (1, tq), 1)) // SLC_LEN
    valid = blk <= cur
    forced = (blk == 0) | (blk == cur) | (blk == cur - 1)
    score = jnp.where(valid, jnp.where(forced, SEL_FORCE, imp_ref[...].T), NEG)
    ridx = lax.broadcasted_iota(jnp.int32, (ns, tq), 0).astype(jnp.float32)

    def body(_, carry):
        sc, sel = carry
        _, pick = _extract_max(sc, ridx)
        return jnp.where(pick, LOWEST, sc), jnp.where(pick, 1.0, sel)

    _, sel = lax.fori_loop(0, n_sel, body, (score, jnp.zeros((ns, tq), jnp.float32)))
    bias = jnp.where((sel > 0.5) & valid, 0.0, NEG).T
    extra = sb_ref.shape[3] - ns
    if extra:
        bias = jnp.concatenate([bias, jnp.full((tq, extra), NEG, jnp.float32)], axis=1)
    sb_ref[0, 0] = bias.astype(sb_ref.dtype)


def _nsa_compressed(proj, cmp_kv, batch, seq):
    tq = 128
    nq = seq // tq
    nb = seq // CMP_STRIDE
    ns = seq // SLC_LEN
    n_sel = min(SLC_TOPN, ns)
    c_start = np.arange(nb) * CMP_STRIDE
    s_start = np.arange(ns) * SLC_LEN
    ov = np.maximum(np.minimum(c_start[:, None] + CMP_LEN, s_start[None, :] + SLC_LEN)
                    - np.maximum(c_start[:, None], s_start[None, :]), 0).astype(np.float32) / CMP_LEN
    ov[nb - 1] = 0.0
    qb = COL_QRAW // (NSA_GROUP * HEAD_DIM)
    return pl.pallas_call(
        functools.partial(_nsa_cmp_kernel, tq=tq, n_sel=n_sel,
                          n_var=4 if (nb % (4 * 128) == 0 and nq % 4 == 0) else 1),
        out_shape=[jax.ShapeDtypeStruct((batch * seq, NSA_Q), CDT),
                   jax.ShapeDtypeStruct((batch, NSA_KV_HEADS, seq, max(ns, 128)), CDT)],
        grid=(batch, NSA_KV_HEADS, nq),
        in_specs=[pl.BlockSpec((tq, NSA_GROUP * HEAD_DIM), lambda b, k, i: (b * nq + i, qb + k)),
                  pl.BlockSpec((1, 1, 1, nb, HEAD_DIM), lambda b, k, i: (0, b, k, 0, 0)),
                  pl.BlockSpec((1, 1, 1, nb, HEAD_DIM), lambda b, k, i: (1, b, k, 0, 0)),
                  pl.BlockSpec((nb, ns), lambda b, k, i: (0, 0))],
        out_specs=[pl.BlockSpec((tq, NSA_GROUP * HEAD_DIM), lambda b, k, i: (b * nq + i, k)),
                   pl.BlockSpec((1, 1, tq, max(ns, 128)), lambda b, k, i: (b, k, i, 0))],
        scratch_shapes=[pltpu.VMEM((tq, ns), jnp.float32)],
        compiler_params=_params(("arbitrary", "arbitrary", "arbitrary")),
        name="nsa_compressed",
    )(proj, cmp_kv, cmp_kv, jnp.asarray(ov, CDT))


def _flash_tile(s_ref, p_ref, m_ref, acc_ref, rc, tq, bias_ref=None, causal=None):
    rows, tk = s_ref.shape

    def tile(r0, c0):
        s = s_ref[r0:r0 + rc, c0:c0 + 128]
        if bias_ref is not None:
            s = s + bias_ref[r0 % tq:r0 % tq + rc, c0:c0 + 128]
        if causal is not None:
            kpos, t0 = causal
            t = t0 + (r0 % tq) + lax.broadcasted_iota(jnp.int32, (rc, 1), 0)
            s = jnp.where(kpos[:, c0:c0 + 128] <= t, s, NEG)
        return s

    for r0 in range(0, rows, rc):
        mx = tile(r0, 0)
        for c0 in range(128, tk, 128):
            mx = jnp.maximum(mx, tile(r0, c0))
        m_prev = m_ref[r0:r0 + rc, :]
        m_new = jnp.maximum(m_prev, jnp.max(mx, axis=-1, keepdims=True))
        acc_ref[r0:r0 + rc, :] = acc_ref[r0:r0 + rc, :] * jnp.exp2(m_prev - m_new)
        m_ref[r0:r0 + rc, :] = m_new
    for r0 in range(0, rows, rc):
        m_new = m_ref[r0:r0 + rc, :]
        for c0 in range(0, tk, 128):
            p_ref[r0:r0 + rc, c0:c0 + 128] = jnp.exp2(tile(r0, c0) - m_new).astype(p_ref.dtype)


def _nsa_sel_kernel(q_ref, kt_ref, va_ref, sb_ref, o_ref,
                    qp_ref, q4_ref, sbf_ref, s_ref, p_ref, m_ref, acc_ref, *, tq, tk, rc):
    i = pl.program_id(2)
    s0 = i * tq
    nsp = sb_ref.shape[3]
    per_tile = tk // SLC_LEN
    q4 = _stack_heads(q_ref[...])
    qp_ref[...] = jnp.concatenate([q4, jnp.zeros_like(q4)], axis=1)
    sbf_ref[...] = sb_ref[0, 0].astype(jnp.float32)
    lane = lax.broadcasted_iota(jnp.int32, (tq, 128), 1)
    bias_lanes = (lane >= HEAD_DIM) & (lane < HEAD_DIM + per_tile)
    m_ref[...] = jnp.full(m_ref.shape, NEG, jnp.float32)
    acc_ref[...] = jnp.zeros(acc_ref.shape, jnp.float32)

    def step(kt, masked):
        start = pl.multiple_of(kt * tk, tk)
        shift = (HEAD_DIM - kt * per_tile + nsp) % nsp
        rolled = pltpu.roll(sbf_ref[...], shift, axis=1)[:, :128].astype(q4_ref.dtype)
        for g in range(NSA_GROUP):
            q4_ref[g * tq:(g + 1) * tq, :] = jnp.where(bias_lanes, rolled, qp_ref[g * tq:(g + 1) * tq, :])
        s_ref[...] = _dot(q4_ref[...], kt_ref[0, 0, :, pl.ds(start, tk)])
        causal = (start + lax.broadcasted_iota(jnp.int32, (1, tk), 1), s0) if masked else None
        _flash_tile(s_ref, p_ref, m_ref, acc_ref, rc, tq, causal=causal)
        acc_ref[...] += _dot(p_ref[...], va_ref[0, 0, pl.ds(start, tk), :])

    def body(kp, carry):
        step(2 * kp, False)
        step(2 * kp + 1, False)
        return carry

    n_full = s0 // tk
    lax.fori_loop(0, n_full // 2, body, 0)

    @pl.when(n_full % 2 == 1)
    def _():
        step(n_full - 1, False)

    step(n_full, True)
    acc = acc_ref[...]
    o = acc[:, :HEAD_DIM] / acc[:, HEAD_DIM:]
    o_ref[...] = _unstack_heads(o, tq).astype(o_ref.dtype)


def _nsa_selected(proj, ks_t, vs_aug, selb, batch, seq):
    tq = 256
    tk = 512
    nq = seq // tq
    nsp = selb.shape[3]
    per_tile = tk // SLC_LEN
    blk_in_tile = (jnp.arange(seq) // SLC_LEN) % per_tile
    onehot = (jnp.arange(128 - HEAD_DIM)[:, None] == blk_in_tile[None, :]).astype(CDT)
    k_aug = jnp.concatenate(
        [ks_t, jnp.broadcast_to(onehot, (batch, NSA_KV_HEADS, 128 - HEAD_DIM, seq))], axis=2)
    qb = COL_QROT // (NSA_GROUP * HEAD_DIM)
    rows = NSA_GROUP * tq
    return pl.pallas_call(
        functools.partial(_nsa_sel_kernel, tq=tq, tk=tk, rc=64),
        out_shape=jax.ShapeDtypeStruct((batch * seq, NSA_Q), CDT),
        grid=(batch, NSA_KV_HEADS, nq),
        in_specs=[pl.BlockSpec((tq, NSA_GROUP * HEAD_DIM), lambda b, k, i: (b * nq + i, qb + k)),
                  pl.BlockSpec((1, 1, 128, seq), lambda b, k, i: (b, k, 0, 0)),
                  pl.BlockSpec((1, 1, seq, 128), lambda b, k, i: (b, k, 0, 0)),
                  pl.BlockSpec((1, 1, tq, nsp), lambda b, k, i: (b, k, i, 0))],
        out_specs=pl.BlockSpec((tq, NSA_GROUP * HEAD_DIM), lambda b, k, i: (b * nq + i, k)),
        scratch_shapes=[pltpu.VMEM((rows, 128), CDT), pltpu.VMEM((rows, 128), CDT),
                        pltpu.VMEM((tq, nsp), jnp.float32), pltpu.VMEM((rows, tk), jnp.float32),
                        pltpu.VMEM((rows, tk), CDT),
                        pltpu.VMEM((rows, 128), jnp.float32), pltpu.VMEM((rows, 128), jnp.float32)],
        compiler_params=_params(("arbitrary", "arbitrary", "arbitrary")),
        name="nsa_selected",
    )(proj, k_aug, vs_aug, selb)


def _nsa_win_kernel(q_ref, kt_ref, v_ref, o_ref, *, tq):
    k = pl.program_id(1)
    i = pl.program_id(2)
    s0 = i * tq
    span = WIN + tq
    start = pl.multiple_of(jnp.maximum(s0 - WIN, 0), tq)
    q4 = _stack_heads(q_ref[...])
    s = _dot(q4, kt_ref[0, 0, :, pl.ds(start, span)])
    tq_col = s0 + lax.broadcasted_iota(jnp.int32, (tq, 1), 0)
    t4 = jnp.concatenate([tq_col] * NSA_GROUP, axis=0)
    kpos = start + lax.broadcasted_iota(jnp.int32, (1, span), 1)
    mask = (kpos <= t4) & (kpos > t4 - WIN)
    s = jnp.where(mask, s, NEG)
    m = jnp.max(s, axis=-1, keepdims=True)
    p = jnp.where(mask, jnp.exp2(s - m), 0.0)
    l = jnp.maximum(jnp.sum(p, axis=-1, keepdims=True), 1e-30)
    o = _dot(p.astype(CDT), v_ref[pl.ds(start, span), :]) / l
    o = jnp.where(k == 0, o[:, :HEAD_DIM], o[:, HEAD_DIM:])
    o_ref[...] = _unstack_heads(o, tq).astype(o_ref.dtype)


def _nsa_window(proj, kw_t, batch, seq):
    tq = 256
    nq = seq // tq
    qb = COL_QROT // (NSA_GROUP * HEAD_DIM)
    vb = COL_VW // 128
    return pl.pallas_call(
        functools.partial(_nsa_win_kernel, tq=tq),
        out_shape=jax.ShapeDtypeStruct((batch * seq, NSA_Q), CDT),
        grid=(batch, NSA_KV_HEADS, nq),
        in_specs=[pl.BlockSpec((tq, NSA_GROUP * HEAD_DIM), lambda b, k, i: (b * nq + i, qb + k)),
                  pl.BlockSpec((1, 1, HEAD_DIM, seq), lambda b, k, i: (b, k, 0, 0)),
                  pl.BlockSpec((seq, 128), lambda b, k, i: (b, vb))],
        out_specs=pl.BlockSpec((tq, NSA_GROUP * HEAD_DIM), lambda b, k, i: (b * nq + i, k)),
        compiler_params=_params(("arbitrary", "arbitrary", "arbitrary")),
        name="nsa_window",
    )(proj, kw_t, proj)


def _moba_mean_kernel(k_ref, o_ref):
    seq, w = k_ref.shape
    nbm = seq // MOBA_BLOCK
    k = k_ref[...].astype(jnp.float32).reshape(nbm, MOBA_BLOCK, w)
    o_ref[0] = jnp.sum(k, axis=1) * (1.0 / MOBA_BLOCK)


def _moba_kmean(proj, batch, seq):
    nbm = seq // MOBA_BLOCK
    kb = COL_MK // 128
    return pl.pallas_call(
        _moba_mean_kernel,
        out_shape=jax.ShapeDtypeStruct((batch, nbm, MOBA_W), jnp.float32),
        grid=(batch, MOBA_W // 128),
        in_specs=[pl.BlockSpec((seq, 128), lambda b, j: (b, kb + j))],
        out_specs=pl.BlockSpec((1, nbm, 128), lambda b, j: (b, 0, j)),
        compiler_params=_params(("arbitrary", "arbitrary")),
        name="moba_kmean",
    )(proj)


def _moba_kernel(q_ref, ka_ref, va_ref, km_ref, o_ref, qa_ref, s_ref, p_ref, m_ref, acc_ref,
                 *, tq, tk, n_top, rc):
    i = pl.program_id(2)
    s0 = i * tq
    nbm = km_ref.shape[1]
    blk = lax.broadcasted_iota(jnp.int32, (nbm, 1), 0)
    cur = (s0 + lax.broadcasted_iota(jnp.int32, (1, tq), 1)) // MOBA_BLOCK
    ridx = lax.broadcasted_iota(jnp.int32, (nbm, tq), 0).astype(jnp.float32)
    for hh in range(2):
        q = q_ref[:, hh * HEAD_DIM:(hh + 1) * HEAD_DIM]
        km_hi, km_lo = _split_hi_lo(km_ref[0][:, hh * HEAD_DIM:(hh + 1) * HEAD_DIM])
        gs = _dot_nt(km_hi, q) + _dot_nt(km_lo, q)
        gs = jnp.where(blk < cur, gs, NEG)

        def body(_, carry):
            sc, sel = carry
            _, pick = _extract_max(sc, ridx)
            return jnp.where(pick, LOWEST, sc), jnp.where(pick, 1.0, sel)

        _, sel = lax.fori_loop(0, n_top, body, (gs, jnp.zeros((nbm, tq), jnp.float32)))
        open_blk = ((sel > 0.5) & (gs > NEG * 0.5)) | (blk == cur)
        bias_t = jnp.where(open_blk, 0.0, NEG)
        lead = -nbm % 128
        if lead:
            bias_t = jnp.concatenate([jnp.zeros((lead, tq), jnp.float32), bias_t], axis=0)
        bias = bias_t.T[:, lead:].astype(CDT)
        qa_ref[hh] = jnp.concatenate([q, bias], axis=1)
    m_ref[...] = jnp.full(m_ref.shape, NEG, jnp.float32)
    acc_ref[...] = jnp.zeros(acc_ref.shape, jnp.float32)

    def step(kt, masked):
        start = pl.multiple_of(kt * tk, tk)
        for hh in range(2):
            s_ref[hh] = _dot(qa_ref[hh], ka_ref[0, hh, :, pl.ds(start, tk)])
        causal = (start + lax.broadcasted_iota(jnp.int32, (1, tk), 1), s0) if masked else None
        for hh in range(2):
            _flash_tile(s_ref.at[hh], p_ref.at[hh], m_ref.at[hh], acc_ref.at[hh], rc, tq, causal=causal)
            acc_ref[hh] += _dot(p_ref[hh], va_ref[0, hh, pl.ds(start, tk), :])

    per_trip = tq // tk

    def body(kp, carry):
        for d in range(per_trip):
            step(kp * per_trip + d, False)
        return carry

    n_full = s0 // tk
    lax.fori_loop(0, n_full // per_trip, body, 0)
    for d in range(per_trip):
        step(n_full + d, True)
    outs = []
    for hh in range(2):
        acc = acc_ref[hh]
        outs.append(acc[:, :HEAD_DIM] / acc[:, HEAD_DIM:])
    o_ref[...] = jnp.concatenate(outs, axis=1).astype(o_ref.dtype)


def _moba(proj, mk_aug_t, mv_aug, kmean, batch, seq):
    tq = min(1024, seq)
    tk = 512
    nq = seq // tq
    nbm = seq // MOBA_BLOCK
    n_top = min(MOBA_TOPK, nbm)
    qb = COL_MQ // 128
    aug = HEAD_DIM + nbm
    return pl.pallas_call(
        functools.partial(_moba_kernel, tq=tq, tk=tk, n_top=n_top, rc=64),
        out_shape=jax.ShapeDtypeStruct((batch * seq, MOBA_W), CDT),
        grid=(batch, MOBA_HEADS // 2, nq),
        in_specs=[pl.BlockSpec((tq, 128), lambda b, p, i: (b * nq + i, qb + p)),
                  pl.BlockSpec((1, 2, aug, seq), lambda b, p, i: (b, p, 0, 0)),
                  pl.BlockSpec((1, 2, seq, 128), lambda b, p, i: (b, p, 0, 0)),
                  pl.BlockSpec((1, nbm, 128), lambda b, p, i: (b, 0, p))],
        out_specs=pl.BlockSpec((tq, 128), lambda b, p, i: (b * nq + i, p)),
        scratch_shapes=[pltpu.VMEM((2, tq, aug), CDT), pltpu.VMEM((2, tq, tk), jnp.float32),
                        pltpu.VMEM((2, tq, tk), CDT), pltpu.VMEM((2, tq, 128), jnp.float32),
                        pltpu.VMEM((2, tq, 128), jnp.float32)],
        compiler_params=_params(("arbitrary", "arbitrary", "arbitrary")),
        name="moba",
    )(proj, mk_aug_t, mv_aug, kmean)


def _merge_kernel(oc_ref, os_ref, ow_ref, om_ref, ng_ref, gn_ref, gm_ref, x_ref, ga_ref,
                  ex_ref, wun_ref, wum_ref, wo_ref, o_ref):
    gates = _sigmoid(ng_ref[...].astype(jnp.float32))
    hi, lo = _split_hi_lo(gates)
    e = _dot(hi, ex_ref[...]) + _dot(lo, ex_ref[...])
    o_nsa = (e[:, :NSA_Q] * oc_ref[...].astype(jnp.float32)
             + e[:, NSA_Q:2 * NSA_Q] * os_ref[...].astype(jnp.float32)
             + e[:, 2 * NSA_Q:] * ow_ref[...].astype(jnp.float32))
    y = (_sigmoid(gn_ref[...].astype(jnp.float32)) * _dot(o_nsa.astype(CDT), wun_ref[...])
         + _sigmoid(gm_ref[...].astype(jnp.float32)) * _dot(om_ref[...], wum_ref[...]))
    o_ref[...] = x_ref[...] + ga_ref[0] * _dot(y.astype(CDT), wo_ref[...])


def _merge(o_c, o_s, o_w, o_m, proj, x2d, ga, w_up_nsa, w_up_moba, w_out, seq):
    n, d = x2d.shape
    tm = min(512, seq)
    per_seq = seq // tm
    ng_w = COL_GN - COL_NG
    ex = np.zeros((ng_w, 3 * NSA_Q), np.float32)
    for h in range(NSA_HEADS):
        for j in range(3):
            ex[h * 3 + j, j * NSA_Q + h * HEAD_DIM: j * NSA_Q + (h + 1) * HEAD_DIM] = 1.0
    row = lambda i: (i, 0)
    const = lambda i: (0, 0)
    return pl.pallas_call(
        _merge_kernel,
        out_shape=jax.ShapeDtypeStruct((n, d), jnp.float32),
        grid=(n // tm,),
        in_specs=[pl.BlockSpec((tm, NSA_Q), row), pl.BlockSpec((tm, NSA_Q), row),
                  pl.BlockSpec((tm, NSA_Q), row), pl.BlockSpec((tm, MOBA_W), row),
                  pl.BlockSpec((tm, ng_w), lambda i: (i, COL_NG // ng_w)),
                  pl.BlockSpec((tm, d), lambda i: (i, COL_GN // d)),
                  pl.BlockSpec((tm, d), lambda i: (i, COL_GM // d)),
                  pl.BlockSpec((tm, d), row),
                  pl.BlockSpec((1, 1, d), lambda i: (i // per_seq, 0, 0)),
                  pl.BlockSpec((ng_w, 3 * NSA_Q), const),
                  pl.BlockSpec((NSA_Q, d), const), pl.BlockSpec((MOBA_W, d), const),
                  pl.BlockSpec((d, d), const)],
        out_specs=pl.BlockSpec((tm, d), row),
        compiler_params=_params(("arbitrary",)),
        name="mixer_merge",
    )(o_c, o_s, o_w, o_m, proj, proj, proj, x2d, ga, jnp.asarray(ex, CDT),
      w_up_nsa.astype(CDT), w_up_moba.astype(CDT), w_out.astype(CDT))


def _extract_max(x, ridx):
    m = jnp.max(x, axis=0, keepdims=True)
    idx = jnp.min(jnp.where(x == m, ridx, float(x.shape[0])), axis=0, keepdims=True)
    return m, ridx == idx


def _peer_score_kernel(q_ref, k1_ref, k2_ref, cnt_ref, rk_ref, e1_ref, e2_ref):
    half = PEER_QDIM // 2
    k = PEER_TOPK
    q = q_ref[...]
    s1_all = _dot_nt(k1_ref[...], q[:, :half])
    s2_all = _dot_nt(k2_ref[...], q[:, half:])
    t = 128
    ridx = lax.broadcasted_iota(jnp.int32, (PEER_NKEYS, t), 0).astype(jnp.float32)
    unranked = jnp.full((PEER_NKEYS, t), float(k), jnp.float32)
    chunks = list(range(0, s1_all.shape[1], t))
    shifted = []
    for c0 in chunks:
        for s_all in (s1_all, s2_all):
            s = s_all[:, c0:c0 + t]
            shifted.append(s - jnp.max(s, axis=0, keepdims=True))
    xs = list(shifted)
    rks = [unranked] * len(xs)
    tops = [[] for _ in xs]
    for i in range(k):
        for n in range(len(xs)):
            m, pick = _extract_max(xs[n], ridx)
            xs[n] = jnp.where(pick, LOWEST, xs[n])
            rks[n] = jnp.where(pick, float(i), rks[n])
            tops[n].append(m)

    v2_alls, cands = [], []
    for c in range(len(chunks)):
        v1, v2_all = tops[2 * c], jnp.concatenate(tops[2 * c + 1], axis=0)
        pieces = [v1[i] + v2_all[0:k // (i + 1)] for i in range(k)]
        pad = -sum(p.shape[0] for p in pieces) % 8
        cands.append(jnp.concatenate(pieces + [jnp.full((pad, t), LOWEST, jnp.float32)], axis=0))
        v2_alls.append(v2_all)
    cidx = lax.broadcasted_iota(jnp.int32, cands[0].shape, 0).astype(jnp.float32)
    vals = [[] for _ in chunks]
    for i in range(k):
        for c in range(len(chunks)):
            m, pick = _extract_max(cands[c], cidx)
            cands[c] = jnp.where(pick, LOWEST, cands[c])
            vals[c].append(m)

    for c, c0 in enumerate(chunks):
        v1, v2_all, val = tops[2 * c], v2_alls[c], vals[c]
        tau = val[k - 1]
        z = val[0] - val[0] + 1.0
        for i in range(1, k):
            z = z + jnp.exp(val[i] - val[0])
        cnt = jnp.zeros((PEER_NKEYS, t), jnp.float32)
        for i in range(k):
            n_i = jnp.sum(jnp.where(v1[i] + v2_all >= tau, 1.0, 0.0), axis=0, keepdims=True)
            cnt = jnp.where(rks[2 * c] == float(i), n_i, cnt)
        cnt_ref[0, :, c0:c0 + t] = cnt
        rk_ref[0, :, c0:c0 + t] = rks[2 * c + 1].astype(rk_ref.dtype)
        e1_ref[0, :, c0:c0 + t] = jnp.exp(shifted[2 * c] - val[0]) / z
        e2_ref[0, :, c0:c0 + t] = jnp.exp(shifted[2 * c + 1]).astype(e2_ref.dtype)


def _peer_scores(qp, k1, k2):
    n = qp.shape[0]
    tt = 512
    f32 = jax.ShapeDtypeStruct((PEER_HEADS, PEER_NKEYS, n), jnp.float32)
    cdt = jax.ShapeDtypeStruct((PEER_HEADS, PEER_NKEYS, n), CDT)
    big = pl.BlockSpec((1, PEER_NKEYS, tt), lambda i, h: (h, 0, i))
    return pl.pallas_call(
        _peer_score_kernel,
        out_shape=[f32, cdt, f32, cdt],
        grid=(n // tt, PEER_HEADS),
        in_specs=[pl.BlockSpec((tt, PEER_QDIM), lambda i, h: (i, h)),
                  pl.BlockSpec((PEER_NKEYS, PEER_QDIM // 2), lambda i, h: (0, 0)),
                  pl.BlockSpec((PEER_NKEYS, PEER_QDIM // 2), lambda i, h: (0, 0))],
        out_specs=[big, big, big, big],
        compiler_params=_params(("arbitrary", "arbitrary")),
        name="peer_scores",
    )(qp, k1.astype(CDT), k2.astype(CDT))


def _row_to_rows(row, n):
    tile_rows = 16
    tile = jnp.broadcast_to(row, (tile_rows, row.shape[1])).astype(CDT)
    return jnp.concatenate([tile] * (n // tile_rows), axis=0)


def _peer_expert_kernel(h_ref, u_ref, vt_ref, cnt_ref, rk_ref, e1_ref, e2_ref, x_ref, ga_ref,
                        o_ref, acc_ref, *, eb):
    j = pl.program_id(1)
    tt = h_ref.shape[0]

    @pl.when(j == 0)
    def _():
        acc_ref[...] = jnp.zeros(acc_ref.shape, jnp.float32)

    parts = []
    piece = 2 * PEER_NKEYS
    for e0 in range(0, eb, piece):
        sc = _dot_nt(u_ref[e0:e0 + piece, :], h_ref[...])
        ws = []
        for al in range(e0 // PEER_NKEYS, (e0 + piece) // PEER_NKEYS):
            a = j * (eb // PEER_NKEYS) + al
            w = None
            for hd in range(PEER_HEADS):
                cnt_a = _row_to_rows(cnt_ref[hd, pl.ds(a, 1), :], PEER_NKEYS)
                e1_a = _row_to_rows(e1_ref[hd, pl.ds(a, 1), :], PEER_NKEYS)
                contrib = jnp.where(rk_ref[hd] < cnt_a, e2_ref[hd] * e1_a, jnp.zeros((), CDT))
                w = contrib if w is None else w + contrib
            ws.append(w)
        parts.append(jnp.concatenate(ws, axis=0) * _gelu(sc.astype(CDT)))
    pw = jnp.concatenate(parts, axis=0)
    acc_ref[...] += _dot(vt_ref[...], pw)

    @pl.when(j == pl.num_programs(1) - 1)
    def _():
        o_ref[...] = x_ref[...] + ga_ref[0] * acc_ref[...].T


def _peer_experts(h2, u, v_t, cnt, rk2, e1, e2, x2d, ga, seq):
    n, d = x2d.shape
    tt = min(512, seq)
    per_seq = seq // tt
    eb = 2048
    n_blk = u.shape[0] // eb
    big = pl.BlockSpec((PEER_HEADS, PEER_NKEYS, tt), lambda i, j: (0, 0, i))
    return pl.pallas_call(
        functools.partial(_peer_expert_kernel, eb=eb),
        out_shape=jax.ShapeDtypeStruct((n, d), jnp.float32),
        grid=(n // tt, n_blk),
        in_specs=[pl.BlockSpec((tt, d), lambda i, j: (i, 0)),
                  pl.BlockSpec((eb, d), lambda i, j: (j, 0)),
                  pl.BlockSpec((d, eb), lambda i, j: (0, j)),
                  big, big, big, big,
                  pl.BlockSpec((tt, d), lambda i, j: (i, 0)),
                  pl.BlockSpec((1, 1, d), lambda i, j: (i // per_seq, 0, 0))],
        out_specs=pl.BlockSpec((tt, d), lambda i, j: (i, 0)),
        scratch_shapes=[pltpu.VMEM((d, tt), jnp.float32)],
        compiler_params=_params(("arbitrary", "arbitrary")),
        name="peer_experts",
    )(h2, u, v_t, cnt, rk2, e1, e2, x2d, ga)


def _rms_kernel(x_ref, g_ref, o_ref):
    x = x_ref[...]
    ms = jnp.mean(x * x, axis=-1, keepdims=True)
    o_ref[...] = x * lax.rsqrt(ms + RMS_EPS) * g_ref[...]


def _final_norm(x2d, g):
    n, d = x2d.shape
    tm = 512
    return pl.pallas_call(
        _rms_kernel,
        out_shape=jax.ShapeDtypeStruct((n, d), jnp.float32),
        grid=(n // tm,),
        in_specs=[pl.BlockSpec((tm, d), lambda i: (i, 0)), pl.BlockSpec((1, d), lambda i: (0, 0))],
        out_specs=pl.BlockSpec((tm, d), lambda i: (i, 0)),
        compiler_params=_params(("arbitrary",)),
        name="final_rmsnorm",
    )(x2d, g.reshape(1, d))


def _key_major(proj, col, heads, batch, seq):
    k = proj[:, col:col + heads * HEAD_DIM].reshape(batch, seq, heads, HEAD_DIM)
    return jnp.transpose(k, (0, 2, 3, 1))


def _value_with_ones(proj, col, heads, batch, seq):
    v = proj[:, col:col + heads * HEAD_DIM].reshape(batch, seq, heads, HEAD_DIM)
    v = jnp.transpose(v, (0, 2, 1, 3))
    return jnp.concatenate([v, jnp.ones_like(v)], axis=-1)


def kernel(x, c, w_ada, b_ada, g_attn, g_ffn, w_in, cmp_pe, cmp_w1, cmp_w2, w_up_nsa, w_up_moba, w_out,
           peer_wq, peer_k1, peer_k2, peer_u, peer_v, g_final):
    batch, seq, d = x.shape
    depth = w_ada.shape[0]
    n = batch * seq
    nbm = seq // MOBA_BLOCK
    x2d = x.reshape(n, d)
    mod = _adaln_mod(c, w_ada, b_ada)
    rope = _rope_tables(seq)
    blk_onehot = (jnp.arange(nbm)[:, None] == jnp.arange(seq)[None, :] // MOBA_BLOCK).astype(CDT)
    for l in range(depth):
        sh1, sc1, ga1, sh2, sc2, ga2 = [m.reshape(batch, 1, d) for m in jnp.split(mod[l], 6, axis=-1)]
        proj = _norm_mod_matmul(x2d, g_attn[l], sc1, sh1, _reorder_w_in(w_in[l]), seq,
                                rope=rope, n_rope=ROPE_COLS // PROJ_TN)
        cmp_kv = _nsa_compress(proj, cmp_pe[l], cmp_w1[l], cmp_w2[l], batch, seq)
        o_c, selb = _nsa_compressed(proj, cmp_kv, batch, seq)
        o_s = _nsa_selected(proj, _key_major(proj, COL_KS, NSA_KV_HEADS, batch, seq),
                            _value_with_ones(proj, COL_VS, NSA_KV_HEADS, batch, seq), selb, batch, seq)
        o_w = _nsa_window(proj, _key_major(proj, COL_KW, NSA_KV_HEADS, batch, seq), batch, seq)
        mk_t = _key_major(proj, COL_MK, MOBA_HEADS, batch, seq)
        mk_aug = jnp.concatenate(
            [mk_t, jnp.broadcast_to(blk_onehot, (batch, MOBA_HEADS, nbm, seq))], axis=2)
        o_m = _moba(proj, mk_aug, _value_with_ones(proj, COL_MV, MOBA_HEADS, batch, seq),
                    _moba_kmean(proj, batch, seq), batch, seq)
        x2d = _merge(o_c, o_s, o_w, o_m, proj, x2d, ga1, w_up_nsa[l], w_up_moba[l], w_out[l], seq)
        qp, h2 = _norm_mod_matmul(x2d, g_ffn[l], sc2, sh2, peer_wq[l].astype(CDT), seq, emit_h=True)
        cnt, rk2, e1, e2 = _peer_scores(qp, peer_k1[l], peer_k2[l])
        x2d = _peer_experts(h2, peer_u[l].astype(CDT), peer_v[l].T.astype(CDT), cnt, rk2, e1, e2,
                            x2d, ga2, seq)
    return _final_norm(x2d, g_final).reshape(batch, seq, d)
```

```python
import functools

import jax
import jax.numpy as jnp
import numpy as np
from jax import lax
from jax.experimental import pallas as pl
from jax.experimental.pallas import tpu as pltpu

D_MODEL = 1024
HEAD_DIM = 64
ROT_DIM = HEAD_DIM // 4
ROPE_THETA = 500000.0
NSA_HEADS = 8
NSA_KV_HEADS = 2
NSA_GROUP = NSA_HEADS // NSA_KV_HEADS
CMP_LEN = 32
CMP_STRIDE = 16
CMP_HID = 2 * HEAD_DIM
SLC_LEN = 64
SLC_TOPN = 16
WIN = 512
MOBA_HEADS = 8
MOBA_BLOCK = 256
MOBA_TOPK = 3
PEER_HEADS = 8
PEER_NKEYS = 128
PEER_QDIM = 256
PEER_TOPK = 16
RMS_EPS = 1e-6
NEG = -1e30
SEL_FORCE = 1e4
LOWEST = -3.0e38
LOG2E = 1.4426950408889634

NSA_Q = NSA_HEADS * HEAD_DIM
NSA_KV = NSA_KV_HEADS * HEAD_DIM
MOBA_W = MOBA_HEADS * HEAD_DIM

CDT = jnp.bfloat16
V7X_VMEM_LIMIT = 56 * 1024 * 1024

COL_QROT = 0
COL_KS = 512
COL_KW = 640
COL_MQ = 768
COL_MK = 1280
ROPE_COLS = 1792
COL_NG = 1792
COL_GN = 2048
COL_GM = 3072
COL_QRAW = 4096
COL_KC = 4608
COL_VC = 4736
COL_VS = 4864
COL_VW = 4992
COL_MV = 5120
PROJ_COLS = 5632
PROJ_TN = 256


def _params(sem):
    return pltpu.CompilerParams(dimension_semantics=sem, vmem_limit_bytes=V7X_VMEM_LIMIT)


def _dot(a, b):
    return jnp.dot(a, b, preferred_element_type=jnp.float32)


def _dot_nt(a, b):
    return lax.dot_general(a, b, (((1,), (1,)), ((), ())), preferred_element_type=jnp.float32)


def _split_hi_lo(x):
    hi = x.astype(CDT)
    lo = (x - hi.astype(jnp.float32)).astype(CDT)
    return hi, lo


def _gelu(x):
    return 0.5 * x * (1.0 + jnp.tanh(0.7978845608028654 * (x + 0.044715 * (x * x * x))))


def _sigmoid(x):
    return 1.0 / (1.0 + jnp.exp(-x))


def _mod_kernel(c_ref, w_ref, b_ref, o_ref):
    c = c_ref[...]
    sc = c * _sigmoid(c)
    o_ref[0] = jnp.dot(sc, w_ref[0], preferred_element_type=jnp.float32,
                       precision=lax.Precision.HIGHEST) + b_ref[0]


def _adaln_mod(c, w_ada, b_ada):
    depth, d, six_d = w_ada.shape
    b = c.shape[0]
    rows = 8
    c_pad = jnp.zeros((rows, d), jnp.float32).at[:b].set(c)
    tn = 1024
    out = pl.pallas_call(
        _mod_kernel,
        out_shape=jax.ShapeDtypeStruct((depth, rows, six_d), jnp.float32),
        grid=(depth, six_d // tn),
        in_specs=[pl.BlockSpec((rows, d), lambda l, j: (0, 0)),
                  pl.BlockSpec((1, d, tn), lambda l, j: (l, 0, j)),
                  pl.BlockSpec((1, 1, tn), lambda l, j: (l, 0, j))],
        out_specs=pl.BlockSpec((1, rows, tn), lambda l, j: (l, 0, j)),
        compiler_params=_params(("arbitrary", "arbitrary")),
        name="adaln_mod",
    )(c_pad, w_ada, b_ada.reshape(depth, 1, six_d))
    return out[:, :b]


def _nmm_kernel(*refs, n_rope, emit_h):
    if n_rope:
        x_ref, g_ref, sc_ref, sh_ref, w_ref, cos_ref, sa_ref, sb_ref = refs[:8]
        rest = refs[8:]
    else:
        x_ref, g_ref, sc_ref, sh_ref, w_ref = refs[:5]
        rest = refs[5:]
    if emit_h:
        o_ref, ho_ref, h_ref = rest
    else:
        o_ref, h_ref = rest
    j = pl.program_id(1)

    @pl.when(j == 0)
    def _():
        x = x_ref[...]
        ms = jnp.mean(x * x, axis=-1, keepdims=True)
        y = x * lax.rsqrt(ms + RMS_EPS) * g_ref[...]
        h = (y * (1.0 + sc_ref[0]) + sh_ref[0]).astype(h_ref.dtype)
        h_ref[...] = h
        if emit_h:
            ho_ref[...] = h

    acc = _dot(h_ref[...], w_ref[...])
    if n_rope:
        @pl.when(j < n_rope)
        def _():
            cos, sa, sb = cos_ref[...], sa_ref[...], sb_ref[...]
            parts = []
            for c0 in range(0, acc.shape[1], 128):
                a = acc[:, c0:c0 + 128]
                parts.append(a * cos + pltpu.roll(a, 128 - ROT_DIM // 2, axis=1) * sa
                             + pltpu.roll(a, ROT_DIM // 2, axis=1) * sb)
            o_ref[...] = jnp.concatenate(parts, axis=1).astype(o_ref.dtype)

        @pl.when(j >= n_rope)
        def _():
            o_ref[...] = acc.astype(o_ref.dtype)
    else:
        o_ref[...] = acc.astype(o_ref.dtype)


def _norm_mod_matmul(x2d, g, sc, sh, w, seq, rope=None, n_rope=0, emit_h=False, tn=PROJ_TN):
    n, d = x2d.shape
    cols = w.shape[1]
    tm = min(1024, seq)
    per_seq = seq // tm
    in_specs = [pl.BlockSpec((tm, d), lambda i, j: (i, 0)),
                pl.BlockSpec((1, d), lambda i, j: (0, 0)),
                pl.BlockSpec((1, 1, d), lambda i, j: (i // per_seq, 0, 0)),
                pl.BlockSpec((1, 1, d), lambda i, j: (i // per_seq, 0, 0)),
                pl.BlockSpec((d, tn), lambda i, j: (0, j))]
    args = [x2d, g.reshape(1, d), sc, sh, w]
    if n_rope:
        in_specs += [pl.BlockSpec((tm, 128), lambda i, j: (i % per_seq, 0))] * 3
        args += list(rope)
    out_shape = [jax.ShapeDtypeStruct((n, cols), CDT)]
    out_specs = [pl.BlockSpec((tm, tn), lambda i, j: (i, j))]
    if emit_h:
        out_shape.append(jax.ShapeDtypeStruct((n, d), CDT))
        out_specs.append(pl.BlockSpec((tm, d), lambda i, j: (i, 0)))
    res = pl.pallas_call(
        functools.partial(_nmm_kernel, n_rope=n_rope, emit_h=emit_h),
        out_shape=out_shape,
        grid=(n // tm, cols // tn),
        in_specs=in_specs,
        out_specs=out_specs,
        scratch_shapes=[pltpu.VMEM((tm, d), CDT)],
        compiler_params=_params(("arbitrary", "arbitrary")),
        name="norm_mod_matmul",
    )(*args)
    return res if emit_h else res[0]


def _rope_tables(seq):
    half = ROT_DIM // 2
    inv = ROPE_THETA ** (-jnp.arange(half, dtype=jnp.float32) / half)
    ang = jnp.arange(seq, dtype=jnp.float32)[:, None] * inv[None, :]
    cos, sin = jnp.cos(ang), jnp.sin(ang)
    ones = jnp.ones((seq, HEAD_DIM - ROT_DIM), jnp.float32)
    zeros = jnp.zeros((seq, HEAD_DIM - ROT_DIM), jnp.float32)
    zh = jnp.zeros((seq, half), jnp.float32)
    c64 = jnp.concatenate([cos, cos, ones], axis=1)
    sa64 = jnp.concatenate([-sin, zh, zeros], axis=1)
    sb64 = jnp.concatenate([zh, sin, zeros], axis=1)
    return tuple(jnp.concatenate([t, t], axis=1) for t in (c64, sa64, sb64))


def _reorder_w_in(w_in):
    o = 0
    pieces = {}
    for name, size in (("nq", NSA_Q), ("kc", NSA_KV), ("vc", NSA_KV), ("ks", NSA_KV), ("vs", NSA_KV),
                       ("kw", NSA_KV), ("vw", NSA_KV), ("ng", 3 * NSA_HEADS), ("mq", MOBA_W),
                       ("mk", MOBA_W), ("mv", MOBA_W), ("gn", D_MODEL), ("gm", D_MODEL)):
        pieces[name] = w_in[:, o:o + size]
        o += size
    scale = HEAD_DIM ** -0.5 * LOG2E
    pad = jnp.zeros((w_in.shape[0], COL_GN - COL_NG - 3 * NSA_HEADS), w_in.dtype)
    w = jnp.concatenate([pieces["nq"] * scale, pieces["ks"], pieces["kw"], pieces["mq"] * scale,
                         pieces["mk"], pieces["ng"], pad, pieces["gn"], pieces["gm"],
                         pieces["nq"] * scale, pieces["kc"], pieces["vc"], pieces["vs"],
                         pieces["vw"], pieces["mv"]], axis=1)
    assert w.shape[1] == PROJ_COLS
    return w.astype(CDT)


def _compress_kernel(x_ref, w1c_ref, pe_ref, w1_ref, w2_ref, o_ref):
    nb = x_ref.shape[3]
    ab = _dot(x_ref[0, 0, 0], w1c_ref[0])
    a = ab[:, :CMP_HID]
    b_next = pltpu.roll(ab[:, CMP_HID:], nb - 1, axis=0)
    bias = _dot(pe_ref[0], w1_ref[0])[0:1]
    hid = _gelu(a + b_next + bias)
    o_ref[0, 0, 0] = _dot(hid.astype(CDT), w2_ref[0]).astype(o_ref.dtype)


def _nsa_compress(proj, cmp_pe, cmp_w1, cmp_w2, batch, seq):
    nb = seq // CMP_STRIDE
    half = CMP_STRIDE * HEAD_DIM
    kcvc = proj[:, COL_KC:COL_KC + 2 * NSA_KV].reshape(batch, nb, CMP_STRIDE, 2, NSA_KV_HEADS, HEAD_DIM)
    x = jnp.transpose(kcvc, (3, 0, 4, 1, 2, 5)).reshape(2, batch, NSA_KV_HEADS, nb, half)
    w1c = jnp.concatenate([cmp_w1[:, :half], cmp_w1[:, half:]], axis=2).astype(CDT)
    pe = jnp.broadcast_to(cmp_pe.reshape(2, 1, CMP_LEN * HEAD_DIM), (2, 8, CMP_LEN * HEAD_DIM)).astype(CDT)
    return pl.pallas_call(
        _compress_kernel,
        out_shape=jax.ShapeDtypeStruct((2, batch, NSA_KV_HEADS, nb, HEAD_DIM), CDT),
        grid=(2, batch, NSA_KV_HEADS),
        in_specs=[pl.BlockSpec((1, 1, 1, nb, half), lambda w, b, k: (w, b, k, 0, 0)),
                  pl.BlockSpec((1, half, 2 * CMP_HID), lambda w, b, k: (w, 0, 0)),
                  pl.BlockSpec((1, 8, 2 * half), lambda w, b, k: (w, 0, 0)),
                  pl.BlockSpec((1, 2 * half, CMP_HID), lambda w, b, k: (w, 0, 0)),
                  pl.BlockSpec((1, CMP_HID, HEAD_DIM), lambda w, b, k: (w, 0, 0))],
        out_specs=pl.BlockSpec((1, 1, 1, nb, HEAD_DIM), lambda w, b, k: (w, b, k, 0, 0)),
        compiler_params=_params(("arbitrary", "arbitrary", "arbitrary")),
        name="nsa_compress",
    )(x, w1c, pe, cmp_w1.astype(CDT), cmp_w2.astype(CDT))


def _stack_heads(qblk):
    return jnp.concatenate([qblk[:, g * HEAD_DIM:(g + 1) * HEAD_DIM] for g in range(NSA_GROUP)], axis=0)


def _unstack_heads(o, tq):
    return jnp.concatenate([o[g * tq:(g + 1) * tq] for g in range(NSA_GROUP)], axis=1)


def _topk_mask(score, n_sel):
    shape = score.shape
    lane = lax.broadcasted_iota(jnp.int32, shape, 1).astype(jnp.float32)
    width = float(shape[1])

    def body(_, carry):
        sc, sel = carry
        m = jnp.max(sc, axis=-1, keepdims=True)
        idx = jnp.min(jnp.where(sc == m, lane, width), axis=-1, keepdims=True)
        pick = lane == idx
        return jnp.where(pick, LOWEST, sc), jnp.where(pick, 1.0, sel)

    _, sel = lax.fori_loop(0, n_sel, body, (score, jnp.zeros(shape, jnp.float32)))
    return sel


def _nsa_cmp_kernel(q_ref, kc_ref, vc_ref, ov_ref, oc_ref, sb_ref, imp_ref, *, tq, n_sel, n_var):
    i = pl.program_id(2)
    s0 = i * tq
    nb = kc_ref.shape[3]
    ns = ov_ref.shape[1]
    tq_col = s0 + lax.broadcasted_iota(jnp.int32, (tq, 1), 0)
    t4 = jnp.concatenate([tq_col] * NSA_GROUP, axis=0)

    def attend(nk):
        q4 = _stack_heads(q_ref[...])
        s = _dot_nt(q4, kc_ref[0, 0, 0, :nk, :])
        cend = lax.broadcasted_iota(jnp.int32, (1, nk), 1) * CMP_STRIDE + (CMP_LEN - 1)
        mask = cend <= t4
        s = jnp.where(mask, s, NEG)
        m = jnp.max(s, axis=-1, keepdims=True)
        p = jnp.where(mask, jnp.exp2(s - m), 0.0)
        p = p / jnp.maximum(jnp.sum(p, axis=-1, keepdims=True), 1e-30)
        o = _dot(p.astype(CDT), vc_ref[0, 0, 0, :nk, :])
        oc_ref[...] = _unstack_heads(o, tq).astype(oc_ref.dtype)
        psum = p[0:tq]
        for g in range(1, NSA_GROUP):
            psum = psum + p[g * tq:(g + 1) * tq]
        hi, lo = _split_hi_lo(psum)
        imp_ref[...] = _dot(hi, ov_ref[:nk, :]) + _dot(lo, ov_ref[:nk, :])

    tiles_per_var = pl.num_programs(2) // n_var
    for v in range(n_var):
        @pl.when(i // tiles_per_var == v)
        def _(v=v):
            attend(nb * (v + 1) // n_var)

    blk = lax.broadcasted_iota(jnp.int32, (ns, 1), 0)
    lanes = min(128, tq)
    ridx = lax.broadcasted_iota(jnp.int32, (ns, lanes), 0).astype(jnp.float32)
    imp_t = imp_ref[...].T
    scores, valids = [], []
    for c0 in range(0, tq, lanes):
        cur = (s0 + c0 + lax.broadcasted_iota(jnp.int32, (1, lanes), 1)) // SLC_LEN
        valid = blk <= cur
        forced = (blk == 0) | (blk == cur) | (blk == cur - 1)
        scores.append(jnp.where(valid, jnp.where(forced, SEL_FORCE, imp_t[:, c0:c0 + lanes]), NEG))
        valids.append(valid)
    sels = [jnp.zeros((ns, lanes), jnp.float32)] * len(scores)
    for _ in range(n_sel):
        for c in range(len(scores)):
            _, pick = _extract_max(scores[c], ridx)
            scores[c] = jnp.where(pick, LOWEST, scores[c])
            sels[c] = jnp.where(pick, 1.0, sels[c])
    bias = jnp.concatenate([jnp.where((sel > 0.5) & valid, 0.0, NEG) for sel, valid in zip(sels, valids)],
                           axis=1).T
    extra = sb_ref.shape[3] - ns
    if extra:
        bias = jnp.concatenate([bias, jnp.full((tq, extra), NEG, jnp.float32)], axis=1)
    sb_ref[0, 0] = bias.astype(sb_ref.dtype)


def _nsa_compressed(proj, cmp_kv, batch, seq):
    tq = 256
    nq = seq // tq
    nb = seq // CMP_STRIDE
    ns = seq // SLC_LEN
    n_sel = min(SLC_TOPN, ns)
    c_start = np.arange(nb) * CMP_STRIDE
    s_start = np.arange(ns) * SLC_LEN
    ov = np.maximum(np.minimum(c_start[:, None] + CMP_LEN, s_start[None, :] + SLC_LEN)
                    - np.maximum(c_start[:, None], s_start[None, :]), 0).astype(np.float32) / CMP_LEN
    ov[nb - 1] = 0.0
    qb = COL_QRAW // (NSA_GROUP * HEAD_DIM)
    return pl.pallas_call(
        functools.partial(_nsa_cmp_kernel, tq=tq, n_sel=n_sel,
                          n_var=4 if (nb % (4 * 128) == 0 and nq % 4 == 0) else 1),
        out_shape=[jax.ShapeDtypeStruct((batch * seq, NSA_Q), CDT),
                   jax.ShapeDtypeStruct((batch, NSA_KV_HEADS, seq, max(ns, 128)), CDT)],
        grid=(batch, NSA_KV_HEADS, nq),
        in_specs=[pl.BlockSpec((tq, NSA_GROUP * HEAD_DIM), lambda b, k, i: (b * nq + i, qb + k)),
                  pl.BlockSpec((1, 1, 1, nb, HEAD_DIM), lambda b, k, i: (0, b, k, 0, 0)),
                  pl.BlockSpec((1, 1, 1, nb, HEAD_DIM), lambda b, k, i: (1, b, k, 0, 0)),
                  pl.BlockSpec((nb, ns), lambda b, k, i: (0, 0))],
        out_specs=[pl.BlockSpec((tq, NSA_GROUP * HEAD_DIM), lambda b, k, i: (b * nq + i, k)),
                   pl.BlockSpec((1, 1, tq, max(ns, 128)), lambda b, k, i: (b, k, i, 0))],
        scratch_shapes=[pltpu.VMEM((tq, ns), jnp.float32)],
        compiler_params=_params(("arbitrary", "arbitrary", "arbitrary")),
        name="nsa_compressed",
    )(proj, cmp_kv, cmp_kv, jnp.asarray(ov, CDT))


def _flash_tile(s_ref, p_ref, m_ref, acc_ref, rc, tq, bias_ref=None, causal=None):
    rows, tk = s_ref.shape

    def tile(r0, c0):
        s = s_ref[r0:r0 + rc, c0:c0 + 128]
        if bias_ref is not None:
            s = s + bias_ref[r0 % tq:r0 % tq + rc, c0:c0 + 128]
        if causal is not None:
            kpos, t0 = causal
            t = t0 + (r0 % tq) + lax.broadcasted_iota(jnp.int32, (rc, 1), 0)
            s = jnp.where(kpos[:, c0:c0 + 128] <= t, s, NEG)
        return s

    for r0 in range(0, rows, rc):
        mx = tile(r0, 0)
        for c0 in range(128, tk, 128):
            mx = jnp.maximum(mx, tile(r0, c0))
        m_prev = m_ref[r0:r0 + rc, :]
        m_new = jnp.maximum(m_prev, jnp.max(mx, axis=-1, keepdims=True))
        acc_ref[r0:r0 + rc, :] = acc_ref[r0:r0 + rc, :] * jnp.exp2(m_prev - m_new)
        m_ref[r0:r0 + rc, :] = m_new
    for r0 in range(0, rows, rc):
        m_new = m_ref[r0:r0 + rc, :]
        for c0 in range(0, tk, 128):
            p_ref[r0:r0 + rc, c0:c0 + 128] = jnp.exp2(tile(r0, c0) - m_new).astype(p_ref.dtype)


def _nsa_sel_kernel(q_ref, kt_ref, va_ref, sb_ref, o_ref,
                    qp_ref, q4_ref, sbf_ref, s_ref, p_ref, m_ref, acc_ref, *, tq, tk, rc):
    i = pl.program_id(2)
    s0 = i * tq
    nsp = sb_ref.shape[3]
    per_tile = tk // SLC_LEN
    q4 = _stack_heads(q_ref[...])
    qp_ref[...] = jnp.concatenate([q4, jnp.zeros_like(q4)], axis=1)
    sbf_ref[...] = sb_ref[0, 0].astype(jnp.float32)
    lane = lax.broadcasted_iota(jnp.int32, (tq, 128), 1)
    bias_lanes = (lane >= HEAD_DIM) & (lane < HEAD_DIM + per_tile)
    m_ref[...] = jnp.full(m_ref.shape, NEG, jnp.float32)
    acc_ref[...] = jnp.zeros(acc_ref.shape, jnp.float32)

    def step(kt, masked):
        start = pl.multiple_of(kt * tk, tk)
        shift = (HEAD_DIM - kt * per_tile + nsp) % nsp
        rolled = pltpu.roll(sbf_ref[...], shift, axis=1)[:, :128].astype(q4_ref.dtype)
        for g in range(NSA_GROUP):
            q4_ref[g * tq:(g + 1) * tq, :] = jnp.where(bias_lanes, rolled, qp_ref[g * tq:(g + 1) * tq, :])
        s_ref[...] = _dot(q4_ref[...], kt_ref[0, 0, :, pl.ds(start, tk)])
        causal = (start + lax.broadcasted_iota(jnp.int32, (1, tk), 1), s0) if masked else None
        _flash_tile(s_ref, p_ref, m_ref, acc_ref, rc, tq, causal=causal)
        acc_ref[...] += _dot(p_ref[...], va_ref[0, 0, pl.ds(start, tk), :])

    unroll = 4

    def body(kp, carry):
        for d in range(unroll):
            step(unroll * kp + d, False)
        return carry

    n_full = s0 // tk
    n_trips = n_full // unroll
    lax.fori_loop(0, n_trips, body, 0)
    for r in range(1, unroll):
        @pl.when(n_full - n_trips * unroll >= r)
        def _(r=r):
            step(n_trips * unroll + r - 1, False)

    step(n_full, True)
    acc = acc_ref[...]
    o = acc[:, :HEAD_DIM] / acc[:, HEAD_DIM:]
    o_ref[...] = _unstack_heads(o, tq).astype(o_ref.dtype)


def _nsa_selected(proj, ks_t, vs_aug, selb, batch, seq):
    tq = 256
    tk = 512
    nq = seq // tq
    nsp = selb.shape[3]
    per_tile = tk // SLC_LEN
    blk_in_tile = (jnp.arange(seq) // SLC_LEN) % per_tile
    onehot = (jnp.arange(128 - HEAD_DIM)[:, None] == blk_in_tile[None, :]).astype(CDT)
    k_aug = jnp.concatenate(
        [ks_t, jnp.broadcast_to(onehot, (batch, NSA_KV_HEADS, 128 - HEAD_DIM, seq))], axis=2)
    qb = COL_QROT // (NSA_GROUP * HEAD_DIM)
    rows = NSA_GROUP * tq
    return pl.pallas_call(
        functools.partial(_nsa_sel_kernel, tq=tq, tk=tk, rc=64),
        out_shape=jax.ShapeDtypeStruct((batch * seq, NSA_Q), CDT),
        grid=(batch, NSA_KV_HEADS, nq),
        in_specs=[pl.BlockSpec((tq, NSA_GROUP * HEAD_DIM), lambda b, k, i: (b * nq + i, qb + k)),
                  pl.BlockSpec((1, 1, 128, seq), lambda b, k, i: (b, k, 0, 0)),
                  pl.BlockSpec((1, 1, seq, 128), lambda b, k, i: (b, k, 0, 0)),
                  pl.BlockSpec((1, 1, tq, nsp), lambda b, k, i: (b, k, i, 0))],
        out_specs=pl.BlockSpec((tq, NSA_GROUP * HEAD_DIM), lambda b, k, i: (b * nq + i, k)),
        scratch_shapes=[pltpu.VMEM((rows, 128), CDT), pltpu.VMEM((rows, 128), CDT),
                        pltpu.VMEM((tq, nsp), jnp.float32), pltpu.VMEM((rows, tk), jnp.float32),
                        pltpu.VMEM((rows, tk), CDT),
                        pltpu.VMEM((rows, 128), jnp.float32), pltpu.VMEM((rows, 128), jnp.float32)],
        compiler_params=_params(("arbitrary", "arbitrary", "arbitrary")),
        name="nsa_selected",
    )(proj, k_aug, vs_aug, selb)


def _nsa_win_kernel(q_ref, kt_ref, v_ref, o_ref, *, tq):
    k = pl.program_id(1)
    i = pl.program_id(2)
    s0 = i * tq
    span = WIN + tq
    start = pl.multiple_of(jnp.maximum(s0 - WIN, 0), tq)
    q4 = _stack_heads(q_ref[...])
    s = _dot(q4, kt_ref[0, 0, :, pl.ds(start, span)])
    tq_col = s0 + lax.broadcasted_iota(jnp.int32, (tq, 1), 0)
    t4 = jnp.concatenate([tq_col] * NSA_GROUP, axis=0)
    kpos = start + lax.broadcasted_iota(jnp.int32, (1, span), 1)
    mask = (kpos <= t4) & (kpos > t4 - WIN)
    s = jnp.where(mask, s, NEG)
    m = jnp.max(s, axis=-1, keepdims=True)
    p = jnp.where(mask, jnp.exp2(s - m), 0.0)
    l = jnp.maximum(jnp.sum(p, axis=-1, keepdims=True), 1e-30)
    o = _dot(p.astype(CDT), v_ref[pl.ds(start, span), :]) / l
    o = jnp.where(k == 0, o[:, :HEAD_DIM], o[:, HEAD_DIM:])
    o_ref[...] = _unstack_heads(o, tq).astype(o_ref.dtype)


def _nsa_window(proj, kw_t, batch, seq):
    tq = 256
    nq = seq // tq
    qb = COL_QROT // (NSA_GROUP * HEAD_DIM)
    vb = COL_VW // 128
    return pl.pallas_call(
        functools.partial(_nsa_win_kernel, tq=tq),
        out_shape=jax.ShapeDtypeStruct((batch * seq, NSA_Q), CDT),
        grid=(batch, NSA_KV_HEADS, nq),
        in_specs=[pl.BlockSpec((tq, NSA_GROUP * HEAD_DIM), lambda b, k, i: (b * nq + i, qb + k)),
                  pl.BlockSpec((1, 1, HEAD_DIM, seq), lambda b, k, i: (b, k, 0, 0)),
                  pl.BlockSpec((seq, 128), lambda b, k, i: (b, vb))],
        out_specs=pl.BlockSpec((tq, NSA_GROUP * HEAD_DIM), lambda b, k, i: (b * nq + i, k)),
        compiler_params=_params(("arbitrary", "arbitrary", "arbitrary")),
        name="nsa_window",
    )(proj, kw_t, proj)


def _moba_mean_kernel(k_ref, o_ref):
    seq, w = k_ref.shape
    nbm = seq // MOBA_BLOCK
    k = k_ref[...].astype(jnp.float32).reshape(nbm, MOBA_BLOCK, w)
    o_ref[0] = jnp.sum(k, axis=1) * (1.0 / MOBA_BLOCK)


def _moba_kmean(proj, batch, seq):
    nbm = seq // MOBA_BLOCK
    kb = COL_MK // 128
    return pl.pallas_call(
        _moba_mean_kernel,
        out_shape=jax.ShapeDtypeStruct((batch, nbm, MOBA_W), jnp.float32),
        grid=(batch, MOBA_W // 128),
        in_specs=[pl.BlockSpec((seq, 128), lambda b, j: (b, kb + j))],
        out_specs=pl.BlockSpec((1, nbm, 128), lambda b, j: (b, 0, j)),
        compiler_params=_params(("arbitrary", "arbitrary")),
        name="moba_kmean",
    )(proj)


def _moba_kernel(q_ref, ka_ref, va_ref, km_ref, o_ref, qa_ref, s_ref, p_ref, m_ref, acc_ref,
                 *, tq, tk, n_top, rc):
    i = pl.program_id(2)
    s0 = i * tq
    nbm = km_ref.shape[1]
    blk = lax.broadcasted_iota(jnp.int32, (nbm, 1), 0)
    cur = (s0 + lax.broadcasted_iota(jnp.int32, (1, tq), 1)) // MOBA_BLOCK
    ridx = lax.broadcasted_iota(jnp.int32, (nbm, tq), 0).astype(jnp.float32)
    for hh in range(2):
        q = q_ref[:, hh * HEAD_DIM:(hh + 1) * HEAD_DIM]
        km_hi, km_lo = _split_hi_lo(km_ref[0][:, hh * HEAD_DIM:(hh + 1) * HEAD_DIM])
        gs = _dot_nt(km_hi, q) + _dot_nt(km_lo, q)
        gs = jnp.where(blk < cur, gs, NEG)

        def body(_, carry):
            sc, sel = carry
            _, pick = _extract_max(sc, ridx)
            return jnp.where(pick, LOWEST, sc), jnp.where(pick, 1.0, sel)

        _, sel = lax.fori_loop(0, n_top, body, (gs, jnp.zeros((nbm, tq), jnp.float32)))
        open_blk = ((sel > 0.5) & (gs > NEG * 0.5)) | (blk == cur)
        bias_t = jnp.where(open_blk, 0.0, NEG)
        lead = -nbm % 128
        if lead:
            bias_t = jnp.concatenate([jnp.zeros((lead, tq), jnp.float32), bias_t], axis=0)
        bias = bias_t.T[:, lead:].astype(CDT)
        qa_ref[hh] = jnp.concatenate([q, bias], axis=1)
    m_ref[...] = jnp.full(m_ref.shape, NEG, jnp.float32)
    acc_ref[...] = jnp.zeros(acc_ref.shape, jnp.float32)

    def step(kt, masked):
        start = pl.multiple_of(kt * tk, tk)
        for hh in range(2):
            s_ref[hh] = _dot(qa_ref[hh], ka_ref[0, hh, :, pl.ds(start, tk)])
        causal = (start + lax.broadcasted_iota(jnp.int32, (1, tk), 1), s0) if masked else None
        for hh in range(2):
            _flash_tile(s_ref.at[hh], p_ref.at[hh], m_ref.at[hh], acc_ref.at[hh], rc, tq, causal=causal)
            acc_ref[hh] += _dot(p_ref[hh], va_ref[0, hh, pl.ds(start, tk), :])

    per_q = tq // tk
    per_trip = 2 * per_q

    def body(kp, carry):
        for d in range(per_trip):
            step(kp * per_trip + d, False)
        return carry

    n_full = s0 // tk
    n_trips = n_full // per_trip
    lax.fori_loop(0, n_trips, body, 0)

    @pl.when(n_full - n_trips * per_trip == per_q)
    def _():
        for d in range(per_q):
            step(n_trips * per_trip + d, False)

    for d in range(per_q):
        step(n_full + d, True)
    outs = []
    for hh in range(2):
        acc = acc_ref[hh]
        outs.append(acc[:, :HEAD_DIM] / acc[:, HEAD_DIM:])
    o_ref[...] = jnp.concatenate(outs, axis=1).astype(o_ref.dtype)


def _moba(proj, mk_aug_t, mv_aug, kmean, batch, seq):
    tq = min(1024, seq)
    tk = 512
    nq = seq // tq
    nbm = seq // MOBA_BLOCK
    n_top = min(MOBA_TOPK, nbm)
    qb = COL_MQ // 128
    aug = HEAD_DIM + nbm
    return pl.pallas_call(
        functools.partial(_moba_kernel, tq=tq, tk=tk, n_top=n_top, rc=64),
        out_shape=jax.ShapeDtypeStruct((batch * seq, MOBA_W), CDT),
        grid=(batch, MOBA_HEADS // 2, nq),
        in_specs=[pl.BlockSpec((tq, 128), lambda b, p, i: (b * nq + i, qb + p)),
                  pl.BlockSpec((1, 2, aug, seq), lambda b, p, i: (b, p, 0, 0)),
                  pl.BlockSpec((1, 2, seq, 128), lambda b, p, i: (b, p, 0, 0)),
                  pl.BlockSpec((1, nbm, 128), lambda b, p, i: (b, 0, p))],
        out_specs=pl.BlockSpec((tq, 128), lambda b, p, i: (b * nq + i, p)),
        scratch_shapes=[pltpu.VMEM((2, tq, aug), CDT), pltpu.VMEM((2, tq, tk), jnp.float32),
                        pltpu.VMEM((2, tq, tk), CDT), pltpu.VMEM((2, tq, 128), jnp.float32),
                        pltpu.VMEM((2, tq, 128), jnp.float32)],
        compiler_params=_params(("arbitrary", "arbitrary", "arbitrary")),
        name="moba",
    )(proj, mk_aug_t, mv_aug, kmean)


def _merge_kernel(oc_ref, os_ref, ow_ref, om_ref, ng_ref, gn_ref, gm_ref, x_ref, ga_ref,
                  ex_ref, wun_ref, wum_ref, wo_ref, o_ref):
    gates = _sigmoid(ng_ref[...].astype(jnp.float32))
    hi, lo = _split_hi_lo(gates)
    e = _dot(hi, ex_ref[...]) + _dot(lo, ex_ref[...])
    o_nsa = (e[:, :NSA_Q] * oc_ref[...].astype(jnp.float32)
             + e[:, NSA_Q:2 * NSA_Q] * os_ref[...].astype(jnp.float32)
             + e[:, 2 * NSA_Q:] * ow_ref[...].astype(jnp.float32))
    y = (_sigmoid(gn_ref[...].astype(jnp.float32)) * _dot(o_nsa.astype(CDT), wun_ref[...])
         + _sigmoid(gm_ref[...].astype(jnp.float32)) * _dot(om_ref[...], wum_ref[...]))
    o_ref[...] = x_ref[...] + ga_ref[0] * _dot(y.astype(CDT), wo_ref[...])


def _merge(o_c, o_s, o_w, o_m, proj, x2d, ga, w_up_nsa, w_up_moba, w_out, seq):
    n, d = x2d.shape
    tm = min(512, seq)
    per_seq = seq // tm
    ng_w = COL_GN - COL_NG
    ex = np.zeros((ng_w, 3 * NSA_Q), np.float32)
    for h in range(NSA_HEADS):
        for j in range(3):
            ex[h * 3 + j, j * NSA_Q + h * HEAD_DIM: j * NSA_Q + (h + 1) * HEAD_DIM] = 1.0
    row = lambda i: (i, 0)
    const = lambda i: (0, 0)
    return pl.pallas_call(
        _merge_kernel,
        out_shape=jax.ShapeDtypeStruct((n, d), jnp.float32),
        grid=(n // tm,),
        in_specs=[pl.BlockSpec((tm, NSA_Q), row), pl.BlockSpec((tm, NSA_Q), row),
                  pl.BlockSpec((tm, NSA_Q), row), pl.BlockSpec((tm, MOBA_W), row),
                  pl.BlockSpec((tm, ng_w), lambda i: (i, COL_NG // ng_w)),
                  pl.BlockSpec((tm, d), lambda i: (i, COL_GN // d)),
                  pl.BlockSpec((tm, d), lambda i: (i, COL_GM // d)),
                  pl.BlockSpec((tm, d), row),
                  pl.BlockSpec((1, 1, d), lambda i: (i // per_seq, 0, 0)),
                  pl.BlockSpec((ng_w, 3 * NSA_Q), const),
                  pl.BlockSpec((NSA_Q, d), const), pl.BlockSpec((MOBA_W, d), const),
                  pl.BlockSpec((d, d), const)],
        out_specs=pl.BlockSpec((tm, d), row),
        compiler_params=_params(("arbitrary",)),
        name="mixer_merge",
    )(o_c, o_s, o_w, o_m, proj, proj, proj, x2d, ga, jnp.asarray(ex, CDT),
      w_up_nsa.astype(CDT), w_up_moba.astype(CDT), w_out.astype(CDT))


def _extract_max(x, ridx):
    m = jnp.max(x, axis=0, keepdims=True)
    idx = jnp.min(jnp.where(x == m, ridx, float(x.shape[0])), axis=0, keepdims=True)
    return m, ridx == idx


def _peer_score_kernel(q_ref, k1_ref, k2_ref, cnt_ref, rk_ref, e1_ref, e2_ref):
    half = PEER_QDIM // 2
    k = PEER_TOPK
    q = q_ref[...]
    s1_all = _dot_nt(k1_ref[...], q[:, :half])
    s2_all = _dot_nt(k2_ref[...], q[:, half:])
    t = 128
    ridx = lax.broadcasted_iota(jnp.int32, (PEER_NKEYS, t), 0).astype(jnp.float32)
    unranked = jnp.full((PEER_NKEYS, t), float(k), jnp.float32)
    chunks = list(range(0, s1_all.shape[1], t))
    shifted = []
    for c0 in chunks:
        for s_all in (s1_all, s2_all):
            s = s_all[:, c0:c0 + t]
            shifted.append(s - jnp.max(s, axis=0, keepdims=True))
    xs = list(shifted)
    rks = [unranked] * len(xs)
    tops = [[] for _ in xs]
    for i in range(k):
        for n in range(len(xs)):
            m, pick = _extract_max(xs[n], ridx)
            xs[n] = jnp.where(pick, LOWEST, xs[n])
            rks[n] = jnp.where(pick, float(i), rks[n])
            tops[n].append(m)

    v2_alls, cands = [], []
    for c in range(len(chunks)):
        v1, v2_all = tops[2 * c], jnp.concatenate(tops[2 * c + 1], axis=0)
        pieces = [v1[i] + v2_all[0:k // (i + 1)] for i in range(k)]
        pad = -sum(p.shape[0] for p in pieces) % 8
        cands.append(jnp.concatenate(pieces + [jnp.full((pad, t), LOWEST, jnp.float32)], axis=0))
        v2_alls.append(v2_all)
    cidx = lax.broadcasted_iota(jnp.int32, cands[0].shape, 0).astype(jnp.float32)
    vals = [[] for _ in chunks]
    for i in range(k):
        for c in range(len(chunks)):
            m, pick = _extract_max(cands[c], cidx)
            cands[c] = jnp.where(pick, LOWEST, cands[c])
            vals[c].append(m)

    for c, c0 in enumerate(chunks):
        v1, v2_all, val = tops[2 * c], v2_alls[c], vals[c]
        tau = val[k - 1]
        z = val[0] - val[0] + 1.0
        for i in range(1, k):
            z = z + jnp.exp(val[i] - val[0])
        cnt = jnp.zeros((PEER_NKEYS, t), jnp.float32)
        for i in range(k):
            n_i = jnp.sum(jnp.where(v1[i] + v2_all >= tau, 1.0, 0.0), axis=0, keepdims=True)
            cnt = jnp.where(rks[2 * c] == float(i), n_i, cnt)
        cnt_ref[0, :, c0:c0 + t] = cnt
        rk_ref[0, :, c0:c0 + t] = rks[2 * c + 1].astype(rk_ref.dtype)
        e1_ref[0, :, c0:c0 + t] = jnp.exp(shifted[2 * c] - val[0]) / z
        e2_ref[0, :, c0:c0 + t] = jnp.exp(shifted[2 * c + 1]).astype(e2_ref.dtype)


def _peer_scores(qp, k1, k2):
    n = qp.shape[0]
    tt = 512
    f32 = jax.ShapeDtypeStruct((PEER_HEADS, PEER_NKEYS, n), jnp.float32)
    cdt = jax.ShapeDtypeStruct((PEER_HEADS, PEER_NKEYS, n), CDT)
    big = pl.BlockSpec((1, PEER_NKEYS, tt), lambda i, h: (h, 0, i))
    return pl.pallas_call(
        _peer_score_kernel,
        out_shape=[f32, cdt, f32, cdt],
        grid=(n // tt, PEER_HEADS),
        in_specs=[pl.BlockSpec((tt, PEER_QDIM), lambda i, h: (i, h)),
                  pl.BlockSpec((PEER_NKEYS, PEER_QDIM // 2), lambda i, h: (0, 0)),
                  pl.BlockSpec((PEER_NKEYS, PEER_QDIM // 2), lambda i, h: (0, 0))],
        out_specs=[big, big, big, big],
        compiler_params=_params(("arbitrary", "arbitrary")),
        name="peer_scores",
    )(qp, k1.astype(CDT), k2.astype(CDT))


def _row_to_rows(row, n):
    tile_rows = 16
    tile = jnp.broadcast_to(row, (tile_rows, row.shape[1])).astype(CDT)
    return jnp.concatenate([tile] * (n // tile_rows), axis=0)


def _peer_expert_kernel(h_ref, u_ref, vt_ref, cnt_ref, rk_ref, e1_ref, e2_ref, x_ref, ga_ref,
                        o_ref, acc_ref, *, eb):
    j = pl.program_id(1)
    tt = h_ref.shape[0]

    @pl.when(j == 0)
    def _():
        acc_ref[...] = jnp.zeros(acc_ref.shape, jnp.float32)

    parts = []
    piece = 2 * PEER_NKEYS
    for e0 in range(0, eb, piece):
        sc = _dot_nt(u_ref[e0:e0 + piece, :], h_ref[...])
        ws = []
        for al in range(e0 // PEER_NKEYS, (e0 + piece) // PEER_NKEYS):
            a = j * (eb // PEER_NKEYS) + al
            w = None
            for hd in range(PEER_HEADS):
                cnt_a = _row_to_rows(cnt_ref[hd, pl.ds(a, 1), :], PEER_NKEYS)
                e1_a = _row_to_rows(e1_ref[hd, pl.ds(a, 1), :], PEER_NKEYS)
                contrib = jnp.where(rk_ref[hd] < cnt_a, e2_ref[hd] * e1_a, jnp.zeros((), CDT))
                w = contrib if w is None else w + contrib
            ws.append(w)
        parts.append(jnp.concatenate(ws, axis=0) * _gelu(sc.astype(CDT)))
    pw = jnp.concatenate(parts, axis=0)
    acc_ref[...] += _dot(vt_ref[...], pw)

    @pl.when(j == pl.num_programs(1) - 1)
    def _():
        o_ref[...] = x_ref[...] + ga_ref[0] * acc_ref[...].T


def _peer_experts(h2, u, v_t, cnt, rk2, e1, e2, x2d, ga, seq):
    n, d = x2d.shape
    tt = min(512, seq)
    per_seq = seq // tt
    eb = 2048
    n_blk = u.shape[0] // eb
    big = pl.BlockSpec((PEER_HEADS, PEER_NKEYS, tt), lambda i, j: (0, 0, i))
    return pl.pallas_call(
        functools.partial(_peer_expert_kernel, eb=eb),
        out_shape=jax.ShapeDtypeStruct((n, d), jnp.float32),
        grid=(n // tt, n_blk),
        in_specs=[pl.BlockSpec((tt, d), lambda i, j: (i, 0)),
                  pl.BlockSpec((eb, d), lambda i, j: (j, 0)),
                  pl.BlockSpec((d, eb), lambda i, j: (0, j)),
                  big, big, big, big,
                  pl.BlockSpec((tt, d), lambda i, j: (i, 0)),
                  pl.BlockSpec((1, 1, d), lambda i, j: (i // per_seq, 0, 0))],
        out_specs=pl.BlockSpec((tt, d), lambda i, j: (i, 0)),
        scratch_shapes=[pltpu.VMEM((d, tt), jnp.float32)],
        compiler_params=_params(("arbitrary", "arbitrary")),
        name="peer_experts",
    )(h2, u, v_t, cnt, rk2, e1, e2, x2d, ga)


def _rms_kernel(x_ref, g_ref, o_ref):
    x = x_ref[...]
    ms = jnp.mean(x * x, axis=-1, keepdims=True)
    o_ref[...] = x * lax.rsqrt(ms + RMS_EPS) * g_ref[...]


def _final_norm(x2d, g):
    n, d = x2d.shape
    tm = 512
    return pl.pallas_call(
        _rms_kernel,
        out_shape=jax.ShapeDtypeStruct((n, d), jnp.float32),
        grid=(n // tm,),
        in_specs=[pl.BlockSpec((tm, d), lambda i: (i, 0)), pl.BlockSpec((1, d), lambda i: (0, 0))],
        out_specs=pl.BlockSpec((tm, d), lambda i: (i, 0)),
        compiler_params=_params(("arbitrary",)),
        name="final_rmsnorm",
    )(x2d, g.reshape(1, d))


def _key_major(proj, col, heads, batch, seq):
    k = proj[:, col:col + heads * HEAD_DIM].reshape(batch, seq, heads, HEAD_DIM)
    return jnp.transpose(k, (0, 2, 3, 1))


def _value_with_ones(proj, col, heads, batch, seq):
    v = proj[:, col:col + heads * HEAD_DIM].reshape(batch, seq, heads, HEAD_DIM)
    v = jnp.transpose(v, (0, 2, 1, 3))
    return jnp.concatenate([v, jnp.ones_like(v)], axis=-1)


def kernel(x, c, w_ada, b_ada, g_attn, g_ffn, w_in, cmp_pe, cmp_w1, cmp_w2, w_up_nsa, w_up_moba, w_out,
           peer_wq, peer_k1, peer_k2, peer_u, peer_v, g_final):
    batch, seq, d = x.shape
    depth = w_ada.shape[0]
    n = batch * seq
    nbm = seq // MOBA_BLOCK
    x2d = x.reshape(n, d)
    mod = _adaln_mod(c, w_ada, b_ada)
    rope = _rope_tables(seq)
    blk_onehot = (jnp.arange(nbm)[:, None] == jnp.arange(seq)[None, :] // MOBA_BLOCK).astype(CDT)
    for l in range(depth):
        sh1, sc1, ga1, sh2, sc2, ga2 = [m.reshape(batch, 1, d) for m in jnp.split(mod[l], 6, axis=-1)]
        proj = _norm_mod_matmul(x2d, g_attn[l], sc1, sh1, _reorder_w_in(w_in[l]), seq,
                                rope=rope, n_rope=ROPE_COLS // PROJ_TN)
        cmp_kv = _nsa_compress(proj, cmp_pe[l], cmp_w1[l], cmp_w2[l], batch, seq)
        o_c, selb = _nsa_compressed(proj, cmp_kv, batch, seq)
        o_s = _nsa_selected(proj, _key_major(proj, COL_KS, NSA_KV_HEADS, batch, seq),
                            _value_with_ones(proj, COL_VS, NSA_KV_HEADS, batch, seq), selb, batch, seq)
        o_w = _nsa_window(proj, _key_major(proj, COL_KW, NSA_KV_HEADS, batch, seq), batch, seq)
        mk_t = _key_major(proj, COL_MK, MOBA_HEADS, batch, seq)
        mk_aug = jnp.concatenate(
            [mk_t, jnp.broadcast_to(blk_onehot, (batch, MOBA_HEADS, nbm, seq))], axis=2)
        o_m = _moba(proj, mk_aug, _value_with_ones(proj, COL_MV, MOBA_HEADS, batch, seq),
                    _moba_kmean(proj, batch, seq), batch, seq)
        x2d = _merge(o_c, o_s, o_w, o_m, proj, x2d, ga1, w_up_nsa[l], w_up_moba[l], w_out[l], seq)
        qp, h2 = _norm_mod_matmul(x2d, g_ffn[l], sc2, sh2, peer_wq[l].astype(CDT), seq, emit_h=True)
        cnt, rk2, e1, e2 = _peer_scores(qp, peer_k1[l], peer_k2[l])
        x2d = _peer_experts(h2, peer_u[l].astype(CDT), peer_v[l].T.astype(CDT), cnt, rk2, e1, e2,
                            x2d, ga2, seq)
    return _final_norm(x2d, g_final).reshape(batch, seq, d)
```

```python
import functools

import jax
import jax.numpy as jnp
import numpy as np
from jax import lax
from jax.experimental import pallas as pl
from jax.experimental.pallas import tpu as pltpu

D_MODEL = 1024
HEAD_DIM = 64
ROT_DIM = HEAD_DIM // 4
ROPE_THETA = 500000.0
NSA_HEADS = 8
NSA_KV_HEADS = 2
NSA_GROUP = NSA_HEADS // NSA_KV_HEADS
CMP_LEN = 32
CMP_STRIDE = 16
CMP_HID = 2 * HEAD_DIM
SLC_LEN = 64
SLC_TOPN = 16
WIN = 512
MOBA_HEADS = 8
MOBA_BLOCK = 256
MOBA_TOPK = 3
PEER_HEADS = 8
PEER_NKEYS = 128
PEER_QDIM = 256
PEER_TOPK = 16
RMS_EPS = 1e-6
NEG = -1e30
SEL_FORCE = 1e4
LOWEST = -3.0e38
LOG2E = 1.4426950408889634

NSA_Q = NSA_HEADS * HEAD_DIM
NSA_KV = NSA_KV_HEADS * HEAD_DIM
MOBA_W = MOBA_HEADS * HEAD_DIM

CDT = jnp.bfloat16
V7X_VMEM_LIMIT = 56 * 1024 * 1024

COL_QROT = 0
COL_KS = 512
COL_KW = 640
COL_MQ = 768
COL_MK = 1280
ROPE_COLS = 1792
COL_NG = 1792
COL_GN = 2048
COL_GM = 3072
COL_QRAW = 4096
COL_KC = 4608
COL_VC = 4736
COL_VS = 4864
COL_VW = 4992
COL_MV = 5120
PROJ_COLS = 5632
PROJ_TN = 256


def _params(sem):
    return pltpu.CompilerParams(dimension_semantics=sem, vmem_limit_bytes=V7X_VMEM_LIMIT)


def _dot(a, b):
    return jnp.dot(a, b, preferred_element_type=jnp.float32)


def _dot_nt(a, b):
    return lax.dot_general(a, b, (((1,), (1,)), ((), ())), preferred_element_type=jnp.float32)


def _split_hi_lo(x):
    hi = x.astype(CDT)
    lo = (x - hi.astype(jnp.float32)).astype(CDT)
    return hi, lo


def _gelu(x):
    return 0.5 * x * (1.0 + jnp.tanh(0.7978845608028654 * (x + 0.044715 * (x * x * x))))


def _sigmoid(x):
    return 1.0 / (1.0 + jnp.exp(-x))


def _mod_kernel(c_ref, w_ref, b_ref, o_ref):
    c = c_ref[...]
    sc = c * _sigmoid(c)
    o_ref[0] = jnp.dot(sc, w_ref[0], preferred_element_type=jnp.float32,
                       precision=lax.Precision.HIGHEST) + b_ref[0]


def _adaln_mod(c, w_ada, b_ada):
    depth, d, six_d = w_ada.shape
    b = c.shape[0]
    rows = 8
    c_pad = jnp.zeros((rows, d), jnp.float32).at[:b].set(c)
    tn = 1024
    out = pl.pallas_call(
        _mod_kernel,
        out_shape=jax.ShapeDtypeStruct((depth, rows, six_d), jnp.float32),
        grid=(depth, six_d // tn),
        in_specs=[pl.BlockSpec((rows, d), lambda l, j: (0, 0)),
                  pl.BlockSpec((1, d, tn), lambda l, j: (l, 0, j)),
                  pl.BlockSpec((1, 1, tn), lambda l, j: (l, 0, j))],
        out_specs=pl.BlockSpec((1, rows, tn), lambda l, j: (l, 0, j)),
        compiler_params=_params(("arbitrary", "arbitrary")),
        name="adaln_mod",
    )(c_pad, w_ada, b_ada.reshape(depth, 1, six_d))
    return out[:, :b]


def _nmm_kernel(*refs, n_rope, emit_h, tn):
    if n_rope:
        x_ref, g_ref, sc_ref, sh_ref, w_ref, cos_ref, sa_ref, sb_ref = refs[:8]
        rest = refs[8:]
    else:
        x_ref, g_ref, sc_ref, sh_ref, w_ref = refs[:5]
        rest = refs[5:]
    if emit_h:
        o_ref, ho_ref, h_ref = rest
    else:
        o_ref, h_ref = rest
    x = x_ref[...]
    ms = jnp.mean(x * x, axis=-1, keepdims=True)
    y = x * lax.rsqrt(ms + RMS_EPS) * g_ref[...]
    h = (y * (1.0 + sc_ref[0]) + sh_ref[0]).astype(h_ref.dtype)
    h_ref[...] = h
    if emit_h:
        ho_ref[...] = h

    for j in range(w_ref.shape[1] // tn):
        acc = _dot(h_ref[...], w_ref[:, j * tn:(j + 1) * tn])
        if j < n_rope:
            cos, sa, sb = cos_ref[...], sa_ref[...], sb_ref[...]
            for c0 in range(0, tn, 128):
                a = acc[:, c0:c0 + 128]
                r = (a * cos + pltpu.roll(a, 128 - ROT_DIM // 2, axis=1) * sa
                     + pltpu.roll(a, ROT_DIM // 2, axis=1) * sb)
                o_ref[:, j * tn + c0:j * tn + c0 + 128] = r.astype(o_ref.dtype)
        else:
            o_ref[:, j * tn:(j + 1) * tn] = acc.astype(o_ref.dtype)


def _norm_mod_matmul(x2d, g, sc, sh, w, seq, rope=None, n_rope=0, emit_h=False, tn=PROJ_TN):
    n, d = x2d.shape
    cols = w.shape[1]
    tm = min(512, seq)
    per_seq = seq // tm
    in_specs = [pl.BlockSpec((tm, d), lambda i: (i, 0)),
                pl.BlockSpec((1, d), lambda i: (0, 0)),
                pl.BlockSpec((1, 1, d), lambda i: (i // per_seq, 0, 0)),
                pl.BlockSpec((1, 1, d), lambda i: (i // per_seq, 0, 0)),
                pl.BlockSpec((d, cols), lambda i: (0, 0))]
    args = [x2d, g.reshape(1, d), sc, sh, w]
    if n_rope:
        in_specs += [pl.BlockSpec((tm, 128), lambda i: (i % per_seq, 0))] * 3
        args += list(rope)
    out_shape = [jax.ShapeDtypeStruct((n, cols), CDT)]
    out_specs = [pl.BlockSpec((tm, cols), lambda i: (i, 0))]
    if emit_h:
        out_shape.append(jax.ShapeDtypeStruct((n, d), CDT))
        out_specs.append(pl.BlockSpec((tm, d), lambda i: (i, 0)))
    res = pl.pallas_call(
        functools.partial(_nmm_kernel, n_rope=n_rope, emit_h=emit_h, tn=tn),
        out_shape=out_shape,
        grid=(n // tm,),
        in_specs=in_specs,
        out_specs=out_specs,
        scratch_shapes=[pltpu.VMEM((tm, d), CDT)],
        compiler_params=_params(("arbitrary",)),
        name="norm_mod_matmul",
    )(*args)
    return res if emit_h else res[0]


def _rope_tables(seq):
    half = ROT_DIM // 2
    inv = ROPE_THETA ** (-jnp.arange(half, dtype=jnp.float32) / half)
    ang = jnp.arange(seq, dtype=jnp.float32)[:, None] * inv[None, :]
    cos, sin = jnp.cos(ang), jnp.sin(ang)
    ones = jnp.ones((seq, HEAD_DIM - ROT_DIM), jnp.float32)
    zeros = jnp.zeros((seq, HEAD_DIM - ROT_DIM), jnp.float32)
    zh = jnp.zeros((seq, half), jnp.float32)
    c64 = jnp.concatenate([cos, cos, ones], axis=1)
    sa64 = jnp.concatenate([-sin, zh, zeros], axis=1)
    sb64 = jnp.concatenate([zh, sin, zeros], axis=1)
    return tuple(jnp.concatenate([t, t], axis=1) for t in (c64, sa64, sb64))


def _reorder_w_in(w_in):
    o = 0
    pieces = {}
    for name, size in (("nq", NSA_Q), ("kc", NSA_KV), ("vc", NSA_KV), ("ks", NSA_KV), ("vs", NSA_KV),
                       ("kw", NSA_KV), ("vw", NSA_KV), ("ng", 3 * NSA_HEADS), ("mq", MOBA_W),
                       ("mk", MOBA_W), ("mv", MOBA_W), ("gn", D_MODEL), ("gm", D_MODEL)):
        pieces[name] = w_in[:, o:o + size]
        o += size
    scale = HEAD_DIM ** -0.5 * LOG2E
    pad = jnp.zeros((w_in.shape[0], COL_GN - COL_NG - 3 * NSA_HEADS), w_in.dtype)
    w = jnp.concatenate([pieces["nq"] * scale, pieces["ks"], pieces["kw"], pieces["mq"] * scale,
                         pieces["mk"], pieces["ng"], pad, pieces["gn"], pieces["gm"],
                         pieces["nq"] * scale, pieces["kc"], pieces["vc"], pieces["vs"],
                         pieces["vw"], pieces["mv"]], axis=1)
    assert w.shape[1] == PROJ_COLS
    return w.astype(CDT)


def _compress_kernel(x_ref, w1c_ref, pe_ref, w1_ref, w2_ref, o_ref):
    nb = x_ref.shape[3]
    ab = _dot(x_ref[0, 0, 0], w1c_ref[0])
    a = ab[:, :CMP_HID]
    b_next = pltpu.roll(ab[:, CMP_HID:], nb - 1, axis=0)
    bias = _dot(pe_ref[0], w1_ref[0])[0:1]
    hid = _gelu(a + b_next + bias)
    o_ref[0, 0, 0] = _dot(hid.astype(CDT), w2_ref[0]).astype(o_ref.dtype)


def _nsa_compress(proj, cmp_pe, cmp_w1, cmp_w2, batch, seq):
    nb = seq // CMP_STRIDE
    half = CMP_STRIDE * HEAD_DIM
    kcvc = proj[:, COL_KC:COL_KC + 2 * NSA_KV].reshape(batch, nb, CMP_STRIDE, 2, NSA_KV_HEADS, HEAD_DIM)
    x = jnp.transpose(kcvc, (3, 0, 4, 1, 2, 5)).reshape(2, batch, NSA_KV_HEADS, nb, half)
    w1c = jnp.concatenate([cmp_w1[:, :half], cmp_w1[:, half:]], axis=2).astype(CDT)
    pe = jnp.broadcast_to(cmp_pe.reshape(2, 1, CMP_LEN * HEAD_DIM), (2, 8, CMP_LEN * HEAD_DIM)).astype(CDT)
    return pl.pallas_call(
        _compress_kernel,
        out_shape=jax.ShapeDtypeStruct((2, batch, NSA_KV_HEADS, nb, HEAD_DIM), CDT),
        grid=(2, batch, NSA_KV_HEADS),
        in_specs=[pl.BlockSpec((1, 1, 1, nb, half), lambda w, b, k: (w, b, k, 0, 0)),
                  pl.BlockSpec((1, half, 2 * CMP_HID), lambda w, b, k: (w, 0, 0)),
                  pl.BlockSpec((1, 8, 2 * half), lambda w, b, k: (w, 0, 0)),
                  pl.BlockSpec((1, 2 * half, CMP_HID), lambda w, b, k: (w, 0, 0)),
                  pl.BlockSpec((1, CMP_HID, HEAD_DIM), lambda w, b, k: (w, 0, 0))],
        out_specs=pl.BlockSpec((1, 1, 1, nb, HEAD_DIM), lambda w, b, k: (w, b, k, 0, 0)),
        compiler_params=_params(("arbitrary", "arbitrary", "arbitrary")),
        name="nsa_compress",
    )(x, w1c, pe, cmp_w1.astype(CDT), cmp_w2.astype(CDT))


def _stack_heads(qblk):
    return jnp.concatenate([qblk[:, g * HEAD_DIM:(g + 1) * HEAD_DIM] for g in range(NSA_GROUP)], axis=0)


def _unstack_heads(o, tq):
    return jnp.concatenate([o[g * tq:(g + 1) * tq] for g in range(NSA_GROUP)], axis=1)


def _topk_mask(score, n_sel):
    shape = score.shape
    lane = lax.broadcasted_iota(jnp.int32, shape, 1).astype(jnp.float32)
    width = float(shape[1])

    def body(_, carry):
        sc, sel = carry
        m = jnp.max(sc, axis=-1, keepdims=True)
        idx = jnp.min(jnp.where(sc == m, lane, width), axis=-1, keepdims=True)
        pick = lane == idx
        return jnp.where(pick, LOWEST, sc), jnp.where(pick, 1.0, sel)

    _, sel = lax.fori_loop(0, n_sel, body, (score, jnp.zeros(shape, jnp.float32)))
    return sel


def _nsa_cmp_kernel(q_ref, kc_ref, vc_ref, ov_ref, oc_ref, sb_ref, imp_ref, *, tq, n_sel, n_var):
    i = pl.program_id(2)
    s0 = i * tq
    nb = kc_ref.shape[3]
    ns = ov_ref.shape[1]
    tq_col = s0 + lax.broadcasted_iota(jnp.int32, (tq, 1), 0)
    t4 = jnp.concatenate([tq_col] * NSA_GROUP, axis=0)

    def attend(nk):
        q4 = _stack_heads(q_ref[...])
        s = _dot_nt(q4, kc_ref[0, 0, 0, :nk, :])
        cend = lax.broadcasted_iota(jnp.int32, (1, nk), 1) * CMP_STRIDE + (CMP_LEN - 1)
        mask = cend <= t4
        s = jnp.where(mask, s, NEG)
        m = jnp.max(s, axis=-1, keepdims=True)
        p = jnp.where(mask, jnp.exp2(s - m), 0.0)
        p = p / jnp.maximum(jnp.sum(p, axis=-1, keepdims=True), 1e-30)
        o = _dot(p.astype(CDT), vc_ref[0, 0, 0, :nk, :])
        oc_ref[...] = _unstack_heads(o, tq).astype(oc_ref.dtype)
        psum = p[0:tq]
        for g in range(1, NSA_GROUP):
            psum = psum + p[g * tq:(g + 1) * tq]
        hi, lo = _split_hi_lo(psum)
        imp_ref[...] = _dot(hi, ov_ref[:nk, :]) + _dot(lo, ov_ref[:nk, :])

    tiles_per_var = pl.num_programs(2) // n_var
    for v in range(n_var):
        @pl.when(i // tiles_per_var == v)
        def _(v=v):
            attend(nb * (v + 1) // n_var)

    blk = lax.broadcasted_iota(jnp.int32, (ns, 1), 0)
    lanes = min(128, tq)
    ridx = lax.broadcasted_iota(jnp.int32, (ns, lanes), 0).astype(jnp.float32)
    imp_t = imp_ref[...].T
    scores, valids = [], []
    for c0 in range(0, tq, lanes):
        cur = (s0 + c0 + lax.broadcasted_iota(jnp.int32, (1, lanes), 1)) // SLC_LEN
        valid = blk <= cur
        forced = (blk == 0) | (blk == cur) | (blk == cur - 1)
        scores.append(jnp.where(valid, jnp.where(forced, SEL_FORCE, imp_t[:, c0:c0 + lanes]), NEG))
        valids.append(valid)
    sels = [jnp.zeros((ns, lanes), jnp.float32)] * len(scores)
    for _ in range(n_sel):
        for c in range(len(scores)):
            _, pick = _extract_max(scores[c], ridx)
            scores[c] = jnp.where(pick, LOWEST, scores[c])
            sels[c] = jnp.where(pick, 1.0, sels[c])
    bias = jnp.concatenate([jnp.where((sel > 0.5) & valid, 0.0, NEG) for sel, valid in zip(sels, valids)],
                           axis=1).T
    extra = sb_ref.shape[3] - ns
    if extra:
        bias = jnp.concatenate([bias, jnp.full((tq, extra), NEG, jnp.float32)], axis=1)
    sb_ref[0, 0] = bias.astype(sb_ref.dtype)


def _nsa_compressed(proj, cmp_kv, batch, seq):
    tq = 256
    nq = seq // tq
    nb = seq // CMP_STRIDE
    ns = seq // SLC_LEN
    n_sel = min(SLC_TOPN, ns)
    c_start = np.arange(nb) * CMP_STRIDE
    s_start = np.arange(ns) * SLC_LEN
    ov = np.maximum(np.minimum(c_start[:, None] + CMP_LEN, s_start[None, :] + SLC_LEN)
                    - np.maximum(c_start[:, None], s_start[None, :]), 0).astype(np.float32) / CMP_LEN
    ov[nb - 1] = 0.0
    qb = COL_QRAW // (NSA_GROUP * HEAD_DIM)
    return pl.pallas_call(
        functools.partial(_nsa_cmp_kernel, tq=tq, n_sel=n_sel,
                          n_var=4 if (nb % (4 * 128) == 0 and nq % 4 == 0) else 1),
        out_shape=[jax.ShapeDtypeStruct((batch * seq, NSA_Q), CDT),
                   jax.ShapeDtypeStruct((batch, NSA_KV_HEADS, seq, max(ns, 128)), CDT)],
        grid=(batch, NSA_KV_HEADS, nq),
        in_specs=[pl.BlockSpec((tq, NSA_GROUP * HEAD_DIM), lambda b, k, i: (b * nq + i, qb + k)),
                  pl.BlockSpec((1, 1, 1, nb, HEAD_DIM), lambda b, k, i: (0, b, k, 0, 0)),
                  pl.BlockSpec((1, 1, 1, nb, HEAD_DIM), lambda b, k, i: (1, b, k, 0, 0)),
                  pl.BlockSpec((nb, ns), lambda b, k, i: (0, 0))],
        out_specs=[pl.BlockSpec((tq, NSA_GROUP * HEAD_DIM), lambda b, k, i: (b * nq + i, k)),
                   pl.BlockSpec((1, 1, tq, max(ns, 128)), lambda b, k, i: (b, k, i, 0))],
        scratch_shapes=[pltpu.VMEM((tq, ns), jnp.float32)],
        compiler_params=_params(("arbitrary", "arbitrary", "arbitrary")),
        name="nsa_compressed",
    )(proj, cmp_kv, cmp_kv, jnp.asarray(ov, CDT))


def _flash_tile(s_ref, p_ref, m_ref, acc_ref, rc, tq, bias_ref=None, causal=None):
    rows, tk = s_ref.shape

    def tile(r0, c0):
        s = s_ref[r0:r0 + rc, c0:c0 + 128]
        if bias_ref is not None:
            s = s + bias_ref[r0 % tq:r0 % tq + rc, c0:c0 + 128]
        if causal is not None:
            kpos, t0 = causal
            t = t0 + (r0 % tq) + lax.broadcasted_iota(jnp.int32, (rc, 1), 0)
            s = jnp.where(kpos[:, c0:c0 + 128] <= t, s, NEG)
        return s

    for r0 in range(0, rows, rc):
        mx = tile(r0, 0)
        for c0 in range(128, tk, 128):
            mx = jnp.maximum(mx, tile(r0, c0))
        m_prev = m_ref[r0:r0 + rc, :]
        m_new = jnp.maximum(m_prev, jnp.max(mx, axis=-1, keepdims=True))
        acc_ref[r0:r0 + rc, :] = acc_ref[r0:r0 + rc, :] * jnp.exp2(m_prev - m_new)
        m_ref[r0:r0 + rc, :] = m_new
    for r0 in range(0, rows, rc):
        m_new = m_ref[r0:r0 + rc, :]
        for c0 in range(0, tk, 128):
            p_ref[r0:r0 + rc, c0:c0 + 128] = jnp.exp2(tile(r0, c0) - m_new).astype(p_ref.dtype)


def _nsa_sel_kernel(q_ref, kt_ref, va_ref, sb_ref, o_ref,
                    qp_ref, q4_ref, sbf_ref, s_ref, p_ref, m_ref, acc_ref, *, tq, tk, rc):
    i = pl.program_id(2)
    s0 = i * tq
    nsp = sb_ref.shape[3]
    per_tile = tk // SLC_LEN
    q4 = _stack_heads(q_ref[...])
    qp_ref[...] = jnp.concatenate([q4, jnp.zeros_like(q4)], axis=1)
    sbf_ref[...] = sb_ref[0, 0].astype(jnp.float32)
    lane = lax.broadcasted_iota(jnp.int32, (tq, 128), 1)
    bias_lanes = (lane >= HEAD_DIM) & (lane < HEAD_DIM + per_tile)
    m_ref[...] = jnp.full(m_ref.shape, NEG, jnp.float32)
    acc_ref[...] = jnp.zeros(acc_ref.shape, jnp.float32)

    def step(kt, masked):
        start = pl.multiple_of(kt * tk, tk)
        shift = (HEAD_DIM - kt * per_tile + nsp) % nsp
        rolled = pltpu.roll(sbf_ref[...], shift, axis=1)[:, :128].astype(q4_ref.dtype)
        for g in range(NSA_GROUP):
            q4_ref[g * tq:(g + 1) * tq, :] = jnp.where(bias_lanes, rolled, qp_ref[g * tq:(g + 1) * tq, :])
        s_ref[...] = _dot(q4_ref[...], kt_ref[0, 0, :, pl.ds(start, tk)])
        causal = (start + lax.broadcasted_iota(jnp.int32, (1, tk), 1), s0) if masked else None
        _flash_tile(s_ref, p_ref, m_ref, acc_ref, rc, tq, causal=causal)
        acc_ref[...] += _dot(p_ref[...], va_ref[0, 0, pl.ds(start, tk), :])

    unroll = 8

    def body(kp, carry):
        for d in range(unroll):
            step(unroll * kp + d, False)
        return carry

    n_full = s0 // tk
    n_trips = n_full // unroll
    lax.fori_loop(0, n_trips, body, 0)
    done = n_trips * unroll
    size = unroll // 2
    while size >= 1:
        take = ((n_full - done) // size) % 2 if size > 1 else (n_full - done) % 2

        @pl.when(take == 1)
        def _(size=size, first=done + ((n_full - done) // (2 * size)) * (2 * size)):
            for d in range(size):
                step(first + d, False)

        size //= 2
    step(n_full, True)
    acc = acc_ref[...]
    o = acc[:, :HEAD_DIM] / acc[:, HEAD_DIM:]
    o_ref[...] = _unstack_heads(o, tq).astype(o_ref.dtype)


def _nsa_selected(proj, ks_t, vs_aug, selb, batch, seq):
    tq = 256
    tk = 512
    nq = seq // tq
    nsp = selb.shape[3]
    per_tile = tk // SLC_LEN
    blk_in_tile = (jnp.arange(seq) // SLC_LEN) % per_tile
    onehot = (jnp.arange(128 - HEAD_DIM)[:, None] == blk_in_tile[None, :]).astype(CDT)
    k_aug = jnp.concatenate(
        [ks_t, jnp.broadcast_to(onehot, (batch, NSA_KV_HEADS, 128 - HEAD_DIM, seq))], axis=2)
    qb = COL_QROT // (NSA_GROUP * HEAD_DIM)
    rows = NSA_GROUP * tq
    return pl.pallas_call(
        functools.partial(_nsa_sel_kernel, tq=tq, tk=tk, rc=64),
        out_shape=jax.ShapeDtypeStruct((batch * seq, NSA_Q), CDT),
        grid=(batch, NSA_KV_HEADS, nq),
        in_specs=[pl.BlockSpec((tq, NSA_GROUP * HEAD_DIM), lambda b, k, i: (b * nq + i, qb + k)),
                  pl.BlockSpec((1, 1, 128, seq), lambda b, k, i: (b, k, 0, 0)),
                  pl.BlockSpec((1, 1, seq, 128), lambda b, k, i: (b, k, 0, 0)),
                  pl.BlockSpec((1, 1, tq, nsp), lambda b, k, i: (b, k, i, 0))],
        out_specs=pl.BlockSpec((tq, NSA_GROUP * HEAD_DIM), lambda b, k, i: (b * nq + i, k)),
        scratch_shapes=[pltpu.VMEM((rows, 128), CDT), pltpu.VMEM((rows, 128), CDT),
                        pltpu.VMEM((tq, nsp), jnp.float32), pltpu.VMEM((rows, tk), jnp.float32),
                        pltpu.VMEM((rows, tk), CDT),
                        pltpu.VMEM((rows, 128), jnp.float32), pltpu.VMEM((rows, 128), jnp.float32)],
        compiler_params=_params(("arbitrary", "arbitrary", "arbitrary")),
        name="nsa_selected",
    )(proj, k_aug, vs_aug, selb)


def _nsa_win_kernel(q_ref, kt_ref, v_ref, o_ref, *, tq):
    k = pl.program_id(1)
    i = pl.program_id(2)
    s0 = i * tq
    span = WIN + tq
    start = pl.multiple_of(jnp.maximum(s0 - WIN, 0), tq)
    q4 = _stack_heads(q_ref[...])
    s = _dot(q4, kt_ref[0, 0, :, pl.ds(start, span)])
    tq_col = s0 + lax.broadcasted_iota(jnp.int32, (tq, 1), 0)
    t4 = jnp.concatenate([tq_col] * NSA_GROUP, axis=0)
    kpos = start + lax.broadcasted_iota(jnp.int32, (1, span), 1)
    mask = (kpos <= t4) & (kpos > t4 - WIN)
    s = jnp.where(mask, s, NEG)
    m = jnp.max(s, axis=-1, keepdims=True)
    p = jnp.where(mask, jnp.exp2(s - m), 0.0)
    l = jnp.maximum(jnp.sum(p, axis=-1, keepdims=True), 1e-30)
    o = _dot(p.astype(CDT), v_ref[pl.ds(start, span), :]) / l
    o = jnp.where(k == 0, o[:, :HEAD_DIM], o[:, HEAD_DIM:])
    o_ref[...] = _unstack_heads(o, tq).astype(o_ref.dtype)


def _nsa_window(proj, kw_t, batch, seq):
    tq = 256
    nq = seq // tq
    qb = COL_QROT // (NSA_GROUP * HEAD_DIM)
    vb = COL_VW // 128
    return pl.pallas_call(
        functools.partial(_nsa_win_kernel, tq=tq),
        out_shape=jax.ShapeDtypeStruct((batch * seq, NSA_Q), CDT),
        grid=(batch, NSA_KV_HEADS, nq),
        in_specs=[pl.BlockSpec((tq, NSA_GROUP * HEAD_DIM), lambda b, k, i: (b * nq + i, qb + k)),
                  pl.BlockSpec((1, 1, HEAD_DIM, seq), lambda b, k, i: (b, k, 0, 0)),
                  pl.BlockSpec((seq, 128), lambda b, k, i: (b, vb))],
        out_specs=pl.BlockSpec((tq, NSA_GROUP * HEAD_DIM), lambda b, k, i: (b * nq + i, k)),
        compiler_params=_params(("arbitrary", "arbitrary", "arbitrary")),
        name="nsa_window",
    )(proj, kw_t, proj)


def _moba_mean_kernel(k_ref, o_ref):
    seq, w = k_ref.shape
    nbm = seq // MOBA_BLOCK
    k = k_ref[...].astype(jnp.float32).reshape(nbm, MOBA_BLOCK, w)
    o_ref[0] = jnp.sum(k, axis=1) * (1.0 / MOBA_BLOCK)


def _moba_kmean(proj, batch, seq):
    nbm = seq // MOBA_BLOCK
    kb = COL_MK // 128
    return pl.pallas_call(
        _moba_mean_kernel,
        out_shape=jax.ShapeDtypeStruct((batch, nbm, MOBA_W), jnp.float32),
        grid=(batch, MOBA_W // 128),
        in_specs=[pl.BlockSpec((seq, 128), lambda b, j: (b, kb + j))],
        out_specs=pl.BlockSpec((1, nbm, 128), lambda b, j: (b, 0, j)),
        compiler_params=_params(("arbitrary", "arbitrary")),
        name="moba_kmean",
    )(proj)


def _moba_kernel(q_ref, ka_ref, va_ref, km_ref, o_ref, qa_ref, s_ref, p_ref, m_ref, acc_ref,
                 *, tq, tk, n_top, rc):
    i = pl.program_id(2)
    s0 = i * tq
    nbm = km_ref.shape[1]
    blk = lax.broadcasted_iota(jnp.int32, (nbm, 1), 0)
    cur = (s0 + lax.broadcasted_iota(jnp.int32, (1, tq), 1)) // MOBA_BLOCK
    ridx = lax.broadcasted_iota(jnp.int32, (nbm, tq), 0).astype(jnp.float32)
    for hh in range(2):
        q = q_ref[:, hh * HEAD_DIM:(hh + 1) * HEAD_DIM]
        km_hi, km_lo = _split_hi_lo(km_ref[0][:, hh * HEAD_DIM:(hh + 1) * HEAD_DIM])
        gs = _dot_nt(km_hi, q) + _dot_nt(km_lo, q)
        gs = jnp.where(blk < cur, gs, NEG)

        def body(_, carry):
            sc, sel = carry
            _, pick = _extract_max(sc, ridx)
            return jnp.where(pick, LOWEST, sc), jnp.where(pick, 1.0, sel)

        _, sel = lax.fori_loop(0, n_top, body, (gs, jnp.zeros((nbm, tq), jnp.float32)))
        open_blk = ((sel > 0.5) & (gs > NEG * 0.5)) | (blk == cur)
        bias_t = jnp.where(open_blk, 0.0, NEG)
        lead = -nbm % 128
        if lead:
            bias_t = jnp.concatenate([jnp.zeros((lead, tq), jnp.float32), bias_t], axis=0)
        bias = bias_t.T[:, lead:].astype(CDT)
        qa_ref[hh] = jnp.concatenate([q, bias], axis=1)
    m_ref[...] = jnp.full(m_ref.shape, NEG, jnp.float32)
    acc_ref[...] = jnp.zeros(acc_ref.shape, jnp.float32)

    def step(kt, masked):
        start = pl.multiple_of(kt * tk, tk)
        for hh in range(2):
            s_ref[hh] = _dot(qa_ref[hh], ka_ref[0, hh, :, pl.ds(start, tk)])
        causal = (start + lax.broadcasted_iota(jnp.int32, (1, tk), 1), s0) if masked else None
        for hh in range(2):
            _flash_tile(s_ref.at[hh], p_ref.at[hh], m_ref.at[hh], acc_ref.at[hh], rc, tq, causal=causal)
            acc_ref[hh] += _dot(p_ref[hh], va_ref[0, hh, pl.ds(start, tk), :])

    per_q = tq // tk
    per_trip = 4 * per_q

    def body(kp, carry):
        for d in range(per_trip):
            step(kp * per_trip + d, False)
        return carry

    n_full = s0 // tk
    n_trips = n_full // per_trip
    lax.fori_loop(0, n_trips, body, 0)
    done = n_trips * per_trip
    size = per_trip // 2
    while size >= per_q:
        take = ((n_full - done) // size) % 2

        @pl.when(take == 1)
        def _(size=size, first=done + ((n_full - done) // (2 * size)) * (2 * size)):
            for d in range(size):
                step(first + d, False)

        size //= 2

    for d in range(per_q):
        step(n_full + d, True)
    outs = []
    for hh in range(2):
        acc = acc_ref[hh]
        outs.append(acc[:, :HEAD_DIM] / acc[:, HEAD_DIM:])
    o_ref[...] = jnp.concatenate(outs, axis=1).astype(o_ref.dtype)


def _moba(proj, mk_aug_t, mv_aug, kmean, batch, seq):
    tq = min(1024, seq)
    tk = 512
    nq = seq // tq
    nbm = seq // MOBA_BLOCK
    n_top = min(MOBA_TOPK, nbm)
    qb = COL_MQ // 128
    aug = HEAD_DIM + nbm
    return pl.pallas_call(
        functools.partial(_moba_kernel, tq=tq, tk=tk, n_top=n_top, rc=64),
        out_shape=jax.ShapeDtypeStruct((batch * seq, MOBA_W), CDT),
        grid=(batch, MOBA_HEADS // 2, nq),
        in_specs=[pl.BlockSpec((tq, 128), lambda b, p, i: (b * nq + i, qb + p)),
                  pl.BlockSpec((1, 2, aug, seq), lambda b, p, i: (b, p, 0, 0)),
                  pl.BlockSpec((1, 2, seq, 128), lambda b, p, i: (b, p, 0, 0)),
                  pl.BlockSpec((1, nbm, 128), lambda b, p, i: (b, 0, p))],
        out_specs=pl.BlockSpec((tq, 128), lambda b, p, i: (b * nq + i, p)),
        scratch_shapes=[pltpu.VMEM((2, tq, aug), CDT), pltpu.VMEM((2, tq, tk), jnp.float32),
                        pltpu.VMEM((2, tq, tk), CDT), pltpu.VMEM((2, tq, 128), jnp.float32),
                        pltpu.VMEM((2, tq, 128), jnp.float32)],
        compiler_params=_params(("arbitrary", "arbitrary", "arbitrary")),
        name="moba",
    )(proj, mk_aug_t, mv_aug, kmean)


def _merge_kernel(oc_ref, os_ref, ow_ref, om_ref, ng_ref, gn_ref, gm_ref, x_ref, ga_ref,
                  ex_ref, wun_ref, wum_ref, wo_ref, o_ref):
    gates = _sigmoid(ng_ref[...].astype(jnp.float32))
    hi, lo = _split_hi_lo(gates)
    e = _dot(hi, ex_ref[...]) + _dot(lo, ex_ref[...])
    o_nsa = (e[:, :NSA_Q] * oc_ref[...].astype(jnp.float32)
             + e[:, NSA_Q:2 * NSA_Q] * os_ref[...].astype(jnp.float32)
             + e[:, 2 * NSA_Q:] * ow_ref[...].astype(jnp.float32))
    y = (_sigmoid(gn_ref[...].astype(jnp.float32)) * _dot(o_nsa.astype(CDT), wun_ref[...])
         + _sigmoid(gm_ref[...].astype(jnp.float32)) * _dot(om_ref[...], wum_ref[...]))
    o_ref[...] = x_ref[...] + ga_ref[0] * _dot(y.astype(CDT), wo_ref[...])


def _merge(o_c, o_s, o_w, o_m, proj, x2d, ga, w_up_nsa, w_up_moba, w_out, seq):
    n, d = x2d.shape
    tm = min(512, seq)
    per_seq = seq // tm
    ng_w = COL_GN - COL_NG
    ex = np.zeros((ng_w, 3 * NSA_Q), np.float32)
    for h in range(NSA_HEADS):
        for j in range(3):
            ex[h * 3 + j, j * NSA_Q + h * HEAD_DIM: j * NSA_Q + (h + 1) * HEAD_DIM] = 1.0
    row = lambda i: (i, 0)
    const = lambda i: (0, 0)
    return pl.pallas_call(
        _merge_kernel,
        out_shape=jax.ShapeDtypeStruct((n, d), jnp.float32),
        grid=(n // tm,),
        in_specs=[pl.BlockSpec((tm, NSA_Q), row), pl.BlockSpec((tm, NSA_Q), row),
                  pl.BlockSpec((tm, NSA_Q), row), pl.BlockSpec((tm, MOBA_W), row),
                  pl.BlockSpec((tm, ng_w), lambda i: (i, COL_NG // ng_w)),
                  pl.BlockSpec((tm, d), lambda i: (i, COL_GN // d)),
                  pl.BlockSpec((tm, d), lambda i: (i, COL_GM // d)),
                  pl.BlockSpec((tm, d), row),
                  pl.BlockSpec((1, 1, d), lambda i: (i // per_seq, 0, 0)),
                  pl.BlockSpec((ng_w, 3 * NSA_Q), const),
                  pl.BlockSpec((NSA_Q, d), const), pl.BlockSpec((MOBA_W, d), const),
                  pl.BlockSpec((d, d), const)],
        out_specs=pl.BlockSpec((tm, d), row),
        compiler_params=_params(("arbitrary",)),
        name="mixer_merge",
    )(o_c, o_s, o_w, o_m, proj, proj, proj, x2d, ga, jnp.asarray(ex, CDT),
      w_up_nsa.astype(CDT), w_up_moba.astype(CDT), w_out.astype(CDT))


def _extract_max(x, ridx):
    m = jnp.max(x, axis=0, keepdims=True)
    idx = jnp.min(jnp.where(x == m, ridx, float(x.shape[0])), axis=0, keepdims=True)
    return m, ridx == idx


def _peer_score_kernel(q_ref, k1_ref, k2_ref, cnt_ref, rk_ref, e1_ref, e2_ref):
    half = PEER_QDIM // 2
    k = PEER_TOPK
    q = q_ref[...]
    s1_all = _dot_nt(k1_ref[...], q[:, :half])
    s2_all = _dot_nt(k2_ref[...], q[:, half:])
    t = 128
    ridx = lax.broadcasted_iota(jnp.int32, (PEER_NKEYS, t), 0).astype(jnp.float32)
    unranked = jnp.full((PEER_NKEYS, t), float(k), jnp.float32)
    chunks = list(range(0, s1_all.shape[1], t))
    shifted = []
    for c0 in chunks:
        for s_all in (s1_all, s2_all):
            s = s_all[:, c0:c0 + t]
            shifted.append(s - jnp.max(s, axis=0, keepdims=True))
    xs = list(shifted)
    rks = [unranked] * len(xs)
    tops = [[] for _ in xs]
    for i in range(k):
        for n in range(len(xs)):
            m, pick = _extract_max(xs[n], ridx)
            xs[n] = jnp.where(pick, LOWEST, xs[n])
            rks[n] = jnp.where(pick, float(i), rks[n])
            tops[n].append(m)

    v2_alls, cands = [], []
    for c in range(len(chunks)):
        v1, v2_all = tops[2 * c], jnp.concatenate(tops[2 * c + 1], axis=0)
        pieces = [v1[i] + v2_all[0:k // (i + 1)] for i in range(k)]
        pad = -sum(p.shape[0] for p in pieces) % 8
        cands.append(jnp.concatenate(pieces + [jnp.full((pad, t), LOWEST, jnp.float32)], axis=0))
        v2_alls.append(v2_all)
    cidx = lax.broadcasted_iota(jnp.int32, cands[0].shape, 0).astype(jnp.float32)
    vals = [[] for _ in chunks]
    for i in range(k):
        for c in range(len(chunks)):
            m, pick = _extract_max(cands[c], cidx)
            cands[c] = jnp.where(pick, LOWEST, cands[c])
            vals[c].append(m)

    for c, c0 in enumerate(chunks):
        v1, v2_all, val = tops[2 * c], v2_alls[c], vals[c]
        tau = val[k - 1]
        z = val[0] - val[0] + 1.0
        for i in range(1, k):
            z = z + jnp.exp(val[i] - val[0])
        cnt = jnp.zeros((PEER_NKEYS, t), jnp.float32)
        for i in range(k):
            n_i = jnp.sum(jnp.where(v1[i] + v2_all >= tau, 1.0, 0.0), axis=0, keepdims=True)
            cnt = jnp.where(rks[2 * c] == float(i), n_i, cnt)
        cnt_ref[0, :, c0:c0 + t] = cnt
        rk_ref[0, :, c0:c0 + t] = rks[2 * c + 1].astype(rk_ref.dtype)
        e1_ref[0, :, c0:c0 + t] = jnp.exp(shifted[2 * c] - val[0]) / z
        e2_ref[0, :, c0:c0 + t] = jnp.exp(shifted[2 * c + 1]).astype(e2_ref.dtype)


def _peer_scores(qp, k1, k2):
    n = qp.shape[0]
    tt = 512
    f32 = jax.ShapeDtypeStruct((PEER_HEADS, PEER_NKEYS, n), jnp.float32)
    cdt = jax.ShapeDtypeStruct((PEER_HEADS, PEER_NKEYS, n), CDT)
    big = pl.BlockSpec((1, PEER_NKEYS, tt), lambda i, h: (h, 0, i))
    return pl.pallas_call(
        _peer_score_kernel,
        out_shape=[f32, cdt, f32, cdt],
        grid=(n // tt, PEER_HEADS),
        in_specs=[pl.BlockSpec((tt, PEER_QDIM), lambda i, h: (i, h)),
                  pl.BlockSpec((PEER_NKEYS, PEER_QDIM // 2), lambda i, h: (0, 0)),
                  pl.BlockSpec((PEER_NKEYS, PEER_QDIM // 2), lambda i, h: (0, 0))],
        out_specs=[big, big, big, big],
        compiler_params=_params(("arbitrary", "arbitrary")),
        name="peer_scores",
    )(qp, k1.astype(CDT), k2.astype(CDT))


def _row_to_rows(row, n):
    tile_rows = 16
    tile = jnp.broadcast_to(row, (tile_rows, row.shape[1])).astype(CDT)
    return jnp.concatenate([tile] * (n // tile_rows), axis=0)


def _peer_expert_kernel(h_ref, u_ref, vt_ref, cnt_ref, rk_ref, e1_ref, e2_ref, x_ref, ga_ref,
                        o_ref, acc_ref, *, eb):
    j = pl.program_id(1)
    tt = h_ref.shape[0]

    @pl.when(j == 0)
    def _():
        acc_ref[...] = jnp.zeros(acc_ref.shape, jnp.float32)

    parts = []
    piece = 2 * PEER_NKEYS
    for e0 in range(0, eb, piece):
        sc = _dot_nt(u_ref[e0:e0 + piece, :], h_ref[...])
        ws = []
        for al in range(e0 // PEER_NKEYS, (e0 + piece) // PEER_NKEYS):
            a = j * (eb // PEER_NKEYS) + al
            w = None
            for hd in range(PEER_HEADS):
                cnt_a = _row_to_rows(cnt_ref[hd, pl.ds(a, 1), :], PEER_NKEYS)
                e1_a = _row_to_rows(e1_ref[hd, pl.ds(a, 1), :], PEER_NKEYS)
                contrib = jnp.where(rk_ref[hd] < cnt_a, e2_ref[hd] * e1_a, jnp.zeros((), CDT))
                w = contrib if w is None else w + contrib
            ws.append(w)
        parts.append(jnp.concatenate(ws, axis=0) * _gelu(sc.astype(CDT)))
    pw = jnp.concatenate(parts, axis=0)
    acc_ref[...] += _dot(vt_ref[...], pw)

    @pl.when(j == pl.num_programs(1) - 1)
    def _():
        o_ref[...] = x_ref[...] + ga_ref[0] * acc_ref[...].T


def _peer_experts(h2, u, v_t, cnt, rk2, e1, e2, x2d, ga, seq):
    n, d = x2d.shape
    tt = min(512, seq)
    per_seq = seq // tt
    eb = 2048
    n_blk = u.shape[0] // eb
    big = pl.BlockSpec((PEER_HEADS, PEER_NKEYS, tt), lambda i, j: (0, 0, i))
    return pl.pallas_call(
        functools.partial(_peer_expert_kernel, eb=eb),
        out_shape=jax.ShapeDtypeStruct((n, d), jnp.float32),
        grid=(n // tt, n_blk),
        in_specs=[pl.BlockSpec((tt, d), lambda i, j: (i, 0)),
                  pl.BlockSpec((eb, d), lambda i, j: (j, 0)),
                  pl.BlockSpec((d, eb), lambda i, j: (0, j)),
                  big, big, big, big,
                  pl.BlockSpec((tt, d), lambda i, j: (i, 0)),
                  pl.BlockSpec((1, 1, d), lambda i, j: (i // per_seq, 0, 0))],
        out_specs=pl.BlockSpec((tt, d), lambda i, j: (i, 0)),
        scratch_shapes=[pltpu.VMEM((d, tt), jnp.float32)],
        compiler_params=_params(("arbitrary", "arbitrary")),
        name="peer_experts",
    )(h2, u, v_t, cnt, rk2, e1, e2, x2d, ga)


def _rms_kernel(x_ref, g_ref, o_ref):
    x = x_ref[...]
    ms = jnp.mean(x * x, axis=-1, keepdims=True)
    o_ref[...] = x * lax.rsqrt(ms + RMS_EPS) * g_ref[...]


def _final_norm(x2d, g):
    n, d = x2d.shape
    tm = 512
    return pl.pallas_call(
        _rms_kernel,
        out_shape=jax.ShapeDtypeStruct((n, d), jnp.float32),
        grid=(n // tm,),
        in_specs=[pl.BlockSpec((tm, d), lambda i: (i, 0)), pl.BlockSpec((1, d), lambda i: (0, 0))],
        out_specs=pl.BlockSpec((tm, d), lambda i: (i, 0)),
        compiler_params=_params(("arbitrary",)),
        name="final_rmsnorm",
    )(x2d, g.reshape(1, d))


def _key_major(proj, col, heads, batch, seq):
    k = proj[:, col:col + heads * HEAD_DIM].reshape(batch, seq, heads, HEAD_DIM)
    return jnp.transpose(k, (0, 2, 3, 1))


def _value_with_ones(proj, col, heads, batch, seq):
    v = proj[:, col:col + heads * HEAD_DIM].reshape(batch, seq, heads, HEAD_DIM)
    v = jnp.transpose(v, (0, 2, 1, 3))
    return jnp.concatenate([v, jnp.ones_like(v)], axis=-1)


def kernel(x, c, w_ada, b_ada, g_attn, g_ffn, w_in, cmp_pe, cmp_w1, cmp_w2, w_up_nsa, w_up_moba, w_out,
           peer_wq, peer_k1, peer_k2, peer_u, peer_v, g_final):
    batch, seq, d = x.shape
    depth = w_ada.shape[0]
    n = batch * seq
    nbm = seq // MOBA_BLOCK
    x2d = x.reshape(n, d)
    mod = _adaln_mod(c, w_ada, b_ada)
    rope = _rope_tables(seq)
    blk_onehot = (jnp.arange(nbm)[:, None] == jnp.arange(seq)[None, :] // MOBA_BLOCK).astype(CDT)
    for l in range(depth):
        sh1, sc1, ga1, sh2, sc2, ga2 = [m.reshape(batch, 1, d) for m in jnp.split(mod[l], 6, axis=-1)]
        proj = _norm_mod_matmul(x2d, g_attn[l], sc1, sh1, _reorder_w_in(w_in[l]), seq,
                                rope=rope, n_rope=ROPE_COLS // PROJ_TN)
        cmp_kv = _nsa_compress(proj, cmp_pe[l], cmp_w1[l], cmp_w2[l], batch, seq)
        o_c, selb = _nsa_compressed(proj, cmp_kv, batch, seq)
        o_s = _nsa_selected(proj, _key_major(proj, COL_KS, NSA_KV_HEADS, batch, seq),
                            _value_with_ones(proj, COL_VS, NSA_KV_HEADS, batch, seq), selb, batch, seq)
        o_w = _nsa_window(proj, _key_major(proj, COL_KW, NSA_KV_HEADS, batch, seq), batch, seq)
        mk_t = _key_major(proj, COL_MK, MOBA_HEADS, batch, seq)
        mk_aug = jnp.concatenate(
            [mk_t, jnp.broadcast_to(blk_onehot, (batch, MOBA_HEADS, nbm, seq))], axis=2)
        o_m = _moba(proj, mk_aug, _value_with_ones(proj, COL_MV, MOBA_HEADS, batch, seq),
                    _moba_kmean(proj, batch, seq), batch, seq)
        x2d = _merge(o_c, o_s, o_w, o_m, proj, x2d, ga1, w_up_nsa[l], w_up_moba[l], w_out[l], seq)
        qp, h2 = _norm_mod_matmul(x2d, g_ffn[l], sc2, sh2, peer_wq[l].astype(CDT), seq, emit_h=True)
        cnt, rk2, e1, e2 = _peer_scores(qp, peer_k1[l], peer_k2[l])
        x2d = _peer_experts(h2, peer_u[l].astype(CDT), peer_v[l].T.astype(CDT), cnt, rk2, e1, e2,
                            x2d, ga2, seq)
    return _final_norm(x2d, g_final).reshape(batch, seq, d)
```

```python
import functools

import jax
import jax.numpy as jnp
import numpy as np
from jax import lax
from jax.experimental import pallas as pl
from jax.experimental.pallas import tpu as pltpu

D_MODEL = 1024
HEAD_DIM = 64
ROT_DIM = HEAD_DIM // 4
ROPE_THETA = 500000.0
NSA_HEADS = 8
NSA_KV_HEADS = 2
NSA_GROUP = NSA_HEADS // NSA_KV_HEADS
CMP_LEN = 32
CMP_STRIDE = 16
CMP_HID = 2 * HEAD_DIM
SLC_LEN = 64
SLC_TOPN = 16
WIN = 512
MOBA_HEADS = 8
MOBA_BLOCK = 256
MOBA_TOPK = 3
PEER_HEADS = 8
PEER_NKEYS = 128
PEER_QDIM = 256
PEER_TOPK = 16
RMS_EPS = 1e-6
NEG = -1e30
SEL_FORCE = 1e4
LOWEST = -3.0e38
LOG2E = 1.4426950408889634

NSA_Q = NSA_HEADS * HEAD_DIM
NSA_KV = NSA_KV_HEADS * HEAD_DIM
MOBA_W = MOBA_HEADS * HEAD_DIM

CDT = jnp.bfloat16
V7X_VMEM_LIMIT = 56 * 1024 * 1024

COL_QROT = 0
COL_KS = 512
COL_KW = 640
COL_MQ = 768
COL_MK = 1280
ROPE_COLS = 1792
COL_NG = 1792
COL_GN = 2048
COL_GM = 3072
COL_QRAW = 4096
COL_KC = 4608
COL_VC = 4736
COL_VS = 4864
COL_VW = 4992
COL_MV = 5120
PROJ_COLS = 5632
PROJ_TN = 256


def _params(sem):
    return pltpu.CompilerParams(dimension_semantics=sem, vmem_limit_bytes=V7X_VMEM_LIMIT)


def _dot(a, b):
    return jnp.dot(a, b, preferred_element_type=jnp.float32)


def _dot_nt(a, b):
    return lax.dot_general(a, b, (((1,), (1,)), ((), ())), preferred_element_type=jnp.float32)


def _split_hi_lo(x):
    hi = x.astype(CDT)
    lo = (x - hi.astype(jnp.float32)).astype(CDT)
    return hi, lo


def _gelu(x):
    return 0.5 * x * (1.0 + jnp.tanh(0.7978845608028654 * (x + 0.044715 * (x * x * x))))


def _sigmoid(x):
    return 1.0 / (1.0 + jnp.exp(-x))


def _mod_kernel(c_ref, w_ref, b_ref, o_ref):
    c = c_ref[...]
    sc = c * _sigmoid(c)
    o_ref[0] = jnp.dot(sc, w_ref[0], preferred_element_type=jnp.float32,
                       precision=lax.Precision.HIGHEST) + b_ref[0]


def _adaln_mod(c, w_ada, b_ada):
    depth, d, six_d = w_ada.shape
    b = c.shape[0]
    rows = 8
    c_pad = jnp.zeros((rows, d), jnp.float32).at[:b].set(c)
    tn = 1024
    out = pl.pallas_call(
        _mod_kernel,
        out_shape=jax.ShapeDtypeStruct((depth, rows, six_d), jnp.float32),
        grid=(depth, six_d // tn),
        in_specs=[pl.BlockSpec((rows, d), lambda l, j: (0, 0)),
                  pl.BlockSpec((1, d, tn), lambda l, j: (l, 0, j)),
                  pl.BlockSpec((1, 1, tn), lambda l, j: (l, 0, j))],
        out_specs=pl.BlockSpec((1, rows, tn), lambda l, j: (l, 0, j)),
        compiler_params=_params(("arbitrary", "arbitrary")),
        name="adaln_mod",
    )(c_pad, w_ada, b_ada.reshape(depth, 1, six_d))
    return out[:, :b]


def _nmm_kernel(*refs, n_rope, emit_h, tn):
    if n_rope:
        x_ref, g_ref, sc_ref, sh_ref, w_ref, cos_ref, sa_ref, sb_ref = refs[:8]
        rest = refs[8:]
    else:
        x_ref, g_ref, sc_ref, sh_ref, w_ref = refs[:5]
        rest = refs[5:]
    if emit_h:
        o_ref, ho_ref, h_ref = rest
    else:
        o_ref, h_ref = rest
    x = x_ref[...]
    ms = jnp.mean(x * x, axis=-1, keepdims=True)
    y = x * lax.rsqrt(ms + RMS_EPS) * g_ref[...]
    h = (y * (1.0 + sc_ref[0]) + sh_ref[0]).astype(h_ref.dtype)
    h_ref[...] = h
    if emit_h:
        ho_ref[...] = h

    for j in range(w_ref.shape[1] // tn):
        acc = _dot(h_ref[...], w_ref[:, j * tn:(j + 1) * tn])
        if j < n_rope:
            cos, sa, sb = cos_ref[...], sa_ref[...], sb_ref[...]
            for c0 in range(0, tn, 128):
                a = acc[:, c0:c0 + 128]
                r = (a * cos + pltpu.roll(a, 128 - ROT_DIM // 2, axis=1) * sa
                     + pltpu.roll(a, ROT_DIM // 2, axis=1) * sb)
                o_ref[:, j * tn + c0:j * tn + c0 + 128] = r.astype(o_ref.dtype)
        else:
            o_ref[:, j * tn:(j + 1) * tn] = acc.astype(o_ref.dtype)


def _norm_mod_matmul(x2d, g, sc, sh, w, seq, rope=None, n_rope=0, emit_h=False, tn=PROJ_TN):
    n, d = x2d.shape
    cols = w.shape[1]
    tm = min(512, seq)
    per_seq = seq // tm
    in_specs = [pl.BlockSpec((tm, d), lambda i: (i, 0)),
                pl.BlockSpec((1, d), lambda i: (0, 0)),
                pl.BlockSpec((1, 1, d), lambda i: (i // per_seq, 0, 0)),
                pl.BlockSpec((1, 1, d), lambda i: (i // per_seq, 0, 0)),
                pl.BlockSpec((d, cols), lambda i: (0, 0))]
    args = [x2d, g.reshape(1, d), sc, sh, w]
    if n_rope:
        in_specs += [pl.BlockSpec((tm, 128), lambda i: (i % per_seq, 0))] * 3
        args += list(rope)
    out_shape = [jax.ShapeDtypeStruct((n, cols), CDT)]
    out_specs = [pl.BlockSpec((tm, cols), lambda i: (i, 0))]
    if emit_h:
        out_shape.append(jax.ShapeDtypeStruct((n, d), CDT))
        out_specs.append(pl.BlockSpec((tm, d), lambda i: (i, 0)))
    res = pl.pallas_call(
        functools.partial(_nmm_kernel, n_rope=n_rope, emit_h=emit_h, tn=tn),
        out_shape=out_shape,
        grid=(n // tm,),
        in_specs=in_specs,
        out_specs=out_specs,
        scratch_shapes=[pltpu.VMEM((tm, d), CDT)],
        compiler_params=_params(("arbitrary",)),
        name="norm_mod_matmul",
    )(*args)
    return res if emit_h else res[0]


def _rope_tables(seq):
    half = ROT_DIM // 2
    inv = ROPE_THETA ** (-jnp.arange(half, dtype=jnp.float32) / half)
    ang = jnp.arange(seq, dtype=jnp.float32)[:, None] * inv[None, :]
    cos, sin = jnp.cos(ang), jnp.sin(ang)
    ones = jnp.ones((seq, HEAD_DIM - ROT_DIM), jnp.float32)
    zeros = jnp.zeros((seq, HEAD_DIM - ROT_DIM), jnp.float32)
    zh = jnp.zeros((seq, half), jnp.float32)
    c64 = jnp.concatenate([cos, cos, ones], axis=1)
    sa64 = jnp.concatenate([-sin, zh, zeros], axis=1)
    sb64 = jnp.concatenate([zh, sin, zeros], axis=1)
    return tuple(jnp.concatenate([t, t], axis=1) for t in (c64, sa64, sb64))


def _reorder_w_in(w_in):
    o = 0
    pieces = {}
    for name, size in (("nq", NSA_Q), ("kc", NSA_KV), ("vc", NSA_KV), ("ks", NSA_KV), ("vs", NSA_KV),
                       ("kw", NSA_KV), ("vw", NSA_KV), ("ng", 3 * NSA_HEADS), ("mq", MOBA_W),
                       ("mk", MOBA_W), ("mv", MOBA_W), ("gn", D_MODEL), ("gm", D_MODEL)):
        pieces[name] = w_in[:, o:o + size]
        o += size
    scale = HEAD_DIM ** -0.5 * LOG2E
    pad = jnp.zeros((w_in.shape[0], COL_GN - COL_NG - 3 * NSA_HEADS), w_in.dtype)
    w = jnp.concatenate([pieces["nq"] * scale, pieces["ks"], pieces["kw"], pieces["mq"] * scale,
                         pieces["mk"], pieces["ng"], pad, pieces["gn"], pieces["gm"],
                         pieces["nq"] * scale, pieces["kc"], pieces["vc"], pieces["vs"],
                         pieces["vw"], pieces["mv"]], axis=1)
    assert w.shape[1] == PROJ_COLS
    return w.astype(CDT)


def _compress_kernel(x_ref, w1c_ref, pe_ref, w1_ref, w2_ref, o_ref):
    nb = x_ref.shape[3]
    ab = _dot(x_ref[0, 0, 0], w1c_ref[0])
    a = ab[:, :CMP_HID]
    b_next = pltpu.roll(ab[:, CMP_HID:], nb - 1, axis=0)
    bias = _dot(pe_ref[0], w1_ref[0])[0:1]
    hid = _gelu(a + b_next + bias)
    o_ref[0, 0, 0] = _dot(hid.astype(CDT), w2_ref[0]).astype(o_ref.dtype)


def _nsa_compress(proj, cmp_pe, cmp_w1, cmp_w2, batch, seq):
    nb = seq // CMP_STRIDE
    half = CMP_STRIDE * HEAD_DIM
    kcvc = proj[:, COL_KC:COL_KC + 2 * NSA_KV].reshape(batch, nb, CMP_STRIDE, 2, NSA_KV_HEADS, HEAD_DIM)
    x = jnp.transpose(kcvc, (3, 0, 4, 1, 2, 5)).reshape(2, batch, NSA_KV_HEADS, nb, half)
    w1c = jnp.concatenate([cmp_w1[:, :half], cmp_w1[:, half:]], axis=2).astype(CDT)
    pe = jnp.broadcast_to(cmp_pe.reshape(2, 1, CMP_LEN * HEAD_DIM), (2, 8, CMP_LEN * HEAD_DIM)).astype(CDT)
    return pl.pallas_call(
        _compress_kernel,
        out_shape=jax.ShapeDtypeStruct((2, batch, NSA_KV_HEADS, nb, HEAD_DIM), CDT),
        grid=(2, batch, NSA_KV_HEADS),
        in_specs=[pl.BlockSpec((1, 1, 1, nb, half), lambda w, b, k: (w, b, k, 0, 0)),
                  pl.BlockSpec((1, half, 2 * CMP_HID), lambda w, b, k: (w, 0, 0)),
                  pl.BlockSpec((1, 8, 2 * half), lambda w, b, k: (w, 0, 0)),
                  pl.BlockSpec((1, 2 * half, CMP_HID), lambda w, b, k: (w, 0, 0)),
                  pl.BlockSpec((1, CMP_HID, HEAD_DIM), lambda w, b, k: (w, 0, 0))],
        out_specs=pl.BlockSpec((1, 1, 1, nb, HEAD_DIM), lambda w, b, k: (w, b, k, 0, 0)),
        compiler_params=_params(("arbitrary", "arbitrary", "arbitrary")),
        name="nsa_compress",
    )(x, w1c, pe, cmp_w1.astype(CDT), cmp_w2.astype(CDT))


def _stack_heads(qblk):
    return jnp.concatenate([qblk[:, g * HEAD_DIM:(g + 1) * HEAD_DIM] for g in range(NSA_GROUP)], axis=0)


def _unstack_heads(o, tq):
    return jnp.concatenate([o[g * tq:(g + 1) * tq] for g in range(NSA_GROUP)], axis=1)


def _topk_mask(score, n_sel):
    shape = score.shape
    lane = lax.broadcasted_iota(jnp.int32, shape, 1).astype(jnp.float32)
    width = float(shape[1])

    def body(_, carry):
        sc, sel = carry
        m = jnp.max(sc, axis=-1, keepdims=True)
        idx = jnp.min(jnp.where(sc == m, lane, width), axis=-1, keepdims=True)
        pick = lane == idx
        return jnp.where(pick, LOWEST, sc), jnp.where(pick, 1.0, sel)

    _, sel = lax.fori_loop(0, n_sel, body, (score, jnp.zeros(shape, jnp.float32)))
    return sel


def _nsa_cmp_kernel(q_ref, kc_ref, vc_ref, ov_ref, oc_ref, sb_ref, imp_ref, *, tq, n_sel, n_var):
    i = pl.program_id(2)
    s0 = i * tq
    nb = kc_ref.shape[3]
    ns = ov_ref.shape[1]
    tq_col = s0 + lax.broadcasted_iota(jnp.int32, (tq, 1), 0)
    t4 = jnp.concatenate([tq_col] * NSA_GROUP, axis=0)

    def attend(nk):
        q4 = _stack_heads(q_ref[...])
        s = _dot_nt(q4, kc_ref[0, 0, 0, :nk, :])
        cend = lax.broadcasted_iota(jnp.int32, (1, nk), 1) * CMP_STRIDE + (CMP_LEN - 1)
        mask = cend <= t4
        s = jnp.where(mask, s, NEG)
        m = jnp.max(s, axis=-1, keepdims=True)
        p = jnp.where(mask, jnp.exp2(s - m), 0.0)
        p = p / jnp.maximum(jnp.sum(p, axis=-1, keepdims=True), 1e-30)
        o = _dot(p.astype(CDT), vc_ref[0, 0, 0, :nk, :])
        oc_ref[...] = _unstack_heads(o, tq).astype(oc_ref.dtype)
        psum = p[0:tq]
        for g in range(1, NSA_GROUP):
            psum = psum + p[g * tq:(g + 1) * tq]
        hi, lo = _split_hi_lo(psum)
        imp_ref[...] = _dot(hi, ov_ref[:nk, :]) + _dot(lo, ov_ref[:nk, :])

    tiles_per_var = pl.num_programs(2) // n_var
    for v in range(n_var):
        @pl.when(i // tiles_per_var == v)
        def _(v=v):
            attend(nb * (v + 1) // n_var)

    blk = lax.broadcasted_iota(jnp.int32, (ns, 1), 0)
    lanes = min(128, tq)
    ridx = lax.broadcasted_iota(jnp.int32, (ns, lanes), 0).astype(jnp.float32)
    imp_t = imp_ref[...].T
    scores, valids = [], []
    for c0 in range(0, tq, lanes):
        cur = (s0 + c0 + lax.broadcasted_iota(jnp.int32, (1, lanes), 1)) // SLC_LEN
        valid = blk <= cur
        forced = (blk == 0) | (blk == cur) | (blk == cur - 1)
        scores.append(jnp.where(valid, jnp.where(forced, SEL_FORCE, imp_t[:, c0:c0 + lanes]), NEG))
        valids.append(valid)
    sels = [jnp.zeros((ns, lanes), jnp.float32)] * len(scores)
    for _ in range(n_sel):
        for c in range(len(scores)):
            _, pick = _extract_max(scores[c], ridx)
            scores[c] = jnp.where(pick, LOWEST, scores[c])
            sels[c] = jnp.where(pick, 1.0, sels[c])
    bias = jnp.concatenate([jnp.where((sel > 0.5) & valid, 0.0, NEG) for sel, valid in zip(sels, valids)],
                           axis=1).T
    extra = sb_ref.shape[3] - ns
    if extra:
        bias = jnp.concatenate([bias, jnp.full((tq, extra), NEG, jnp.float32)], axis=1)
    sb_ref[0, 0] = bias.astype(sb_ref.dtype)


def _nsa_compressed(proj, cmp_kv, batch, seq):
    tq = 256
    nq = seq // tq
    nb = seq // CMP_STRIDE
    ns = seq // SLC_LEN
    n_sel = min(SLC_TOPN, ns)
    c_start = np.arange(nb) * CMP_STRIDE
    s_start = np.arange(ns) * SLC_LEN
    ov = np.maximum(np.minimum(c_start[:, None] + CMP_LEN, s_start[None, :] + SLC_LEN)
                    - np.maximum(c_start[:, None], s_start[None, :]), 0).astype(np.float32) / CMP_LEN
    ov[nb - 1] = 0.0
    qb = COL_QRAW // (NSA_GROUP * HEAD_DIM)
    return pl.pallas_call(
        functools.partial(_nsa_cmp_kernel, tq=tq, n_sel=n_sel,
                          n_var=4 if (nb % (4 * 128) == 0 and nq % 4 == 0) else 1),
        out_shape=[jax.ShapeDtypeStruct((batch * seq, NSA_Q), CDT),
                   jax.ShapeDtypeStruct((batch, NSA_KV_HEADS, seq, max(ns, 128)), CDT)],
        grid=(batch, NSA_KV_HEADS, nq),
        in_specs=[pl.BlockSpec((tq, NSA_GROUP * HEAD_DIM), lambda b, k, i: (b * nq + i, qb + k)),
                  pl.BlockSpec((1, 1, 1, nb, HEAD_DIM), lambda b, k, i: (0, b, k, 0, 0)),
                  pl.BlockSpec((1, 1, 1, nb, HEAD_DIM), lambda b, k, i: (1, b, k, 0, 0)),
                  pl.BlockSpec((nb, ns), lambda b, k, i: (0, 0))],
        out_specs=[pl.BlockSpec((tq, NSA_GROUP * HEAD_DIM), lambda b, k, i: (b * nq + i, k)),
                   pl.BlockSpec((1, 1, tq, max(ns, 128)), lambda b, k, i: (b, k, i, 0))],
        scratch_shapes=[pltpu.VMEM((tq, ns), jnp.float32)],
        compiler_params=_params(("arbitrary", "arbitrary", "arbitrary")),
        name="nsa_compressed",
    )(proj, cmp_kv, cmp_kv, jnp.asarray(ov, CDT))


def _flash_tile(s_ref, p_ref, m_ref, acc_ref, rc, tq, bias_ref=None, causal=None):
    rows, tk = s_ref.shape

    def tile(r0, c0):
        s = s_ref[r0:r0 + rc, c0:c0 + 128]
        if bias_ref is not None:
            s = s + bias_ref[r0 % tq:r0 % tq + rc, c0:c0 + 128]
        if causal is not None:
            kpos, t0 = causal
            t = t0 + (r0 % tq) + lax.broadcasted_iota(jnp.int32, (rc, 1), 0)
            s = jnp.where(kpos[:, c0:c0 + 128] <= t, s, NEG)
        return s

    for r0 in range(0, rows, rc):
        mx = tile(r0, 0)
        for c0 in range(128, tk, 128):
            mx = jnp.maximum(mx, tile(r0, c0))
        m_prev = m_ref[r0:r0 + rc, :]
        m_new = jnp.maximum(m_prev, jnp.max(mx, axis=-1, keepdims=True))
        acc_ref[r0:r0 + rc, :] = acc_ref[r0:r0 + rc, :] * jnp.exp2(m_prev - m_new)
        m_ref[r0:r0 + rc, :] = m_new
    for r0 in range(0, rows, rc):
        m_new = m_ref[r0:r0 + rc, :]
        for c0 in range(0, tk, 128):
            p_ref[r0:r0 + rc, c0:c0 + 128] = jnp.exp2(tile(r0, c0) - m_new).astype(p_ref.dtype)


def _nsa_sel_kernel(q_ref, kt_ref, va_ref, sb_ref, o_ref,
                    qp_ref, q4_ref, sbf_ref, s_ref, p_ref, m_ref, acc_ref, *, tq, tk, rc):
    i = pl.program_id(2)
    s0 = i * tq
    nsp = sb_ref.shape[3]
    per_tile = tk // SLC_LEN
    q4 = _stack_heads(q_ref[...])
    qp_ref[...] = jnp.concatenate([q4, jnp.zeros_like(q4)], axis=1)
    sbf_ref[...] = sb_ref[0, 0].astype(jnp.float32)
    lane = lax.broadcasted_iota(jnp.int32, (tq, 128), 1)
    bias_lanes = (lane >= HEAD_DIM) & (lane < HEAD_DIM + per_tile)
    m_ref[...] = jnp.full(m_ref.shape, NEG, jnp.float32)
    acc_ref[...] = jnp.zeros(acc_ref.shape, jnp.float32)

    def step(kt, masked):
        start = pl.multiple_of(kt * tk, tk)
        shift = (HEAD_DIM - kt * per_tile + nsp) % nsp
        rolled = pltpu.roll(sbf_ref[...], shift, axis=1)[:, :128].astype(q4_ref.dtype)
        for g in range(NSA_GROUP):
            q4_ref[g * tq:(g + 1) * tq, :] = jnp.where(bias_lanes, rolled, qp_ref[g * tq:(g + 1) * tq, :])
        s_ref[...] = _dot(q4_ref[...], kt_ref[0, 0, :, pl.ds(start, tk)])
        causal = (start + lax.broadcasted_iota(jnp.int32, (1, tk), 1), s0) if masked else None
        _flash_tile(s_ref, p_ref, m_ref, acc_ref, rc, tq, causal=causal)
        acc_ref[...] += _dot(p_ref[...], va_ref[0, 0, pl.ds(start, tk), :])

    unroll = 8

    def body(kp, carry):
        for d in range(unroll):
            step(unroll * kp + d, False)
        return carry

    n_full = s0 // tk
    n_trips = n_full // unroll
    lax.fori_loop(0, n_trips, body, 0)
    done = n_trips * unroll
    size = unroll // 2
    while size >= 1:
        take = ((n_full - done) // size) % 2 if size > 1 else (n_full - done) % 2

        @pl.when(take == 1)
        def _(size=size, first=done + ((n_full - done) // (2 * size)) * (2 * size)):
            for d in range(size):
                step(first + d, False)

        size //= 2
    step(n_full, True)
    acc = acc_ref[...]
    o = acc[:, :HEAD_DIM] / acc[:, HEAD_DIM:]
    o_ref[...] = _unstack_heads(o, tq).astype(o_ref.dtype)


NSA_SEL_TK = 512


def _nsa_selected(proj, k_aug, vs_aug, selb, batch, seq):
    tq = 256
    tk = NSA_SEL_TK
    nq = seq // tq
    nsp = selb.shape[3]
    qb = COL_QROT // (NSA_GROUP * HEAD_DIM)
    rows = NSA_GROUP * tq
    return pl.pallas_call(
        functools.partial(_nsa_sel_kernel, tq=tq, tk=tk, rc=64),
        out_shape=jax.ShapeDtypeStruct((batch * seq, NSA_Q), CDT),
        grid=(batch, NSA_KV_HEADS, nq),
        in_specs=[pl.BlockSpec((tq, NSA_GROUP * HEAD_DIM), lambda b, k, i: (b * nq + i, qb + k)),
                  pl.BlockSpec((1, 1, 128, seq), lambda b, k, i: (b, k, 0, 0)),
                  pl.BlockSpec((1, 1, seq, 128), lambda b, k, i: (b, k, 0, 0)),
                  pl.BlockSpec((1, 1, tq, nsp), lambda b, k, i: (b, k, i, 0))],
        out_specs=pl.BlockSpec((tq, NSA_GROUP * HEAD_DIM), lambda b, k, i: (b * nq + i, k)),
        scratch_shapes=[pltpu.VMEM((rows, 128), CDT), pltpu.VMEM((rows, 128), CDT),
                        pltpu.VMEM((tq, nsp), jnp.float32), pltpu.VMEM((rows, tk), jnp.float32),
                        pltpu.VMEM((rows, tk), CDT),
                        pltpu.VMEM((rows, 128), jnp.float32), pltpu.VMEM((rows, 128), jnp.float32)],
        compiler_params=_params(("arbitrary", "arbitrary", "arbitrary")),
        name="nsa_selected",
    )(proj, k_aug, vs_aug, selb)


def _nsa_win_kernel(q_ref, kt_ref, v_ref, o_ref, *, tq):
    k = pl.program_id(1)
    i = pl.program_id(2)
    s0 = i * tq
    span = WIN + tq
    start = pl.multiple_of(jnp.maximum(s0 - WIN, 0), tq)
    q4 = _stack_heads(q_ref[...])
    s = _dot(q4, kt_ref[0, 0, :, pl.ds(start, span)])
    tq_col = s0 + lax.broadcasted_iota(jnp.int32, (tq, 1), 0)
    t4 = jnp.concatenate([tq_col] * NSA_GROUP, axis=0)
    kpos = start + lax.broadcasted_iota(jnp.int32, (1, span), 1)
    mask = (kpos <= t4) & (kpos > t4 - WIN)
    s = jnp.where(mask, s, NEG)
    m = jnp.max(s, axis=-1, keepdims=True)
    p = jnp.where(mask, jnp.exp2(s - m), 0.0)
    l = jnp.maximum(jnp.sum(p, axis=-1, keepdims=True), 1e-30)
    o = _dot(p.astype(CDT), v_ref[pl.ds(start, span), :]) / l
    o = jnp.where(k == 0, o[:, :HEAD_DIM], o[:, HEAD_DIM:])
    o_ref[...] = _unstack_heads(o, tq).astype(o_ref.dtype)


def _nsa_window(proj, kw_t, batch, seq):
    tq = 256
    nq = seq // tq
    qb = COL_QROT // (NSA_GROUP * HEAD_DIM)
    vb = COL_VW // 128
    return pl.pallas_call(
        functools.partial(_nsa_win_kernel, tq=tq),
        out_shape=jax.ShapeDtypeStruct((batch * seq, NSA_Q), CDT),
        grid=(batch, NSA_KV_HEADS, nq),
        in_specs=[pl.BlockSpec((tq, NSA_GROUP * HEAD_DIM), lambda b, k, i: (b * nq + i, qb + k)),
                  pl.BlockSpec((1, 1, HEAD_DIM, seq), lambda b, k, i: (b, k, 0, 0)),
                  pl.BlockSpec((seq, 128), lambda b, k, i: (b, vb))],
        out_specs=pl.BlockSpec((tq, NSA_GROUP * HEAD_DIM), lambda b, k, i: (b * nq + i, k)),
        compiler_params=_params(("arbitrary", "arbitrary", "arbitrary")),
        name="nsa_window",
    )(proj, kw_t, proj)


def _moba_mean_kernel(k_ref, o_ref):
    seq, w = k_ref.shape
    nbm = seq // MOBA_BLOCK
    k = k_ref[...].astype(jnp.float32).reshape(nbm, MOBA_BLOCK, w)
    o_ref[0] = jnp.sum(k, axis=1) * (1.0 / MOBA_BLOCK)


def _moba_kmean(proj, batch, seq):
    nbm = seq // MOBA_BLOCK
    kb = COL_MK // 128
    return pl.pallas_call(
        _moba_mean_kernel,
        out_shape=jax.ShapeDtypeStruct((batch, nbm, MOBA_W), jnp.float32),
        grid=(batch, MOBA_W // 128),
        in_specs=[pl.BlockSpec((seq, 128), lambda b, j: (b, kb + j))],
        out_specs=pl.BlockSpec((1, nbm, 128), lambda b, j: (b, 0, j)),
        compiler_params=_params(("arbitrary", "arbitrary")),
        name="moba_kmean",
    )(proj)


def _moba_kernel(q_ref, ka_ref, va_ref, km_ref, o_ref, qa_ref, s_ref, p_ref, m_ref, acc_ref,
                 *, tq, tk, n_top, rc):
    i = pl.program_id(2)
    s0 = i * tq
    nbm = km_ref.shape[1]
    blk = lax.broadcasted_iota(jnp.int32, (nbm, 1), 0)
    cur = (s0 + lax.broadcasted_iota(jnp.int32, (1, tq), 1)) // MOBA_BLOCK
    ridx = lax.broadcasted_iota(jnp.int32, (nbm, tq), 0).astype(jnp.float32)
    for hh in range(2):
        q = q_ref[:, hh * HEAD_DIM:(hh + 1) * HEAD_DIM]
        km_hi, km_lo = _split_hi_lo(km_ref[0][:, hh * HEAD_DIM:(hh + 1) * HEAD_DIM])
        gs = _dot_nt(km_hi, q) + _dot_nt(km_lo, q)
        gs = jnp.where(blk < cur, gs, NEG)

        def body(_, carry):
            sc, sel = carry
            _, pick = _extract_max(sc, ridx)
            return jnp.where(pick, LOWEST, sc), jnp.where(pick, 1.0, sel)

        _, sel = lax.fori_loop(0, n_top, body, (gs, jnp.zeros((nbm, tq), jnp.float32)))
        open_blk = ((sel > 0.5) & (gs > NEG * 0.5)) | (blk == cur)
        bias_t = jnp.where(open_blk, 0.0, NEG)
        lead = -nbm % 128
        if lead:
            bias_t = jnp.concatenate([jnp.zeros((lead, tq), jnp.float32), bias_t], axis=0)
        bias = bias_t.T[:, lead:].astype(CDT)
        qa_ref[hh] = jnp.concatenate([q, bias], axis=1)
    m_ref[...] = jnp.full(m_ref.shape, NEG, jnp.float32)
    acc_ref[...] = jnp.zeros(acc_ref.shape, jnp.float32)

    def step(kt, masked):
        start = pl.multiple_of(kt * tk, tk)
        for hh in range(2):
            s_ref[hh] = _dot(qa_ref[hh], ka_ref[0, hh, :, pl.ds(start, tk)])
        causal = (start + lax.broadcasted_iota(jnp.int32, (1, tk), 1), s0) if masked else None
        for hh in range(2):
            _flash_tile(s_ref.at[hh], p_ref.at[hh], m_ref.at[hh], acc_ref.at[hh], rc, tq, causal=causal)
            acc_ref[hh] += _dot(p_ref[hh], va_ref[0, hh, pl.ds(start, tk), :])

    per_q = tq // tk
    per_trip = 4 * per_q

    def body(kp, carry):
        for d in range(per_trip):
            step(kp * per_trip + d, False)
        return carry

    n_full = s0 // tk
    n_trips = n_full // per_trip
    lax.fori_loop(0, n_trips, body, 0)
    done = n_trips * per_trip
    size = per_trip // 2
    while size >= per_q:
        take = ((n_full - done) // size) % 2

        @pl.when(take == 1)
        def _(size=size, first=done + ((n_full - done) // (2 * size)) * (2 * size)):
            for d in range(size):
                step(first + d, False)

        size //= 2

    for d in range(per_q):
        step(n_full + d, True)
    outs = []
    for hh in range(2):
        acc = acc_ref[hh]
        outs.append(acc[:, :HEAD_DIM] / acc[:, HEAD_DIM:])
    o_ref[...] = jnp.concatenate(outs, axis=1).astype(o_ref.dtype)


def _moba(proj, mk_aug_t, mv_aug, kmean, batch, seq):
    tq = min(1024, seq)
    tk = 512
    nq = seq // tq
    nbm = seq // MOBA_BLOCK
    n_top = min(MOBA_TOPK, nbm)
    qb = COL_MQ // 128
    aug = HEAD_DIM + nbm
    return pl.pallas_call(
        functools.partial(_moba_kernel, tq=tq, tk=tk, n_top=n_top, rc=64),
        out_shape=jax.ShapeDtypeStruct((batch * seq, MOBA_W), CDT),
        grid=(batch, MOBA_HEADS // 2, nq),
        in_specs=[pl.BlockSpec((tq, 128), lambda b, p, i: (b * nq + i, qb + p)),
                  pl.BlockSpec((1, 2, aug, seq), lambda b, p, i: (b, p, 0, 0)),
                  pl.BlockSpec((1, 2, seq, 128), lambda b, p, i: (b, p, 0, 0)),
                  pl.BlockSpec((1, nbm, 128), lambda b, p, i: (b, 0, p))],
        out_specs=pl.BlockSpec((tq, 128), lambda b, p, i: (b * nq + i, p)),
        scratch_shapes=[pltpu.VMEM((2, tq, aug), CDT), pltpu.VMEM((2, tq, tk), jnp.float32),
                        pltpu.VMEM((2, tq, tk), CDT), pltpu.VMEM((2, tq, 128), jnp.float32),
                        pltpu.VMEM((2, tq, 128), jnp.float32)],
        compiler_params=_params(("arbitrary", "arbitrary", "arbitrary")),
        name="moba",
    )(proj, mk_aug_t, mv_aug, kmean)


def _merge_kernel(oc_ref, os_ref, ow_ref, om_ref, ng_ref, gn_ref, gm_ref, x_ref, ga_ref,
                  ex_ref, wun_ref, wum_ref, wo_ref, o_ref):
    gates = _sigmoid(ng_ref[...].astype(jnp.float32))
    hi, lo = _split_hi_lo(gates)
    e = _dot(hi, ex_ref[...]) + _dot(lo, ex_ref[...])
    o_nsa = (e[:, :NSA_Q] * oc_ref[...].astype(jnp.float32)
             + e[:, NSA_Q:2 * NSA_Q] * os_ref[...].astype(jnp.float32)
             + e[:, 2 * NSA_Q:] * ow_ref[...].astype(jnp.float32))
    y = (_sigmoid(gn_ref[...].astype(jnp.float32)) * _dot(o_nsa.astype(CDT), wun_ref[...])
         + _sigmoid(gm_ref[...].astype(jnp.float32)) * _dot(om_ref[...], wum_ref[...]))
    o_ref[...] = x_ref[...] + ga_ref[0] * _dot(y.astype(CDT), wo_ref[...])


def _merge(o_c, o_s, o_w, o_m, proj, x2d, ga, w_up_nsa, w_up_moba, w_out, seq):
    n, d = x2d.shape
    tm = min(512, seq)
    per_seq = seq // tm
    ng_w = COL_GN - COL_NG
    ex = np.zeros((ng_w, 3 * NSA_Q), np.float32)
    for h in range(NSA_HEADS):
        for j in range(3):
            ex[h * 3 + j, j * NSA_Q + h * HEAD_DIM: j * NSA_Q + (h + 1) * HEAD_DIM] = 1.0
    row = lambda i: (i, 0)
    const = lambda i: (0, 0)
    return pl.pallas_call(
        _merge_kernel,
        out_shape=jax.ShapeDtypeStruct((n, d), jnp.float32),
        grid=(n // tm,),
        in_specs=[pl.BlockSpec((tm, NSA_Q), row), pl.BlockSpec((tm, NSA_Q), row),
                  pl.BlockSpec((tm, NSA_Q), row), pl.BlockSpec((tm, MOBA_W), row),
                  pl.BlockSpec((tm, ng_w), lambda i: (i, COL_NG // ng_w)),
                  pl.BlockSpec((tm, d), lambda i: (i, COL_GN // d)),
                  pl.BlockSpec((tm, d), lambda i: (i, COL_GM // d)),
                  pl.BlockSpec((tm, d), row),
                  pl.BlockSpec((1, 1, d), lambda i: (i // per_seq, 0, 0)),
                  pl.BlockSpec((ng_w, 3 * NSA_Q), const),
                  pl.BlockSpec((NSA_Q, d), const), pl.BlockSpec((MOBA_W, d), const),
                  pl.BlockSpec((d, d), const)],
        out_specs=pl.BlockSpec((tm, d), row),
        compiler_params=_params(("arbitrary",)),
        name="mixer_merge",
    )(o_c, o_s, o_w, o_m, proj, proj, proj, x2d, ga, jnp.asarray(ex, CDT),
      w_up_nsa.astype(CDT), w_up_moba.astype(CDT), w_out.astype(CDT))


def _extract_max(x, ridx):
    m = jnp.max(x, axis=0, keepdims=True)
    idx = jnp.min(jnp.where(x == m, ridx, float(x.shape[0])), axis=0, keepdims=True)
    return m, ridx == idx


def _peer_score_kernel(q_ref, k1_ref, k2_ref, cnt_ref, rk_ref, e1_ref, e2_ref):
    half = PEER_QDIM // 2
    k = PEER_TOPK
    q = q_ref[...]
    s1_all = _dot_nt(k1_ref[...], q[:, :half])
    s2_all = _dot_nt(k2_ref[...], q[:, half:])
    t = 128
    ridx = lax.broadcasted_iota(jnp.int32, (PEER_NKEYS, t), 0).astype(jnp.float32)
    unranked = jnp.full((PEER_NKEYS, t), float(k), jnp.float32)
    chunks = list(range(0, s1_all.shape[1], t))
    shifted = []
    for c0 in chunks:
        for s_all in (s1_all, s2_all):
            s = s_all[:, c0:c0 + t]
            shifted.append(s - jnp.max(s, axis=0, keepdims=True))
    xs = list(shifted)
    rks = [unranked] * len(xs)
    tops = [[] for _ in xs]
    for i in range(k):
        for n in range(len(xs)):
            m, pick = _extract_max(xs[n], ridx)
            xs[n] = jnp.where(pick, LOWEST, xs[n])
            rks[n] = jnp.where(pick, float(i), rks[n])
            tops[n].append(m)

    v2_alls, cands = [], []
    for c in range(len(chunks)):
        v1, v2_all = tops[2 * c], jnp.concatenate(tops[2 * c + 1], axis=0)
        pieces = [v1[i] + v2_all[0:k // (i + 1)] for i in range(k)]
        pad = -sum(p.shape[0] for p in pieces) % 8
        cands.append(jnp.concatenate(pieces + [jnp.full((pad, t), LOWEST, jnp.float32)], axis=0))
        v2_alls.append(v2_all)
    cidx = lax.broadcasted_iota(jnp.int32, cands[0].shape, 0).astype(jnp.float32)
    vals = [[] for _ in chunks]
    for i in range(k):
        for c in range(len(chunks)):
            m, pick = _extract_max(cands[c], cidx)
            cands[c] = jnp.where(pick, LOWEST, cands[c])
            vals[c].append(m)

    for c, c0 in enumerate(chunks):
        v1, v2_all, val = tops[2 * c], v2_alls[c], vals[c]
        tau = val[k - 1]
        z = val[0] - val[0] + 1.0
        for i in range(1, k):
            z = z + jnp.exp(val[i] - val[0])
        cnt = jnp.zeros((PEER_NKEYS, t), jnp.float32)
        for i in range(k):
            n_i = jnp.sum(jnp.where(v1[i] + v2_all >= tau, 1.0, 0.0), axis=0, keepdims=True)
            cnt = jnp.where(rks[2 * c] == float(i), n_i, cnt)
        cnt_ref[0, :, c0:c0 + t] = cnt
        rk_ref[0, :, c0:c0 + t] = rks[2 * c + 1].astype(rk_ref.dtype)
        e1_ref[0, :, c0:c0 + t] = jnp.exp(shifted[2 * c] - val[0]) / z
        e2_ref[0, :, c0:c0 + t] = jnp.exp(shifted[2 * c + 1]).astype(e2_ref.dtype)


def _peer_scores(qp, k1, k2):
    n = qp.shape[0]
    tt = 512
    f32 = jax.ShapeDtypeStruct((PEER_HEADS, PEER_NKEYS, n), jnp.float32)
    cdt = jax.ShapeDtypeStruct((PEER_HEADS, PEER_NKEYS, n), CDT)
    big = pl.BlockSpec((1, PEER_NKEYS, tt), lambda i, h: (h, 0, i))
    return pl.pallas_call(
        _peer_score_kernel,
        out_shape=[f32, cdt, f32, cdt],
        grid=(n // tt, PEER_HEADS),
        in_specs=[pl.BlockSpec((tt, PEER_QDIM), lambda i, h: (i, h)),
                  pl.BlockSpec((PEER_NKEYS, PEER_QDIM // 2), lambda i, h: (0, 0)),
                  pl.BlockSpec((PEER_NKEYS, PEER_QDIM // 2), lambda i, h: (0, 0))],
        out_specs=[big, big, big, big],
        compiler_params=_params(("arbitrary", "arbitrary")),
        name="peer_scores",
    )(qp, k1.astype(CDT), k2.astype(CDT))


def _row_to_rows(row, n):
    tile_rows = 16
    tile = jnp.broadcast_to(row, (tile_rows, row.shape[1])).astype(CDT)
    return jnp.concatenate([tile] * (n // tile_rows), axis=0)


def _peer_expert_kernel(h_ref, u_ref, vt_ref, cnt_ref, rk_ref, e1_ref, e2_ref, x_ref, ga_ref,
                        o_ref, acc_ref, *, eb):
    j = pl.program_id(1)
    tt = h_ref.shape[0]

    @pl.when(j == 0)
    def _():
        acc_ref[...] = jnp.zeros(acc_ref.shape, jnp.float32)

    parts = []
    piece = 2 * PEER_NKEYS
    for e0 in range(0, eb, piece):
        sc = _dot_nt(u_ref[e0:e0 + piece, :], h_ref[...])
        ws = []
        for al in range(e0 // PEER_NKEYS, (e0 + piece) // PEER_NKEYS):
            a = j * (eb // PEER_NKEYS) + al
            w = None
            for hd in range(PEER_HEADS):
                cnt_a = _row_to_rows(cnt_ref[hd, pl.ds(a, 1), :], PEER_NKEYS)
                e1_a = _row_to_rows(e1_ref[hd, pl.ds(a, 1), :], PEER_NKEYS)
                contrib = jnp.where(rk_ref[hd] < cnt_a, e2_ref[hd] * e1_a, jnp.zeros((), CDT))
                w = contrib if w is None else w + contrib
            ws.append(w)
        parts.append(jnp.concatenate(ws, axis=0) * _gelu(sc.astype(CDT)))
    pw = jnp.concatenate(parts, axis=0)
    acc_ref[...] += _dot(vt_ref[...], pw)

    @pl.when(j == pl.num_programs(1) - 1)
    def _():
        o_ref[...] = x_ref[...] + ga_ref[0] * acc_ref[...].T


def _peer_experts(h2, u, v_t, cnt, rk2, e1, e2, x2d, ga, seq):
    n, d = x2d.shape
    tt = min(512, seq)
    per_seq = seq // tt
    eb = 2048
    n_blk = u.shape[0] // eb
    big = pl.BlockSpec((PEER_HEADS, PEER_NKEYS, tt), lambda i, j: (0, 0, i))
    return pl.pallas_call(
        functools.partial(_peer_expert_kernel, eb=eb),
        out_shape=jax.ShapeDtypeStruct((n, d), jnp.float32),
        grid=(n // tt, n_blk),
        in_specs=[pl.BlockSpec((tt, d), lambda i, j: (i, 0)),
                  pl.BlockSpec((eb, d), lambda i, j: (j, 0)),
                  pl.BlockSpec((d, eb), lambda i, j: (0, j)),
                  big, big, big, big,
                  pl.BlockSpec((tt, d), lambda i, j: (i, 0)),
                  pl.BlockSpec((1, 1, d), lambda i, j: (i // per_seq, 0, 0))],
        out_specs=pl.BlockSpec((tt, d), lambda i, j: (i, 0)),
        scratch_shapes=[pltpu.VMEM((d, tt), jnp.float32)],
        compiler_params=_params(("arbitrary", "arbitrary")),
        name="peer_experts",
    )(h2, u, v_t, cnt, rk2, e1, e2, x2d, ga)


def _rms_kernel(x_ref, g_ref, o_ref):
    x = x_ref[...]
    ms = jnp.mean(x * x, axis=-1, keepdims=True)
    o_ref[...] = x * lax.rsqrt(ms + RMS_EPS) * g_ref[...]


def _final_norm(x2d, g):
    n, d = x2d.shape
    tm = 512
    return pl.pallas_call(
        _rms_kernel,
        out_shape=jax.ShapeDtypeStruct((n, d), jnp.float32),
        grid=(n // tm,),
        in_specs=[pl.BlockSpec((tm, d), lambda i: (i, 0)), pl.BlockSpec((1, d), lambda i: (0, 0))],
        out_specs=pl.BlockSpec((tm, d), lambda i: (i, 0)),
        compiler_params=_params(("arbitrary",)),
        name="final_rmsnorm",
    )(x2d, g.reshape(1, d))


def _kv_layout_kernel(kk_ref, mk0_ref, mk1_ref, vv_ref, mv0_ref, mv1_ref,
                      ks_ref, kw_ref, mk_ref, vs_ref, mv_ref, *, ts, sel_per_tile):
    i = pl.program_id(1)
    pos = i * ts + lax.broadcasted_iota(jnp.int32, (1, ts), 1)
    nbm = mk_ref.shape[2] - HEAD_DIM
    sel_rows = ks_ref.shape[2] - HEAD_DIM
    sel_onehot = (lax.broadcasted_iota(jnp.int32, (sel_rows, 1), 0)
                  == (pos // SLC_LEN) % sel_per_tile).astype(CDT)
    moba_onehot = (lax.broadcasted_iota(jnp.int32, (nbm, 1), 0) == pos // MOBA_BLOCK).astype(CDT)
    ones = jnp.ones((ts, HEAD_DIM), CDT)

    kk_t = kk_ref[...].astype(jnp.float32).T
    for h in range(NSA_KV_HEADS):
        ks_ref[0, h] = jnp.concatenate([kk_t[h * HEAD_DIM:(h + 1) * HEAD_DIM].astype(CDT), sel_onehot], axis=0)
        kw_ref[0, h] = kk_t[(NSA_KV_HEADS + h) * HEAD_DIM:(NSA_KV_HEADS + h + 1) * HEAD_DIM].astype(CDT)
        vs_ref[0, h] = jnp.concatenate([vv_ref[:, h * HEAD_DIM:(h + 1) * HEAD_DIM], ones], axis=1)
    per_ref = mk0_ref.shape[1] // HEAD_DIM
    for half, (k_in, v_in) in enumerate(((mk0_ref, mv0_ref), (mk1_ref, mv1_ref))):
        k_t = k_in[...].astype(jnp.float32).T
        for hh in range(per_ref):
            h = half * per_ref + hh
            mk_ref[0, h] = jnp.concatenate([k_t[hh * HEAD_DIM:(hh + 1) * HEAD_DIM].astype(CDT), moba_onehot],
                                           axis=0)
            mv_ref[0, h] = jnp.concatenate([v_in[:, hh * HEAD_DIM:(hh + 1) * HEAD_DIM], ones], axis=1)


def _kv_layout(proj, batch, seq, sel_per_tile):
    ts = min(512, seq)
    nt = seq // ts
    nbm = seq // MOBA_BLOCK
    w = 4 * HEAD_DIM
    col = lambda c: (lambda b, i: (b * nt + i, c // w))
    head_major = lambda heads, rows: pl.BlockSpec((1, heads, rows, ts), lambda b, i: (b, 0, 0, i))
    token_major = lambda heads: pl.BlockSpec((1, heads, ts, 128), lambda b, i: (b, 0, i, 0))
    return pl.pallas_call(
        functools.partial(_kv_layout_kernel, ts=ts, sel_per_tile=sel_per_tile),
        out_shape=[jax.ShapeDtypeStruct((batch, NSA_KV_HEADS, 128, seq), CDT),
                   jax.ShapeDtypeStruct((batch, NSA_KV_HEADS, HEAD_DIM, seq), CDT),
                   jax.ShapeDtypeStruct((batch, MOBA_HEADS, HEAD_DIM + nbm, seq), CDT),
                   jax.ShapeDtypeStruct((batch, NSA_KV_HEADS, seq, 128), CDT),
                   jax.ShapeDtypeStruct((batch, MOBA_HEADS, seq, 128), CDT)],
        grid=(batch, nt),
        in_specs=[pl.BlockSpec((ts, w), col(COL_KS)),
                  pl.BlockSpec((ts, w), col(COL_MK)), pl.BlockSpec((ts, w), col(COL_MK + w)),
                  pl.BlockSpec((ts, w), col(COL_VS)),
                  pl.BlockSpec((ts, w), col(COL_MV)), pl.BlockSpec((ts, w), col(COL_MV + w))],
        out_specs=[head_major(NSA_KV_HEADS, 128), head_major(NSA_KV_HEADS, HEAD_DIM),
                   head_major(MOBA_HEADS, HEAD_DIM + nbm), token_major(NSA_KV_HEADS), token_major(MOBA_HEADS)],
        compiler_params=_params(("arbitrary", "arbitrary")),
        name="kv_layout",
    )(proj, proj, proj, proj, proj, proj)


def kernel(x, c, w_ada, b_ada, g_attn, g_ffn, w_in, cmp_pe, cmp_w1, cmp_w2, w_up_nsa, w_up_moba, w_out,
           peer_wq, peer_k1, peer_k2, peer_u, peer_v, g_final):
    batch, seq, d = x.shape
    depth = w_ada.shape[0]
    n = batch * seq
    nbm = seq // MOBA_BLOCK
    x2d = x.reshape(n, d)
    mod = _adaln_mod(c, w_ada, b_ada)
    rope = _rope_tables(seq)
    for l in range(depth):
        sh1, sc1, ga1, sh2, sc2, ga2 = [m.reshape(batch, 1, d) for m in jnp.split(mod[l], 6, axis=-1)]
        proj = _norm_mod_matmul(x2d, g_attn[l], sc1, sh1, _reorder_w_in(w_in[l]), seq,
                                rope=rope, n_rope=ROPE_COLS // PROJ_TN)
        cmp_kv = _nsa_compress(proj, cmp_pe[l], cmp_w1[l], cmp_w2[l], batch, seq)
        o_c, selb = _nsa_compressed(proj, cmp_kv, batch, seq)
        ks_aug, kw_t, mk_aug, vs_aug, mv_aug = _kv_layout(proj, batch, seq, NSA_SEL_TK // SLC_LEN)
        o_s = _nsa_selected(proj, ks_aug, vs_aug, selb, batch, seq)
        o_w = _nsa_window(proj, kw_t, batch, seq)
        o_m = _moba(proj, mk_aug, mv_aug, _moba_kmean(proj, batch, seq), batch, seq)
        x2d = _merge(o_c, o_s, o_w, o_m, proj, x2d, ga1, w_up_nsa[l], w_up_moba[l], w_out[l], seq)
        qp, h2 = _norm_mod_matmul(x2d, g_ffn[l], sc2, sh2, peer_wq[l].astype(CDT), seq, emit_h=True)
        cnt, rk2, e1, e2 = _peer_scores(qp, peer_k1[l], peer_k2[l])
        x2d = _peer_experts(h2, peer_u[l].astype(CDT), peer_v[l].T.astype(CDT), cnt, rk2, e1, e2,
                            x2d, ga2, seq)
    return _final_norm(x2d, g_final).reshape(batch, seq, d)
```

```python
import functools

import jax
import jax.numpy as jnp
import numpy as np
from jax import lax
from jax.experimental import pallas as pl
from jax.experimental.pallas import tpu as pltpu

D_MODEL = 1024
HEAD_DIM = 64
ROT_DIM = HEAD_DIM // 4
ROPE_THETA = 500000.0
NSA_HEADS = 8
NSA_KV_HEADS = 2
NSA_GROUP = NSA_HEADS // NSA_KV_HEADS
CMP_LEN = 32
CMP_STRIDE = 16
CMP_HID = 2 * HEAD_DIM
SLC_LEN = 64
SLC_TOPN = 16
WIN = 512
MOBA_HEADS = 8
MOBA_BLOCK = 256
MOBA_TOPK = 3
PEER_HEADS = 8
PEER_NKEYS = 128
PEER_QDIM = 256
PEER_TOPK = 16
RMS_EPS = 1e-6
NEG = -1e30
SEL_FORCE = 1e4
LOWEST = -3.0e38
LOG2E = 1.4426950408889634

NSA_Q = NSA_HEADS * HEAD_DIM
NSA_KV = NSA_KV_HEADS * HEAD_DIM
MOBA_W = MOBA_HEADS * HEAD_DIM

CDT = jnp.bfloat16
V7X_VMEM_LIMIT = 56 * 1024 * 1024

COL_QROT = 0
COL_KS = 512
COL_KW = 640
COL_MQ = 768
COL_MK = 1280
ROPE_COLS = 1792
COL_NG = 1792
COL_GN = 2048
COL_GM = 3072
COL_QRAW = 4096
COL_KC = 4608
COL_VC = 4736
COL_VS = 4864
COL_VW = 4992
COL_MV = 5120
PROJ_COLS = 5632
PROJ_TN = 256


def _params(sem):
    return pltpu.CompilerParams(dimension_semantics=sem, vmem_limit_bytes=V7X_VMEM_LIMIT)


def _dot(a, b):
    return jnp.dot(a, b, preferred_element_type=jnp.float32)


def _dot_nt(a, b):
    return lax.dot_general(a, b, (((1,), (1,)), ((), ())), preferred_element_type=jnp.float32)


def _split_hi_lo(x):
    hi = x.astype(CDT)
    lo = (x - hi.astype(jnp.float32)).astype(CDT)
    return hi, lo


def _gelu(x):
    return 0.5 * x * (1.0 + jnp.tanh(0.7978845608028654 * (x + 0.044715 * (x * x * x))))


def _sigmoid(x):
    return 1.0 / (1.0 + jnp.exp(-x))


def _mod_kernel(c_ref, w_ref, b_ref, o_ref):
    c = c_ref[...]
    sc = c * _sigmoid(c)
    o_ref[0] = jnp.dot(sc, w_ref[0], preferred_element_type=jnp.float32,
                       precision=lax.Precision.HIGHEST) + b_ref[0]


def _adaln_mod(c, w_ada, b_ada):
    depth, d, six_d = w_ada.shape
    b = c.shape[0]
    rows = 8
    c_pad = jnp.zeros((rows, d), jnp.float32).at[:b].set(c)
    tn = 1024
    out = pl.pallas_call(
        _mod_kernel,
        out_shape=jax.ShapeDtypeStruct((depth, rows, six_d), jnp.float32),
        grid=(depth, six_d // tn),
        in_specs=[pl.BlockSpec((rows, d), lambda l, j: (0, 0)),
                  pl.BlockSpec((1, d, tn), lambda l, j: (l, 0, j)),
                  pl.BlockSpec((1, 1, tn), lambda l, j: (l, 0, j))],
        out_specs=pl.BlockSpec((1, rows, tn), lambda l, j: (l, 0, j)),
        compiler_params=_params(("arbitrary", "arbitrary")),
        name="adaln_mod",
    )(c_pad, w_ada, b_ada.reshape(depth, 1, six_d))
    return out[:, :b]


def _nmm_kernel(*refs, n_rope, emit_h, tn):
    if n_rope:
        x_ref, g_ref, sc_ref, sh_ref, w_ref, cos_ref, sa_ref, sb_ref = refs[:8]
        rest = refs[8:]
    else:
        x_ref, g_ref, sc_ref, sh_ref, w_ref = refs[:5]
        rest = refs[5:]
    if emit_h:
        o_ref, ho_ref, h_ref = rest
    else:
        o_ref, h_ref = rest
    x = x_ref[...]
    ms = jnp.mean(x * x, axis=-1, keepdims=True)
    y = x * lax.rsqrt(ms + RMS_EPS) * g_ref[...]
    h = (y * (1.0 + sc_ref[0]) + sh_ref[0]).astype(h_ref.dtype)
    h_ref[...] = h
    if emit_h:
        ho_ref[...] = h

    for j in range(w_ref.shape[1] // tn):
        acc = _dot(h_ref[...], w_ref[:, j * tn:(j + 1) * tn])
        if j < n_rope:
            cos, sa, sb = cos_ref[...], sa_ref[...], sb_ref[...]
            for c0 in range(0, tn, 128):
                a = acc[:, c0:c0 + 128]
                r = (a * cos + pltpu.roll(a, 128 - ROT_DIM // 2, axis=1) * sa
                     + pltpu.roll(a, ROT_DIM // 2, axis=1) * sb)
                o_ref[:, j * tn + c0:j * tn + c0 + 128] = r.astype(o_ref.dtype)
        else:
            o_ref[:, j * tn:(j + 1) * tn] = acc.astype(o_ref.dtype)


def _norm_mod_matmul(x2d, g, sc, sh, w, seq, rope=None, n_rope=0, emit_h=False, tn=PROJ_TN):
    n, d = x2d.shape
    cols = w.shape[1]
    tm = min(512, seq)
    per_seq = seq // tm
    in_specs = [pl.BlockSpec((tm, d), lambda i: (i, 0)),
                pl.BlockSpec((1, d), lambda i: (0, 0)),
                pl.BlockSpec((1, 1, d), lambda i: (i // per_seq, 0, 0)),
                pl.BlockSpec((1, 1, d), lambda i: (i // per_seq, 0, 0)),
                pl.BlockSpec((d, cols), lambda i: (0, 0))]
    args = [x2d, g.reshape(1, d), sc, sh, w]
    if n_rope:
        in_specs += [pl.BlockSpec((tm, 128), lambda i: (i % per_seq, 0))] * 3
        args += list(rope)
    out_shape = [jax.ShapeDtypeStruct((n, cols), CDT)]
    out_specs = [pl.BlockSpec((tm, cols), lambda i: (i, 0))]
    if emit_h:
        out_shape.append(jax.ShapeDtypeStruct((n, d), CDT))
        out_specs.append(pl.BlockSpec((tm, d), lambda i: (i, 0)))
    res = pl.pallas_call(
        functools.partial(_nmm_kernel, n_rope=n_rope, emit_h=emit_h, tn=tn),
        out_shape=out_shape,
        grid=(n // tm,),
        in_specs=in_specs,
        out_specs=out_specs,
        scratch_shapes=[pltpu.VMEM((tm, d), CDT)],
        compiler_params=_params(("arbitrary",)),
        name="norm_mod_matmul",
    )(*args)
    return res if emit_h else res[0]


def _rope_tables(seq):
    half = ROT_DIM // 2
    inv = ROPE_THETA ** (-jnp.arange(half, dtype=jnp.float32) / half)
    ang = jnp.arange(seq, dtype=jnp.float32)[:, None] * inv[None, :]
    cos, sin = jnp.cos(ang), jnp.sin(ang)
    ones = jnp.ones((seq, HEAD_DIM - ROT_DIM), jnp.float32)
    zeros = jnp.zeros((seq, HEAD_DIM - ROT_DIM), jnp.float32)
    zh = jnp.zeros((seq, half), jnp.float32)
    c64 = jnp.concatenate([cos, cos, ones], axis=1)
    sa64 = jnp.concatenate([-sin, zh, zeros], axis=1)
    sb64 = jnp.concatenate([zh, sin, zeros], axis=1)
    return tuple(jnp.concatenate([t, t], axis=1) for t in (c64, sa64, sb64))


def _reorder_w_in(w_in):
    o = 0
    pieces = {}
    for name, size in (("nq", NSA_Q), ("kc", NSA_KV), ("vc", NSA_KV), ("ks", NSA_KV), ("vs", NSA_KV),
                       ("kw", NSA_KV), ("vw", NSA_KV), ("ng", 3 * NSA_HEADS), ("mq", MOBA_W),
                       ("mk", MOBA_W), ("mv", MOBA_W), ("gn", D_MODEL), ("gm", D_MODEL)):
        pieces[name] = w_in[:, o:o + size]
        o += size
    scale = HEAD_DIM ** -0.5 * LOG2E
    pad = jnp.zeros((w_in.shape[0], COL_GN - COL_NG - 3 * NSA_HEADS), w_in.dtype)
    w = jnp.concatenate([pieces["nq"] * scale, pieces["ks"], pieces["kw"], pieces["mq"] * scale,
                         pieces["mk"], pieces["ng"], pad, pieces["gn"], pieces["gm"],
                         pieces["nq"] * scale, pieces["kc"], pieces["vc"], pieces["vs"],
                         pieces["vw"], pieces["mv"]], axis=1)
    assert w.shape[1] == PROJ_COLS
    return w.astype(CDT)


def _compress_kernel(x_ref, w1c_ref, pe_ref, w1_ref, w2_ref, o_ref):
    nb = x_ref.shape[3]
    ab = _dot(x_ref[0, 0, 0], w1c_ref[0])
    a = ab[:, :CMP_HID]
    b_next = pltpu.roll(ab[:, CMP_HID:], nb - 1, axis=0)
    bias = _dot(pe_ref[0], w1_ref[0])[0:1]
    hid = _gelu(a + b_next + bias)
    o_ref[0, 0, 0] = _dot(hid.astype(CDT), w2_ref[0]).astype(o_ref.dtype)


def _nsa_compress(proj, cmp_pe, cmp_w1, cmp_w2, batch, seq):
    nb = seq // CMP_STRIDE
    half = CMP_STRIDE * HEAD_DIM
    kcvc = proj[:, COL_KC:COL_KC + 2 * NSA_KV].reshape(batch, nb, CMP_STRIDE, 2, NSA_KV_HEADS, HEAD_DIM)
    x = jnp.transpose(kcvc, (3, 0, 4, 1, 2, 5)).reshape(2, batch, NSA_KV_HEADS, nb, half)
    w1c = jnp.concatenate([cmp_w1[:, :half], cmp_w1[:, half:]], axis=2).astype(CDT)
    pe = jnp.broadcast_to(cmp_pe.reshape(2, 1, CMP_LEN * HEAD_DIM), (2, 8, CMP_LEN * HEAD_DIM)).astype(CDT)
    return pl.pallas_call(
        _compress_kernel,
        out_shape=jax.ShapeDtypeStruct((2, batch, NSA_KV_HEADS, nb, HEAD_DIM), CDT),
        grid=(2, batch, NSA_KV_HEADS),
        in_specs=[pl.BlockSpec((1, 1, 1, nb, half), lambda w, b, k: (w, b, k, 0, 0)),
                  pl.BlockSpec((1, half, 2 * CMP_HID), lambda w, b, k: (w, 0, 0)),
                  pl.BlockSpec((1, 8, 2 * half), lambda w, b, k: (w, 0, 0)),
                  pl.BlockSpec((1, 2 * half, CMP_HID), lambda w, b, k: (w, 0, 0)),
                  pl.BlockSpec((1, CMP_HID, HEAD_DIM), lambda w, b, k: (w, 0, 0))],
        out_specs=pl.BlockSpec((1, 1, 1, nb, HEAD_DIM), lambda w, b, k: (w, b, k, 0, 0)),
        compiler_params=_params(("arbitrary", "arbitrary", "arbitrary")),
        name="nsa_compress",
    )(x, w1c, pe, cmp_w1.astype(CDT), cmp_w2.astype(CDT))


def _stack_heads(qblk):
    return jnp.concatenate([qblk[:, g * HEAD_DIM:(g + 1) * HEAD_DIM] for g in range(NSA_GROUP)], axis=0)


def _unstack_heads(o, tq):
    return jnp.concatenate([o[g * tq:(g + 1) * tq] for g in range(NSA_GROUP)], axis=1)


def _extract_max(x, ridx):
    m = jnp.max(x, axis=0, keepdims=True)
    idx = jnp.min(jnp.where(x == m, ridx, float(x.shape[0])), axis=0, keepdims=True)
    return m, ridx == idx


def _nsa_cmp_kernel(q_ref, kc_ref, vc_ref, ov_ref, oc_ref, sb_ref, imp_ref, *, tq, n_sel, n_var):
    i = pl.program_id(2)
    s0 = i * tq
    nb = kc_ref.shape[3]
    ns = ov_ref.shape[1]
    tq_col = s0 + lax.broadcasted_iota(jnp.int32, (tq, 1), 0)
    t4 = jnp.concatenate([tq_col] * NSA_GROUP, axis=0)

    def attend(nk):
        q4 = _stack_heads(q_ref[...])
        s = _dot_nt(q4, kc_ref[0, 0, 0, :nk, :])
        cend = lax.broadcasted_iota(jnp.int32, (1, nk), 1) * CMP_STRIDE + (CMP_LEN - 1)
        mask = cend <= t4
        s = jnp.where(mask, s, NEG)
        m = jnp.max(s, axis=-1, keepdims=True)
        p = jnp.where(mask, jnp.exp2(s - m), 0.0)
        p = p / jnp.maximum(jnp.sum(p, axis=-1, keepdims=True), 1e-30)
        o = _dot(p.astype(CDT), vc_ref[0, 0, 0, :nk, :])
        oc_ref[...] = _unstack_heads(o, tq).astype(oc_ref.dtype)
        psum = p[0:tq]
        for g in range(1, NSA_GROUP):
            psum = psum + p[g * tq:(g + 1) * tq]
        hi, lo = _split_hi_lo(psum)
        imp_ref[...] = _dot(hi, ov_ref[:nk, :]) + _dot(lo, ov_ref[:nk, :])

    tiles_per_var = pl.num_programs(2) // n_var
    for v in range(n_var):
        @pl.when(i // tiles_per_var == v)
        def _(v=v):
            attend(nb * (v + 1) // n_var)

    blk = lax.broadcasted_iota(jnp.int32, (ns, 1), 0)
    lanes = min(128, tq)
    ridx = lax.broadcasted_iota(jnp.int32, (ns, lanes), 0).astype(jnp.float32)
    imp_t = imp_ref[...].T
    scores, valids = [], []
    for c0 in range(0, tq, lanes):
        cur = (s0 + c0 + lax.broadcasted_iota(jnp.int32, (1, lanes), 1)) // SLC_LEN
        valid = blk <= cur
        forced = (blk == 0) | (blk == cur) | (blk == cur - 1)
        scores.append(jnp.where(valid, jnp.where(forced, SEL_FORCE, imp_t[:, c0:c0 + lanes]), NEG))
        valids.append(valid)
    sels = [jnp.zeros((ns, lanes), jnp.float32)] * len(scores)
    for _ in range(n_sel):
        for c in range(len(scores)):
            _, pick = _extract_max(scores[c], ridx)
            scores[c] = jnp.where(pick, LOWEST, scores[c])
            sels[c] = jnp.where(pick, 1.0, sels[c])
    bias = jnp.concatenate([jnp.where((sel > 0.5) & valid, 0.0, NEG) for sel, valid in zip(sels, valids)],
                           axis=1).T
    extra = sb_ref.shape[3] - ns
    if extra:
        bias = jnp.concatenate([bias, jnp.full((tq, extra), NEG, jnp.float32)], axis=1)
    sb_ref[0, 0] = bias.astype(sb_ref.dtype)


def _nsa_compressed(proj, cmp_kv, batch, seq):
    tq = 256
    nq = seq // tq
    nb = seq // CMP_STRIDE
    ns = seq // SLC_LEN
    n_sel = min(SLC_TOPN, ns)
    c_start = np.arange(nb) * CMP_STRIDE
    s_start = np.arange(ns) * SLC_LEN
    ov = np.maximum(np.minimum(c_start[:, None] + CMP_LEN, s_start[None, :] + SLC_LEN)
                    - np.maximum(c_start[:, None], s_start[None, :]), 0).astype(np.float32) / CMP_LEN
    ov[nb - 1] = 0.0
    qb = COL_QRAW // (NSA_GROUP * HEAD_DIM)
    return pl.pallas_call(
        functools.partial(_nsa_cmp_kernel, tq=tq, n_sel=n_sel,
                          n_var=4 if (nb % (4 * 128) == 0 and nq % 4 == 0) else 1),
        out_shape=[jax.ShapeDtypeStruct((batch * seq, NSA_Q), CDT),
                   jax.ShapeDtypeStruct((batch, NSA_KV_HEADS, seq, max(ns, 128)), CDT)],
        grid=(batch, NSA_KV_HEADS, nq),
        in_specs=[pl.BlockSpec((tq, NSA_GROUP * HEAD_DIM), lambda b, k, i: (b * nq + i, qb + k)),
                  pl.BlockSpec((1, 1, 1, nb, HEAD_DIM), lambda b, k, i: (0, b, k, 0, 0)),
                  pl.BlockSpec((1, 1, 1, nb, HEAD_DIM), lambda b, k, i: (1, b, k, 0, 0)),
                  pl.BlockSpec((nb, ns), lambda b, k, i: (0, 0))],
        out_specs=[pl.BlockSpec((tq, NSA_GROUP * HEAD_DIM), lambda b, k, i: (b * nq + i, k)),
                   pl.BlockSpec((1, 1, tq, max(ns, 128)), lambda b, k, i: (b, k, i, 0))],
        scratch_shapes=[pltpu.VMEM((tq, ns), jnp.float32)],
        compiler_params=_params(("arbitrary", "arbitrary", "arbitrary")),
        name="nsa_compressed",
    )(proj, cmp_kv, cmp_kv, jnp.asarray(ov, CDT))


def _flash_tile(s_ref, p_ref, m_ref, acc_ref, rc, tq, causal=None):
    rows, tk = s_ref.shape

    def tile(r0, c0):
        s = s_ref[r0:r0 + rc, c0:c0 + 128]
        if causal is not None:
            kpos, t0 = causal
            t = t0 + (r0 % tq) + lax.broadcasted_iota(jnp.int32, (rc, 1), 0)
            s = jnp.where(kpos[:, c0:c0 + 128] <= t, s, NEG)
        return s

    for r0 in range(0, rows, rc):
        mx = tile(r0, 0)
        for c0 in range(128, tk, 128):
            mx = jnp.maximum(mx, tile(r0, c0))
        m_prev = m_ref[r0:r0 + rc, :]
        m_new = jnp.maximum(m_prev, jnp.max(mx, axis=-1, keepdims=True))
        acc_ref[r0:r0 + rc, :] = acc_ref[r0:r0 + rc, :] * jnp.exp2(m_prev - m_new)
        m_ref[r0:r0 + rc, :] = m_new
    for r0 in range(0, rows, rc):
        m_new = m_ref[r0:r0 + rc, :]
        for c0 in range(0, tk, 128):
            p_ref[r0:r0 + rc, c0:c0 + 128] = jnp.exp2(tile(r0, c0) - m_new).astype(p_ref.dtype)


def _nsa_sel_kernel(q_ref, kt_ref, va_ref, sb_ref, o_ref,
                    qp_ref, q4_ref, sbf_ref, s_ref, p_ref, m_ref, acc_ref, *, tq, tk, rc):
    i = pl.program_id(2)
    s0 = i * tq
    nsp = sb_ref.shape[3]
    per_tile = tk // SLC_LEN
    q4 = _stack_heads(q_ref[...])
    qp_ref[...] = jnp.concatenate([q4, jnp.zeros_like(q4)], axis=1)
    sbf_ref[...] = sb_ref[0, 0].astype(jnp.float32)
    lane = lax.broadcasted_iota(jnp.int32, (tq, 128), 1)
    bias_lanes = (lane >= HEAD_DIM) & (lane < HEAD_DIM + per_tile)
    m_ref[...] = jnp.full(m_ref.shape, NEG, jnp.float32)
    acc_ref[...] = jnp.zeros(acc_ref.shape, jnp.float32)

    def step(kt, masked):
        start = pl.multiple_of(kt * tk, tk)
        shift = (HEAD_DIM - kt * per_tile + nsp) % nsp
        rolled = pltpu.roll(sbf_ref[...], shift, axis=1)[:, :128].astype(q4_ref.dtype)
        for g in range(NSA_GROUP):
            q4_ref[g * tq:(g + 1) * tq, :] = jnp.where(bias_lanes, rolled, qp_ref[g * tq:(g + 1) * tq, :])
        s_ref[...] = _dot(q4_ref[...], kt_ref[0, 0, :, pl.ds(start, tk)])
        causal = (start + lax.broadcasted_iota(jnp.int32, (1, tk), 1), s0) if masked else None
        _flash_tile(s_ref, p_ref, m_ref, acc_ref, rc, tq, causal=causal)
        acc_ref[...] += _dot(p_ref[...], va_ref[0, 0, pl.ds(start, tk), :])

    unroll = 8

    def body(kp, carry):
        for d in range(unroll):
            step(unroll * kp + d, False)
        return carry

    n_full = s0 // tk
    n_trips = n_full // unroll
    lax.fori_loop(0, n_trips, body, 0)
    done = n_trips * unroll
    size = unroll // 2
    while size >= 1:
        take = ((n_full - done) // size) % 2 if size > 1 else (n_full - done) % 2

        @pl.when(take == 1)
        def _(size=size, first=done + ((n_full - done) // (2 * size)) * (2 * size)):
            for d in range(size):
                step(first + d, False)

        size //= 2
    step(n_full, True)
    acc = acc_ref[...]
    o = acc[:, :HEAD_DIM] / acc[:, HEAD_DIM:]
    o_ref[...] = _unstack_heads(o, tq).astype(o_ref.dtype)


NSA_SEL_TK = 512


def _nsa_selected(proj, k_aug, vs_aug, selb, batch, seq):
    tq = 256
    tk = NSA_SEL_TK
    nq = seq // tq
    nsp = selb.shape[3]
    qb = COL_QROT // (NSA_GROUP * HEAD_DIM)
    rows = NSA_GROUP * tq
    return pl.pallas_call(
        functools.partial(_nsa_sel_kernel, tq=tq, tk=tk, rc=64),
        out_shape=jax.ShapeDtypeStruct((batch * seq, NSA_Q), CDT),
        grid=(batch, NSA_KV_HEADS, nq),
        in_specs=[pl.BlockSpec((tq, NSA_GROUP * HEAD_DIM), lambda b, k, i: (b * nq + i, qb + k)),
                  pl.BlockSpec((1, 1, 128, seq), lambda b, k, i: (b, k, 0, 0)),
                  pl.BlockSpec((1, 1, seq, 128), lambda b, k, i: (b, k, 0, 0)),
                  pl.BlockSpec((1, 1, tq, nsp), lambda b, k, i: (b, k, i, 0))],
        out_specs=pl.BlockSpec((tq, NSA_GROUP * HEAD_DIM), lambda b, k, i: (b * nq + i, k)),
        scratch_shapes=[pltpu.VMEM((rows, 128), CDT), pltpu.VMEM((rows, 128), CDT),
                        pltpu.VMEM((tq, nsp), jnp.float32), pltpu.VMEM((rows, tk), jnp.float32),
                        pltpu.VMEM((rows, tk), CDT),
                        pltpu.VMEM((rows, 128), jnp.float32), pltpu.VMEM((rows, 128), jnp.float32)],
        compiler_params=_params(("arbitrary", "arbitrary", "arbitrary")),
        name="nsa_selected",
    )(proj, k_aug, vs_aug, selb)


def _nsa_win_kernel(q_ref, kt_ref, va_ref, o_ref, q4_ref, s_ref, p_ref, *, tq, rc):
    i = pl.program_id(2)
    s0 = i * tq
    span = WIN + tq
    rows = NSA_GROUP * tq
    start = pl.multiple_of(jnp.maximum(s0 - WIN, 0), tq)
    q4_ref[...] = _stack_heads(q_ref[...])
    s_ref[...] = _dot(q4_ref[...], kt_ref[0, 0, :, pl.ds(start, span)])
    kpos = start + lax.broadcasted_iota(jnp.int32, (1, span), 1)
    for r0 in range(0, rows, rc):
        t = s0 + (r0 % tq) + lax.broadcasted_iota(jnp.int32, (rc, 1), 0)
        tiles = []
        for c0 in range(0, span, 128):
            kp = kpos[:, c0:c0 + 128]
            tiles.append(jnp.where((kp <= t) & (kp > t - WIN), s_ref[r0:r0 + rc, c0:c0 + 128], NEG))
        mx = tiles[0]
        for tile in tiles[1:]:
            mx = jnp.maximum(mx, tile)
        m = jnp.max(mx, axis=-1, keepdims=True)
        for c, tile in enumerate(tiles):
            p_ref[r0:r0 + rc, c * 128:(c + 1) * 128] = jnp.exp2(tile - m).astype(p_ref.dtype)
    acc = _dot(p_ref[...], va_ref[0, 0, pl.ds(start, span), :])
    o = acc[:, :HEAD_DIM] / acc[:, HEAD_DIM:]
    o_ref[...] = _unstack_heads(o, tq).astype(o_ref.dtype)


def _nsa_window(proj, kw_t, vw_aug, batch, seq):
    tq = 256
    nq = seq // tq
    qb = COL_QROT // (NSA_GROUP * HEAD_DIM)
    rows = NSA_GROUP * tq
    span = WIN + tq
    return pl.pallas_call(
        functools.partial(_nsa_win_kernel, tq=tq, rc=32),
        out_shape=jax.ShapeDtypeStruct((batch * seq, NSA_Q), CDT),
        grid=(batch, NSA_KV_HEADS, nq),
        in_specs=[pl.BlockSpec((tq, NSA_GROUP * HEAD_DIM), lambda b, k, i: (b * nq + i, qb + k)),
                  pl.BlockSpec((1, 1, HEAD_DIM, seq), lambda b, k, i: (b, k, 0, 0)),
                  pl.BlockSpec((1, 1, seq, 128), lambda b, k, i: (b, k, 0, 0))],
        out_specs=pl.BlockSpec((tq, NSA_GROUP * HEAD_DIM), lambda b, k, i: (b * nq + i, k)),
        scratch_shapes=[pltpu.VMEM((rows, HEAD_DIM), CDT), pltpu.VMEM((rows, span), jnp.float32),
                        pltpu.VMEM((rows, span), CDT)],
        compiler_params=_params(("arbitrary", "arbitrary", "arbitrary")),
        name="nsa_window",
    )(proj, kw_t, vw_aug)


def _moba_mean_kernel(k_ref, o_ref):
    seq, w = k_ref.shape
    nbm = seq // MOBA_BLOCK
    k = k_ref[...].astype(jnp.float32).reshape(nbm, MOBA_BLOCK, w)
    o_ref[0] = jnp.sum(k, axis=1) * (1.0 / MOBA_BLOCK)


def _moba_kmean(proj, batch, seq):
    nbm = seq // MOBA_BLOCK
    kb = COL_MK // 128
    return pl.pallas_call(
        _moba_mean_kernel,
        out_shape=jax.ShapeDtypeStruct((batch, nbm, MOBA_W), jnp.float32),
        grid=(batch, MOBA_W // 128),
        in_specs=[pl.BlockSpec((seq, 128), lambda b, j: (b, kb + j))],
        out_specs=pl.BlockSpec((1, nbm, 128), lambda b, j: (b, 0, j)),
        compiler_params=_params(("arbitrary", "arbitrary")),
        name="moba_kmean",
    )(proj)


def _moba_kernel(q_ref, ka_ref, va_ref, km_ref, o_ref, qa_ref, s_ref, p_ref, m_ref, acc_ref,
                 *, tq, tk, n_top, rc):
    i = pl.program_id(2)
    s0 = i * tq
    nbm = km_ref.shape[1]
    blk = lax.broadcasted_iota(jnp.int32, (nbm, 1), 0)
    cur = (s0 + lax.broadcasted_iota(jnp.int32, (1, tq), 1)) // MOBA_BLOCK
    ridx = lax.broadcasted_iota(jnp.int32, (nbm, tq), 0).astype(jnp.float32)
    for hh in range(2):
        q = q_ref[:, hh * HEAD_DIM:(hh + 1) * HEAD_DIM]
        km_hi, km_lo = _split_hi_lo(km_ref[0][:, hh * HEAD_DIM:(hh + 1) * HEAD_DIM])
        gs = _dot_nt(km_hi, q) + _dot_nt(km_lo, q)
        gs = jnp.where(blk < cur, gs, NEG)

        def body(_, carry):
            sc, sel = carry
            _, pick = _extract_max(sc, ridx)
            return jnp.where(pick, LOWEST, sc), jnp.where(pick, 1.0, sel)

        _, sel = lax.fori_loop(0, n_top, body, (gs, jnp.zeros((nbm, tq), jnp.float32)))
        open_blk = ((sel > 0.5) & (gs > NEG * 0.5)) | (blk == cur)
        bias_t = jnp.where(open_blk, 0.0, NEG)
        lead = -nbm % 128
        if lead:
            bias_t = jnp.concatenate([jnp.zeros((lead, tq), jnp.float32), bias_t], axis=0)
        bias = bias_t.T[:, lead:].astype(CDT)
        qa_ref[hh] = jnp.concatenate([q, bias], axis=1)
    m_ref[...] = jnp.full(m_ref.shape, NEG, jnp.float32)
    acc_ref[...] = jnp.zeros(acc_ref.shape, jnp.float32)

    def step(kt, masked):
        start = pl.multiple_of(kt * tk, tk)
        for hh in range(2):
            s_ref[hh] = _dot(qa_ref[hh], ka_ref[0, hh, :, pl.ds(start, tk)])
        causal = (start + lax.broadcasted_iota(jnp.int32, (1, tk), 1), s0) if masked else None
        for hh in range(2):
            _flash_tile(s_ref.at[hh], p_ref.at[hh], m_ref.at[hh], acc_ref.at[hh], rc, tq, causal=causal)
            acc_ref[hh] += _dot(p_ref[hh], va_ref[0, hh, pl.ds(start, tk), :])

    per_q = tq // tk
    per_trip = 4 * per_q

    def body(kp, carry):
        for d in range(per_trip):
            step(kp * per_trip + d, False)
        return carry

    n_full = s0 // tk
    n_trips = n_full // per_trip
    lax.fori_loop(0, n_trips, body, 0)
    done = n_trips * per_trip
    size = per_trip // 2
    while size >= per_q:
        take = ((n_full - done) // size) % 2

        @pl.when(take == 1)
        def _(size=size, first=done + ((n_full - done) // (2 * size)) * (2 * size)):
            for d in range(size):
                step(first + d, False)

        size //= 2

    for d in range(per_q):
        step(n_full + d, True)
    outs = []
    for hh in range(2):
        acc = acc_ref[hh]
        outs.append(acc[:, :HEAD_DIM] / acc[:, HEAD_DIM:])
    o_ref[...] = jnp.concatenate(outs, axis=1).astype(o_ref.dtype)


def _moba(proj, mk_aug_t, mv_aug, kmean, batch, seq):
    tq = min(1024, seq)
    tk = 512
    nq = seq // tq
    nbm = seq // MOBA_BLOCK
    n_top = min(MOBA_TOPK, nbm)
    qb = COL_MQ // 128
    aug = HEAD_DIM + nbm
    return pl.pallas_call(
        functools.partial(_moba_kernel, tq=tq, tk=tk, n_top=n_top, rc=64),
        out_shape=jax.ShapeDtypeStruct((batch * seq, MOBA_W), CDT),
        grid=(batch, MOBA_HEADS // 2, nq),
        in_specs=[pl.BlockSpec((tq, 128), lambda b, p, i: (b * nq + i, qb + p)),
                  pl.BlockSpec((1, 2, aug, seq), lambda b, p, i: (b, p, 0, 0)),
                  pl.BlockSpec((1, 2, seq, 128), lambda b, p, i: (b, p, 0, 0)),
                  pl.BlockSpec((1, nbm, 128), lambda b, p, i: (b, 0, p))],
        out_specs=pl.BlockSpec((tq, 128), lambda b, p, i: (b * nq + i, p)),
        scratch_shapes=[pltpu.VMEM((2, tq, aug), CDT), pltpu.VMEM((2, tq, tk), jnp.float32),
                        pltpu.VMEM((2, tq, tk), CDT), pltpu.VMEM((2, tq, 128), jnp.float32),
                        pltpu.VMEM((2, tq, 128), jnp.float32)],
        compiler_params=_params(("arbitrary", "arbitrary", "arbitrary")),
        name="moba",
    )(proj, mk_aug_t, mv_aug, kmean)


def _merge_kernel(oc_ref, os_ref, ow_ref, om_ref, ng_ref, gn_ref, gm_ref, x_ref, ga_ref,
                  ex_ref, wun_ref, wum_ref, wo_ref, o_ref):
    gates = _sigmoid(ng_ref[...].astype(jnp.float32))
    hi, lo = _split_hi_lo(gates)
    e = _dot(hi, ex_ref[...]) + _dot(lo, ex_ref[...])
    o_nsa = (e[:, :NSA_Q] * oc_ref[...].astype(jnp.float32)
             + e[:, NSA_Q:2 * NSA_Q] * os_ref[...].astype(jnp.float32)
             + e[:, 2 * NSA_Q:] * ow_ref[...].astype(jnp.float32))
    y = (_sigmoid(gn_ref[...].astype(jnp.float32)) * _dot(o_nsa.astype(CDT), wun_ref[...])
         + _sigmoid(gm_ref[...].astype(jnp.float32)) * _dot(om_ref[...], wum_ref[...]))
    o_ref[...] = x_ref[...] + ga_ref[0] * _dot(y.astype(CDT), wo_ref[...])


def _merge(o_c, o_s, o_w, o_m, proj, x2d, ga, w_up_nsa, w_up_moba, w_out, seq):
    n, d = x2d.shape
    tm = min(512, seq)
    per_seq = seq // tm
    ng_w = COL_GN - COL_NG
    ex = np.zeros((ng_w, 3 * NSA_Q), np.float32)
    for h in range(NSA_HEADS):
        for j in range(3):
            ex[h * 3 + j, j * NSA_Q + h * HEAD_DIM: j * NSA_Q + (h + 1) * HEAD_DIM] = 1.0
    row = lambda i: (i, 0)
    const = lambda i: (0, 0)
    return pl.pallas_call(
        _merge_kernel,
        out_shape=jax.ShapeDtypeStruct((n, d), jnp.float32),
        grid=(n // tm,),
        in_specs=[pl.BlockSpec((tm, NSA_Q), row), pl.BlockSpec((tm, NSA_Q), row),
                  pl.BlockSpec((tm, NSA_Q), row), pl.BlockSpec((tm, MOBA_W), row),
                  pl.BlockSpec((tm, ng_w), lambda i: (i, COL_NG // ng_w)),
                  pl.BlockSpec((tm, d), lambda i: (i, COL_GN // d)),
                  pl.BlockSpec((tm, d), lambda i: (i, COL_GM // d)),
                  pl.BlockSpec((tm, d), row),
                  pl.BlockSpec((1, 1, d), lambda i: (i // per_seq, 0, 0)),
                  pl.BlockSpec((ng_w, 3 * NSA_Q), const),
                  pl.BlockSpec((NSA_Q, d), const), pl.BlockSpec((MOBA_W, d), const),
                  pl.BlockSpec((d, d), const)],
        out_specs=pl.BlockSpec((tm, d), row),
        compiler_params=_params(("arbitrary",)),
        name="mixer_merge",
    )(o_c, o_s, o_w, o_m, proj, proj, proj, x2d, ga, jnp.asarray(ex, CDT),
      w_up_nsa.astype(CDT), w_up_moba.astype(CDT), w_out.astype(CDT))


def _peer_score_kernel(q_ref, k1_ref, k2_ref, cnt_ref, rk_ref, e1_ref, e2_ref):
    half = PEER_QDIM // 2
    k = PEER_TOPK
    q = q_ref[...]
    s1_all = _dot_nt(k1_ref[...], q[:, :half])
    s2_all = _dot_nt(k2_ref[...], q[:, half:])
    t = 128
    ridx = lax.broadcasted_iota(jnp.int32, (PEER_NKEYS, t), 0).astype(jnp.float32)
    unranked = jnp.full((PEER_NKEYS, t), float(k), jnp.float32)
    chunks = list(range(0, s1_all.shape[1], t))
    shifted = []
    for c0 in chunks:
        for s_all in (s1_all, s2_all):
            s = s_all[:, c0:c0 + t]
            shifted.append(s - jnp.max(s, axis=0, keepdims=True))
    xs = list(shifted)
    rks = [unranked] * len(xs)
    tops = [[] for _ in xs]
    for i in range(k):
        for n in range(len(xs)):
            m, pick = _extract_max(xs[n], ridx)
            xs[n] = jnp.where(pick, LOWEST, xs[n])
            rks[n] = jnp.where(pick, float(i), rks[n])
            tops[n].append(m)

    v2_alls, cands = [], []
    for c in range(len(chunks)):
        v1, v2_all = tops[2 * c], jnp.concatenate(tops[2 * c + 1], axis=0)
        pieces = [v1[i] + v2_all[0:k // (i + 1)] for i in range(k)]
        pad = -sum(p.shape[0] for p in pieces) % 8
        cands.append(jnp.concatenate(pieces + [jnp.full((pad, t), LOWEST, jnp.float32)], axis=0))
        v2_alls.append(v2_all)
    cidx = lax.broadcasted_iota(jnp.int32, cands[0].shape, 0).astype(jnp.float32)
    vals = [[] for _ in chunks]
    for i in range(k):
        for c in range(len(chunks)):
            m, pick = _extract_max(cands[c], cidx)
            cands[c] = jnp.where(pick, LOWEST, cands[c])
            vals[c].append(m)

    for c, c0 in enumerate(chunks):
        v1, v2_all, val = tops[2 * c], v2_alls[c], vals[c]
        tau = val[k - 1]
        z = val[0] - val[0] + 1.0
        for i in range(1, k):
            z = z + jnp.exp(val[i] - val[0])
        cnt = jnp.zeros((PEER_NKEYS, t), jnp.float32)
        for i in range(k):
            n_i = jnp.sum(jnp.where(v1[i] + v2_all >= tau, 1.0, 0.0), axis=0, keepdims=True)
            cnt = jnp.where(rks[2 * c] == float(i), n_i, cnt)
        cnt_ref[0, :, c0:c0 + t] = cnt
        rk_ref[0, :, c0:c0 + t] = rks[2 * c + 1].astype(rk_ref.dtype)
        e1_ref[0, :, c0:c0 + t] = jnp.exp(shifted[2 * c] - val[0]) / z
        e2_ref[0, :, c0:c0 + t] = jnp.exp(shifted[2 * c + 1]).astype(e2_ref.dtype)


def _peer_scores(qp, k1, k2):
    n = qp.shape[0]
    tt = 512
    f32 = jax.ShapeDtypeStruct((PEER_HEADS, PEER_NKEYS, n), jnp.float32)
    cdt = jax.ShapeDtypeStruct((PEER_HEADS, PEER_NKEYS, n), CDT)
    big = pl.BlockSpec((1, PEER_NKEYS, tt), lambda i, h: (h, 0, i))
    return pl.pallas_call(
        _peer_score_kernel,
        out_shape=[f32, cdt, f32, cdt],
        grid=(n // tt, PEER_HEADS),
        in_specs=[pl.BlockSpec((tt, PEER_QDIM), lambda i, h: (i, h)),
                  pl.BlockSpec((PEER_NKEYS, PEER_QDIM // 2), lambda i, h: (0, 0)),
                  pl.BlockSpec((PEER_NKEYS, PEER_QDIM // 2), lambda i, h: (0, 0))],
        out_specs=[big, big, big, big],
        compiler_params=_params(("arbitrary", "arbitrary")),
        name="peer_scores",
    )(qp, k1.astype(CDT), k2.astype(CDT))


def _row_to_rows(row, n):
    tile_rows = 16
    tile = jnp.broadcast_to(row, (tile_rows, row.shape[1])).astype(CDT)
    return jnp.concatenate([tile] * (n // tile_rows), axis=0)


def _peer_expert_kernel(h_ref, u_ref, vt_ref, cnt_ref, rk_ref, e1_ref, e2_ref, x_ref, ga_ref,
                        o_ref, acc_ref, *, eb):
    j = pl.program_id(1)
    tt = h_ref.shape[0]

    @pl.when(j == 0)
    def _():
        acc_ref[...] = jnp.zeros(acc_ref.shape, jnp.float32)

    parts = []
    piece = 2 * PEER_NKEYS
    for e0 in range(0, eb, piece):
        sc = _dot_nt(u_ref[e0:e0 + piece, :], h_ref[...])
        ws = []
        for al in range(e0 // PEER_NKEYS, (e0 + piece) // PEER_NKEYS):
            a = j * (eb // PEER_NKEYS) + al
            w = None
            for hd in range(PEER_HEADS):
                cnt_a = _row_to_rows(cnt_ref[hd, pl.ds(a, 1), :], PEER_NKEYS)
                e1_a = _row_to_rows(e1_ref[hd, pl.ds(a, 1), :], PEER_NKEYS)
                contrib = jnp.where(rk_ref[hd] < cnt_a, e2_ref[hd] * e1_a, jnp.zeros((), CDT))
                w = contrib if w is None else w + contrib
            ws.append(w)
        parts.append(jnp.concatenate(ws, axis=0) * _gelu(sc.astype(CDT)))
    pw = jnp.concatenate(parts, axis=0)
    acc_ref[...] += _dot(vt_ref[...], pw)

    @pl.when(j == pl.num_programs(1) - 1)
    def _():
        o_ref[...] = x_ref[...] + ga_ref[0] * acc_ref[...].T


def _peer_experts(h2, u, v_t, cnt, rk2, e1, e2, x2d, ga, seq):
    n, d = x2d.shape
    tt = min(512, seq)
    per_seq = seq // tt
    eb = 2048
    n_blk = u.shape[0] // eb
    big = pl.BlockSpec((PEER_HEADS, PEER_NKEYS, tt), lambda i, j: (0, 0, i))
    return pl.pallas_call(
        functools.partial(_peer_expert_kernel, eb=eb),
        out_shape=jax.ShapeDtypeStruct((n, d), jnp.float32),
        grid=(n // tt, n_blk),
        in_specs=[pl.BlockSpec((tt, d), lambda i, j: (i, 0)),
                  pl.BlockSpec((eb, d), lambda i, j: (j, 0)),
                  pl.BlockSpec((d, eb), lambda i, j: (0, j)),
                  big, big, big, big,
                  pl.BlockSpec((tt, d), lambda i, j: (i, 0)),
                  pl.BlockSpec((1, 1, d), lambda i, j: (i // per_seq, 0, 0))],
        out_specs=pl.BlockSpec((tt, d), lambda i, j: (i, 0)),
        scratch_shapes=[pltpu.VMEM((d, tt), jnp.float32)],
        compiler_params=_params(("arbitrary", "arbitrary")),
        name="peer_experts",
    )(h2, u, v_t, cnt, rk2, e1, e2, x2d, ga)


def _rms_kernel(x_ref, g_ref, o_ref):
    x = x_ref[...]
    ms = jnp.mean(x * x, axis=-1, keepdims=True)
    o_ref[...] = x * lax.rsqrt(ms + RMS_EPS) * g_ref[...]


def _final_norm(x2d, g):
    n, d = x2d.shape
    tm = 512
    return pl.pallas_call(
        _rms_kernel,
        out_shape=jax.ShapeDtypeStruct((n, d), jnp.float32),
        grid=(n // tm,),
        in_specs=[pl.BlockSpec((tm, d), lambda i: (i, 0)), pl.BlockSpec((1, d), lambda i: (0, 0))],
        out_specs=pl.BlockSpec((tm, d), lambda i: (i, 0)),
        compiler_params=_params(("arbitrary",)),
        name="final_rmsnorm",
    )(x2d, g.reshape(1, d))


def _kv_layout_kernel(kk_ref, mk0_ref, mk1_ref, vv_ref, mv0_ref, mv1_ref,
                      ks_ref, kw_ref, mk_ref, vs_ref, vw_ref, mv_ref, *, ts, sel_per_tile):
    i = pl.program_id(1)
    pos = i * ts + lax.broadcasted_iota(jnp.int32, (1, ts), 1)
    nbm = mk_ref.shape[2] - HEAD_DIM
    sel_rows = ks_ref.shape[2] - HEAD_DIM
    sel_onehot = (lax.broadcasted_iota(jnp.int32, (sel_rows, 1), 0)
                  == (pos // SLC_LEN) % sel_per_tile).astype(CDT)
    moba_onehot = (lax.broadcasted_iota(jnp.int32, (nbm, 1), 0) == pos // MOBA_BLOCK).astype(CDT)
    ones = jnp.ones((ts, HEAD_DIM), CDT)

    kk_t = kk_ref[...].astype(jnp.float32).T
    for h in range(NSA_KV_HEADS):
        ks_ref[0, h] = jnp.concatenate([kk_t[h * HEAD_DIM:(h + 1) * HEAD_DIM].astype(CDT), sel_onehot], axis=0)
        kw_ref[0, h] = kk_t[(NSA_KV_HEADS + h) * HEAD_DIM:(NSA_KV_HEADS + h + 1) * HEAD_DIM].astype(CDT)
        vs_ref[0, h] = jnp.concatenate([vv_ref[:, h * HEAD_DIM:(h + 1) * HEAD_DIM], ones], axis=1)
        vw_ref[0, h] = jnp.concatenate(
            [vv_ref[:, (NSA_KV_HEADS + h) * HEAD_DIM:(NSA_KV_HEADS + h + 1) * HEAD_DIM], ones], axis=1)
    per_ref = mk0_ref.shape[1] // HEAD_DIM
    for half, (k_in, v_in) in enumerate(((mk0_ref, mv0_ref), (mk1_ref, mv1_ref))):
        k_t = k_in[...].astype(jnp.float32).T
        for hh in range(per_ref):
            h = half * per_ref + hh
            mk_ref[0, h] = jnp.concatenate([k_t[hh * HEAD_DIM:(hh + 1) * HEAD_DIM].astype(CDT), moba_onehot],
                                           axis=0)
            mv_ref[0, h] = jnp.concatenate([v_in[:, hh * HEAD_DIM:(hh + 1) * HEAD_DIM], ones], axis=1)


def _kv_layout(proj, batch, seq, sel_per_tile):
    ts = min(512, seq)
    nt = seq // ts
    nbm = seq // MOBA_BLOCK
    w = 4 * HEAD_DIM
    col = lambda c: (lambda b, i: (b * nt + i, c // w))
    head_major = lambda heads, rows: pl.BlockSpec((1, heads, rows, ts), lambda b, i: (b, 0, 0, i))
    token_major = lambda heads: pl.BlockSpec((1, heads, ts, 128), lambda b, i: (b, 0, i, 0))
    return pl.pallas_call(
        functools.partial(_kv_layout_kernel, ts=ts, sel_per_tile=sel_per_tile),
        out_shape=[jax.ShapeDtypeStruct((batch, NSA_KV_HEADS, 128, seq), CDT),
                   jax.ShapeDtypeStruct((batch, NSA_KV_HEADS, HEAD_DIM, seq), CDT),
                   jax.ShapeDtypeStruct((batch, MOBA_HEADS, HEAD_DIM + nbm, seq), CDT),
                   jax.ShapeDtypeStruct((batch, NSA_KV_HEADS, seq, 128), CDT),
                   jax.ShapeDtypeStruct((batch, NSA_KV_HEADS, seq, 128), CDT),
                   jax.ShapeDtypeStruct((batch, MOBA_HEADS, seq, 128), CDT)],
        grid=(batch, nt),
        in_specs=[pl.BlockSpec((ts, w), col(COL_KS)),
                  pl.BlockSpec((ts, w), col(COL_MK)), pl.BlockSpec((ts, w), col(COL_MK + w)),
                  pl.BlockSpec((ts, w), col(COL_VS)),
                  pl.BlockSpec((ts, w), col(COL_MV)), pl.BlockSpec((ts, w), col(COL_MV + w))],
        out_specs=[head_major(NSA_KV_HEADS, 128), head_major(NSA_KV_HEADS, HEAD_DIM),
                   head_major(MOBA_HEADS, HEAD_DIM + nbm), token_major(NSA_KV_HEADS),
                   token_major(NSA_KV_HEADS), token_major(MOBA_HEADS)],
        compiler_params=_params(("arbitrary", "arbitrary")),
        name="kv_layout",
    )(proj, proj, proj, proj, proj, proj)


def kernel(x, c, w_ada, b_ada, g_attn, g_ffn, w_in, cmp_pe, cmp_w1, cmp_w2, w_up_nsa, w_up_moba, w_out,
           peer_wq, peer_k1, peer_k2, peer_u, peer_v, g_final):
    batch, seq, d = x.shape
    depth = w_ada.shape[0]
    n = batch * seq
    x2d = x.reshape(n, d)
    mod = _adaln_mod(c, w_ada, b_ada)
    rope = _rope_tables(seq)
    for l in range(depth):
        sh1, sc1, ga1, sh2, sc2, ga2 = [m.reshape(batch, 1, d) for m in jnp.split(mod[l], 6, axis=-1)]
        proj = _norm_mod_matmul(x2d, g_attn[l], sc1, sh1, _reorder_w_in(w_in[l]), seq,
                                rope=rope, n_rope=ROPE_COLS // PROJ_TN)
        cmp_kv = _nsa_compress(proj, cmp_pe[l], cmp_w1[l], cmp_w2[l], batch, seq)
        o_c, selb = _nsa_compressed(proj, cmp_kv, batch, seq)
        ks_aug, kw_t, mk_aug, vs_aug, vw_aug, mv_aug = _kv_layout(proj, batch, seq, NSA_SEL_TK // SLC_LEN)
        o_s = _nsa_selected(proj, ks_aug, vs_aug, selb, batch, seq)
        o_w = _nsa_window(proj, kw_t, vw_aug, batch, seq)
        o_m = _moba(proj, mk_aug, mv_aug, _moba_kmean(proj, batch, seq), batch, seq)
        x2d = _merge(o_c, o_s, o_w, o_m, proj, x2d, ga1, w_up_nsa[l], w_up_moba[l], w_out[l], seq)
        qp, h2 = _norm_mod_matmul(x2d, g_ffn[l], sc2, sh2, peer_wq[l].astype(CDT), seq, emit_h=True)
        cnt, rk2, e1, e2 = _peer_scores(qp, peer_k1[l], peer_k2[l])
        x2d = _peer_experts(h2, peer_u[l].astype(CDT), peer_v[l].T.astype(CDT), cnt, rk2, e1, e2,
                            x2d, ga2, seq)
    return _final_norm(x2d, g_final).reshape(batch, seq, d)
```

```python
import functools

import jax
import jax.numpy as jnp
import numpy as np
from jax import lax
from jax.experimental import pallas as pl
from jax.experimental.pallas import tpu as pltpu

D_MODEL = 1024
HEAD_DIM = 64
ROT_DIM = HEAD_DIM // 4
ROPE_THETA = 500000.0
NSA_HEADS = 8
NSA_KV_HEADS = 2
NSA_GROUP = NSA_HEADS // NSA_KV_HEADS
CMP_LEN = 32
CMP_STRIDE = 16
CMP_HID = 2 * HEAD_DIM
SLC_LEN = 64
SLC_TOPN = 16
WIN = 512
MOBA_HEADS = 8
MOBA_BLOCK = 256
MOBA_TOPK = 3
PEER_HEADS = 8
PEER_NKEYS = 128
PEER_QDIM = 256
PEER_TOPK = 16
RMS_EPS = 1e-6
NEG = -1e30
SEL_FORCE = 1e4
LOWEST = -3.0e38
LOG2E = 1.4426950408889634

NSA_Q = NSA_HEADS * HEAD_DIM
NSA_KV = NSA_KV_HEADS * HEAD_DIM
MOBA_W = MOBA_HEADS * HEAD_DIM

CDT = jnp.bfloat16
V7X_VMEM_LIMIT = 56 * 1024 * 1024

COL_QROT = 0
COL_KS = 512
COL_KW = 640
COL_MQ = 768
COL_MK = 1280
ROPE_COLS = 1792
COL_NG = 1792
COL_GN = 2048
COL_GM = 3072
COL_QRAW = 4096
COL_KC = 4608
COL_VC = 4736
COL_VS = 4864
COL_VW = 4992
COL_MV = 5120
PROJ_COLS = 5632
PROJ_TN = 256


def _params(sem):
    return pltpu.CompilerParams(dimension_semantics=sem, vmem_limit_bytes=V7X_VMEM_LIMIT)


def _dot(a, b):
    return jnp.dot(a, b, preferred_element_type=jnp.float32)


def _dot_nt(a, b):
    return lax.dot_general(a, b, (((1,), (1,)), ((), ())), preferred_element_type=jnp.float32)


def _split_hi_lo(x):
    hi = x.astype(CDT)
    lo = (x - hi.astype(jnp.float32)).astype(CDT)
    return hi, lo


def _gelu(x):
    return 0.5 * x * (1.0 + jnp.tanh(0.7978845608028654 * (x + 0.044715 * (x * x * x))))


def _sigmoid(x):
    return 1.0 / (1.0 + jnp.exp(-x))


def _mod_kernel(c_ref, w_ref, b_ref, o_ref):
    c = c_ref[...]
    sc = c * _sigmoid(c)
    o_ref[0] = jnp.dot(sc, w_ref[0], preferred_element_type=jnp.float32,
                       precision=lax.Precision.HIGHEST) + b_ref[0]


def _adaln_mod(c, w_ada, b_ada):
    depth, d, six_d = w_ada.shape
    b = c.shape[0]
    rows = 8
    c_pad = jnp.zeros((rows, d), jnp.float32).at[:b].set(c)
    tn = 1024
    out = pl.pallas_call(
        _mod_kernel,
        out_shape=jax.ShapeDtypeStruct((depth, rows, six_d), jnp.float32),
        grid=(depth, six_d // tn),
        in_specs=[pl.BlockSpec((rows, d), lambda l, j: (0, 0)),
                  pl.BlockSpec((1, d, tn), lambda l, j: (l, 0, j)),
                  pl.BlockSpec((1, 1, tn), lambda l, j: (l, 0, j))],
        out_specs=pl.BlockSpec((1, rows, tn), lambda l, j: (l, 0, j)),
        compiler_params=_params(("arbitrary", "arbitrary")),
        name="adaln_mod",
    )(c_pad, w_ada, b_ada.reshape(depth, 1, six_d))
    return out[:, :b]


def _nmm_kernel(*refs, n_rope, emit_h, tn):
    if n_rope:
        x_ref, g_ref, sc_ref, sh_ref, w_ref, cos_ref, sa_ref, sb_ref = refs[:8]
        rest = refs[8:]
    else:
        x_ref, g_ref, sc_ref, sh_ref, w_ref = refs[:5]
        rest = refs[5:]
    if emit_h:
        o_ref, ho_ref, h_ref = rest
    else:
        o_ref, h_ref = rest
    x = x_ref[...]
    ms = jnp.mean(x * x, axis=-1, keepdims=True)
    y = x * lax.rsqrt(ms + RMS_EPS) * g_ref[...]
    h = (y * (1.0 + sc_ref[0]) + sh_ref[0]).astype(h_ref.dtype)
    h_ref[...] = h
    if emit_h:
        ho_ref[...] = h

    for j in range(w_ref.shape[1] // tn):
        acc = _dot(h_ref[...], w_ref[:, j * tn:(j + 1) * tn])
        if j < n_rope:
            cos, sa, sb = cos_ref[...], sa_ref[...], sb_ref[...]
            for c0 in range(0, tn, 128):
                a = acc[:, c0:c0 + 128]
                r = (a * cos + pltpu.roll(a, 128 - ROT_DIM // 2, axis=1) * sa
                     + pltpu.roll(a, ROT_DIM // 2, axis=1) * sb)
                o_ref[:, j * tn + c0:j * tn + c0 + 128] = r.astype(o_ref.dtype)
        else:
            o_ref[:, j * tn:(j + 1) * tn] = acc.astype(o_ref.dtype)


def _norm_mod_matmul(x2d, g, sc, sh, w, seq, rope=None, n_rope=0, emit_h=False, tn=PROJ_TN):
    n, d = x2d.shape
    cols = w.shape[1]
    tm = min(512, seq)
    per_seq = seq // tm
    in_specs = [pl.BlockSpec((tm, d), lambda i: (i, 0)),
                pl.BlockSpec((1, d), lambda i: (0, 0)),
                pl.BlockSpec((1, 1, d), lambda i: (i // per_seq, 0, 0)),
                pl.BlockSpec((1, 1, d), lambda i: (i // per_seq, 0, 0)),
                pl.BlockSpec((d, cols), lambda i: (0, 0))]
    args = [x2d, g.reshape(1, d), sc, sh, w]
    if n_rope:
        in_specs += [pl.BlockSpec((tm, 128), lambda i: (i % per_seq, 0))] * 3
        args += list(rope)
    out_shape = [jax.ShapeDtypeStruct((n, cols), CDT)]
    out_specs = [pl.BlockSpec((tm, cols), lambda i: (i, 0))]
    if emit_h:
        out_shape.append(jax.ShapeDtypeStruct((n, d), CDT))
        out_specs.append(pl.BlockSpec((tm, d), lambda i: (i, 0)))
    res = pl.pallas_call(
        functools.partial(_nmm_kernel, n_rope=n_rope, emit_h=emit_h, tn=tn),
        out_shape=out_shape,
        grid=(n // tm,),
        in_specs=in_specs,
        out_specs=out_specs,
        scratch_shapes=[pltpu.VMEM((tm, d), CDT)],
        compiler_params=_params(("arbitrary",)),
        name="norm_mod_matmul",
    )(*args)
    return res if emit_h else res[0]


def _rope_tables(seq):
    half = ROT_DIM // 2
    inv = ROPE_THETA ** (-jnp.arange(half, dtype=jnp.float32) / half)
    ang = jnp.arange(seq, dtype=jnp.float32)[:, None] * inv[None, :]
    cos, sin = jnp.cos(ang), jnp.sin(ang)
    ones = jnp.ones((seq, HEAD_DIM - ROT_DIM), jnp.float32)
    zeros = jnp.zeros((seq, HEAD_DIM - ROT_DIM), jnp.float32)
    zh = jnp.zeros((seq, half), jnp.float32)
    c64 = jnp.concatenate([cos, cos, ones], axis=1)
    sa64 = jnp.concatenate([-sin, zh, zeros], axis=1)
    sb64 = jnp.concatenate([zh, sin, zeros], axis=1)
    return tuple(jnp.concatenate([t, t], axis=1) for t in (c64, sa64, sb64))


def _reorder_w_in(w_in):
    o = 0
    pieces = {}
    for name, size in (("nq", NSA_Q), ("kc", NSA_KV), ("vc", NSA_KV), ("ks", NSA_KV), ("vs", NSA_KV),
                       ("kw", NSA_KV), ("vw", NSA_KV), ("ng", 3 * NSA_HEADS), ("mq", MOBA_W),
                       ("mk", MOBA_W), ("mv", MOBA_W), ("gn", D_MODEL), ("gm", D_MODEL)):
        pieces[name] = w_in[:, o:o + size]
        o += size
    scale = HEAD_DIM ** -0.5 * LOG2E
    pad = jnp.zeros((w_in.shape[0], COL_GN - COL_NG - 3 * NSA_HEADS), w_in.dtype)
    w = jnp.concatenate([pieces["nq"] * scale, pieces["ks"], pieces["kw"], pieces["mq"] * scale,
                         pieces["mk"], pieces["ng"], pad, pieces["gn"], pieces["gm"],
                         pieces["nq"] * scale, pieces["kc"], pieces["vc"], pieces["vs"],
                         pieces["vw"], pieces["mv"]], axis=1)
    assert w.shape[1] == PROJ_COLS
    return w.astype(CDT)


def _compress_kernel(x_ref, w1c_ref, pe_ref, w1_ref, w2_ref, o_ref):
    nb = x_ref.shape[3]
    ab = _dot(x_ref[0, 0, 0], w1c_ref[0])
    a = ab[:, :CMP_HID]
    b_next = pltpu.roll(ab[:, CMP_HID:], nb - 1, axis=0)
    bias = _dot(pe_ref[0], w1_ref[0])[0:1]
    hid = _gelu(a + b_next + bias)
    o_ref[0, 0, 0] = _dot(hid.astype(CDT), w2_ref[0]).astype(o_ref.dtype)


def _nsa_compress(proj, cmp_pe, cmp_w1, cmp_w2, batch, seq):
    nb = seq // CMP_STRIDE
    half = CMP_STRIDE * HEAD_DIM
    kcvc = proj[:, COL_KC:COL_KC + 2 * NSA_KV].reshape(batch, nb, CMP_STRIDE, 2, NSA_KV_HEADS, HEAD_DIM)
    x = jnp.transpose(kcvc, (3, 0, 4, 1, 2, 5)).reshape(2, batch, NSA_KV_HEADS, nb, half)
    w1c = jnp.concatenate([cmp_w1[:, :half], cmp_w1[:, half:]], axis=2).astype(CDT)
    pe = jnp.broadcast_to(cmp_pe.reshape(2, 1, CMP_LEN * HEAD_DIM), (2, 8, CMP_LEN * HEAD_DIM)).astype(CDT)
    return pl.pallas_call(
        _compress_kernel,
        out_shape=jax.ShapeDtypeStruct((2, batch, NSA_KV_HEADS, nb, HEAD_DIM), CDT),
        grid=(2, batch, NSA_KV_HEADS),
        in_specs=[pl.BlockSpec((1, 1, 1, nb, half), lambda w, b, k: (w, b, k, 0, 0)),
                  pl.BlockSpec((1, half, 2 * CMP_HID), lambda w, b, k: (w, 0, 0)),
                  pl.BlockSpec((1, 8, 2 * half), lambda w, b, k: (w, 0, 0)),
                  pl.BlockSpec((1, 2 * half, CMP_HID), lambda w, b, k: (w, 0, 0)),
                  pl.BlockSpec((1, CMP_HID, HEAD_DIM), lambda w, b, k: (w, 0, 0))],
        out_specs=pl.BlockSpec((1, 1, 1, nb, HEAD_DIM), lambda w, b, k: (w, b, k, 0, 0)),
        compiler_params=_params(("arbitrary", "arbitrary", "arbitrary")),
        name="nsa_compress",
    )(x, w1c, pe, cmp_w1.astype(CDT), cmp_w2.astype(CDT))


def _stack_heads(qblk):
    return jnp.concatenate([qblk[:, g * HEAD_DIM:(g + 1) * HEAD_DIM] for g in range(NSA_GROUP)], axis=0)


def _unstack_heads(o, tq):
    return jnp.concatenate([o[g * tq:(g + 1) * tq] for g in range(NSA_GROUP)], axis=1)


def _extract_max(x, ridx):
    m = jnp.max(x, axis=0, keepdims=True)
    idx = jnp.min(jnp.where(x == m, ridx, float(x.shape[0])), axis=0, keepdims=True)
    return m, ridx == idx


def _nsa_cmp_kernel(q_ref, kc_ref, vc_ref, ov_ref, oc_ref, sb_ref, imp_ref, *, tq, n_sel, n_var):
    i = pl.program_id(2)
    s0 = i * tq
    nb = kc_ref.shape[3]
    ns = ov_ref.shape[1]
    tq_col = s0 + lax.broadcasted_iota(jnp.int32, (tq, 1), 0)
    t4 = jnp.concatenate([tq_col] * NSA_GROUP, axis=0)

    def attend(nk):
        q4 = _stack_heads(q_ref[...])
        s = _dot_nt(q4, kc_ref[0, 0, 0, :nk, :])
        cend = lax.broadcasted_iota(jnp.int32, (1, nk), 1) * CMP_STRIDE + (CMP_LEN - 1)
        mask = cend <= t4
        s = jnp.where(mask, s, NEG)
        m = jnp.max(s, axis=-1, keepdims=True)
        p = jnp.where(mask, jnp.exp2(s - m), 0.0)
        p = p / jnp.maximum(jnp.sum(p, axis=-1, keepdims=True), 1e-30)
        o = _dot(p.astype(CDT), vc_ref[0, 0, 0, :nk, :])
        oc_ref[...] = _unstack_heads(o, tq).astype(oc_ref.dtype)
        psum = p[0:tq]
        for g in range(1, NSA_GROUP):
            psum = psum + p[g * tq:(g + 1) * tq]
        hi, lo = _split_hi_lo(psum)
        imp_ref[...] = _dot(hi, ov_ref[:nk, :]) + _dot(lo, ov_ref[:nk, :])

    tiles_per_var = pl.num_programs(2) // n_var
    for v in range(n_var):
        @pl.when(i // tiles_per_var == v)
        def _(v=v):
            attend(nb * (v + 1) // n_var)

    blk = lax.broadcasted_iota(jnp.int32, (ns, 1), 0)
    lanes = min(128, tq)
    ridx = lax.broadcasted_iota(jnp.int32, (ns, lanes), 0).astype(jnp.float32)
    imp_t = imp_ref[...].T
    scores, valids = [], []
    for c0 in range(0, tq, lanes):
        cur = (s0 + c0 + lax.broadcasted_iota(jnp.int32, (1, lanes), 1)) // SLC_LEN
        valid = blk <= cur
        forced = (blk == 0) | (blk == cur) | (blk == cur - 1)
        scores.append(jnp.where(valid, jnp.where(forced, SEL_FORCE, imp_t[:, c0:c0 + lanes]), NEG))
        valids.append(valid)
    sels = [jnp.zeros((ns, lanes), jnp.float32)] * len(scores)
    for _ in range(n_sel):
        for c in range(len(scores)):
            _, pick = _extract_max(scores[c], ridx)
            scores[c] = jnp.where(pick, LOWEST, scores[c])
            sels[c] = jnp.where(pick, 1.0, sels[c])
    bias = jnp.concatenate([jnp.where((sel > 0.5) & valid, 0.0, NEG) for sel, valid in zip(sels, valids)],
                           axis=1).T
    extra = sb_ref.shape[3] - ns
    if extra:
        bias = jnp.concatenate([bias, jnp.full((tq, extra), NEG, jnp.float32)], axis=1)
    sb_ref[0, 0] = bias.astype(sb_ref.dtype)


def _nsa_compressed(proj, cmp_kv, batch, seq):
    tq = 256
    nq = seq // tq
    nb = seq // CMP_STRIDE
    ns = seq // SLC_LEN
    n_sel = min(SLC_TOPN, ns)
    c_start = np.arange(nb) * CMP_STRIDE
    s_start = np.arange(ns) * SLC_LEN
    ov = np.maximum(np.minimum(c_start[:, None] + CMP_LEN, s_start[None, :] + SLC_LEN)
                    - np.maximum(c_start[:, None], s_start[None, :]), 0).astype(np.float32) / CMP_LEN
    ov[nb - 1] = 0.0
    qb = COL_QRAW // (NSA_GROUP * HEAD_DIM)
    return pl.pallas_call(
        functools.partial(_nsa_cmp_kernel, tq=tq, n_sel=n_sel,
                          n_var=4 if (nb % (4 * 128) == 0 and nq % 4 == 0) else 1),
        out_shape=[jax.ShapeDtypeStruct((batch * seq, NSA_Q), CDT),
                   jax.ShapeDtypeStruct((batch, NSA_KV_HEADS, seq, max(ns, 128)), CDT)],
        grid=(batch, NSA_KV_HEADS, nq),
        in_specs=[pl.BlockSpec((tq, NSA_GROUP * HEAD_DIM), lambda b, k, i: (b * nq + i, qb + k)),
                  pl.BlockSpec((1, 1, 1, nb, HEAD_DIM), lambda b, k, i: (0, b, k, 0, 0)),
                  pl.BlockSpec((1, 1, 1, nb, HEAD_DIM), lambda b, k, i: (1, b, k, 0, 0)),
                  pl.BlockSpec((nb, ns), lambda b, k, i: (0, 0))],
        out_specs=[pl.BlockSpec((tq, NSA_GROUP * HEAD_DIM), lambda b, k, i: (b * nq + i, k)),
                   pl.BlockSpec((1, 1, tq, max(ns, 128)), lambda b, k, i: (b, k, i, 0))],
        scratch_shapes=[pltpu.VMEM((tq, ns), jnp.float32)],
        compiler_params=_params(("arbitrary", "arbitrary", "arbitrary")),
        name="nsa_compressed",
    )(proj, cmp_kv, cmp_kv, jnp.asarray(ov, CDT))


def _flash_tile(s_ref, p_ref, m_ref, acc_ref, rc, tq, causal=None):
    rows, tk = s_ref.shape

    def tile(r0, c0):
        s = s_ref[r0:r0 + rc, c0:c0 + 128]
        if causal is not None:
            kpos, t0 = causal
            t = t0 + (r0 % tq) + lax.broadcasted_iota(jnp.int32, (rc, 1), 0)
            s = jnp.where(kpos[:, c0:c0 + 128] <= t, s, NEG)
        return s

    for r0 in range(0, rows, rc):
        mx = tile(r0, 0)
        for c0 in range(128, tk, 128):
            mx = jnp.maximum(mx, tile(r0, c0))
        m_prev = m_ref[r0:r0 + rc, :]
        m_new = jnp.maximum(m_prev, jnp.max(mx, axis=-1, keepdims=True))
        acc_ref[r0:r0 + rc, :] = acc_ref[r0:r0 + rc, :] * jnp.exp2(m_prev - m_new)
        m_ref[r0:r0 + rc, :] = m_new
    for r0 in range(0, rows, rc):
        m_new = m_ref[r0:r0 + rc, :]
        for c0 in range(0, tk, 128):
            p_ref[r0:r0 + rc, c0:c0 + 128] = jnp.exp2(tile(r0, c0) - m_new).astype(p_ref.dtype)


def _nsa_sel_kernel(q_ref, kt_ref, va_ref, sb_ref, o_ref,
                    qp_ref, q4_ref, sbf_ref, s_ref, p_ref, m_ref, acc_ref, *, tq, tk, rc):
    i = pl.program_id(2)
    s0 = i * tq
    nsp = sb_ref.shape[3]
    per_tile = tk // SLC_LEN
    q4 = _stack_heads(q_ref[...])
    qp_ref[...] = jnp.concatenate([q4, jnp.zeros_like(q4)], axis=1)
    sbf_ref[...] = sb_ref[0, 0].astype(jnp.float32)
    lane = lax.broadcasted_iota(jnp.int32, (tq, 128), 1)
    bias_lanes = (lane >= HEAD_DIM) & (lane < HEAD_DIM + per_tile)
    m_ref[...] = jnp.full(m_ref.shape, NEG, jnp.float32)
    acc_ref[...] = jnp.zeros(acc_ref.shape, jnp.float32)

    def step(kt, masked):
        start = pl.multiple_of(kt * tk, tk)
        shift = (HEAD_DIM - kt * per_tile + nsp) % nsp
        rolled = pltpu.roll(sbf_ref[...], shift, axis=1)[:, :128].astype(q4_ref.dtype)
        for g in range(NSA_GROUP):
            q4_ref[g * tq:(g + 1) * tq, :] = jnp.where(bias_lanes, rolled, qp_ref[g * tq:(g + 1) * tq, :])
        s_ref[...] = _dot(q4_ref[...], kt_ref[0, 0, :, pl.ds(start, tk)])
        causal = (start + lax.broadcasted_iota(jnp.int32, (1, tk), 1), s0) if masked else None
        _flash_tile(s_ref, p_ref, m_ref, acc_ref, rc, tq, causal=causal)
        acc_ref[...] += _dot(p_ref[...], va_ref[0, 0, pl.ds(start, tk), :])

    unroll = 8

    def body(kp, carry):
        for d in range(unroll):
            step(unroll * kp + d, False)
        return carry

    n_full = s0 // tk
    n_trips = n_full // unroll
    lax.fori_loop(0, n_trips, body, 0)
    done = n_trips * unroll
    size = unroll // 2
    while size >= 1:
        take = ((n_full - done) // size) % 2 if size > 1 else (n_full - done) % 2

        @pl.when(take == 1)
        def _(size=size, first=done + ((n_full - done) // (2 * size)) * (2 * size)):
            for d in range(size):
                step(first + d, False)

        size //= 2
    step(n_full, True)
    acc = acc_ref[...]
    o = acc[:, :HEAD_DIM] / acc[:, HEAD_DIM:]
    o_ref[...] = _unstack_heads(o, tq).astype(o_ref.dtype)


NSA_SEL_TK = 512


def _nsa_selected(proj, k_aug, vs_aug, selb, batch, seq):
    tq = 256
    tk = NSA_SEL_TK
    nq = seq // tq
    nsp = selb.shape[3]
    qb = COL_QROT // (NSA_GROUP * HEAD_DIM)
    rows = NSA_GROUP * tq
    return pl.pallas_call(
        functools.partial(_nsa_sel_kernel, tq=tq, tk=tk, rc=64),
        out_shape=jax.ShapeDtypeStruct((batch * seq, NSA_Q), CDT),
        grid=(batch, NSA_KV_HEADS, nq),
        in_specs=[pl.BlockSpec((tq, NSA_GROUP * HEAD_DIM), lambda b, k, i: (b * nq + i, qb + k)),
                  pl.BlockSpec((1, 1, 128, seq), lambda b, k, i: (b, k, 0, 0)),
                  pl.BlockSpec((1, 1, seq, 128), lambda b, k, i: (b, k, 0, 0)),
                  pl.BlockSpec((1, 1, tq, nsp), lambda b, k, i: (b, k, i, 0))],
        out_specs=pl.BlockSpec((tq, NSA_GROUP * HEAD_DIM), lambda b, k, i: (b * nq + i, k)),
        scratch_shapes=[pltpu.VMEM((rows, 128), CDT), pltpu.VMEM((rows, 128), CDT),
                        pltpu.VMEM((tq, nsp), jnp.float32), pltpu.VMEM((rows, tk), jnp.float32),
                        pltpu.VMEM((rows, tk), CDT),
                        pltpu.VMEM((rows, 128), jnp.float32), pltpu.VMEM((rows, 128), jnp.float32)],
        compiler_params=_params(("arbitrary", "arbitrary", "arbitrary")),
        name="nsa_selected",
    )(proj, k_aug, vs_aug, selb)


def _nsa_win_kernel(q_ref, kt_ref, va_ref, o_ref, q4_ref, s_ref, p_ref, *, tq, rc):
    i = pl.program_id(2)
    s0 = i * tq
    span = WIN + tq
    rows = NSA_GROUP * tq
    start = pl.multiple_of(jnp.maximum(s0 - WIN, 0), tq)
    q4_ref[...] = _stack_heads(q_ref[...])
    s_ref[...] = _dot(q4_ref[...], kt_ref[0, 0, :, pl.ds(start, span)])
    kpos = start + lax.broadcasted_iota(jnp.int32, (1, span), 1)
    for r0 in range(0, rows, rc):
        t = s0 + (r0 % tq) + lax.broadcasted_iota(jnp.int32, (rc, 1), 0)
        tiles = []
        for c0 in range(0, span, 128):
            kp = kpos[:, c0:c0 + 128]
            tiles.append(jnp.where((kp <= t) & (kp > t - WIN), s_ref[r0:r0 + rc, c0:c0 + 128], NEG))
        mx = tiles[0]
        for tile in tiles[1:]:
            mx = jnp.maximum(mx, tile)
        m = jnp.max(mx, axis=-1, keepdims=True)
        for c, tile in enumerate(tiles):
            p_ref[r0:r0 + rc, c * 128:(c + 1) * 128] = jnp.exp2(tile - m).astype(p_ref.dtype)
    acc = _dot(p_ref[...], va_ref[0, 0, pl.ds(start, span), :])
    o = acc[:, :HEAD_DIM] / acc[:, HEAD_DIM:]
    o_ref[...] = _unstack_heads(o, tq).astype(o_ref.dtype)


def _nsa_window(proj, kw_t, vw_aug, batch, seq):
    tq = 256
    nq = seq // tq
    qb = COL_QROT // (NSA_GROUP * HEAD_DIM)
    rows = NSA_GROUP * tq
    span = WIN + tq
    return pl.pallas_call(
        functools.partial(_nsa_win_kernel, tq=tq, rc=32),
        out_shape=jax.ShapeDtypeStruct((batch * seq, NSA_Q), CDT),
        grid=(batch, NSA_KV_HEADS, nq),
        in_specs=[pl.BlockSpec((tq, NSA_GROUP * HEAD_DIM), lambda b, k, i: (b * nq + i, qb + k)),
                  pl.BlockSpec((1, 1, HEAD_DIM, seq), lambda b, k, i: (b, k, 0, 0)),
                  pl.BlockSpec((1, 1, seq, 128), lambda b, k, i: (b, k, 0, 0))],
        out_specs=pl.BlockSpec((tq, NSA_GROUP * HEAD_DIM), lambda b, k, i: (b * nq + i, k)),
        scratch_shapes=[pltpu.VMEM((rows, HEAD_DIM), CDT), pltpu.VMEM((rows, span), jnp.float32),
                        pltpu.VMEM((rows, span), CDT)],
        compiler_params=_params(("arbitrary", "arbitrary", "arbitrary")),
        name="nsa_window",
    )(proj, kw_t, vw_aug)


def _moba_mean_kernel(k_ref, o_ref):
    seq, w = k_ref.shape
    nbm = seq // MOBA_BLOCK
    k = k_ref[...].astype(jnp.float32).reshape(nbm, MOBA_BLOCK, w)
    o_ref[0] = jnp.sum(k, axis=1) * (1.0 / MOBA_BLOCK)


def _moba_kmean(proj, batch, seq):
    nbm = seq // MOBA_BLOCK
    kb = COL_MK // 128
    return pl.pallas_call(
        _moba_mean_kernel,
        out_shape=jax.ShapeDtypeStruct((batch, nbm, MOBA_W), jnp.float32),
        grid=(batch, MOBA_W // 128),
        in_specs=[pl.BlockSpec((seq, 128), lambda b, j: (b, kb + j))],
        out_specs=pl.BlockSpec((1, nbm, 128), lambda b, j: (b, 0, j)),
        compiler_params=_params(("arbitrary", "arbitrary")),
        name="moba_kmean",
    )(proj)


def _moba_kernel(q_ref, ka_ref, va_ref, km_ref, o_ref, qa_ref, s_ref, p_ref, m_ref, acc_ref,
                 *, tq, tk, n_top, rc):
    i = pl.program_id(2)
    s0 = i * tq
    nbm = km_ref.shape[1]
    blk = lax.broadcasted_iota(jnp.int32, (nbm, 1), 0)
    cur = (s0 + lax.broadcasted_iota(jnp.int32, (1, tq), 1)) // MOBA_BLOCK
    ridx = lax.broadcasted_iota(jnp.int32, (nbm, tq), 0).astype(jnp.float32)
    for hh in range(2):
        q = q_ref[:, hh * HEAD_DIM:(hh + 1) * HEAD_DIM]
        km_hi, km_lo = _split_hi_lo(km_ref[0][:, hh * HEAD_DIM:(hh + 1) * HEAD_DIM])
        gs = _dot_nt(km_hi, q) + _dot_nt(km_lo, q)
        gs = jnp.where(blk < cur, gs, NEG)

        def body(_, carry):
            sc, sel = carry
            _, pick = _extract_max(sc, ridx)
            return jnp.where(pick, LOWEST, sc), jnp.where(pick, 1.0, sel)

        _, sel = lax.fori_loop(0, n_top, body, (gs, jnp.zeros((nbm, tq), jnp.float32)))
        open_blk = ((sel > 0.5) & (gs > NEG * 0.5)) | (blk == cur)
        bias_t = jnp.where(open_blk, 0.0, NEG)
        lead = -nbm % 128
        if lead:
            bias_t = jnp.concatenate([jnp.zeros((lead, tq), jnp.float32), bias_t], axis=0)
        bias = bias_t.T[:, lead:].astype(CDT)
        qa_ref[hh] = jnp.concatenate([q, bias], axis=1)
    m_ref[...] = jnp.full(m_ref.shape, NEG, jnp.float32)
    acc_ref[...] = jnp.zeros(acc_ref.shape, jnp.float32)

    def step(kt, masked):
        start = pl.multiple_of(kt * tk, tk)
        for hh in range(2):
            s_ref[hh] = _dot(qa_ref[hh], ka_ref[0, hh, :, pl.ds(start, tk)])
        causal = (start + lax.broadcasted_iota(jnp.int32, (1, tk), 1), s0) if masked else None
        for hh in range(2):
            _flash_tile(s_ref.at[hh], p_ref.at[hh], m_ref.at[hh], acc_ref.at[hh], rc, tq, causal=causal)
            acc_ref[hh] += _dot(p_ref[hh], va_ref[0, hh, pl.ds(start, tk), :])

    per_q = tq // tk
    per_trip = 4 * per_q

    def body(kp, carry):
        for d in range(per_trip):
            step(kp * per_trip + d, False)
        return carry

    n_full = s0 // tk
    n_trips = n_full // per_trip
    lax.fori_loop(0, n_trips, body, 0)
    done = n_trips * per_trip
    size = per_trip // 2
    while size >= per_q:
        take = ((n_full - done) // size) % 2

        @pl.when(take == 1)
        def _(size=size, first=done + ((n_full - done) // (2 * size)) * (2 * size)):
            for d in range(size):
                step(first + d, False)

        size //= 2

    for d in range(per_q):
        step(n_full + d, True)
    outs = []
    for hh in range(2):
        acc = acc_ref[hh]
        outs.append(acc[:, :HEAD_DIM] / acc[:, HEAD_DIM:])
    o_ref[...] = jnp.concatenate(outs, axis=1).astype(o_ref.dtype)


def _moba(proj, mk_aug_t, mv_aug, kmean, batch, seq):
    tq = min(1024, seq)
    tk = 512
    nq = seq // tq
    nbm = seq // MOBA_BLOCK
    n_top = min(MOBA_TOPK, nbm)
    qb = COL_MQ // 128
    aug = HEAD_DIM + nbm
    return pl.pallas_call(
        functools.partial(_moba_kernel, tq=tq, tk=tk, n_top=n_top, rc=64),
        out_shape=jax.ShapeDtypeStruct((batch * seq, MOBA_W), CDT),
        grid=(batch, MOBA_HEADS // 2, nq),
        in_specs=[pl.BlockSpec((tq, 128), lambda b, p, i: (b * nq + i, qb + p)),
                  pl.BlockSpec((1, 2, aug, seq), lambda b, p, i: (b, p, 0, 0)),
                  pl.BlockSpec((1, 2, seq, 128), lambda b, p, i: (b, p, 0, 0)),
                  pl.BlockSpec((1, nbm, 128), lambda b, p, i: (b, 0, p))],
        out_specs=pl.BlockSpec((tq, 128), lambda b, p, i: (b * nq + i, p)),
        scratch_shapes=[pltpu.VMEM((2, tq, aug), CDT), pltpu.VMEM((2, tq, tk), jnp.float32),
                        pltpu.VMEM((2, tq, tk), CDT), pltpu.VMEM((2, tq, 128), jnp.float32),
                        pltpu.VMEM((2, tq, 128), jnp.float32)],
        compiler_params=_params(("arbitrary", "arbitrary", "arbitrary")),
        name="moba",
    )(proj, mk_aug_t, mv_aug, kmean)


def _merge_kernel(oc_ref, os_ref, ow_ref, om_ref, ng_ref, gn_ref, gm_ref, x_ref, ga_ref,
                  ex_ref, wun_ref, wum_ref, wo_ref, o_ref):
    gates = _sigmoid(ng_ref[...].astype(jnp.float32))
    hi, lo = _split_hi_lo(gates)
    e = _dot(hi, ex_ref[...]) + _dot(lo, ex_ref[...])
    o_nsa = (e[:, :NSA_Q] * oc_ref[...].astype(jnp.float32)
             + e[:, NSA_Q:2 * NSA_Q] * os_ref[...].astype(jnp.float32)
             + e[:, 2 * NSA_Q:] * ow_ref[...].astype(jnp.float32))
    y = (_sigmoid(gn_ref[...].astype(jnp.float32)) * _dot(o_nsa.astype(CDT), wun_ref[...])
         + _sigmoid(gm_ref[...].astype(jnp.float32)) * _dot(om_ref[...], wum_ref[...]))
    o_ref[...] = x_ref[...] + ga_ref[0] * _dot(y.astype(CDT), wo_ref[...])


def _merge(o_c, o_s, o_w, o_m, proj, x2d, ga, w_up_nsa, w_up_moba, w_out, seq):
    n, d = x2d.shape
    tm = min(512, seq)
    per_seq = seq // tm
    ng_w = COL_GN - COL_NG
    ex = np.zeros((ng_w, 3 * NSA_Q), np.float32)
    for h in range(NSA_HEADS):
        for j in range(3):
            ex[h * 3 + j, j * NSA_Q + h * HEAD_DIM: j * NSA_Q + (h + 1) * HEAD_DIM] = 1.0
    row = lambda i: (i, 0)
    const = lambda i: (0, 0)
    return pl.pallas_call(
        _merge_kernel,
        out_shape=jax.ShapeDtypeStruct((n, d), jnp.float32),
        grid=(n // tm,),
        in_specs=[pl.BlockSpec((tm, NSA_Q), row), pl.BlockSpec((tm, NSA_Q), row),
                  pl.BlockSpec((tm, NSA_Q), row), pl.BlockSpec((tm, MOBA_W), row),
                  pl.BlockSpec((tm, ng_w), lambda i: (i, COL_NG // ng_w)),
                  pl.BlockSpec((tm, d), lambda i: (i, COL_GN // d)),
                  pl.BlockSpec((tm, d), lambda i: (i, COL_GM // d)),
                  pl.BlockSpec((tm, d), row),
                  pl.BlockSpec((1, 1, d), lambda i: (i // per_seq, 0, 0)),
                  pl.BlockSpec((ng_w, 3 * NSA_Q), const),
                  pl.BlockSpec((NSA_Q, d), const), pl.BlockSpec((MOBA_W, d), const),
                  pl.BlockSpec((d, d), const)],
        out_specs=pl.BlockSpec((tm, d), row),
        compiler_params=_params(("arbitrary",)),
        name="mixer_merge",
    )(o_c, o_s, o_w, o_m, proj, proj, proj, x2d, ga, jnp.asarray(ex, CDT),
      w_up_nsa.astype(CDT), w_up_moba.astype(CDT), w_out.astype(CDT))


def _peer_score_kernel(q_ref, k1_ref, k2_ref, cnt_ref, rk_ref, e1_ref, e2_ref):
    half = PEER_QDIM // 2
    k = PEER_TOPK
    q = q_ref[...]
    s1_all = _dot_nt(k1_ref[...], q[:, :half])
    s2_all = _dot_nt(k2_ref[...], q[:, half:])
    t = 128
    ridx = lax.broadcasted_iota(jnp.int32, (PEER_NKEYS, t), 0).astype(jnp.float32)
    unranked = jnp.full((PEER_NKEYS, t), float(k), jnp.float32)
    chunks = list(range(0, s1_all.shape[1], t))
    shifted = []
    for c0 in chunks:
        for s_all in (s1_all, s2_all):
            s = s_all[:, c0:c0 + t]
            shifted.append(s - jnp.max(s, axis=0, keepdims=True))
    xs = list(shifted)
    rks = [unranked] * len(xs)
    tops = [[] for _ in xs]
    for i in range(k):
        for n in range(len(xs)):
            m, pick = _extract_max(xs[n], ridx)
            xs[n] = jnp.where(pick, LOWEST, xs[n])
            rks[n] = jnp.where(pick, float(i), rks[n])
            tops[n].append(m)

    v2_alls, cands = [], []
    for c in range(len(chunks)):
        v1, v2_all = tops[2 * c], jnp.concatenate(tops[2 * c + 1], axis=0)
        pieces = [v1[i] + v2_all[0:k // (i + 1)] for i in range(k)]
        pad = -sum(p.shape[0] for p in pieces) % 8
        cands.append(jnp.concatenate(pieces + [jnp.full((pad, t), LOWEST, jnp.float32)], axis=0))
        v2_alls.append(v2_all)
    cidx = lax.broadcasted_iota(jnp.int32, cands[0].shape, 0).astype(jnp.float32)
    vals = [[] for _ in chunks]
    for i in range(k):
        for c in range(len(chunks)):
            m, pick = _extract_max(cands[c], cidx)
            cands[c] = jnp.where(pick, LOWEST, cands[c])
            vals[c].append(m)

    for c, c0 in enumerate(chunks):
        v1, v2_all, val = tops[2 * c], v2_alls[c], vals[c]
        tau = val[k - 1]
        z = val[0] - val[0] + 1.0
        for i in range(1, k):
            z = z + jnp.exp(val[i] - val[0])
        cnt = jnp.zeros((PEER_NKEYS, t), jnp.float32)
        for i in range(k):
            n_i = jnp.sum(jnp.where(v1[i] + v2_all >= tau, 1.0, 0.0), axis=0, keepdims=True)
            cnt = jnp.where(rks[2 * c] == float(i), n_i, cnt)
        cnt_ref[0, :, c0:c0 + t] = cnt
        rk_ref[0, :, c0:c0 + t] = rks[2 * c + 1].astype(rk_ref.dtype)
        e1_ref[0, :, c0:c0 + t] = jnp.exp(shifted[2 * c] - val[0]) / z
        e2_ref[0, :, c0:c0 + t] = jnp.exp(shifted[2 * c + 1]).astype(e2_ref.dtype)


def _peer_scores(qp, k1, k2):
    n = qp.shape[0]
    tt = 512
    f32 = jax.ShapeDtypeStruct((PEER_HEADS, PEER_NKEYS, n), jnp.float32)
    cdt = jax.ShapeDtypeStruct((PEER_HEADS, PEER_NKEYS, n), CDT)
    big = pl.BlockSpec((1, PEER_NKEYS, tt), lambda i, h: (h, 0, i))
    return pl.pallas_call(
        _peer_score_kernel,
        out_shape=[f32, cdt, f32, cdt],
        grid=(n // tt, PEER_HEADS),
        in_specs=[pl.BlockSpec((tt, PEER_QDIM), lambda i, h: (i, h)),
                  pl.BlockSpec((PEER_NKEYS, PEER_QDIM // 2), lambda i, h: (0, 0)),
                  pl.BlockSpec((PEER_NKEYS, PEER_QDIM // 2), lambda i, h: (0, 0))],
        out_specs=[big, big, big, big],
        compiler_params=_params(("arbitrary", "arbitrary")),
        name="peer_scores",
    )(qp, k1.astype(CDT), k2.astype(CDT))


def _row_to_rows(row, n):
    tile_rows = 16
    tile = jnp.broadcast_to(row, (tile_rows, row.shape[1])).astype(CDT)
    return jnp.concatenate([tile] * (n // tile_rows), axis=0)


def _peer_expert_kernel(h_ref, u_ref, vt_ref, cnt_ref, rk_ref, e1_ref, e2_ref, x_ref, ga_ref, gf_ref,
                        o_ref, acc_ref, *, eb, final_norm):
    j = pl.program_id(1)
    tt = h_ref.shape[0]

    @pl.when(j == 0)
    def _():
        acc_ref[...] = jnp.zeros(acc_ref.shape, jnp.float32)

    parts = []
    piece = 2 * PEER_NKEYS
    for e0 in range(0, eb, piece):
        sc = _dot_nt(u_ref[e0:e0 + piece, :], h_ref[...])
        ws = []
        for al in range(e0 // PEER_NKEYS, (e0 + piece) // PEER_NKEYS):
            a = j * (eb // PEER_NKEYS) + al
            w = None
            for hd in range(PEER_HEADS):
                cnt_a = _row_to_rows(cnt_ref[hd, pl.ds(a, 1), :], PEER_NKEYS)
                e1_a = _row_to_rows(e1_ref[hd, pl.ds(a, 1), :], PEER_NKEYS)
                contrib = jnp.where(rk_ref[hd] < cnt_a, e2_ref[hd] * e1_a, jnp.zeros((), CDT))
                w = contrib if w is None else w + contrib
            ws.append(w)
        parts.append(jnp.concatenate(ws, axis=0) * _gelu(sc.astype(CDT)))
    pw = jnp.concatenate(parts, axis=0)
    acc_ref[...] += _dot(vt_ref[...], pw)

    @pl.when(j == pl.num_programs(1) - 1)
    def _():
        y = x_ref[...] + ga_ref[0] * acc_ref[...].T
        if final_norm:
            ms = jnp.mean(y * y, axis=-1, keepdims=True)
            y = y * lax.rsqrt(ms + RMS_EPS) * gf_ref[...]
        o_ref[...] = y


def _peer_experts(h2, u, v_t, cnt, rk2, e1, e2, x2d, ga, g_final, final_norm, seq):
    n, d = x2d.shape
    tt = min(512, seq)
    per_seq = seq // tt
    eb = 2048
    n_blk = u.shape[0] // eb
    big = pl.BlockSpec((PEER_HEADS, PEER_NKEYS, tt), lambda i, j: (0, 0, i))
    return pl.pallas_call(
        functools.partial(_peer_expert_kernel, eb=eb, final_norm=final_norm),
        out_shape=jax.ShapeDtypeStruct((n, d), jnp.float32),
        grid=(n // tt, n_blk),
        in_specs=[pl.BlockSpec((tt, d), lambda i, j: (i, 0)),
                  pl.BlockSpec((eb, d), lambda i, j: (j, 0)),
                  pl.BlockSpec((d, eb), lambda i, j: (0, j)),
                  big, big, big, big,
                  pl.BlockSpec((tt, d), lambda i, j: (i, 0)),
                  pl.BlockSpec((1, 1, d), lambda i, j: (i // per_seq, 0, 0)),
                  pl.BlockSpec((1, d), lambda i, j: (0, 0))],
        out_specs=pl.BlockSpec((tt, d), lambda i, j: (i, 0)),
        scratch_shapes=[pltpu.VMEM((d, tt), jnp.float32)],
        compiler_params=_params(("arbitrary", "arbitrary")),
        name="peer_experts",
    )(h2, u, v_t, cnt, rk2, e1, e2, x2d, ga, g_final.reshape(1, d))


def _kv_layout_kernel(kk_ref, mk0_ref, mk1_ref, vv_ref, mv0_ref, mv1_ref,
                      ks_ref, kw_ref, mk_ref, vs_ref, vw_ref, mv_ref, *, ts, sel_per_tile):
    i = pl.program_id(1)
    pos = i * ts + lax.broadcasted_iota(jnp.int32, (1, ts), 1)
    nbm = mk_ref.shape[2] - HEAD_DIM
    sel_rows = ks_ref.shape[2] - HEAD_DIM
    sel_onehot = (lax.broadcasted_iota(jnp.int32, (sel_rows, 1), 0)
                  == (pos // SLC_LEN) % sel_per_tile).astype(CDT)
    moba_onehot = (lax.broadcasted_iota(jnp.int32, (nbm, 1), 0) == pos // MOBA_BLOCK).astype(CDT)
    ones = jnp.ones((ts, HEAD_DIM), CDT)

    kk_t = kk_ref[...].astype(jnp.float32).T
    for h in range(NSA_KV_HEADS):
        ks_ref[0, h] = jnp.concatenate([kk_t[h * HEAD_DIM:(h + 1) * HEAD_DIM].astype(CDT), sel_onehot], axis=0)
        kw_ref[0, h] = kk_t[(NSA_KV_HEADS + h) * HEAD_DIM:(NSA_KV_HEADS + h + 1) * HEAD_DIM].astype(CDT)
        vs_ref[0, h] = jnp.concatenate([vv_ref[:, h * HEAD_DIM:(h + 1) * HEAD_DIM], ones], axis=1)
        vw_ref[0, h] = jnp.concatenate(
            [vv_ref[:, (NSA_KV_HEADS + h) * HEAD_DIM:(NSA_KV_HEADS + h + 1) * HEAD_DIM], ones], axis=1)
    per_ref = mk0_ref.shape[1] // HEAD_DIM
    for half, (k_in, v_in) in enumerate(((mk0_ref, mv0_ref), (mk1_ref, mv1_ref))):
        k_t = k_in[...].astype(jnp.float32).T
        for hh in range(per_ref):
            h = half * per_ref + hh
            mk_ref[0, h] = jnp.concatenate([k_t[hh * HEAD_DIM:(hh + 1) * HEAD_DIM].astype(CDT), moba_onehot],
                                           axis=0)
            mv_ref[0, h] = jnp.concatenate([v_in[:, hh * HEAD_DIM:(hh + 1) * HEAD_DIM], ones], axis=1)


def _kv_layout(proj, batch, seq, sel_per_tile):
    ts = min(512, seq)
    nt = seq // ts
    nbm = seq // MOBA_BLOCK
    w = 4 * HEAD_DIM
    col = lambda c: (lambda b, i: (b * nt + i, c // w))
    head_major = lambda heads, rows: pl.BlockSpec((1, heads, rows, ts), lambda b, i: (b, 0, 0, i))
    token_major = lambda heads: pl.BlockSpec((1, heads, ts, 128), lambda b, i: (b, 0, i, 0))
    return pl.pallas_call(
        functools.partial(_kv_layout_kernel, ts=ts, sel_per_tile=sel_per_tile),
        out_shape=[jax.ShapeDtypeStruct((batch, NSA_KV_HEADS, 128, seq), CDT),
                   jax.ShapeDtypeStruct((batch, NSA_KV_HEADS, HEAD_DIM, seq), CDT),
                   jax.ShapeDtypeStruct((batch, MOBA_HEADS, HEAD_DIM + nbm, seq), CDT),
                   jax.ShapeDtypeStruct((batch, NSA_KV_HEADS, seq, 128), CDT),
                   jax.ShapeDtypeStruct((batch, NSA_KV_HEADS, seq, 128), CDT),
                   jax.ShapeDtypeStruct((batch, MOBA_HEADS, seq, 128), CDT)],
        grid=(batch, nt),
        in_specs=[pl.BlockSpec((ts, w), col(COL_KS)),
                  pl.BlockSpec((ts, w), col(COL_MK)), pl.BlockSpec((ts, w), col(COL_MK + w)),
                  pl.BlockSpec((ts, w), col(COL_VS)),
                  pl.BlockSpec((ts, w), col(COL_MV)), pl.BlockSpec((ts, w), col(COL_MV + w))],
        out_specs=[head_major(NSA_KV_HEADS, 128), head_major(NSA_KV_HEADS, HEAD_DIM),
                   head_major(MOBA_HEADS, HEAD_DIM + nbm), token_major(NSA_KV_HEADS),
                   token_major(NSA_KV_HEADS), token_major(MOBA_HEADS)],
        compiler_params=_params(("arbitrary", "arbitrary")),
        name="kv_layout",
    )(proj, proj, proj, proj, proj, proj)


def kernel(x, c, w_ada, b_ada, g_attn, g_ffn, w_in, cmp_pe, cmp_w1, cmp_w2, w_up_nsa, w_up_moba, w_out,
           peer_wq, peer_k1, peer_k2, peer_u, peer_v, g_final):
    batch, seq, d = x.shape
    depth = w_ada.shape[0]
    n = batch * seq
    x2d = x.reshape(n, d)
    mod = _adaln_mod(c, w_ada, b_ada)
    rope = _rope_tables(seq)
    for l in range(depth):
        sh1, sc1, ga1, sh2, sc2, ga2 = [m.reshape(batch, 1, d) for m in jnp.split(mod[l], 6, axis=-1)]
        proj = _norm_mod_matmul(x2d, g_attn[l], sc1, sh1, _reorder_w_in(w_in[l]), seq,
                                rope=rope, n_rope=ROPE_COLS // PROJ_TN)
        cmp_kv = _nsa_compress(proj, cmp_pe[l], cmp_w1[l], cmp_w2[l], batch, seq)
        o_c, selb = _nsa_compressed(proj, cmp_kv, batch, seq)
        ks_aug, kw_t, mk_aug, vs_aug, vw_aug, mv_aug = _kv_layout(proj, batch, seq, NSA_SEL_TK // SLC_LEN)
        o_s = _nsa_selected(proj, ks_aug, vs_aug, selb, batch, seq)
        o_w = _nsa_window(proj, kw_t, vw_aug, batch, seq)
        o_m = _moba(proj, mk_aug, mv_aug, _moba_kmean(proj, batch, seq), batch, seq)
        x2d = _merge(o_c, o_s, o_w, o_m, proj, x2d, ga1, w_up_nsa[l], w_up_moba[l], w_out[l], seq)
        qp, h2 = _norm_mod_matmul(x2d, g_ffn[l], sc2, sh2, peer_wq[l].astype(CDT), seq, emit_h=True)
        cnt, rk2, e1, e2 = _peer_scores(qp, peer_k1[l], peer_k2[l])
        x2d = _peer_experts(h2, peer_u[l].astype(CDT), peer_v[l].T.astype(CDT), cnt, rk2, e1, e2,
                            x2d, ga2, g_final, l == depth - 1, seq)
    return x2d.reshape(batch, seq, d)
```

```python
import functools

import jax
import jax.numpy as jnp
import numpy as np
from jax import lax
from jax.experimental import pallas as pl
from jax.experimental.pallas import tpu as pltpu

D_MODEL = 1024
HEAD_DIM = 64
ROT_DIM = HEAD_DIM // 4
ROPE_THETA = 500000.0
NSA_HEADS = 8
NSA_KV_HEADS = 2
NSA_GROUP = NSA_HEADS // NSA_KV_HEADS
CMP_LEN = 32
CMP_STRIDE = 16
CMP_HID = 2 * HEAD_DIM
SLC_LEN = 64
SLC_TOPN = 16
WIN = 512
MOBA_HEADS = 8
MOBA_BLOCK = 256
MOBA_TOPK = 3
PEER_HEADS = 8
PEER_NKEYS = 128
PEER_QDIM = 256
PEER_TOPK = 16
RMS_EPS = 1e-6
NEG = -1e30
SEL_FORCE = 1e4
LOWEST = -3.0e38
LOG2E = 1.4426950408889634

NSA_Q = NSA_HEADS * HEAD_DIM
NSA_KV = NSA_KV_HEADS * HEAD_DIM
MOBA_W = MOBA_HEADS * HEAD_DIM

CDT = jnp.bfloat16
V7X_VMEM_LIMIT = 56 * 1024 * 1024

COL_QROT = 0
COL_KS = 512
COL_KW = 640
COL_MQ = 768
COL_MK = 1280
ROPE_COLS = 1792
COL_NG = 1792
COL_GN = 2048
COL_GM = 3072
COL_QRAW = 4096
COL_KC = 4608
COL_VC = 4736
COL_VS = 4864
COL_VW = 4992
COL_MV = 5120
PROJ_COLS = 5632
PROJ_TN = 256


def _params(sem):
    return pltpu.CompilerParams(dimension_semantics=sem, vmem_limit_bytes=V7X_VMEM_LIMIT)


def _dot(a, b):
    return jnp.dot(a, b, preferred_element_type=jnp.float32)


def _dot_nt(a, b):
    return lax.dot_general(a, b, (((1,), (1,)), ((), ())), preferred_element_type=jnp.float32)


def _split_hi_lo(x):
    hi = x.astype(CDT)
    lo = (x - hi.astype(jnp.float32)).astype(CDT)
    return hi, lo


def _gelu(x):
    return 0.5 * x * (1.0 + jnp.tanh(0.7978845608028654 * (x + 0.044715 * (x * x * x))))


def _sigmoid(x):
    return 1.0 / (1.0 + jnp.exp(-x))


def _mod_kernel(c_ref, w_ref, b_ref, o_ref):
    c = c_ref[...]
    sc = c * _sigmoid(c)
    o_ref[0] = jnp.dot(sc, w_ref[0], preferred_element_type=jnp.float32,
                       precision=lax.Precision.HIGHEST) + b_ref[0]


def _adaln_mod(c, w_ada, b_ada):
    depth, d, six_d = w_ada.shape
    b = c.shape[0]
    rows = 8
    c_pad = jnp.zeros((rows, d), jnp.float32).at[:b].set(c)
    tn = 1024
    out = pl.pallas_call(
        _mod_kernel,
        out_shape=jax.ShapeDtypeStruct((depth, rows, six_d), jnp.float32),
        grid=(depth, six_d // tn),
        in_specs=[pl.BlockSpec((rows, d), lambda l, j: (0, 0)),
                  pl.BlockSpec((1, d, tn), lambda l, j: (l, 0, j)),
                  pl.BlockSpec((1, 1, tn), lambda l, j: (l, 0, j))],
        out_specs=pl.BlockSpec((1, rows, tn), lambda l, j: (l, 0, j)),
        compiler_params=_params(("arbitrary", "arbitrary")),
        name="adaln_mod",
    )(c_pad, w_ada, b_ada.reshape(depth, 1, six_d))
    return out[:, :b]


def _nmm_kernel(*refs, n_rope, emit_h, tn):
    if n_rope:
        x_ref, g_ref, sc_ref, sh_ref, w_ref, cos_ref, sa_ref, sb_ref = refs[:8]
        rest = refs[8:]
    else:
        x_ref, g_ref, sc_ref, sh_ref, w_ref = refs[:5]
        rest = refs[5:]
    if emit_h:
        o_ref, ho_ref, h_ref = rest
    else:
        o_ref, h_ref = rest
    x = x_ref[...]
    ms = jnp.mean(x * x, axis=-1, keepdims=True)
    y = x * lax.rsqrt(ms + RMS_EPS) * g_ref[...]
    h = (y * (1.0 + sc_ref[0]) + sh_ref[0]).astype(h_ref.dtype)
    h_ref[...] = h
    if emit_h:
        ho_ref[...] = h

    for j in range(w_ref.shape[1] // tn):
        acc = _dot(h_ref[...], w_ref[:, j * tn:(j + 1) * tn])
        if j < n_rope:
            cos, sa, sb = cos_ref[...], sa_ref[...], sb_ref[...]
            for c0 in range(0, tn, 128):
                a = acc[:, c0:c0 + 128]
                r = (a * cos + pltpu.roll(a, 128 - ROT_DIM // 2, axis=1) * sa
                     + pltpu.roll(a, ROT_DIM // 2, axis=1) * sb)
                o_ref[:, j * tn + c0:j * tn + c0 + 128] = r.astype(o_ref.dtype)
        else:
            o_ref[:, j * tn:(j + 1) * tn] = acc.astype(o_ref.dtype)


def _norm_mod_matmul(x2d, g, sc, sh, w, seq, rope=None, n_rope=0, emit_h=False, tn=PROJ_TN):
    n, d = x2d.shape
    cols = w.shape[1]
    tm = min(512, seq)
    per_seq = seq // tm
    in_specs = [pl.BlockSpec((tm, d), lambda i: (i, 0)),
                pl.BlockSpec((1, d), lambda i: (0, 0)),
                pl.BlockSpec((1, 1, d), lambda i: (i // per_seq, 0, 0)),
                pl.BlockSpec((1, 1, d), lambda i: (i // per_seq, 0, 0)),
                pl.BlockSpec((d, cols), lambda i: (0, 0))]
    args = [x2d, g.reshape(1, d), sc, sh, w]
    if n_rope:
        in_specs += [pl.BlockSpec((tm, 128), lambda i: (i % per_seq, 0))] * 3
        args += list(rope)
    out_shape = [jax.ShapeDtypeStruct((n, cols), CDT)]
    out_specs = [pl.BlockSpec((tm, cols), lambda i: (i, 0))]
    if emit_h:
        out_shape.append(jax.ShapeDtypeStruct((n, d), CDT))
        out_specs.append(pl.BlockSpec((tm, d), lambda i: (i, 0)))
    res = pl.pallas_call(
        functools.partial(_nmm_kernel, n_rope=n_rope, emit_h=emit_h, tn=tn),
        out_shape=out_shape,
        grid=(n // tm,),
        in_specs=in_specs,
        out_specs=out_specs,
        scratch_shapes=[pltpu.VMEM((tm, d), CDT)],
        compiler_params=_params(("arbitrary",)),
        name="norm_mod_matmul",
    )(*args)
    return res if emit_h else res[0]


def _rope_tables(seq):
    half = ROT_DIM // 2
    inv = ROPE_THETA ** (-jnp.arange(half, dtype=jnp.float32) / half)
    ang = jnp.arange(seq, dtype=jnp.float32)[:, None] * inv[None, :]
    cos, sin = jnp.cos(ang), jnp.sin(ang)
    ones = jnp.ones((seq, HEAD_DIM - ROT_DIM), jnp.float32)
    zeros = jnp.zeros((seq, HEAD_DIM - ROT_DIM), jnp.float32)
    zh = jnp.zeros((seq, half), jnp.float32)
    c64 = jnp.concatenate([cos, cos, ones], axis=1)
    sa64 = jnp.concatenate([-sin, zh, zeros], axis=1)
    sb64 = jnp.concatenate([zh, sin, zeros], axis=1)
    return tuple(jnp.concatenate([t, t], axis=1) for t in (c64, sa64, sb64))


def _reorder_w_in(w_in):
    o = 0
    pieces = {}
    for name, size in (("nq", NSA_Q), ("kc", NSA_KV), ("vc", NSA_KV), ("ks", NSA_KV), ("vs", NSA_KV),
                       ("kw", NSA_KV), ("vw", NSA_KV), ("ng", 3 * NSA_HEADS), ("mq", MOBA_W),
                       ("mk", MOBA_W), ("mv", MOBA_W), ("gn", D_MODEL), ("gm", D_MODEL)):
        pieces[name] = w_in[:, o:o + size]
        o += size
    scale = HEAD_DIM ** -0.5 * LOG2E
    pad = jnp.zeros((w_in.shape[0], COL_GN - COL_NG - 3 * NSA_HEADS), w_in.dtype)
    w = jnp.concatenate([pieces["nq"] * scale, pieces["ks"], pieces["kw"], pieces["mq"] * scale,
                         pieces["mk"], pieces["ng"], pad, pieces["gn"], pieces["gm"],
                         pieces["nq"] * scale, pieces["kc"], pieces["vc"], pieces["vs"],
                         pieces["vw"], pieces["mv"]], axis=1)
    assert w.shape[1] == PROJ_COLS
    return w.astype(CDT)


def _compress_kernel(x_ref, w1c_ref, pe_ref, w1_ref, w2_ref, o_ref):
    nb = x_ref.shape[3]
    ab = _dot(x_ref[0, 0, 0], w1c_ref[0])
    a = ab[:, :CMP_HID]
    b_next = pltpu.roll(ab[:, CMP_HID:], nb - 1, axis=0)
    bias = _dot(pe_ref[0], w1_ref[0])[0:1]
    hid = _gelu(a + b_next + bias)
    o_ref[0, 0, 0] = _dot(hid.astype(CDT), w2_ref[0]).astype(o_ref.dtype)


def _nsa_compress(proj, cmp_pe, cmp_w1, cmp_w2, batch, seq):
    nb = seq // CMP_STRIDE
    half = CMP_STRIDE * HEAD_DIM
    kcvc = proj[:, COL_KC:COL_KC + 2 * NSA_KV].reshape(batch, nb, CMP_STRIDE, 2, NSA_KV_HEADS, HEAD_DIM)
    x = jnp.transpose(kcvc, (3, 0, 4, 1, 2, 5)).reshape(2, batch, NSA_KV_HEADS, nb, half)
    w1c = jnp.concatenate([cmp_w1[:, :half], cmp_w1[:, half:]], axis=2).astype(CDT)
    pe = jnp.broadcast_to(cmp_pe.reshape(2, 1, CMP_LEN * HEAD_DIM), (2, 8, CMP_LEN * HEAD_DIM)).astype(CDT)
    return pl.pallas_call(
        _compress_kernel,
        out_shape=jax.ShapeDtypeStruct((2, batch, NSA_KV_HEADS, nb, HEAD_DIM), CDT),
        grid=(2, batch, NSA_KV_HEADS),
        in_specs=[pl.BlockSpec((1, 1, 1, nb, half), lambda w, b, k: (w, b, k, 0, 0)),
                  pl.BlockSpec((1, half, 2 * CMP_HID), lambda w, b, k: (w, 0, 0)),
                  pl.BlockSpec((1, 8, 2 * half), lambda w, b, k: (w, 0, 0)),
                  pl.BlockSpec((1, 2 * half, CMP_HID), lambda w, b, k: (w, 0, 0)),
                  pl.BlockSpec((1, CMP_HID, HEAD_DIM), lambda w, b, k: (w, 0, 0))],
        out_specs=pl.BlockSpec((1, 1, 1, nb, HEAD_DIM), lambda w, b, k: (w, b, k, 0, 0)),
        compiler_params=_params(("arbitrary", "arbitrary", "arbitrary")),
        name="nsa_compress",
    )(x, w1c, pe, cmp_w1.astype(CDT), cmp_w2.astype(CDT))


def _stack_heads(qblk):
    return jnp.concatenate([qblk[:, g * HEAD_DIM:(g + 1) * HEAD_DIM] for g in range(NSA_GROUP)], axis=0)


def _unstack_heads(o, tq):
    return jnp.concatenate([o[g * tq:(g + 1) * tq] for g in range(NSA_GROUP)], axis=1)


def _extract_max(x, ridx):
    m = jnp.max(x, axis=0, keepdims=True)
    idx = jnp.min(jnp.where(x == m, ridx, float(x.shape[0])), axis=0, keepdims=True)
    return m, ridx == idx


def _nsa_cmp_kernel(q_ref, kc_ref, vc_ref, ov_ref, oc_ref, sb_ref, imp_ref, *, tq, n_sel, n_var):
    i = pl.program_id(2)
    s0 = i * tq
    nb = kc_ref.shape[3]
    ns = ov_ref.shape[1]
    tq_col = s0 + lax.broadcasted_iota(jnp.int32, (tq, 1), 0)
    t4 = jnp.concatenate([tq_col] * NSA_GROUP, axis=0)

    def attend(nk):
        q4 = _stack_heads(q_ref[...])
        s = _dot_nt(q4, kc_ref[0, 0, 0, :nk, :])
        cend = lax.broadcasted_iota(jnp.int32, (1, nk), 1) * CMP_STRIDE + (CMP_LEN - 1)
        mask = cend <= t4
        s = jnp.where(mask, s, NEG)
        m = jnp.max(s, axis=-1, keepdims=True)
        p = jnp.where(mask, jnp.exp2(s - m), 0.0)
        p = p / jnp.maximum(jnp.sum(p, axis=-1, keepdims=True), 1e-30)
        o = _dot(p.astype(CDT), vc_ref[0, 0, 0, :nk, :])
        oc_ref[...] = _unstack_heads(o, tq).astype(oc_ref.dtype)
        psum = p[0:tq]
        for g in range(1, NSA_GROUP):
            psum = psum + p[g * tq:(g + 1) * tq]
        hi, lo = _split_hi_lo(psum)
        imp_ref[...] = _dot(hi, ov_ref[:nk, :]) + _dot(lo, ov_ref[:nk, :])

    tiles_per_var = pl.num_programs(2) // n_var
    for v in range(n_var):
        @pl.when(i // tiles_per_var == v)
        def _(v=v):
            attend(nb * (v + 1) // n_var)

    blk = lax.broadcasted_iota(jnp.int32, (ns, 1), 0)
    lanes = min(128, tq)
    ridx = lax.broadcasted_iota(jnp.int32, (ns, lanes), 0).astype(jnp.float32)
    imp_t = imp_ref[...].T
    scores, valids = [], []
    for c0 in range(0, tq, lanes):
        cur = (s0 + c0 + lax.broadcasted_iota(jnp.int32, (1, lanes), 1)) // SLC_LEN
        valid = blk <= cur
        forced = (blk == 0) | (blk == cur) | (blk == cur - 1)
        scores.append(jnp.where(valid, jnp.where(forced, SEL_FORCE, imp_t[:, c0:c0 + lanes]), NEG))
        valids.append(valid)
    sels = [jnp.zeros((ns, lanes), jnp.float32)] * len(scores)
    for _ in range(n_sel):
        for c in range(len(scores)):
            _, pick = _extract_max(scores[c], ridx)
            scores[c] = jnp.where(pick, LOWEST, scores[c])
            sels[c] = jnp.where(pick, 1.0, sels[c])
    bias = jnp.concatenate([jnp.where((sel > 0.5) & valid, 0.0, NEG) for sel, valid in zip(sels, valids)],
                           axis=1).T
    extra = sb_ref.shape[3] - ns
    if extra:
        bias = jnp.concatenate([bias, jnp.full((tq, extra), NEG, jnp.float32)], axis=1)
    sb_ref[0, 0] = bias.astype(sb_ref.dtype)


def _nsa_compressed(proj, cmp_kv, batch, seq):
    tq = 256
    nq = seq // tq
    nb = seq // CMP_STRIDE
    ns = seq // SLC_LEN
    n_sel = min(SLC_TOPN, ns)
    c_start = np.arange(nb) * CMP_STRIDE
    s_start = np.arange(ns) * SLC_LEN
    ov = np.maximum(np.minimum(c_start[:, None] + CMP_LEN, s_start[None, :] + SLC_LEN)
                    - np.maximum(c_start[:, None], s_start[None, :]), 0).astype(np.float32) / CMP_LEN
    ov[nb - 1] = 0.0
    qb = COL_QRAW // (NSA_GROUP * HEAD_DIM)
    return pl.pallas_call(
        functools.partial(_nsa_cmp_kernel, tq=tq, n_sel=n_sel,
                          n_var=4 if (nb % (4 * 128) == 0 and nq % 4 == 0) else 1),
        out_shape=[jax.ShapeDtypeStruct((batch * seq, NSA_Q), CDT),
                   jax.ShapeDtypeStruct((batch, NSA_KV_HEADS, seq, max(ns, 128)), CDT)],
        grid=(batch, NSA_KV_HEADS, nq),
        in_specs=[pl.BlockSpec((tq, NSA_GROUP * HEAD_DIM), lambda b, k, i: (b * nq + i, qb + k)),
                  pl.BlockSpec((1, 1, 1, nb, HEAD_DIM), lambda b, k, i: (0, b, k, 0, 0)),
                  pl.BlockSpec((1, 1, 1, nb, HEAD_DIM), lambda b, k, i: (1, b, k, 0, 0)),
                  pl.BlockSpec((nb, ns), lambda b, k, i: (0, 0))],
        out_specs=[pl.BlockSpec((tq, NSA_GROUP * HEAD_DIM), lambda b, k, i: (b * nq + i, k)),
                   pl.BlockSpec((1, 1, tq, max(ns, 128)), lambda b, k, i: (b, k, i, 0))],
        scratch_shapes=[pltpu.VMEM((tq, ns), jnp.float32)],
        compiler_params=_params(("arbitrary", "arbitrary", "arbitrary")),
        name="nsa_compressed",
    )(proj, cmp_kv, cmp_kv, jnp.asarray(ov, CDT))


def _flash_tile(s_ref, p_ref, m_ref, acc_ref, rc, tq, causal=None):
    rows, tk = s_ref.shape

    def tile(r0, c0):
        s = s_ref[r0:r0 + rc, c0:c0 + 128]
        if causal is not None:
            kpos, t0 = causal
            t = t0 + (r0 % tq) + lax.broadcasted_iota(jnp.int32, (rc, 1), 0)
            s = jnp.where(kpos[:, c0:c0 + 128] <= t, s, NEG)
        return s

    for r0 in range(0, rows, rc):
        mx = tile(r0, 0)
        for c0 in range(128, tk, 128):
            mx = jnp.maximum(mx, tile(r0, c0))
        m_prev = m_ref[r0:r0 + rc, :]
        m_new = jnp.maximum(m_prev, jnp.max(mx, axis=-1, keepdims=True))
        acc_ref[r0:r0 + rc, :] = acc_ref[r0:r0 + rc, :] * jnp.exp2(m_prev - m_new)
        m_ref[r0:r0 + rc, :] = m_new
    for r0 in range(0, rows, rc):
        m_new = m_ref[r0:r0 + rc, :]
        for c0 in range(0, tk, 128):
            p_ref[r0:r0 + rc, c0:c0 + 128] = jnp.exp2(tile(r0, c0) - m_new).astype(p_ref.dtype)


def _nsa_sel_kernel(q_ref, kt_ref, va_ref, sb_ref, o_ref,
                    qp_ref, q4_ref, sbf_ref, s_ref, p_ref, m_ref, acc_ref, *, tq, tk, rc):
    i = pl.program_id(2)
    s0 = i * tq
    nsp = sb_ref.shape[3]
    per_tile = tk // SLC_LEN
    q4 = _stack_heads(q_ref[...])
    qp_ref[...] = jnp.concatenate([q4, jnp.zeros_like(q4)], axis=1)
    sbf_ref[...] = sb_ref[0, 0].astype(jnp.float32)
    lane = lax.broadcasted_iota(jnp.int32, (tq, 128), 1)
    bias_lanes = (lane >= HEAD_DIM) & (lane < HEAD_DIM + per_tile)
    m_ref[...] = jnp.full(m_ref.shape, NEG, jnp.float32)
    acc_ref[...] = jnp.zeros(acc_ref.shape, jnp.float32)

    def step(kt, masked):
        start = pl.multiple_of(kt * tk, tk)
        shift = (HEAD_DIM - kt * per_tile + nsp) % nsp
        rolled = pltpu.roll(sbf_ref[...], shift, axis=1)[:, :128].astype(q4_ref.dtype)
        for g in range(NSA_GROUP):
            q4_ref[g * tq:(g + 1) * tq, :] = jnp.where(bias_lanes, rolled, qp_ref[g * tq:(g + 1) * tq, :])
        s_ref[...] = _dot(q4_ref[...], kt_ref[0, 0, :, pl.ds(start, tk)])
        causal = (start + lax.broadcasted_iota(jnp.int32, (1, tk), 1), s0) if masked else None
        _flash_tile(s_ref, p_ref, m_ref, acc_ref, rc, tq, causal=causal)
        acc_ref[...] += _dot(p_ref[...], va_ref[0, 0, pl.ds(start, tk), :])

    unroll = 8

    def body(kp, carry):
        for d in range(unroll):
            step(unroll * kp + d, False)
        return carry

    n_full = s0 // tk
    n_trips = n_full // unroll
    lax.fori_loop(0, n_trips, body, 0)
    done = n_trips * unroll
    size = unroll // 2
    while size >= 1:
        take = ((n_full - done) // size) % 2 if size > 1 else (n_full - done) % 2

        @pl.when(take == 1)
        def _(size=size, first=done + ((n_full - done) // (2 * size)) * (2 * size)):
            for d in range(size):
                step(first + d, False)

        size //= 2
    step(n_full, True)
    acc = acc_ref[...]
    o = acc[:, :HEAD_DIM] / acc[:, HEAD_DIM:]
    o_ref[...] = _unstack_heads(o, tq).astype(o_ref.dtype)


NSA_SEL_TK = 512


def _nsa_selected(proj, k_aug, vs_aug, selb, batch, seq):
    tq = 256
    tk = NSA_SEL_TK
    nq = seq // tq
    nsp = selb.shape[3]
    qb = COL_QROT // (NSA_GROUP * HEAD_DIM)
    rows = NSA_GROUP * tq
    return pl.pallas_call(
        functools.partial(_nsa_sel_kernel, tq=tq, tk=tk, rc=64),
        out_shape=jax.ShapeDtypeStruct((batch * seq, NSA_Q), CDT),
        grid=(batch, NSA_KV_HEADS, nq),
        in_specs=[pl.BlockSpec((tq, NSA_GROUP * HEAD_DIM), lambda b, k, i: (b * nq + i, qb + k)),
                  pl.BlockSpec((1, 1, 128, seq), lambda b, k, i: (b, k, 0, 0)),
                  pl.BlockSpec((1, 1, seq, 128), lambda b, k, i: (b, k, 0, 0)),
                  pl.BlockSpec((1, 1, tq, nsp), lambda b, k, i: (b, k, i, 0))],
        out_specs=pl.BlockSpec((tq, NSA_GROUP * HEAD_DIM), lambda b, k, i: (b * nq + i, k)),
        scratch_shapes=[pltpu.VMEM((rows, 128), CDT), pltpu.VMEM((rows, 128), CDT),
                        pltpu.VMEM((tq, nsp), jnp.float32), pltpu.VMEM((rows, tk), jnp.float32),
                        pltpu.VMEM((rows, tk), CDT),
                        pltpu.VMEM((rows, 128), jnp.float32), pltpu.VMEM((rows, 128), jnp.float32)],
        compiler_params=_params(("arbitrary", "arbitrary", "arbitrary")),
        name="nsa_selected",
    )(proj, k_aug, vs_aug, selb)


def _nsa_win_kernel(q_ref, kt_ref, va_ref, o_ref, q4_ref, s_ref, p_ref, *, tq, rc):
    i = pl.program_id(2)
    s0 = i * tq
    span = WIN + tq
    rows = NSA_GROUP * tq
    start = pl.multiple_of(jnp.maximum(s0 - WIN, 0), tq)
    q4_ref[...] = _stack_heads(q_ref[...])
    s_ref[...] = _dot(q4_ref[...], kt_ref[0, 0, :, pl.ds(start, span)])
    kpos = start + lax.broadcasted_iota(jnp.int32, (1, span), 1)
    for r0 in range(0, rows, rc):
        t = s0 + (r0 % tq) + lax.broadcasted_iota(jnp.int32, (rc, 1), 0)
        tiles = []
        for c0 in range(0, span, 128):
            kp = kpos[:, c0:c0 + 128]
            tiles.append(jnp.where((kp <= t) & (kp > t - WIN), s_ref[r0:r0 + rc, c0:c0 + 128], NEG))
        mx = tiles[0]
        for tile in tiles[1:]:
            mx = jnp.maximum(mx, tile)
        m = jnp.max(mx, axis=-1, keepdims=True)
        for c, tile in enumerate(tiles):
            p_ref[r0:r0 + rc, c * 128:(c + 1) * 128] = jnp.exp2(tile - m).astype(p_ref.dtype)
    acc = _dot(p_ref[...], va_ref[0, 0, pl.ds(start, span), :])
    o = acc[:, :HEAD_DIM] / acc[:, HEAD_DIM:]
    o_ref[...] = _unstack_heads(o, tq).astype(o_ref.dtype)


def _nsa_window(proj, kw_t, vw_aug, batch, seq):
    tq = 256
    nq = seq // tq
    qb = COL_QROT // (NSA_GROUP * HEAD_DIM)
    rows = NSA_GROUP * tq
    span = WIN + tq
    return pl.pallas_call(
        functools.partial(_nsa_win_kernel, tq=tq, rc=32),
        out_shape=jax.ShapeDtypeStruct((batch * seq, NSA_Q), CDT),
        grid=(batch, NSA_KV_HEADS, nq),
        in_specs=[pl.BlockSpec((tq, NSA_GROUP * HEAD_DIM), lambda b, k, i: (b * nq + i, qb + k)),
                  pl.BlockSpec((1, 1, HEAD_DIM, seq), lambda b, k, i: (b, k, 0, 0)),
                  pl.BlockSpec((1, 1, seq, 128), lambda b, k, i: (b, k, 0, 0))],
        out_specs=pl.BlockSpec((tq, NSA_GROUP * HEAD_DIM), lambda b, k, i: (b * nq + i, k)),
        scratch_shapes=[pltpu.VMEM((rows, HEAD_DIM), CDT), pltpu.VMEM((rows, span), jnp.float32),
                        pltpu.VMEM((rows, span), CDT)],
        compiler_params=_params(("arbitrary", "arbitrary", "arbitrary")),
        name="nsa_window",
    )(proj, kw_t, vw_aug)


def _moba_mean_kernel(k_ref, o_ref):
    seq, w = k_ref.shape
    nbm = seq // MOBA_BLOCK
    k = k_ref[...].astype(jnp.float32).reshape(nbm, MOBA_BLOCK, w)
    o_ref[0] = jnp.sum(k, axis=1) * (1.0 / MOBA_BLOCK)


def _moba_kmean(proj, batch, seq):
    nbm = seq // MOBA_BLOCK
    kb = COL_MK // 128
    return pl.pallas_call(
        _moba_mean_kernel,
        out_shape=jax.ShapeDtypeStruct((batch, nbm, MOBA_W), jnp.float32),
        grid=(batch, MOBA_W // 128),
        in_specs=[pl.BlockSpec((seq, 128), lambda b, j: (b, kb + j))],
        out_specs=pl.BlockSpec((1, nbm, 128), lambda b, j: (b, 0, j)),
        compiler_params=_params(("arbitrary", "arbitrary")),
        name="moba_kmean",
    )(proj)


def _moba_kernel(q_ref, ka_ref, va_ref, km_ref, o_ref, qa_ref, s_ref, p_ref, m_ref, acc_ref,
                 *, tq, tk, n_top, rc):
    i = pl.program_id(2)
    s0 = i * tq
    nbm = km_ref.shape[1]
    blk = lax.broadcasted_iota(jnp.int32, (nbm, 1), 0)
    cur = (s0 + lax.broadcasted_iota(jnp.int32, (1, tq), 1)) // MOBA_BLOCK
    ridx = lax.broadcasted_iota(jnp.int32, (nbm, tq), 0).astype(jnp.float32)
    for hh in range(2):
        q = q_ref[:, hh * HEAD_DIM:(hh + 1) * HEAD_DIM]
        km_hi, km_lo = _split_hi_lo(km_ref[0][:, hh * HEAD_DIM:(hh + 1) * HEAD_DIM])
        gs = _dot_nt(km_hi, q) + _dot_nt(km_lo, q)
        gs = jnp.where(blk < cur, gs, NEG)

        def body(_, carry):
            sc, sel = carry
            _, pick = _extract_max(sc, ridx)
            return jnp.where(pick, LOWEST, sc), jnp.where(pick, 1.0, sel)

        _, sel = lax.fori_loop(0, n_top, body, (gs, jnp.zeros((nbm, tq), jnp.float32)))
        open_blk = ((sel > 0.5) & (gs > NEG * 0.5)) | (blk == cur)
        bias_t = jnp.where(open_blk, 0.0, NEG)
        lead = -nbm % 128
        if lead:
            bias_t = jnp.concatenate([jnp.zeros((lead, tq), jnp.float32), bias_t], axis=0)
        bias = bias_t.T[:, lead:].astype(CDT)
        qa_ref[hh] = jnp.concatenate([q, bias], axis=1)
    m_ref[...] = jnp.full(m_ref.shape, NEG, jnp.float32)
    acc_ref[...] = jnp.zeros(acc_ref.shape, jnp.float32)

    def step(kt, masked):
        start = pl.multiple_of(kt * tk, tk)
        for hh in range(2):
            s_ref[hh] = _dot(qa_ref[hh], ka_ref[0, hh, :, pl.ds(start, tk)])
        causal = (start + lax.broadcasted_iota(jnp.int32, (1, tk), 1), s0) if masked else None
        for hh in range(2):
            _flash_tile(s_ref.at[hh], p_ref.at[hh], m_ref.at[hh], acc_ref.at[hh], rc, tq, causal=causal)
            acc_ref[hh] += _dot(p_ref[hh], va_ref[0, hh, pl.ds(start, tk), :])

    per_q = tq // tk
    per_trip = 4 * per_q

    def body(kp, carry):
        for d in range(per_trip):
            step(kp * per_trip + d, False)
        return carry

    n_full = s0 // tk
    n_trips = n_full // per_trip
    lax.fori_loop(0, n_trips, body, 0)
    done = n_trips * per_trip
    size = per_trip // 2
    while size >= per_q:
        take = ((n_full - done) // size) % 2

        @pl.when(take == 1)
        def _(size=size, first=done + ((n_full - done) // (2 * size)) * (2 * size)):
            for d in range(size):
                step(first + d, False)

        size //= 2

    for d in range(per_q):
        step(n_full + d, True)
    outs = []
    for hh in range(2):
        acc = acc_ref[hh]
        outs.append(acc[:, :HEAD_DIM] / acc[:, HEAD_DIM:])
    o_ref[...] = jnp.concatenate(outs, axis=1).astype(o_ref.dtype)


def _moba(proj, mk_aug_t, mv_aug, kmean, batch, seq):
    tq = min(1024, seq)
    tk = 512
    nq = seq // tq
    nbm = seq // MOBA_BLOCK
    n_top = min(MOBA_TOPK, nbm)
    qb = COL_MQ // 128
    aug = HEAD_DIM + nbm
    return pl.pallas_call(
        functools.partial(_moba_kernel, tq=tq, tk=tk, n_top=n_top, rc=64),
        out_shape=jax.ShapeDtypeStruct((batch * seq, MOBA_W), CDT),
        grid=(batch, MOBA_HEADS // 2, nq),
        in_specs=[pl.BlockSpec((tq, 128), lambda b, p, i: (b * nq + i, qb + p)),
                  pl.BlockSpec((1, 2, aug, seq), lambda b, p, i: (b, p, 0, 0)),
                  pl.BlockSpec((1, 2, seq, 128), lambda b, p, i: (b, p, 0, 0)),
                  pl.BlockSpec((1, nbm, 128), lambda b, p, i: (b, 0, p))],
        out_specs=pl.BlockSpec((tq, 128), lambda b, p, i: (b * nq + i, p)),
        scratch_shapes=[pltpu.VMEM((2, tq, aug), CDT), pltpu.VMEM((2, tq, tk), jnp.float32),
                        pltpu.VMEM((2, tq, tk), CDT), pltpu.VMEM((2, tq, 128), jnp.float32),
                        pltpu.VMEM((2, tq, 128), jnp.float32)],
        compiler_params=_params(("arbitrary", "arbitrary", "arbitrary")),
        name="moba",
    )(proj, mk_aug_t, mv_aug, kmean)


def _merge_kernel(oc_ref, os_ref, ow_ref, om_ref, ng_ref, gn_ref, gm_ref, x_ref, ga_ref,
                  ex_ref, wun_ref, wum_ref, wo_ref, o_ref):
    gates = _sigmoid(ng_ref[...].astype(jnp.float32))
    hi, lo = _split_hi_lo(gates)
    e = _dot(hi, ex_ref[...]) + _dot(lo, ex_ref[...])
    o_nsa = (e[:, :NSA_Q] * oc_ref[...].astype(jnp.float32)
             + e[:, NSA_Q:2 * NSA_Q] * os_ref[...].astype(jnp.float32)
             + e[:, 2 * NSA_Q:] * ow_ref[...].astype(jnp.float32))
    y = (_sigmoid(gn_ref[...].astype(jnp.float32)) * _dot(o_nsa.astype(CDT), wun_ref[...])
         + _sigmoid(gm_ref[...].astype(jnp.float32)) * _dot(om_ref[...], wum_ref[...]))
    o_ref[...] = x_ref[...] + ga_ref[0] * _dot(y.astype(CDT), wo_ref[...])


def _merge(o_c, o_s, o_w, o_m, proj, x2d, ga, w_up_nsa, w_up_moba, w_out, seq):
    n, d = x2d.shape
    tm = min(512, seq)
    per_seq = seq // tm
    ng_w = COL_GN - COL_NG
    ex = np.zeros((ng_w, 3 * NSA_Q), np.float32)
    for h in range(NSA_HEADS):
        for j in range(3):
            ex[h * 3 + j, j * NSA_Q + h * HEAD_DIM: j * NSA_Q + (h + 1) * HEAD_DIM] = 1.0
    row = lambda i: (i, 0)
    const = lambda i: (0, 0)
    return pl.pallas_call(
        _merge_kernel,
        out_shape=jax.ShapeDtypeStruct((n, d), jnp.float32),
        grid=(n // tm,),
        in_specs=[pl.BlockSpec((tm, NSA_Q), row), pl.BlockSpec((tm, NSA_Q), row),
                  pl.BlockSpec((tm, NSA_Q), row), pl.BlockSpec((tm, MOBA_W), row),
                  pl.BlockSpec((tm, ng_w), lambda i: (i, COL_NG // ng_w)),
                  pl.BlockSpec((tm, d), lambda i: (i, COL_GN // d)),
                  pl.BlockSpec((tm, d), lambda i: (i, COL_GM // d)),
                  pl.BlockSpec((tm, d), row),
                  pl.BlockSpec((1, 1, d), lambda i: (i // per_seq, 0, 0)),
                  pl.BlockSpec((ng_w, 3 * NSA_Q), const),
                  pl.BlockSpec((NSA_Q, d), const), pl.BlockSpec((MOBA_W, d), const),
                  pl.BlockSpec((d, d), const)],
        out_specs=pl.BlockSpec((tm, d), row),
        compiler_params=_params(("arbitrary",)),
        name="mixer_merge",
    )(o_c, o_s, o_w, o_m, proj, proj, proj, x2d, ga, jnp.asarray(ex, CDT),
      w_up_nsa.astype(CDT), w_up_moba.astype(CDT), w_out.astype(CDT))


def _peer_score_kernel(q_ref, k1_ref, k2_ref, cnt_ref, rk_ref, e1_ref, e2_ref):
    half = PEER_QDIM // 2
    k = PEER_TOPK
    q = q_ref[...]
    s1_all = _dot_nt(k1_ref[...], q[:, :half])
    s2_all = _dot_nt(k2_ref[...], q[:, half:])
    t = 128
    ridx = lax.broadcasted_iota(jnp.int32, (PEER_NKEYS, t), 0).astype(jnp.float32)
    unranked = jnp.full((PEER_NKEYS, t), float(k), jnp.float32)
    chunks = list(range(0, s1_all.shape[1], t))
    shifted = []
    for c0 in chunks:
        for s_all in (s1_all, s2_all):
            s = s_all[:, c0:c0 + t]
            shifted.append(s - jnp.max(s, axis=0, keepdims=True))
    xs = list(shifted)
    rks = [unranked] * len(xs)
    tops = [[] for _ in xs]
    for i in range(k):
        for n in range(len(xs)):
            m, pick = _extract_max(xs[n], ridx)
            xs[n] = jnp.where(pick, LOWEST, xs[n])
            rks[n] = jnp.where(pick, float(i), rks[n])
            tops[n].append(m)

    v2_alls, cands = [], []
    for c in range(len(chunks)):
        v1, v2_all = tops[2 * c], jnp.concatenate(tops[2 * c + 1], axis=0)
        pieces = [v1[i] + v2_all[0:k // (i + 1)] for i in range(k)]
        pad = -sum(p.shape[0] for p in pieces) % 8
        cands.append(jnp.concatenate(pieces + [jnp.full((pad, t), LOWEST, jnp.float32)], axis=0))
        v2_alls.append(v2_all)
    cidx = lax.broadcasted_iota(jnp.int32, cands[0].shape, 0).astype(jnp.float32)
    vals = [[] for _ in chunks]
    for i in range(k):
        for c in range(len(chunks)):
            m, pick = _extract_max(cands[c], cidx)
            cands[c] = jnp.where(pick, LOWEST, cands[c])
            vals[c].append(m)

    for c, c0 in enumerate(chunks):
        v1, v2_all, val = tops[2 * c], v2_alls[c], vals[c]
        tau = val[k - 1]
        z = val[0] - val[0] + 1.0
        for i in range(1, k):
            z = z + jnp.exp(val[i] - val[0])
        cnt = jnp.zeros((PEER_NKEYS, t), jnp.float32)
        for i in range(k):
            n_i = jnp.sum(jnp.where(v1[i] + v2_all >= tau, 1.0, 0.0), axis=0, keepdims=True)
            cnt = jnp.where(rks[2 * c] == float(i), n_i, cnt)
        cnt_ref[0, :, c0:c0 + t] = cnt
        rk_ref[0, :, c0:c0 + t] = rks[2 * c + 1].astype(rk_ref.dtype)
        e1_ref[0, :, c0:c0 + t] = jnp.exp(shifted[2 * c] - val[0]) / z
        e2_ref[0, :, c0:c0 + t] = jnp.exp(shifted[2 * c + 1]).astype(e2_ref.dtype)


def _peer_scores(qp, k1, k2):
    n = qp.shape[0]
    tt = 1024
    f32 = jax.ShapeDtypeStruct((PEER_HEADS, PEER_NKEYS, n), jnp.float32)
    cdt = jax.ShapeDtypeStruct((PEER_HEADS, PEER_NKEYS, n), CDT)
    big = pl.BlockSpec((1, PEER_NKEYS, tt), lambda i, h: (h, 0, i))
    return pl.pallas_call(
        _peer_score_kernel,
        out_shape=[f32, cdt, f32, cdt],
        grid=(n // tt, PEER_HEADS),
        in_specs=[pl.BlockSpec((tt, PEER_QDIM), lambda i, h: (i, h)),
                  pl.BlockSpec((PEER_NKEYS, PEER_QDIM // 2), lambda i, h: (0, 0)),
                  pl.BlockSpec((PEER_NKEYS, PEER_QDIM // 2), lambda i, h: (0, 0))],
        out_specs=[big, big, big, big],
        compiler_params=_params(("arbitrary", "arbitrary")),
        name="peer_scores",
    )(qp, k1.astype(CDT), k2.astype(CDT))


def _row_to_rows(row, n):
    tile_rows = 16
    tile = jnp.broadcast_to(row, (tile_rows, row.shape[1])).astype(CDT)
    return jnp.concatenate([tile] * (n // tile_rows), axis=0)


def _peer_expert_kernel(h_ref, u_ref, vt_ref, cnt_ref, rk_ref, e1_ref, e2_ref, x_ref, ga_ref, gf_ref,
                        o_ref, acc_ref, *, eb, final_norm):
    j = pl.program_id(1)
    tt = h_ref.shape[0]

    @pl.when(j == 0)
    def _():
        acc_ref[...] = jnp.zeros(acc_ref.shape, jnp.float32)

    parts = []
    piece = 2 * PEER_NKEYS
    for e0 in range(0, eb, piece):
        sc = _dot_nt(u_ref[e0:e0 + piece, :], h_ref[...])
        ws = []
        for al in range(e0 // PEER_NKEYS, (e0 + piece) // PEER_NKEYS):
            a = j * (eb // PEER_NKEYS) + al
            w = None
            for hd in range(PEER_HEADS):
                cnt_a = _row_to_rows(cnt_ref[hd, pl.ds(a, 1), :], PEER_NKEYS)
                e1_a = _row_to_rows(e1_ref[hd, pl.ds(a, 1), :], PEER_NKEYS)
                contrib = jnp.where(rk_ref[hd] < cnt_a, e2_ref[hd] * e1_a, jnp.zeros((), CDT))
                w = contrib if w is None else w + contrib
            ws.append(w)
        parts.append(jnp.concatenate(ws, axis=0) * _gelu(sc.astype(CDT)))
    pw = jnp.concatenate(parts, axis=0)
    acc_ref[...] += _dot(vt_ref[...], pw)

    @pl.when(j == pl.num_programs(1) - 1)
    def _():
        y = x_ref[...] + ga_ref[0] * acc_ref[...].T
        if final_norm:
            ms = jnp.mean(y * y, axis=-1, keepdims=True)
            y = y * lax.rsqrt(ms + RMS_EPS) * gf_ref[...]
        o_ref[...] = y


def _peer_experts(h2, u, v_t, cnt, rk2, e1, e2, x2d, ga, g_final, final_norm, seq):
    n, d = x2d.shape
    tt = min(512, seq)
    per_seq = seq // tt
    eb = 2048
    n_blk = u.shape[0] // eb
    big = pl.BlockSpec((PEER_HEADS, PEER_NKEYS, tt), lambda i, j: (0, 0, i))
    return pl.pallas_call(
        functools.partial(_peer_expert_kernel, eb=eb, final_norm=final_norm),
        out_shape=jax.ShapeDtypeStruct((n, d), jnp.float32),
        grid=(n // tt, n_blk),
        in_specs=[pl.BlockSpec((tt, d), lambda i, j: (i, 0)),
                  pl.BlockSpec((eb, d), lambda i, j: (j, 0)),
                  pl.BlockSpec((d, eb), lambda i, j: (0, j)),
                  big, big, big, big,
                  pl.BlockSpec((tt, d), lambda i, j: (i, 0)),
                  pl.BlockSpec((1, 1, d), lambda i, j: (i // per_seq, 0, 0)),
                  pl.BlockSpec((1, d), lambda i, j: (0, 0))],
        out_specs=pl.BlockSpec((tt, d), lambda i, j: (i, 0)),
        scratch_shapes=[pltpu.VMEM((d, tt), jnp.float32)],
        compiler_params=_params(("arbitrary", "arbitrary")),
        name="peer_experts",
    )(h2, u, v_t, cnt, rk2, e1, e2, x2d, ga, g_final.reshape(1, d))


def _kv_layout_kernel(kk_ref, mk0_ref, mk1_ref, vv_ref, mv0_ref, mv1_ref,
                      ks_ref, kw_ref, mk_ref, vs_ref, vw_ref, mv_ref, *, ts, sel_per_tile):
    i = pl.program_id(1)
    pos = i * ts + lax.broadcasted_iota(jnp.int32, (1, ts), 1)
    nbm = mk_ref.shape[2] - HEAD_DIM
    sel_rows = ks_ref.shape[2] - HEAD_DIM
    sel_onehot = (lax.broadcasted_iota(jnp.int32, (sel_rows, 1), 0)
                  == (pos // SLC_LEN) % sel_per_tile).astype(CDT)
    moba_onehot = (lax.broadcasted_iota(jnp.int32, (nbm, 1), 0) == pos // MOBA_BLOCK).astype(CDT)
    ones = jnp.ones((ts, HEAD_DIM), CDT)

    kk_t = kk_ref[...].astype(jnp.float32).T
    for h in range(NSA_KV_HEADS):
        ks_ref[0, h] = jnp.concatenate([kk_t[h * HEAD_DIM:(h + 1) * HEAD_DIM].astype(CDT), sel_onehot], axis=0)
        kw_ref[0, h] = kk_t[(NSA_KV_HEADS + h) * HEAD_DIM:(NSA_KV_HEADS + h + 1) * HEAD_DIM].astype(CDT)
        vs_ref[0, h] = jnp.concatenate([vv_ref[:, h * HEAD_DIM:(h + 1) * HEAD_DIM], ones], axis=1)
        vw_ref[0, h] = jnp.concatenate(
            [vv_ref[:, (NSA_KV_HEADS + h) * HEAD_DIM:(NSA_KV_HEADS + h + 1) * HEAD_DIM], ones], axis=1)
    per_ref = mk0_ref.shape[1] // HEAD_DIM
    for half, (k_in, v_in) in enumerate(((mk0_ref, mv0_ref), (mk1_ref, mv1_ref))):
        k_t = k_in[...].astype(jnp.float32).T
        for hh in range(per_ref):
            h = half * per_ref + hh
            mk_ref[0, h] = jnp.concatenate([k_t[hh * HEAD_DIM:(hh + 1) * HEAD_DIM].astype(CDT), moba_onehot],
                                           axis=0)
            mv_ref[0, h] = jnp.concatenate([v_in[:, hh * HEAD_DIM:(hh + 1) * HEAD_DIM], ones], axis=1)


def _kv_layout(proj, batch, seq, sel_per_tile):
    ts = min(512, seq)
    nt = seq // ts
    nbm = seq // MOBA_BLOCK
    w = 4 * HEAD_DIM
    col = lambda c: (lambda b, i: (b * nt + i, c // w))
    head_major = lambda heads, rows: pl.BlockSpec((1, heads, rows, ts), lambda b, i: (b, 0, 0, i))
    token_major = lambda heads: pl.BlockSpec((1, heads, ts, 128), lambda b, i: (b, 0, i, 0))
    return pl.pallas_call(
        functools.partial(_kv_layout_kernel, ts=ts, sel_per_tile=sel_per_tile),
        out_shape=[jax.ShapeDtypeStruct((batch, NSA_KV_HEADS, 128, seq), CDT),
                   jax.ShapeDtypeStruct((batch, NSA_KV_HEADS, HEAD_DIM, seq), CDT),
                   jax.ShapeDtypeStruct((batch, MOBA_HEADS, HEAD_DIM + nbm, seq), CDT),
                   jax.ShapeDtypeStruct((batch, NSA_KV_HEADS, seq, 128), CDT),
                   jax.ShapeDtypeStruct((batch, NSA_KV_HEADS, seq, 128), CDT),
                   jax.ShapeDtypeStruct((batch, MOBA_HEADS, seq, 128), CDT)],
        grid=(batch, nt),
        in_specs=[pl.BlockSpec((ts, w), col(COL_KS)),
                  pl.BlockSpec((ts, w), col(COL_MK)), pl.BlockSpec((ts, w), col(COL_MK + w)),
                  pl.BlockSpec((ts, w), col(COL_VS)),
                  pl.BlockSpec((ts, w), col(COL_MV)), pl.BlockSpec((ts, w), col(COL_MV + w))],
        out_specs=[head_major(NSA_KV_HEADS, 128), head_major(NSA_KV_HEADS, HEAD_DIM),
                   head_major(MOBA_HEADS, HEAD_DIM + nbm), token_major(NSA_KV_HEADS),
                   token_major(NSA_KV_HEADS), token_major(MOBA_HEADS)],
        compiler_params=_params(("arbitrary", "arbitrary")),
        name="kv_layout",
    )(proj, proj, proj, proj, proj, proj)


def kernel(x, c, w_ada, b_ada, g_attn, g_ffn, w_in, cmp_pe, cmp_w1, cmp_w2, w_up_nsa, w_up_moba, w_out,
           peer_wq, peer_k1, peer_k2, peer_u, peer_v, g_final):
    batch, seq, d = x.shape
    depth = w_ada.shape[0]
    n = batch * seq
    x2d = x.reshape(n, d)
    mod = _adaln_mod(c, w_ada, b_ada)
    rope = _rope_tables(seq)
    for l in range(depth):
        sh1, sc1, ga1, sh2, sc2, ga2 = [m.reshape(batch, 1, d) for m in jnp.split(mod[l], 6, axis=-1)]
        proj = _norm_mod_matmul(x2d, g_attn[l], sc1, sh1, _reorder_w_in(w_in[l]), seq,
                                rope=rope, n_rope=ROPE_COLS // PROJ_TN)
        cmp_kv = _nsa_compress(proj, cmp_pe[l], cmp_w1[l], cmp_w2[l], batch, seq)
        o_c, selb = _nsa_compressed(proj, cmp_kv, batch, seq)
        ks_aug, kw_t, mk_aug, vs_aug, vw_aug, mv_aug = _kv_layout(proj, batch, seq, NSA_SEL_TK // SLC_LEN)
        o_s = _nsa_selected(proj, ks_aug, vs_aug, selb, batch, seq)
        o_w = _nsa_window(proj, kw_t, vw_aug, batch, seq)
        o_m = _moba(proj, mk_aug, mv_aug, _moba_kmean(proj, batch, seq), batch, seq)
        x2d = _merge(o_c, o_s, o_w, o_m, proj, x2d, ga1, w_up_nsa[l], w_up_moba[l], w_out[l], seq)
        qp, h2 = _norm_mod_matmul(x2d, g_ffn[l], sc2, sh2, peer_wq[l].astype(CDT), seq, emit_h=True)
        cnt, rk2, e1, e2 = _peer_scores(qp, peer_k1[l], peer_k2[l])
        x2d = _peer_experts(h2, peer_u[l].astype(CDT), peer_v[l].T.astype(CDT), cnt, rk2, e1, e2,
                            x2d, ga2, g_final, l == depth - 1, seq)
    return x2d.reshape(batch, seq, d)
```
